```python
import math
import jax, jax.numpy as jnp
from jax import lax
import numpy as np

D_MODEL = 1024
BATCH = 8
SEQ = 2048
DEPTH = 1
DEC_BATCH = 16
DEC_SEQ = 64
PAST_LEN = 1024

CHUNK = 64
Q_BLOCK = 128
SB_HEADS = 8
SB_HEAD_DIM = 64
SB_WIDTH = SB_HEADS * SB_HEAD_DIM
ML_HEADS = 4
ML_HEAD_DIM = 128
ML_WIDTH = ML_HEADS * ML_HEAD_DIM
CONV_W = 4
D_FF = -(-8 * D_MODEL // (3 * 256)) * 256
EPS = 1e-6
IN_SPLITS = (SB_WIDTH, SB_WIDTH, SB_WIDTH,
             2 * ML_WIDTH,
             ML_WIDTH, ML_WIDTH,
             ML_HEADS, ML_HEADS,
             D_MODEL, D_MODEL)
IN_WIDTH = 3 * SB_WIDTH + 4 * ML_WIDTH + 2 * ML_HEADS + 2 * D_MODEL

kernel_name = 'stickbreak_mlstm_parallel_adaln_stream_step'


def _rmsnorm(x, g):
    xf = x.astype(jnp.float32)
    y = xf * lax.rsqrt(jnp.mean(xf * xf, axis=-1, keepdims=True) + EPS) * g.astype(jnp.float32)
    return y.astype(x.dtype)


def _stick_breaking(q, k, v, q_start):
    B, L, H, dh = q.shape
    T = k.shape[1]
    blk = min(Q_BLOCK, L)
    nb = L // blk
    qb = jnp.moveaxis(q.reshape(B, nb, blk, H, dh), 1, 0)
    pos = (q_start + jnp.arange(L)).reshape(nb, blk)
    k_pos = jnp.arange(T)
    scale = 1.0 / math.sqrt(dh)

    def one_block(args):
        qi, pi = args
        z = jnp.einsum('bqhd,bkhd->bhqk', qi, k).astype(jnp.float32) * scale
        mask = k_pos[None, :] < pi[:, None]
        log_1mb = jnp.where(mask, jax.nn.log_sigmoid(-z), 0.0)
        tail = lax.cumsum(log_1mb, axis=3, reverse=True) - log_1mb
        a = jnp.where(mask, jnp.exp(jax.nn.log_sigmoid(z) + tail), 0.0)
        return jnp.einsum('bhqk,bkhd->bqhd', a.astype(v.dtype), v)

    out = lax.map(one_block, (qb, pos))
    return jnp.moveaxis(out, 0, 1).reshape(B, L, H * dh)


def _mlstm_chunk(carry, inp):
    C, n, m = carry
    q, k, v, log_i, log_f = inp
    L = q.shape[2]
    b = jnp.cumsum(log_f, axis=-1)
    tril = jnp.tril(jnp.ones((L, L), bool))
    d_log = b[..., :, None] - b[..., None, :] + log_i[..., None, :]
    d_log = jnp.where(tril, d_log, -jnp.inf)
    inter = b + m[..., None]
    m_t = jnp.maximum(inter, jnp.max(d_log, axis=-1))
    w_intra = jnp.exp(d_log - m_t[..., None])
    w_inter = jnp.exp(inter - m_t)
    s = jnp.einsum('bhtd,bhsd->bhts', q, k) * w_intra
    num = jnp.einsum('bhts,bhsd->bhtd', s, v) + w_inter[..., None] * jnp.einsum('bhvk,bhtk->bhtv', C, q)
    den = jnp.sum(s, axis=-1) + w_inter * jnp.einsum('bhk,bhtk->bht', n, q)
    h = num / jnp.maximum(jnp.abs(den), jnp.exp(-m_t))[..., None]
    b_last = b[..., -1]
    g = b_last[..., None] - b + log_i
    m_new = jnp.maximum(b_last + m, jnp.max(g, axis=-1))
    wg = jnp.exp(g - m_new[..., None])
    decay = jnp.exp(b_last + m - m_new)
    C_new = decay[..., None, None] * C + jnp.einsum('bhs,bhsv,bhsk->bhvk', wg, v, k)
    n_new = decay[..., None] * n + jnp.einsum('bhs,bhsk->bhk', wg, k)
    return (C_new, n_new, m_new), h


def _to_chunks(a, n_chunks, csz):
    B, H = a.shape[:2]
    a = a.reshape((B, H, n_chunks, csz) + a.shape[3:])
    return jnp.moveaxis(a, 2, 0)


def _mlstm(q, k, v, log_i, log_f, C0, n0, m0):
    B, L, H, d = q.shape
    f32 = jnp.float32
    qh, kh, vh = (jnp.swapaxes(a.astype(f32), 1, 2) for a in (q, k, v))
    li, lf = (jnp.swapaxes(a.astype(f32), 1, 2) for a in (log_i, log_f))
    csz = min(CHUNK, L)
    nc = L // csz
    xs = tuple(_to_chunks(a, nc, csz) for a in (qh, kh, vh, li, lf))
    carry0 = (C0.astype(f32), n0.astype(f32), m0.astype(f32))
    (C1, n1, m1), hs = lax.scan(_mlstm_chunk, carry0, xs)
    h = jnp.moveaxis(hs, 0, 2).reshape(B, H, L, d)
    return jnp.swapaxes(h, 1, 2), C1, n1, m1


def _layer(x, c, sb_k_past, sb_v_past, C0, n0, m0, conv0,
           norm1_g, norm2_g, w_ada, b_ada, w_in, b_if, w_conv, b_conv, ml_norm_g,
           w_a, w_b, w_out, w_ff_gate, w_ff_up, w_ff_down):
    B, L, _ = x.shape
    q_start = sb_k_past.shape[1]
    mod = jax.nn.silu(c) @ w_ada + b_ada
    sh1, sc1, g1, sh2, sc2, g2 = jnp.split(mod[:, None, :], 6, axis=-1)

    u = _rmsnorm(x, norm1_g) * (1 + sc1) + sh1
    proj = u @ w_in
    idx = np.cumsum(IN_SPLITS)[:-1]
    sq, sk, sv, mqk_pre, mv, mo, mi, mf, ga, gb = jnp.split(proj, idx, axis=-1)

    k_new = sk.reshape(B, L, SB_HEADS, SB_HEAD_DIM)
    v_new = sv.reshape(B, L, SB_HEADS, SB_HEAD_DIM)
    k_all = jnp.concatenate([sb_k_past.astype(k_new.dtype), k_new], axis=1)
    v_all = jnp.concatenate([sb_v_past.astype(v_new.dtype), v_new], axis=1)
    y_a = _stick_breaking(sq.reshape(B, L, SB_HEADS, SB_HEAD_DIM), k_all, v_all, q_start)

    xpad = jnp.concatenate([conv0.astype(mqk_pre.dtype), mqk_pre], axis=1)
    conv = b_conv + sum(xpad[:, j:j + L, :] * w_conv[j] for j in range(CONV_W))
    conv = jax.nn.silu(conv)
    conv_new = xpad[:, L:, :]
    mq, mk = jnp.split(conv, 2, axis=-1)
    mq = mq.reshape(B, L, ML_HEADS, ML_HEAD_DIM)
    mk = mk.reshape(B, L, ML_HEADS, ML_HEAD_DIM) * (1.0 / math.sqrt(ML_HEAD_DIM))
    mvh = mv.reshape(B, L, ML_HEADS, ML_HEAD_DIM)
    log_i = (mi + b_if[:ML_HEADS]).astype(jnp.float32)
    log_f = jax.nn.log_sigmoid((mf + b_if[ML_HEADS:]).astype(jnp.float32))
    h, C1, n1, m1 = _mlstm(mq, mk, mvh, log_i, log_f, C0, n0, m0)
    h = h * lax.rsqrt(jnp.mean(h * h, axis=-1, keepdims=True) + EPS)
    h = (h.reshape(B, L, ML_WIDTH) * ml_norm_g.astype(jnp.float32)).astype(x.dtype)
    y_b = jax.nn.sigmoid(mo) * h

    merged = jax.nn.sigmoid(ga) * (y_a @ w_a) + jax.nn.sigmoid(gb) * (y_b @ w_b)
    x = x + g1 * (merged @ w_out)

    u2 = _rmsnorm(x, norm2_g) * (1 + sc2) + sh2
    x = x + g2 * ((jax.nn.silu(u2 @ w_ff_gate) * (u2 @ w_ff_up)) @ w_ff_down)
    return x, (k_new, v_new, C1, n1, m1, conv_new)


def setup_inputs(seed: int = 0) -> dict:
    key = jax.random.key(seed)
    ks = iter(jax.random.split(key, 40))
    f32 = jnp.float32

    def nrm(shape, scale):
        return jax.random.normal(next(ks), shape, f32) * scale

    def gain(shape):
        return 1.0 + nrm(shape, 0.02)

    b_if = jnp.concatenate([nrm((DEPTH, ML_HEADS), 0.1),
                            jnp.linspace(3.0, 6.0, ML_HEADS, dtype=f32)[None, :] + nrm((DEPTH, ML_HEADS), 0.1)], axis=-1)
    return {
        'x_prompt': nrm((BATCH, SEQ, D_MODEL), 1.0),
        'x_sample': nrm((DEC_BATCH, DEC_SEQ, D_MODEL), 1.0),
        'c_prompt': nrm((BATCH, D_MODEL), 1.0),
        'c_sample': nrm((DEC_BATCH, D_MODEL), 1.0),
        'cache_sb_k': nrm((DEPTH, DEC_BATCH, PAST_LEN, SB_HEADS, SB_HEAD_DIM), 1.0),
        'cache_sb_v': nrm((DEPTH, DEC_BATCH, PAST_LEN, SB_HEADS, SB_HEAD_DIM), 1.0),
        'state_mlstm_C': nrm((DEPTH, DEC_BATCH, ML_HEADS, ML_HEAD_DIM, ML_HEAD_DIM), 0.3),
        'state_mlstm_n': nrm((DEPTH, DEC_BATCH, ML_HEADS, ML_HEAD_DIM), 0.3),
        'state_mlstm_m': nrm((DEPTH, DEC_BATCH, ML_HEADS), 1.0),
        'state_conv': nrm((DEPTH, DEC_BATCH, CONV_W - 1, 2 * ML_WIDTH), 1.0),
        'norm1_g': gain((DEPTH, D_MODEL)),
        'norm2_g': gain((DEPTH, D_MODEL)),
        'w_ada': nrm((DEPTH, D_MODEL, 6 * D_MODEL), 0.5 * D_MODEL ** -0.5),
        'b_ada': nrm((DEPTH, 6 * D_MODEL), 0.02),
        'w_in': nrm((DEPTH, D_MODEL, IN_WIDTH), D_MODEL ** -0.5),
        'b_if': b_if,
        'w_conv': nrm((DEPTH, CONV_W, 2 * ML_WIDTH), CONV_W ** -0.5),
        'b_conv': nrm((DEPTH, 2 * ML_WIDTH), 0.02),
        'ml_norm_g': gain((DEPTH, ML_WIDTH)),
        'w_a': nrm((DEPTH, SB_WIDTH, D_MODEL), SB_WIDTH ** -0.5),
        'w_b': nrm((DEPTH, ML_WIDTH, D_MODEL), ML_WIDTH ** -0.5),
        'w_out': nrm((DEPTH, D_MODEL, D_MODEL), D_MODEL ** -0.5),
        'w_ff_gate': nrm((DEPTH, D_MODEL, D_FF), D_MODEL ** -0.5),
        'w_ff_up': nrm((DEPTH, D_MODEL, D_FF), D_MODEL ** -0.5),
        'w_ff_down': nrm((DEPTH, D_FF, D_MODEL), D_FF ** -0.5),
        'final_g': gain((D_MODEL,)),
    }


def reference(x_prompt, x_sample, c_prompt, c_sample, cache_sb_k, cache_sb_v,
              state_mlstm_C, state_mlstm_n, state_mlstm_m, state_conv,
              norm1_g, norm2_g, w_ada, b_ada, w_in, b_if, w_conv, b_conv, ml_norm_g,
              w_a, w_b, w_out, w_ff_gate, w_ff_up, w_ff_down, final_g):
    f32 = jnp.float32
    bp = x_prompt.shape[0]
    hp, hs = x_prompt, x_sample
    p_states, s_states = [], []
    for l in range(DEPTH):
        params = (norm1_g[l], norm2_g[l], w_ada[l], b_ada[l], w_in[l], b_if[l], w_conv[l], b_conv[l],
                  ml_norm_g[l], w_a[l], w_b[l], w_out[l], w_ff_gate[l], w_ff_up[l], w_ff_down[l])
        empty_kv = jnp.zeros((bp, 0, SB_HEADS, SB_HEAD_DIM), x_prompt.dtype)
        hp, sp = _layer(hp, c_prompt, empty_kv, empty_kv,
                        jnp.zeros((bp, ML_HEADS, ML_HEAD_DIM, ML_HEAD_DIM), f32),
                        jnp.zeros((bp, ML_HEADS, ML_HEAD_DIM), f32),
                        jnp.zeros((bp, ML_HEADS), f32),
                        jnp.zeros((bp, CONV_W - 1, 2 * ML_WIDTH), x_prompt.dtype),
                        *params)
        hs, ss = _layer(hs, c_sample, cache_sb_k[l], cache_sb_v[l], state_mlstm_C[l],
                        state_mlstm_n[l], state_mlstm_m[l], state_conv[l], *params)
        p_states.append(sp)
        s_states.append(ss)
    y_prompt = _rmsnorm(hp, final_g)
    y_sample = _rmsnorm(hs, final_g)
    pk, pv, pC, pn, pm, pconv = (jnp.stack([st[i] for st in p_states]) for i in range(6))
    sk, sv, sC, sn, sm, sconv = (jnp.stack([st[i] for st in s_states]) for i in range(6))
    return (y_prompt, y_sample, pk, pv, pC, pn, pm, pconv, sk, sv, sC, sn, sm, sconv)
```

```python
import functools
import math

import jax
import jax.numpy as jnp
from jax import lax
from jax.experimental import pallas as pl
from jax.experimental.pallas import tpu as pltpu

D_MODEL = 1024
SB_HEADS = 8
SB_HEAD_DIM = 64
SB_WIDTH = SB_HEADS * SB_HEAD_DIM
ML_HEADS = 4
ML_HEAD_DIM = 128
ML_WIDTH = ML_HEADS * ML_HEAD_DIM
CONV_W = 4
D_FF = 2816
EPS = 1e-6
LANES = 128
SUBLANES = 8
ATTN_BLOCK = 128
ML_CHUNK = 128
VMEM_LIMIT = 56 * 1024 * 1024

BF16 = jnp.bfloat16
F32 = jnp.float32

_C_Q, _C_K, _C_V = 0, SB_WIDTH, 2 * SB_WIDTH
_C_MQK = 3 * SB_WIDTH
_C_MV = _C_MQK + 2 * ML_WIDTH
_C_MO = _C_MV + ML_WIDTH
_C_IF = _C_MO + ML_WIDTH
_C_GAB = _C_IF + 2 * ML_HEADS
_C_END = _C_GAB + 2 * D_MODEL


def _sigmoid(x):
    return 1.0 / (1.0 + jnp.exp(-x))


def _log_sigmoid(x):
    return jnp.minimum(x, 0.0) - jnp.log1p(jnp.exp(-jnp.abs(x)))


def _softplus(x):
    return jnp.maximum(x, 0.0) + jnp.log(1.0 + jnp.exp(-jnp.abs(x)))


def _dot(a, b):
    return jnp.dot(a, b, preferred_element_type=F32)


def _dot_nt(a, b):
    return lax.dot_general(a, b, (((1,), (1,)), ((), ())), preferred_element_type=F32)


def _dot_tn(a, b):
    return lax.dot_general(a, b, (((0,), (0,)), ((), ())), preferred_element_type=F32)


def _split_dot(x, tri, parts):
    acc = None
    rem = x
    for i in range(parts):
        piece = rem.astype(BF16)
        term = _dot(piece, tri)
        acc = term if acc is None else acc + term
        if i + 1 < parts:
            rem = rem - piece.astype(F32)
    return acc


def _split_dot_left(tri, x, parts):
    acc = None
    rem = x
    for i in range(parts):
        piece = rem.astype(BF16)
        term = _dot(tri, piece)
        acc = term if acc is None else acc + term
        if i + 1 < parts:
            rem = rem - piece.astype(F32)
    return acc


def _norm_mod(x, g, sc, sh):
    ms = jnp.mean(x * x, axis=-1, keepdims=True)
    return x * lax.rsqrt(ms + EPS) * g * (1.0 + sc) + sh


def _const_spec(shape):
    nd = len(shape)
    return pl.BlockSpec(shape, lambda *_: (0,) * nd, pipeline_mode=pl.Buffered(1))


def _params(n_axes):
    return pltpu.CompilerParams(dimension_semantics=("arbitrary",) * n_axes,
                                vmem_limit_bytes=VMEM_LIMIT)


def _ada_kernel(c_ref, w_ref, b_ref, o_ref):
    c = c_ref[...]
    s = c * _sigmoid(c)
    o_ref[...] = _dot(s.astype(BF16), w_ref[...].astype(BF16)) + b_ref[...]


def _ada(c_all, w_ada, b_ada):
    nb = c_all.shape[0]
    n_out = w_ada.shape[1]
    tn = D_MODEL
    return pl.pallas_call(
        _ada_kernel,
        grid=(n_out // tn,),
        in_specs=[pl.BlockSpec((nb, D_MODEL), lambda j: (0, 0)),
                  pl.BlockSpec((D_MODEL, tn), lambda j: (0, j)),
                  pl.BlockSpec((1, tn), lambda j: (0, j))],
        out_specs=pl.BlockSpec((nb, tn), lambda j: (0, j)),
        out_shape=jax.ShapeDtypeStruct((nb, n_out), F32),
        compiler_params=_params(1),
        name="ada",
    )(c_all, w_ada, b_ada.reshape(1, n_out))


def _inproj_kernel(x_ref, mod_ref, conv0_ref, g1_ref, w1_ref, wifc_ref, wifr_ref, bifc_ref, bifr_ref,
                   wconv_ref, bconv_ref,
                   q_ref, k32_ref, v32_ref, kb_ref, vb_ref, mq_ref, mk_ref, mv_ref, mo_ref,
                   gcol_ref, grow_ref, convn_ref,
                   u_sc, xp_sc, *, bt, lt):
    step = pl.program_id(1)
    g1 = g1_ref[...]
    for j in range(bt):
        u = _norm_mod(x_ref[j], g1, mod_ref[j, 1:2, :], mod_ref[j, 0:1, :])
        u_sc[j * lt:(j + 1) * lt, :] = u.astype(BF16)
    u = u_sc[...]

    def mm(lo, hi):
        return _dot(u, w1_ref[:, lo:hi])

    sq = mm(_C_Q, _C_K)
    sk = mm(_C_K, _C_V)
    sv = mm(_C_V, _C_MQK)
    mqk = mm(_C_MQK, _C_MV)
    mv = mm(_C_MV, _C_MO)
    mo = mm(_C_MO, _C_IF)

    gc = _dot(u, wifc_ref[...]) + bifc_ref[...]
    lane = lax.broadcasted_iota(jnp.int32, gc.shape, 1)
    gc = jnp.where(lane >= ML_HEADS, _log_sigmoid(gc), gc)

    wconv = wconv_ref[...]
    bconv = bconv_ref[...]
    k_scale = 1.0 / math.sqrt(ML_HEAD_DIM)
    for j in range(bt):
        rows = slice(j * lt, (j + 1) * lt)
        q_ref[j] = sq[rows].astype(BF16)
        k32_ref[j] = sk[rows]
        v32_ref[j] = sv[rows]
        kb_ref[j] = sk[rows].astype(BF16)
        vb_ref[j] = sv[rows].astype(BF16)
        mv_ref[j] = mv[rows].astype(BF16)
        mo_ref[j] = mo[rows]
        gcol_ref[j] = gc[rows]

        gr = _dot_nt(wifr_ref[...], u_sc[rows, :]) + bifr_ref[...]
        row = lax.broadcasted_iota(jnp.int32, gr.shape, 0)
        grow_ref[j] = jnp.where(row >= ML_HEADS, _log_sigmoid(gr), gr)

        @pl.when(step == 0)
        def _():
            xp_sc[j, SUBLANES - (CONV_W - 1):SUBLANES, :] = conv0_ref[j]

        xp_sc[j, SUBLANES:SUBLANES + lt, :] = mqk[rows]
        conv = bconv
        for i in range(CONV_W):
            off = SUBLANES - (CONV_W - 1) + i
            conv = conv + xp_sc[j, off:off + lt, :] * wconv[i:i + 1, :]
        convn_ref[j] = xp_sc[j, lt + SUBLANES - (CONV_W - 1):lt + SUBLANES, :]
        xp_sc[j, 0:SUBLANES, :] = xp_sc[j, lt:lt + SUBLANES, :]
        conv = conv * _sigmoid(conv)
        mq_ref[j] = conv[:, :ML_WIDTH].astype(BF16)
        mk_ref[j] = (conv[:, ML_WIDTH:] * k_scale).astype(BF16)


def _inproj(x, mod3, conv0, g1, w1, wif_col, wif_row, bif_col, bif_row, wconv, bconv, *, bt, lt):
    nb, seq, _ = x.shape
    assert nb % bt == 0 and seq % lt == 0 and (bt == 1 or lt == seq)
    grid = (nb // bt, seq // lt)
    n1 = w1.shape[1]

    def tok(width):
        return pl.BlockSpec((bt, lt, width), lambda b, l: (b, l, 0))

    def per_batch(rows, width):
        return pl.BlockSpec((bt, rows, width), lambda b, l: (b, 0, 0))

    in_specs = [tok(D_MODEL), per_batch(6, D_MODEL), per_batch(CONV_W - 1, 2 * ML_WIDTH),
                _const_spec((1, D_MODEL)), _const_spec((D_MODEL, n1)),
                _const_spec((D_MODEL, LANES)), _const_spec((SUBLANES, D_MODEL)),
                _const_spec((1, LANES)), _const_spec((SUBLANES, 1)),
                _const_spec((CONV_W, 2 * ML_WIDTH)), _const_spec((1, 2 * ML_WIDTH))]
    out_specs = [tok(SB_WIDTH)] * 5 + [tok(ML_WIDTH)] * 4 + [
        tok(LANES),
        pl.BlockSpec((bt, SUBLANES, lt), lambda b, l: (b, 0, l)),
        per_batch(CONV_W - 1, 2 * ML_WIDTH)]

    def sds(shape, dt):
        return jax.ShapeDtypeStruct(shape, dt)

    out_shape = [sds((nb, seq, SB_WIDTH), BF16), sds((nb, seq, SB_WIDTH), F32), sds((nb, seq, SB_WIDTH), F32),
                 sds((nb, seq, SB_WIDTH), BF16), sds((nb, seq, SB_WIDTH), BF16),
                 sds((nb, seq, ML_WIDTH), BF16), sds((nb, seq, ML_WIDTH), BF16), sds((nb, seq, ML_WIDTH), BF16),
                 sds((nb, seq, ML_WIDTH), F32),
                 sds((nb, seq, LANES), F32), sds((nb, SUBLANES, seq), F32),
                 sds((nb, CONV_W - 1, 2 * ML_WIDTH), F32)]
    return pl.pallas_call(
        functools.partial(_inproj_kernel, bt=bt, lt=lt),
        grid=grid,
        in_specs=in_specs,
        out_specs=out_specs,
        out_shape=out_shape,
        scratch_shapes=[pltpu.VMEM((bt * lt, D_MODEL), BF16),
                        pltpu.VMEM((bt, lt + SUBLANES, 2 * ML_WIDTH), F32)],
        compiler_params=_params(2),
        name="inproj",
    )(x, mod3, conv0, g1, w1, wif_col, wif_row, bif_col, bif_row, wconv, bconv)


def _attn_kernel(q_ref, kd_ref, vd_ref, kp_ref, vp_ref, o_ref, *, tq, tk, n_past):
    n_blocks = pl.program_id(1) if n_past is None else n_past
    scale = 1.0 / math.sqrt(SB_HEAD_DIM)
    lane = lax.broadcasted_iota(jnp.int32, (1, LANES), 1)
    first_head = lane < SB_HEAD_DIM

    rp = lax.broadcasted_iota(jnp.int32, (tk, tk), 0)
    cp = lax.broadcasted_iota(jnp.int32, (tk, tk), 1)
    tri_p = jnp.where(rp > cp, 1.0, 0.0).astype(BF16)
    rd = lax.broadcasted_iota(jnp.int32, (tq, tq), 0)
    cd = lax.broadcasted_iota(jnp.int32, (tq, tq), 1)
    tri_d = jnp.where(rd > cd, 1.0, 0.0).astype(BF16)
    visible = cd < rd

    for p in range(SB_WIDTH // LANES):
        cols = slice(p * LANES, (p + 1) * LANES)
        qp = q_ref[:, cols]
        kd = kd_ref[:, cols]
        vd = vd_ref[:, cols]
        halves = []
        for s in range(2):
            keep = first_head if s == 0 else jnp.logical_not(first_head)
            qm = jnp.where(keep, qp, jnp.zeros_like(qp))

            z = _dot_nt(qm, kd) * scale
            sp = _softplus(z)
            l1m = jnp.where(visible, -sp, 0.0)
            tail = _split_dot(l1m, tri_d, 2)
            a = jnp.where(visible, jnp.exp(z - sp + tail), 0.0)
            acc = _dot(a.astype(BF16), vd)
            carry = jnp.sum(l1m, axis=1, keepdims=True)

            def body(i, state, qm=qm, cols=cols):
                acc, carry = state
                start = pl.multiple_of((n_blocks - 1 - i) * tk, tk)
                kb = kp_ref[pl.ds(start, tk), cols].astype(BF16)
                vb = vp_ref[pl.ds(start, tk), cols].astype(BF16)
                z = _dot_nt(qm, kb) * scale
                sp = _softplus(z)
                tail = _split_dot(-sp, tri_p, 2)
                a = jnp.exp(z - sp + tail + carry)
                acc = acc + _dot(a.astype(BF16), vb)
                carry = carry - jnp.sum(sp, axis=1, keepdims=True)
                return acc, carry

            acc, carry = lax.fori_loop(0, n_blocks, body, (acc, carry))
            halves.append(acc)
        o_ref[:, cols] = jnp.where(first_head, halves[0], halves[1]).astype(BF16)


def _attn(q, k_new, v_new, k_past, v_past, *, tq, n_past):
    nb, seq, _ = q.shape
    t_past = k_past.shape[1]
    tok = pl.BlockSpec((None, tq, SB_WIDTH), lambda b, i: (b, i, 0))
    past = pl.BlockSpec((None, t_past, SB_WIDTH), lambda b, i: (b, 0, 0))
    return pl.pallas_call(
        functools.partial(_attn_kernel, tq=tq, tk=ATTN_BLOCK, n_past=n_past),
        grid=(nb, seq // tq),
        in_specs=[tok, tok, tok, past, past],
        out_specs=tok,
        out_shape=jax.ShapeDtypeStruct((nb, seq, SB_WIDTH), BF16),
        compiler_params=_params(2),
        name="attn",
    )(q, k_new, v_new, k_past, v_past)


def _mlstm_kernel(mq_ref, mk_ref, mv_ref, mo_ref, gcol_ref, grow_ref, c0_ref, n0_ref, m0_ref, gn_ref,
                  yb_ref, c_out_ref, n_out_ref, m_out_ref,
                  c_sc, n_sc, m_sc, *, lc):
    step = pl.program_id(1)

    @pl.when(step == 0)
    def _():
        c_sc[...] = c0_ref[...]
        n_sc[...] = n0_ref[...]
        m_sc[...] = m0_ref[...]

    r = lax.broadcasted_iota(jnp.int32, (lc, lc), 0)
    c = lax.broadcasted_iota(jnp.int32, (lc, lc), 1)
    causal = c <= r
    incl = jnp.where(causal, 1.0, 0.0).astype(BF16)
    incl_t = jnp.where(r <= c, 1.0, 0.0).astype(BF16)

    gcol = gcol_ref[...]
    grow = grow_ref[...]
    bcol = _split_dot_left(incl, gcol, 3)
    brow = _split_dot(grow, incl_t, 3)

    for h in range(ML_HEADS):
        cols = slice(h * ML_HEAD_DIM, (h + 1) * ML_HEAD_DIM)
        q = mq_ref[:, cols]
        k = mk_ref[:, cols]
        v = mv_ref[:, cols]
        li_c = gcol[:, h:h + 1]
        b_c = bcol[:, ML_HEADS + h:ML_HEADS + h + 1]
        li_r = grow[h:h + 1, :]
        b_r = brow[ML_HEADS + h:ML_HEADS + h + 1, :]
        b_last = b_r[:, lc - 1:lc]
        m_prev = m_sc[h:h + 1, 0:1]
        c_prev = c_sc[h]
        n_prev = n_sc[h:h + 1, :]

        d_log = jnp.where(causal, b_c - b_r + li_r, -jnp.inf)
        inter = b_c + m_prev
        m_t = jnp.maximum(inter, jnp.max(d_log, axis=1, keepdims=True))
        w_intra = jnp.exp(d_log - m_t)
        w_inter = jnp.exp(inter - m_t)
        s = _dot_nt(q, k) * w_intra
        qf = q.astype(F32)
        num = _dot(s.astype(BF16), v) + w_inter * _dot_nt(q, c_prev.astype(BF16))
        den = jnp.sum(s, axis=1, keepdims=True) + w_inter * jnp.sum(qf * n_prev, axis=1, keepdims=True)
        hval = num / jnp.maximum(jnp.abs(den), jnp.exp(-m_t))
        hval = hval * lax.rsqrt(jnp.mean(hval * hval, axis=1, keepdims=True) + EPS)
        hval = hval * gn_ref[:, cols]
        yb_ref[:, cols] = (_sigmoid(mo_ref[:, cols]) * hval).astype(BF16)

        g = b_last - b_c + li_c
        m_new = jnp.maximum(b_last + m_prev, jnp.max(g, axis=0, keepdims=True))
        wg = jnp.exp(g - m_new)
        decay = jnp.exp(b_last + m_prev - m_new)
        vw = (wg * v.astype(F32)).astype(BF16)
        c_new = decay * c_prev + _dot_tn(vw, k)
        n_new = decay * n_prev + jnp.sum(wg * k.astype(F32), axis=0, keepdims=True)
        c_sc[h] = c_new
        n_sc[h:h + 1, :] = n_new
        m_sc[h:h + 1, :] = jnp.broadcast_to(m_new, (1, LANES))

    c_out_ref[...] = c_sc[...]
    n_out_ref[...] = n_sc[...]
    m_out_ref[...] = m_sc[...]


def _mlstm(mq, mk, mv, mo, gcol, grow, c0, n0, m0, gn, *, lc):
    nb, seq, _ = mq.shape
    tok = pl.BlockSpec((None, lc, ML_WIDTH), lambda b, i: (b, i, 0))
    c_spec = pl.BlockSpec((None, ML_HEADS, ML_HEAD_DIM, ML_HEAD_DIM), lambda b, i: (b, 0, 0, 0))
    v_spec = pl.BlockSpec((None, ML_HEADS, ML_HEAD_DIM), lambda b, i: (b, 0, 0))
    return pl.pallas_call(
        functools.partial(_mlstm_kernel, lc=lc),
        grid=(nb, seq // lc),
        in_specs=[tok, tok, tok, tok,
                  pl.BlockSpec((None, lc, LANES), lambda b, i: (b, i, 0)),
                  pl.BlockSpec((None, SUBLANES, lc), lambda b, i: (b, 0, i)),
                  c_spec, v_spec, v_spec, _const_spec((1, ML_WIDTH))],
        out_specs=[tok, c_spec, v_spec, v_spec],
        out_shape=[jax.ShapeDtypeStruct((nb, seq, ML_WIDTH), BF16),
                   jax.ShapeDtypeStruct((nb, ML_HEADS, ML_HEAD_DIM, ML_HEAD_DIM), F32),
                   jax.ShapeDtypeStruct((nb, ML_HEADS, ML_HEAD_DIM), F32),
                   jax.ShapeDtypeStruct((nb, ML_HEADS, LANES), F32)],
        scratch_shapes=[pltpu.VMEM((ML_HEADS, ML_HEAD_DIM, ML_HEAD_DIM), F32),
                        pltpu.VMEM((ML_HEADS, ML_HEAD_DIM), F32),
                        pltpu.VMEM((ML_HEADS, LANES), F32)],
        compiler_params=_params(2),
        name="mlstm",
    )(mq, mk, mv, mo, gcol, grow, c0, n0, m0, gn)


def _post_kernel(x_ref, ya_ref, yb_ref, mod_ref, g1_ref, g2_ref, gf_ref,
                 wgab_ref, wa_ref, wb_ref, wout_ref, wfg_ref, wfu_ref, wfd_ref,
                 y_ref, u_sc, *, bt, lt):
    g1 = g1_ref[...]
    for j in range(bt):
        u = _norm_mod(x_ref[j], g1, mod_ref[j, 1:2, :], mod_ref[j, 0:1, :])
        u_sc[j * lt:(j + 1) * lt, :] = u.astype(BF16)
    u = u_sc[...]
    ya = ya_ref[...].reshape(bt * lt, SB_WIDTH)
    yb = yb_ref[...].reshape(bt * lt, ML_WIDTH)
    ga = _dot(u, wgab_ref[:, :D_MODEL])
    gb = _dot(u, wgab_ref[:, D_MODEL:])
    merged = _sigmoid(ga) * _dot(ya, wa_ref[...]) + _sigmoid(gb) * _dot(yb, wb_ref[...])
    attn_out = _dot(merged.astype(BF16), wout_ref[...])

    g2 = g2_ref[...]
    for j in range(bt):
        rows = slice(j * lt, (j + 1) * lt)
        x1 = x_ref[j] + mod_ref[j, 2:3, :] * attn_out[rows]
        y_ref[j] = x1
        u2 = _norm_mod(x1, g2, mod_ref[j, 4:5, :], mod_ref[j, 3:4, :])
        u_sc[rows, :] = u2.astype(BF16)
    u2 = u_sc[...]
    hg = _dot(u2, wfg_ref[...])
    hu = _dot(u2, wfu_ref[...])
    act = (hg * _sigmoid(hg) * hu).astype(BF16)
    ff = _dot(act, wfd_ref[...])
    gf = gf_ref[...]
    for j in range(bt):
        rows = slice(j * lt, (j + 1) * lt)
        x2 = y_ref[j] + mod_ref[j, 5:6, :] * ff[rows]
        ms = jnp.mean(x2 * x2, axis=-1, keepdims=True)
        y_ref[j] = x2 * lax.rsqrt(ms + EPS) * gf


def _post(x, ya, yb, mod3, g1, g2, gf, wgab, wa, wb, wout, wfg, wfu, wfd, *, bt, lt):
    nb, seq, _ = x.shape
    assert nb % bt == 0 and seq % lt == 0

    def tok(width):
        return pl.BlockSpec((bt, lt, width), lambda b, l: (b, l, 0))

    in_specs = [tok(D_MODEL), tok(SB_WIDTH), tok(ML_WIDTH),
                pl.BlockSpec((bt, 6, D_MODEL), lambda b, l: (b, 0, 0)),
                _const_spec((1, D_MODEL)), _const_spec((1, D_MODEL)), _const_spec((1, D_MODEL)),
                _const_spec(wgab.shape), _const_spec(wa.shape), _const_spec(wb.shape), _const_spec(wout.shape),
                _const_spec(wfg.shape), _const_spec(wfu.shape), _const_spec(wfd.shape)]
    return pl.pallas_call(
        functools.partial(_post_kernel, bt=bt, lt=lt),
        grid=(nb // bt, seq // lt),
        in_specs=in_specs,
        out_specs=tok(D_MODEL),
        out_shape=jax.ShapeDtypeStruct((nb, seq, D_MODEL), F32),
        scratch_shapes=[pltpu.VMEM((bt * lt, D_MODEL), BF16)],
        compiler_params=_params(2),
        name="post",
    )(x, ya, yb, mod3, g1, g2, gf, wgab, wa, wb, wout, wfg, wfu, wfd)


def _group(x, mod3, k_past, v_past, c0, n0, m0, conv0, wts, *, bt, lt, tq, n_past, lc):
    (g1, g2, gf, w1, wif_col, wif_row, bif_col, bif_row, wconv, bconv, gn,
     wgab, wa, wb, wout, wfg, wfu, wfd) = wts
    nb, seq, _ = x.shape
    (q, k32, v32, kb, vb, mq, mk, mv, mo, gcol, grow, conv_new) = _inproj(
        x, mod3, conv0, g1, w1, wif_col, wif_row, bif_col, bif_row, wconv, bconv, bt=bt, lt=lt)
    if k_past is None:
        k_past, v_past = kb, vb
    ya = _attn(q, kb, vb, k_past, v_past, tq=tq, n_past=n_past)
    m0b = jnp.broadcast_to(m0[:, :, None], (nb, ML_HEADS, LANES))
    yb, c1, n1, m1 = _mlstm(mq, mk, mv, mo, gcol, grow, c0, n0, m0b, gn, lc=lc)
    y = _post(x, ya, yb, mod3, g1, g2, gf, wgab, wa, wb, wout, wfg, wfu, wfd, bt=bt, lt=lt)
    states = (k32.reshape(1, nb, seq, SB_HEADS, SB_HEAD_DIM), v32.reshape(1, nb, seq, SB_HEADS, SB_HEAD_DIM),
              c1[None], n1[None], m1[None, :, :, 0], conv_new[None])
    return y, states


def kernel(x_prompt, x_sample, c_prompt, c_sample, cache_sb_k, cache_sb_v, state_mlstm_C, state_mlstm_n, state_mlstm_m, state_conv, norm1_g, norm2_g, w_ada, b_ada, w_in, b_if, w_conv, b_conv, ml_norm_g, w_a, w_b, w_out, w_ff_gate, w_ff_up, w_ff_down, final_g):
    assert w_in.shape[0] == 1, "single layer"
    bp, seq_p, _ = x_prompt.shape
    bs, seq_s, _ = x_sample.shape
    past = cache_sb_k.shape[2]

    c_all = jnp.concatenate([c_prompt, c_sample], axis=0)
    mod3 = _ada(c_all, w_ada[0], b_ada[0]).reshape(bp + bs, 6, D_MODEL)

    w = w_in[0]
    w1 = w[:, :_C_IF].astype(BF16)
    wif = w[:, _C_IF:_C_GAB]
    wif_col = jnp.pad(wif, ((0, 0), (0, LANES - 2 * ML_HEADS))).astype(BF16)
    wif_row = wif.T.astype(BF16)
    bif_col = jnp.pad(b_if[0], (0, LANES - 2 * ML_HEADS)).reshape(1, LANES)
    bif_row = b_if[0].reshape(2 * ML_HEADS, 1)
    wts = (norm1_g[0].reshape(1, D_MODEL), norm2_g[0].reshape(1, D_MODEL), final_g.reshape(1, D_MODEL),
           w1, wif_col, wif_row, bif_col, bif_row, w_conv[0], b_conv[0].reshape(1, 2 * ML_WIDTH),
           ml_norm_g[0].reshape(1, ML_WIDTH),
           w[:, _C_GAB:_C_END].astype(BF16), w_a[0].astype(BF16), w_b[0].astype(BF16), w_out[0].astype(BF16),
           w_ff_gate[0].astype(BF16), w_ff_up[0].astype(BF16), w_ff_down[0].astype(BF16))

    zeros = functools.partial(jnp.zeros, dtype=F32)
    y_p, st_p = _group(
        x_prompt, mod3[:bp], None, None,
        zeros((bp, ML_HEADS, ML_HEAD_DIM, ML_HEAD_DIM)), zeros((bp, ML_HEADS, ML_HEAD_DIM)),
        zeros((bp, ML_HEADS)), zeros((bp, CONV_W - 1, 2 * ML_WIDTH)), wts,
        bt=1, lt=256, tq=ATTN_BLOCK, n_past=None, lc=ML_CHUNK)
    y_s, st_s = _group(
        x_sample, mod3[bp:],
        cache_sb_k[0].reshape(bs, past, SB_WIDTH), cache_sb_v[0].reshape(bs, past, SB_WIDTH),
        state_mlstm_C[0], state_mlstm_n[0], state_mlstm_m[0], state_conv[0], wts,
        bt=4, lt=seq_s, tq=seq_s, n_past=past // ATTN_BLOCK, lc=seq_s)
    return (y_p, y_s) + st_p + st_s
```

```python
import functools
import math

import jax
import jax.numpy as jnp
from jax import lax
from jax.experimental import pallas as pl
from jax.experimental.pallas import tpu as pltpu

D_MODEL = 1024
SB_HEADS = 8
SB_HEAD_DIM = 64
SB_WIDTH = SB_HEADS * SB_HEAD_DIM
ML_HEADS = 4
ML_HEAD_DIM = 128
ML_WIDTH = ML_HEADS * ML_HEAD_DIM
CONV_W = 4
D_FF = 2816
EPS = 1e-6
LANES = 128
SUBLANES = 8
ATTN_BLOCK = 128
ML_CHUNK = 128
VMEM_LIMIT = 56 * 1024 * 1024

BF16 = jnp.bfloat16
F32 = jnp.float32

_C_Q, _C_K, _C_V = 0, SB_WIDTH, 2 * SB_WIDTH
_C_MQK = 3 * SB_WIDTH
_C_MV = _C_MQK + 2 * ML_WIDTH
_C_MO = _C_MV + ML_WIDTH
_C_IF = _C_MO + ML_WIDTH
_C_GAB = _C_IF + 2 * ML_HEADS
_C_END = _C_GAB + 2 * D_MODEL


def _sigmoid(x):
    return 1.0 / (1.0 + jnp.exp(-x))


def _log_sigmoid(x):
    return jnp.minimum(x, 0.0) - jnp.log1p(jnp.exp(-jnp.abs(x)))


def _softplus(x):
    return jnp.maximum(x, 0.0) + jnp.log(1.0 + jnp.exp(-jnp.abs(x)))


def _dot(a, b):
    return jnp.dot(a, b, preferred_element_type=F32)


def _dot_nt(a, b):
    return lax.dot_general(a, b, (((1,), (1,)), ((), ())), preferred_element_type=F32)


def _dot_tn(a, b):
    return lax.dot_general(a, b, (((0,), (0,)), ((), ())), preferred_element_type=F32)


def _split_dot(x, tri, parts):
    acc = None
    rem = x
    for i in range(parts):
        piece = rem.astype(BF16)
        term = _dot(piece, tri)
        acc = term if acc is None else acc + term
        if i + 1 < parts:
            rem = rem - piece.astype(F32)
    return acc


def _split_dot_left(tri, x, parts):
    acc = None
    rem = x
    for i in range(parts):
        piece = rem.astype(BF16)
        term = _dot(tri, piece)
        acc = term if acc is None else acc + term
        if i + 1 < parts:
            rem = rem - piece.astype(F32)
    return acc


def _norm_mod(x, g, sc, sh):
    ms = jnp.mean(x * x, axis=-1, keepdims=True)
    return x * lax.rsqrt(ms + EPS) * g * (1.0 + sc) + sh


def _const_spec(shape):
    nd = len(shape)
    return pl.BlockSpec(shape, lambda *_: (0,) * nd, pipeline_mode=pl.Buffered(1))


def _params(n_axes):
    return pltpu.CompilerParams(dimension_semantics=("arbitrary",) * n_axes,
                                vmem_limit_bytes=VMEM_LIMIT)


def _ada_kernel(c_ref, w_ref, b_ref, o_ref):
    c = c_ref[...]
    s = c * _sigmoid(c)
    o_ref[...] = _dot(s.astype(BF16), w_ref[...].astype(BF16)) + b_ref[...]


def _ada(c_all, w_ada, b_ada):
    nb = c_all.shape[0]
    n_out = w_ada.shape[1]
    tn = D_MODEL
    return pl.pallas_call(
        _ada_kernel,
        grid=(n_out // tn,),
        in_specs=[pl.BlockSpec((nb, D_MODEL), lambda j: (0, 0)),
                  pl.BlockSpec((D_MODEL, tn), lambda j: (0, j)),
                  pl.BlockSpec((1, tn), lambda j: (0, j))],
        out_specs=pl.BlockSpec((nb, tn), lambda j: (0, j)),
        out_shape=jax.ShapeDtypeStruct((nb, n_out), F32),
        compiler_params=_params(1),
        name="ada",
    )(c_all, w_ada, b_ada.reshape(1, n_out))


def _inproj_kernel(x_ref, mod_ref, conv0_ref, g1_ref, w1_ref, wifc_ref, wifr_ref, bifc_ref, bifr_ref,
                   wconv_ref, bconv_ref,
                   q_ref, k32_ref, v32_ref, kb_ref, vb_ref, mq_ref, mk_ref, mv_ref, mo_ref,
                   gcol_ref, grow_ref, convn_ref,
                   u_sc, xp_sc, *, bt, lt):
    step = pl.program_id(1)
    g1 = g1_ref[...]
    for j in range(bt):
        u = _norm_mod(x_ref[j], g1, mod_ref[j, 1:2, :], mod_ref[j, 0:1, :])
        u_sc[j * lt:(j + 1) * lt, :] = u.astype(BF16)
    u = u_sc[...]

    def mm(lo, hi):
        return _dot(u, w1_ref[:, lo:hi])

    sq = mm(_C_Q, _C_K)
    sk = mm(_C_K, _C_V)
    sv = mm(_C_V, _C_MQK)
    mqk = mm(_C_MQK, _C_MV)
    mv = mm(_C_MV, _C_MO)
    mo = mm(_C_MO, _C_IF)

    gc = _dot(u, wifc_ref[...]) + bifc_ref[...]
    lane = lax.broadcasted_iota(jnp.int32, gc.shape, 1)
    gc = jnp.where(lane >= ML_HEADS, _log_sigmoid(gc), gc)

    wconv = wconv_ref[...]
    bconv = bconv_ref[...]
    k_scale = 1.0 / math.sqrt(ML_HEAD_DIM)
    for j in range(bt):
        rows = slice(j * lt, (j + 1) * lt)
        q_ref[j] = sq[rows].astype(BF16)
        k32_ref[j] = sk[rows]
        v32_ref[j] = sv[rows]
        kb_ref[j] = sk[rows].astype(BF16)
        vb_ref[j] = sv[rows].astype(BF16)
        mv_ref[j] = mv[rows].astype(BF16)
        mo_ref[j] = mo[rows]
        gcol_ref[j] = gc[rows]

        gr = _dot_nt(wifr_ref[...], u_sc[rows, :]) + bifr_ref[...]
        row = lax.broadcasted_iota(jnp.int32, gr.shape, 0)
        grow_ref[j] = jnp.where(row >= ML_HEADS, _log_sigmoid(gr), gr)

        @pl.when(step == 0)
        def _():
            xp_sc[j, SUBLANES - (CONV_W - 1):SUBLANES, :] = conv0_ref[j]

        xp_sc[j, SUBLANES:SUBLANES + lt, :] = mqk[rows]
        conv = bconv
        for i in range(CONV_W):
            off = SUBLANES - (CONV_W - 1) + i
            conv = conv + xp_sc[j, off:off + lt, :] * wconv[i:i + 1, :]
        convn_ref[j] = xp_sc[j, lt + SUBLANES - (CONV_W - 1):lt + SUBLANES, :]
        xp_sc[j, 0:SUBLANES, :] = xp_sc[j, lt:lt + SUBLANES, :]
        conv = conv * _sigmoid(conv)
        mq_ref[j] = conv[:, :ML_WIDTH].astype(BF16)
        mk_ref[j] = (conv[:, ML_WIDTH:] * k_scale).astype(BF16)


def _inproj(x, mod3, conv0, g1, w1, wif_col, wif_row, bif_col, bif_row, wconv, bconv, *, bt, lt):
    nb, seq, _ = x.shape
    assert nb % bt == 0 and seq % lt == 0 and (bt == 1 or lt == seq)
    grid = (nb // bt, seq // lt)
    n1 = w1.shape[1]

    def tok(width):
        return pl.BlockSpec((bt, lt, width), lambda b, l: (b, l, 0))

    def per_batch(rows, width):
        return pl.BlockSpec((bt, rows, width), lambda b, l: (b, 0, 0))

    in_specs = [tok(D_MODEL), per_batch(6, D_MODEL), per_batch(CONV_W - 1, 2 * ML_WIDTH),
                _const_spec((1, D_MODEL)), _const_spec((D_MODEL, n1)),
                _const_spec((D_MODEL, LANES)), _const_spec((SUBLANES, D_MODEL)),
                _const_spec((1, LANES)), _const_spec((SUBLANES, 1)),
                _const_spec((CONV_W, 2 * ML_WIDTH)), _const_spec((1, 2 * ML_WIDTH))]
    out_specs = [tok(SB_WIDTH)] * 5 + [tok(ML_WIDTH)] * 4 + [
        tok(LANES),
        pl.BlockSpec((bt, SUBLANES, lt), lambda b, l: (b, 0, l)),
        per_batch(CONV_W - 1, 2 * ML_WIDTH)]

    def sds(shape, dt):
        return jax.ShapeDtypeStruct(shape, dt)

    out_shape = [sds((nb, seq, SB_WIDTH), BF16), sds((nb, seq, SB_WIDTH), F32), sds((nb, seq, SB_WIDTH), F32),
                 sds((nb, seq, SB_WIDTH), BF16), sds((nb, seq, SB_WIDTH), BF16),
                 sds((nb, seq, ML_WIDTH), BF16), sds((nb, seq, ML_WIDTH), BF16), sds((nb, seq, ML_WIDTH), BF16),
                 sds((nb, seq, ML_WIDTH), F32),
                 sds((nb, seq, LANES), F32), sds((nb, SUBLANES, seq), F32),
                 sds((nb, CONV_W - 1, 2 * ML_WIDTH), F32)]
    return pl.pallas_call(
        functools.partial(_inproj_kernel, bt=bt, lt=lt),
        grid=grid,
        in_specs=in_specs,
        out_specs=out_specs,
        out_shape=out_shape,
        scratch_shapes=[pltpu.VMEM((bt * lt, D_MODEL), BF16),
                        pltpu.VMEM((bt, lt + SUBLANES, 2 * ML_WIDTH), F32)],
        compiler_params=_params(2),
        name="inproj",
    )(x, mod3, conv0, g1, w1, wif_col, wif_row, bif_col, bif_row, wconv, bconv)


def _cum_matrix(t):
    r = lax.broadcasted_iota(jnp.int32, (2 * t, LANES + t), 0)
    c = lax.broadcasted_iota(jnp.int32, (2 * t, LANES + t), 1)
    return jnp.where((c < LANES) | ((r & (t - 1)) > (c - LANES)), 1.0, 0.0).astype(BF16)


def _attn_kernel(q_ref, kd_ref, vd_ref, kp_ref, vp_ref, o_ref, qm_sc, acc_sc, carry_sc, *, tq, tk, n_past):
    n_blocks = pl.program_id(1) if n_past is None else n_past
    scale = 1.0 / math.sqrt(SB_HEAD_DIM)
    n_pairs = SB_WIDTH // LANES
    lane = lax.broadcasted_iota(jnp.int32, (1, LANES), 1)
    first_head = lane < SB_HEAD_DIM

    for p in range(n_pairs):
        qp = q_ref[:, p * LANES:(p + 1) * LANES] * scale
        zero = jnp.zeros_like(qp)
        qm_sc[(2 * p) * tq:(2 * p + 1) * tq, :] = jnp.where(first_head, qp, zero)
        qm_sc[(2 * p + 1) * tq:(2 * p + 2) * tq, :] = jnp.where(first_head, zero, qp)

    def block(k_of, v_of, t, visible):
        z = jnp.concatenate(
            [_dot_nt(qm_sc[2 * p * tq:(2 * p + 2) * tq, :], k_of(p)) for p in range(n_pairs)], axis=0)
        sp = _softplus(z)
        if visible is not None:
            sp = jnp.where(visible, sp, 0.0)
        hi = sp.astype(BF16)
        lo = (sp - hi.astype(F32)).astype(BF16)
        sums = _dot(jnp.concatenate([hi, lo], axis=1), _cum_matrix(t))
        logit = z - sp - sums[:, LANES:]
        if visible is None:
            carry = carry_sc[...]
            a = jnp.exp(logit - carry)
            carry_sc[...] = carry + sums[:, :LANES]
        else:
            a = jnp.where(visible, jnp.exp(logit), 0.0)
            carry_sc[...] = sums[:, :LANES]
        a = a.astype(BF16)
        for p in range(n_pairs):
            cols = slice(p * LANES, (p + 1) * LANES)
            pv = _dot(a[2 * p * tq:(2 * p + 2) * tq, :], v_of(p))
            out = jnp.where(first_head, pv[:tq], pv[tq:])
            if visible is None:
                acc_sc[:, cols] += out
            else:
                acc_sc[:, cols] = out

    rd = lax.broadcasted_iota(jnp.int32, (SB_HEADS * tq, tq), 0) & (tq - 1)
    cd = lax.broadcasted_iota(jnp.int32, (SB_HEADS * tq, tq), 1)
    block(lambda p: kd_ref[:, p * LANES:(p + 1) * LANES],
          lambda p: vd_ref[:, p * LANES:(p + 1) * LANES],
          tq, cd < rd)

    def body(i, _):
        start = pl.multiple_of((n_blocks - 1 - i) * tk, tk)
        block(lambda p: kp_ref[pl.ds(start, tk), p * LANES:(p + 1) * LANES].astype(BF16),
              lambda p: vp_ref[pl.ds(start, tk), p * LANES:(p + 1) * LANES].astype(BF16),
              tk, None)
        return 0

    lax.fori_loop(0, n_blocks, body, 0)
    o_ref[...] = acc_sc[...].astype(BF16)


def _attn(q, k_new, v_new, k_past, v_past, *, tq, n_past):
    nb, seq, _ = q.shape
    t_past = k_past.shape[1]
    tok = pl.BlockSpec((None, tq, SB_WIDTH), lambda b, i: (b, i, 0))
    past = pl.BlockSpec((None, t_past, SB_WIDTH), lambda b, i: (b, 0, 0))
    return pl.pallas_call(
        functools.partial(_attn_kernel, tq=tq, tk=ATTN_BLOCK, n_past=n_past),
        grid=(nb, seq // tq),
        in_specs=[tok, tok, tok, past, past],
        out_specs=tok,
        out_shape=jax.ShapeDtypeStruct((nb, seq, SB_WIDTH), BF16),
        scratch_shapes=[pltpu.VMEM((SB_HEADS * tq, LANES), BF16),
                        pltpu.VMEM((tq, SB_WIDTH), F32),
                        pltpu.VMEM((SB_HEADS * tq, LANES), F32)],
        compiler_params=_params(2),
        name="attn",
    )(q, k_new, v_new, k_past, v_past)


def _mlstm_kernel(mq_ref, mk_ref, mv_ref, mo_ref, gcol_ref, grow_ref, c0_ref, n0_ref, m0_ref, gn_ref,
                  yb_ref, c_out_ref, n_out_ref, m_out_ref,
                  c_sc, n_sc, m_sc, *, lc):
    step = pl.program_id(1)

    @pl.when(step == 0)
    def _():
        c_sc[...] = c0_ref[...]
        n_sc[...] = n0_ref[...]
        m_sc[...] = m0_ref[...]

    r = lax.broadcasted_iota(jnp.int32, (lc, lc), 0)
    c = lax.broadcasted_iota(jnp.int32, (lc, lc), 1)
    causal = c <= r
    incl = jnp.where(causal, 1.0, 0.0).astype(BF16)
    incl_t = jnp.where(r <= c, 1.0, 0.0).astype(BF16)

    gcol = gcol_ref[...]
    grow = grow_ref[...]
    bcol = _split_dot_left(incl, gcol, 3)
    brow = _split_dot(grow, incl_t, 3)

    for h in range(ML_HEADS):
        cols = slice(h * ML_HEAD_DIM, (h + 1) * ML_HEAD_DIM)
        q = mq_ref[:, cols]
        k = mk_ref[:, cols]
        v = mv_ref[:, cols]
        li_c = gcol[:, h:h + 1]
        b_c = bcol[:, ML_HEADS + h:ML_HEADS + h + 1]
        li_r = grow[h:h + 1, :]
        b_r = brow[ML_HEADS + h:ML_HEADS + h + 1, :]
        b_last = b_r[:, lc - 1:lc]
        m_prev = m_sc[h:h + 1, 0:1]
        c_prev = c_sc[h]
        n_prev = n_sc[h:h + 1, :]

        d_log = jnp.where(causal, b_c - b_r + li_r, -jnp.inf)
        inter = b_c + m_prev
        m_t = jnp.maximum(inter, jnp.max(d_log, axis=1, keepdims=True))
        w_intra = jnp.exp(d_log - m_t)
        w_inter = jnp.exp(inter - m_t)
        s = _dot_nt(q, k) * w_intra
        qf = q.astype(F32)
        num = _dot(s.astype(BF16), v) + w_inter * _dot_nt(q, c_prev.astype(BF16))
        den = jnp.sum(s, axis=1, keepdims=True) + w_inter * jnp.sum(qf * n_prev, axis=1, keepdims=True)
        hval = num / jnp.maximum(jnp.abs(den), jnp.exp(-m_t))
        hval = hval * lax.rsqrt(jnp.mean(hval * hval, axis=1, keepdims=True) + EPS)
        hval = hval * gn_ref[:, cols]
        yb_ref[:, cols] = (_sigmoid(mo_ref[:, cols]) * hval).astype(BF16)

        g = b_last - b_c + li_c
        m_new = jnp.maximum(b_last + m_prev, jnp.max(g, axis=0, keepdims=True))
        wg = jnp.exp(g - m_new)
        decay = jnp.exp(b_last + m_prev - m_new)
        vw = (wg * v.astype(F32)).astype(BF16)
        c_new = decay * c_prev + _dot_tn(vw, k)
        n_new = decay * n_prev + jnp.sum(wg * k.astype(F32), axis=0, keepdims=True)
        c_sc[h] = c_new
        n_sc[h:h + 1, :] = n_new
        m_sc[h:h + 1, :] = jnp.broadcast_to(m_new, (1, LANES))

    c_out_ref[...] = c_sc[...]
    n_out_ref[...] = n_sc[...]
    m_out_ref[...] = m_sc[...]


def _mlstm(mq, mk, mv, mo, gcol, grow, c0, n0, m0, gn, *, lc):
    nb, seq, _ = mq.shape
    tok = pl.BlockSpec((None, lc, ML_WIDTH), lambda b, i: (b, i, 0))
    c_spec = pl.BlockSpec((None, ML_HEADS, ML_HEAD_DIM, ML_HEAD_DIM), lambda b, i: (b, 0, 0, 0))
    v_spec = pl.BlockSpec((None, ML_HEADS, ML_HEAD_DIM), lambda b, i: (b, 0, 0))
    return pl.pallas_call(
        functools.partial(_mlstm_kernel, lc=lc),
        grid=(nb, seq // lc),
        in_specs=[tok, tok, tok, tok,
                  pl.BlockSpec((None, lc, LANES), lambda b, i: (b, i, 0)),
                  pl.BlockSpec((None, SUBLANES, lc), lambda b, i: (b, 0, i)),
                  c_spec, v_spec, v_spec, _const_spec((1, ML_WIDTH))],
        out_specs=[tok, c_spec, v_spec, v_spec],
        out_shape=[jax.ShapeDtypeStruct((nb, seq, ML_WIDTH), BF16),
                   jax.ShapeDtypeStruct((nb, ML_HEADS, ML_HEAD_DIM, ML_HEAD_DIM), F32),
                   jax.ShapeDtypeStruct((nb, ML_HEADS, ML_HEAD_DIM), F32),
                   jax.ShapeDtypeStruct((nb, ML_HEADS, LANES), F32)],
        scratch_shapes=[pltpu.VMEM((ML_HEADS, ML_HEAD_DIM, ML_HEAD_DIM), F32),
                        pltpu.VMEM((ML_HEADS, ML_HEAD_DIM), F32),
                        pltpu.VMEM((ML_HEADS, LANES), F32)],
        compiler_params=_params(2),
        name="mlstm",
    )(mq, mk, mv, mo, gcol, grow, c0, n0, m0, gn)


def _post_kernel(x_ref, ya_ref, yb_ref, mod_ref, g1_ref, g2_ref, gf_ref,
                 wgab_ref, wa_ref, wb_ref, wout_ref, wfg_ref, wfu_ref, wfd_ref,
                 y_ref, u_sc, *, bt, lt):
    g1 = g1_ref[...]
    for j in range(bt):
        u = _norm_mod(x_ref[j], g1, mod_ref[j, 1:2, :], mod_ref[j, 0:1, :])
        u_sc[j * lt:(j + 1) * lt, :] = u.astype(BF16)
    u = u_sc[...]
    ya = ya_ref[...].reshape(bt * lt, SB_WIDTH)
    yb = yb_ref[...].reshape(bt * lt, ML_WIDTH)
    ga = _dot(u, wgab_ref[:, :D_MODEL])
    gb = _dot(u, wgab_ref[:, D_MODEL:])
    merged = _sigmoid(ga) * _dot(ya, wa_ref[...]) + _sigmoid(gb) * _dot(yb, wb_ref[...])
    attn_out = _dot(merged.astype(BF16), wout_ref[...])

    g2 = g2_ref[...]
    for j in range(bt):
        rows = slice(j * lt, (j + 1) * lt)
        x1 = x_ref[j] + mod_ref[j, 2:3, :] * attn_out[rows]
        y_ref[j] = x1
        u2 = _norm_mod(x1, g2, mod_ref[j, 4:5, :], mod_ref[j, 3:4, :])
        u_sc[rows, :] = u2.astype(BF16)
    u2 = u_sc[...]
    hg = _dot(u2, wfg_ref[...])
    hu = _dot(u2, wfu_ref[...])
    act = (hg * _sigmoid(hg) * hu).astype(BF16)
    ff = _dot(act, wfd_ref[...])
    gf = gf_ref[...]
    for j in range(bt):
        rows = slice(j * lt, (j + 1) * lt)
        x2 = y_ref[j] + mod_ref[j, 5:6, :] * ff[rows]
        ms = jnp.mean(x2 * x2, axis=-1, keepdims=True)
        y_ref[j] = x2 * lax.rsqrt(ms + EPS) * gf


def _post(x, ya, yb, mod3, g1, g2, gf, wgab, wa, wb, wout, wfg, wfu, wfd, *, bt, lt):
    nb, seq, _ = x.shape
    assert nb % bt == 0 and seq % lt == 0

    def tok(width):
        return pl.BlockSpec((bt, lt, width), lambda b, l: (b, l, 0))

    in_specs = [tok(D_MODEL), tok(SB_WIDTH), tok(ML_WIDTH),
                pl.BlockSpec((bt, 6, D_MODEL), lambda b, l: (b, 0, 0)),
                _const_spec((1, D_MODEL)), _const_spec((1, D_MODEL)), _const_spec((1, D_MODEL)),
                _const_spec(wgab.shape), _const_spec(wa.shape), _const_spec(wb.shape), _const_spec(wout.shape),
                _const_spec(wfg.shape), _const_spec(wfu.shape), _const_spec(wfd.shape)]
    return pl.pallas_call(
        functools.partial(_post_kernel, bt=bt, lt=lt),
        grid=(nb // bt, seq // lt),
        in_specs=in_specs,
        out_specs=tok(D_MODEL),
        out_shape=jax.ShapeDtypeStruct((nb, seq, D_MODEL), F32),
        scratch_shapes=[pltpu.VMEM((bt * lt, D_MODEL), BF16)],
        compiler_params=_params(2),
        name="post",
    )(x, ya, yb, mod3, g1, g2, gf, wgab, wa, wb, wout, wfg, wfu, wfd)


def _group(x, mod3, k_past, v_past, c0, n0, m0, conv0, wts, *, bt, lt, tq, n_past, lc):
    (g1, g2, gf, w1, wif_col, wif_row, bif_col, bif_row, wconv, bconv, gn,
     wgab, wa, wb, wout, wfg, wfu, wfd) = wts
    nb, seq, _ = x.shape
    (q, k32, v32, kb, vb, mq, mk, mv, mo, gcol, grow, conv_new) = _inproj(
        x, mod3, conv0, g1, w1, wif_col, wif_row, bif_col, bif_row, wconv, bconv, bt=bt, lt=lt)
    if k_past is None:
        k_past, v_past = kb, vb
    ya = _attn(q, kb, vb, k_past, v_past, tq=tq, n_past=n_past)
    m0b = jnp.broadcast_to(m0[:, :, None], (nb, ML_HEADS, LANES))
    yb, c1, n1, m1 = _mlstm(mq, mk, mv, mo, gcol, grow, c0, n0, m0b, gn, lc=lc)
    y = _post(x, ya, yb, mod3, g1, g2, gf, wgab, wa, wb, wout, wfg, wfu, wfd, bt=bt, lt=lt)
    states = (k32.reshape(1, nb, seq, SB_HEADS, SB_HEAD_DIM), v32.reshape(1, nb, seq, SB_HEADS, SB_HEAD_DIM),
              c1[None], n1[None], m1[None, :, :, 0], conv_new[None])
    return y, states


def kernel(x_prompt, x_sample, c_prompt, c_sample, cache_sb_k, cache_sb_v, state_mlstm_C, state_mlstm_n, state_mlstm_m, state_conv, norm1_g, norm2_g, w_ada, b_ada, w_in, b_if, w_conv, b_conv, ml_norm_g, w_a, w_b, w_out, w_ff_gate, w_ff_up, w_ff_down, final_g):
    assert w_in.shape[0] == 1, "single layer"
    bp, seq_p, _ = x_prompt.shape
    bs, seq_s, _ = x_sample.shape
    past = cache_sb_k.shape[2]

    c_all = jnp.concatenate([c_prompt, c_sample], axis=0)
    mod3 = _ada(c_all, w_ada[0], b_ada[0]).reshape(bp + bs, 6, D_MODEL)

    w = w_in[0]
    w1 = w[:, :_C_IF].astype(BF16)
    wif = w[:, _C_IF:_C_GAB]
    wif_col = jnp.pad(wif, ((0, 0), (0, LANES - 2 * ML_HEADS))).astype(BF16)
    wif_row = wif.T.astype(BF16)
    bif_col = jnp.pad(b_if[0], (0, LANES - 2 * ML_HEADS)).reshape(1, LANES)
    bif_row = b_if[0].reshape(2 * ML_HEADS, 1)
    wts = (norm1_g[0].reshape(1, D_MODEL), norm2_g[0].reshape(1, D_MODEL), final_g.reshape(1, D_MODEL),
           w1, wif_col, wif_row, bif_col, bif_row, w_conv[0], b_conv[0].reshape(1, 2 * ML_WIDTH),
           ml_norm_g[0].reshape(1, ML_WIDTH),
           w[:, _C_GAB:_C_END].astype(BF16), w_a[0].astype(BF16), w_b[0].astype(BF16), w_out[0].astype(BF16),
           w_ff_gate[0].astype(BF16), w_ff_up[0].astype(BF16), w_ff_down[0].astype(BF16))

    zeros = functools.partial(jnp.zeros, dtype=F32)
    y_p, st_p = _group(
        x_prompt, mod3[:bp], None, None,
        zeros((bp, ML_HEADS, ML_HEAD_DIM, ML_HEAD_DIM)), zeros((bp, ML_HEADS, ML_HEAD_DIM)),
        zeros((bp, ML_HEADS)), zeros((bp, CONV_W - 1, 2 * ML_WIDTH)), wts,
        bt=1, lt=256, tq=ATTN_BLOCK, n_past=None, lc=ML_CHUNK)
    y_s, st_s = _group(
        x_sample, mod3[bp:],
        cache_sb_k[0].reshape(bs, past, SB_WIDTH), cache_sb_v[0].reshape(bs, past, SB_WIDTH),
        state_mlstm_C[0], state_mlstm_n[0], state_mlstm_m[0], state_conv[0], wts,
        bt=4, lt=seq_s, tq=seq_s, n_past=past // ATTN_BLOCK, lc=seq_s)
    return (y_p, y_s) + st_p + st_s
```

```python
import functools
import math

import jax
import jax.numpy as jnp
from jax import lax
from jax.experimental import pallas as pl
from jax.experimental.pallas import tpu as pltpu

D_MODEL = 1024
SB_HEADS = 8
SB_HEAD_DIM = 64
SB_WIDTH = SB_HEADS * SB_HEAD_DIM
ML_HEADS = 4
ML_HEAD_DIM = 128
ML_WIDTH = ML_HEADS * ML_HEAD_DIM
CONV_W = 4
D_FF = 2816
EPS = 1e-6
LANES = 128
SUBLANES = 8
ATTN_BLOCK = 128
ML_CHUNK = 128
EXP_UNDERFLOW = 104.0
VMEM_LIMIT = 56 * 1024 * 1024

BF16 = jnp.bfloat16
F32 = jnp.float32

_C_Q, _C_K, _C_V = 0, SB_WIDTH, 2 * SB_WIDTH
_C_MQK = 3 * SB_WIDTH
_C_MV = _C_MQK + 2 * ML_WIDTH
_C_MO = _C_MV + ML_WIDTH
_C_IF = _C_MO + ML_WIDTH
_C_GAB = _C_IF + 2 * ML_HEADS
_C_END = _C_GAB + 2 * D_MODEL


def _sigmoid(x):
    return 1.0 / (1.0 + jnp.exp(-x))


def _log_sigmoid(x):
    return jnp.minimum(x, 0.0) - jnp.log1p(jnp.exp(-jnp.abs(x)))


def _softplus(x):
    return jnp.maximum(x, 0.0) + jnp.log(1.0 + jnp.exp(-jnp.abs(x)))


def _dot(a, b):
    return jnp.dot(a, b, preferred_element_type=F32)


def _dot_nt(a, b):
    return lax.dot_general(a, b, (((1,), (1,)), ((), ())), preferred_element_type=F32)


def _dot_tn(a, b):
    return lax.dot_general(a, b, (((0,), (0,)), ((), ())), preferred_element_type=F32)


def _split_dot(x, tri, parts):
    acc = None
    rem = x
    for i in range(parts):
        piece = rem.astype(BF16)
        term = _dot(piece, tri)
        acc = term if acc is None else acc + term
        if i + 1 < parts:
            rem = rem - piece.astype(F32)
    return acc


def _split_dot_left(tri, x, parts):
    acc = None
    rem = x
    for i in range(parts):
        piece = rem.astype(BF16)
        term = _dot(tri, piece)
        acc = term if acc is None else acc + term
        if i + 1 < parts:
            rem = rem - piece.astype(F32)
    return acc


def _norm_mod(x, g, sc, sh):
    ms = jnp.mean(x * x, axis=-1, keepdims=True)
    return x * lax.rsqrt(ms + EPS) * g * (1.0 + sc) + sh


def _const_spec(shape):
    nd = len(shape)
    return pl.BlockSpec(shape, lambda *_: (0,) * nd, pipeline_mode=pl.Buffered(1))


def _params(n_axes):
    return pltpu.CompilerParams(dimension_semantics=("arbitrary",) * n_axes,
                                vmem_limit_bytes=VMEM_LIMIT)


def _ada_kernel(c_ref, w_ref, b_ref, o_ref):
    c = c_ref[...]
    s = c * _sigmoid(c)
    o_ref[...] = _dot(s.astype(BF16), w_ref[...].astype(BF16)) + b_ref[...]


def _ada(c_all, w_ada, b_ada):
    nb = c_all.shape[0]
    n_out = w_ada.shape[1]
    tn = D_MODEL
    return pl.pallas_call(
        _ada_kernel,
        grid=(n_out // tn,),
        in_specs=[pl.BlockSpec((nb, D_MODEL), lambda j: (0, 0)),
                  pl.BlockSpec((D_MODEL, tn), lambda j: (0, j)),
                  pl.BlockSpec((1, tn), lambda j: (0, j))],
        out_specs=pl.BlockSpec((nb, tn), lambda j: (0, j)),
        out_shape=jax.ShapeDtypeStruct((nb, n_out), F32),
        compiler_params=_params(1),
        name="ada",
    )(c_all, w_ada, b_ada.reshape(1, n_out))


def _inproj_kernel(x_ref, mod_ref, conv0_ref, g1_ref, w1_ref, wifc_ref, wifr_ref, bifc_ref, bifr_ref,
                   wconv_ref, bconv_ref,
                   q_ref, k32_ref, v32_ref, kb_ref, vb_ref, mq_ref, mk_ref, mv_ref, mo_ref,
                   gcol_ref, grow_ref, convn_ref,
                   u_sc, xp_sc, *, bt, lt):
    step = pl.program_id(1)
    g1 = g1_ref[...]
    for j in range(bt):
        u = _norm_mod(x_ref[j], g1, mod_ref[j, 1:2, :], mod_ref[j, 0:1, :])
        u_sc[j * lt:(j + 1) * lt, :] = u.astype(BF16)
    u = u_sc[...]

    def mm(lo, hi):
        return _dot(u, w1_ref[:, lo:hi])

    sq = mm(_C_Q, _C_K)
    sk = mm(_C_K, _C_V)
    sv = mm(_C_V, _C_MQK)
    mqk = mm(_C_MQK, _C_MV)
    mv = mm(_C_MV, _C_MO)
    mo = mm(_C_MO, _C_IF)

    gc = _dot(u, wifc_ref[...]) + bifc_ref[...]
    lane = lax.broadcasted_iota(jnp.int32, gc.shape, 1)
    gc = jnp.where(lane >= ML_HEADS, _log_sigmoid(gc), gc)

    wconv = wconv_ref[...]
    bconv = bconv_ref[...]
    k_scale = 1.0 / math.sqrt(ML_HEAD_DIM)
    for j in range(bt):
        rows = slice(j * lt, (j + 1) * lt)
        q_ref[j] = sq[rows].astype(BF16)
        k32_ref[j] = sk[rows]
        v32_ref[j] = sv[rows]
        kb_ref[j] = sk[rows].astype(BF16)
        vb_ref[j] = sv[rows].astype(BF16)
        mv_ref[j] = mv[rows].astype(BF16)
        mo_ref[j] = mo[rows]
        gcol_ref[j] = gc[rows]

        gr = _dot_nt(wifr_ref[...], u_sc[rows, :]) + bifr_ref[...]
        row = lax.broadcasted_iota(jnp.int32, gr.shape, 0)
        grow_ref[j] = jnp.where(row >= ML_HEADS, _log_sigmoid(gr), gr)

        @pl.when(step == 0)
        def _():
            xp_sc[j, SUBLANES - (CONV_W - 1):SUBLANES, :] = conv0_ref[j]

        xp_sc[j, SUBLANES:SUBLANES + lt, :] = mqk[rows]
        conv = bconv
        for i in range(CONV_W):
            off = SUBLANES - (CONV_W - 1) + i
            conv = conv + xp_sc[j, off:off + lt, :] * wconv[i:i + 1, :]
        convn_ref[j] = xp_sc[j, lt + SUBLANES - (CONV_W - 1):lt + SUBLANES, :]
        xp_sc[j, 0:SUBLANES, :] = xp_sc[j, lt:lt + SUBLANES, :]
        conv = conv * _sigmoid(conv)
        mq_ref[j] = conv[:, :ML_WIDTH].astype(BF16)
        mk_ref[j] = (conv[:, ML_WIDTH:] * k_scale).astype(BF16)


def _inproj(x, mod3, conv0, g1, w1, wif_col, wif_row, bif_col, bif_row, wconv, bconv, *, bt, lt):
    nb, seq, _ = x.shape
    assert nb % bt == 0 and seq % lt == 0 and (bt == 1 or lt == seq)
    grid = (nb // bt, seq // lt)
    n1 = w1.shape[1]

    def tok(width):
        return pl.BlockSpec((bt, lt, width), lambda b, l: (b, l, 0))

    def per_batch(rows, width):
        return pl.BlockSpec((bt, rows, width), lambda b, l: (b, 0, 0))

    in_specs = [tok(D_MODEL), per_batch(6, D_MODEL), per_batch(CONV_W - 1, 2 * ML_WIDTH),
                _const_spec((1, D_MODEL)), _const_spec((D_MODEL, n1)),
                _const_spec((D_MODEL, LANES)), _const_spec((SUBLANES, D_MODEL)),
                _const_spec((1, LANES)), _const_spec((SUBLANES, 1)),
                _const_spec((CONV_W, 2 * ML_WIDTH)), _const_spec((1, 2 * ML_WIDTH))]
    out_specs = [tok(SB_WIDTH)] * 5 + [tok(ML_WIDTH)] * 4 + [
        tok(LANES),
        pl.BlockSpec((bt, SUBLANES, lt), lambda b, l: (b, 0, l)),
        per_batch(CONV_W - 1, 2 * ML_WIDTH)]

    def sds(shape, dt):
        return jax.ShapeDtypeStruct(shape, dt)

    out_shape = [sds((nb, seq, SB_WIDTH), BF16), sds((nb, seq, SB_WIDTH), F32), sds((nb, seq, SB_WIDTH), F32),
                 sds((nb, seq, SB_WIDTH), BF16), sds((nb, seq, SB_WIDTH), BF16),
                 sds((nb, seq, ML_WIDTH), BF16), sds((nb, seq, ML_WIDTH), BF16), sds((nb, seq, ML_WIDTH), BF16),
                 sds((nb, seq, ML_WIDTH), F32),
                 sds((nb, seq, LANES), F32), sds((nb, SUBLANES, seq), F32),
                 sds((nb, CONV_W - 1, 2 * ML_WIDTH), F32)]
    return pl.pallas_call(
        functools.partial(_inproj_kernel, bt=bt, lt=lt),
        grid=grid,
        in_specs=in_specs,
        out_specs=out_specs,
        out_shape=out_shape,
        scratch_shapes=[pltpu.VMEM((bt * lt, D_MODEL), BF16),
                        pltpu.VMEM((bt, lt + SUBLANES, 2 * ML_WIDTH), F32)],
        compiler_params=_params(2),
        name="inproj",
    )(x, mod3, conv0, g1, w1, wif_col, wif_row, bif_col, bif_row, wconv, bconv)


def _cum_matrix(t):
    r = lax.broadcasted_iota(jnp.int32, (2 * t, LANES + t), 0)
    c = lax.broadcasted_iota(jnp.int32, (2 * t, LANES + t), 1)
    return jnp.where((c < LANES) | ((r & (t - 1)) > (c - LANES)), 1.0, 0.0).astype(BF16)


def _attn_kernel(q_ref, kd_ref, vd_ref, kp_ref, vp_ref, o_ref, qm_sc, acc_sc, carry_sc, *, tq, tk, n_past):
    n_blocks = pl.program_id(1) if n_past is None else n_past
    scale = 1.0 / math.sqrt(SB_HEAD_DIM)
    n_pairs = SB_WIDTH // LANES
    lane = lax.broadcasted_iota(jnp.int32, (1, LANES), 1)
    first_head = lane < SB_HEAD_DIM

    for p in range(n_pairs):
        qp = q_ref[:, p * LANES:(p + 1) * LANES] * scale
        zero = jnp.zeros_like(qp)
        qm_sc[(2 * p) * tq:(2 * p + 1) * tq, :] = jnp.where(first_head, qp, zero)
        qm_sc[(2 * p + 1) * tq:(2 * p + 2) * tq, :] = jnp.where(first_head, zero, qp)

    def block(k_of, v_of, t, visible):
        z = jnp.concatenate(
            [_dot_nt(qm_sc[2 * p * tq:(2 * p + 2) * tq, :], k_of(p)) for p in range(n_pairs)], axis=0)
        sp = _softplus(z)
        if visible is not None:
            sp = jnp.where(visible, sp, 0.0)
        hi = sp.astype(BF16)
        lo = (sp - hi.astype(F32)).astype(BF16)
        sums = _dot(jnp.concatenate([hi, lo], axis=1), _cum_matrix(t))
        logit = z - sp - sums[:, LANES:]
        if visible is None:
            carry = carry_sc[...]
            a = jnp.exp(logit - carry)
            carry_sc[...] = carry + sums[:, :LANES]
        else:
            a = jnp.where(visible, jnp.exp(logit), 0.0)
            carry_sc[...] = sums[:, :LANES]
        a = a.astype(BF16)
        for p in range(n_pairs):
            cols = slice(p * LANES, (p + 1) * LANES)
            pv = _dot(a[2 * p * tq:(2 * p + 2) * tq, :], v_of(p))
            out = jnp.where(first_head, pv[:tq], pv[tq:])
            if visible is None:
                acc_sc[:, cols] += out
            else:
                acc_sc[:, cols] = out

    rd = lax.broadcasted_iota(jnp.int32, (SB_HEADS * tq, tq), 0) & (tq - 1)
    cd = lax.broadcasted_iota(jnp.int32, (SB_HEADS * tq, tq), 1)
    block(lambda p: kd_ref[:, p * LANES:(p + 1) * LANES],
          lambda p: vd_ref[:, p * LANES:(p + 1) * LANES],
          tq, cd < rd)

    def not_done():
        return jnp.min(carry_sc[...]) <= EXP_UNDERFLOW

    def cond(state):
        i, go = state
        return jnp.logical_and(i < n_blocks, go)

    def body(state):
        i, _ = state
        start = pl.multiple_of((n_blocks - 1 - i) * tk, tk)
        block(lambda p: kp_ref[pl.ds(start, tk), p * LANES:(p + 1) * LANES].astype(BF16),
              lambda p: vp_ref[pl.ds(start, tk), p * LANES:(p + 1) * LANES].astype(BF16),
              tk, None)
        return i + 1, not_done()

    lax.while_loop(cond, body, (jnp.int32(0), not_done()))
    o_ref[...] = acc_sc[...].astype(BF16)


def _attn(q, k_new, v_new, k_past, v_past, *, tq, n_past):
    nb, seq, _ = q.shape
    t_past = k_past.shape[1]
    tok = pl.BlockSpec((None, tq, SB_WIDTH), lambda b, i: (b, i, 0))
    past = pl.BlockSpec((None, t_past, SB_WIDTH), lambda b, i: (b, 0, 0))
    return pl.pallas_call(
        functools.partial(_attn_kernel, tq=tq, tk=ATTN_BLOCK, n_past=n_past),
        grid=(nb, seq // tq),
        in_specs=[tok, tok, tok, past, past],
        out_specs=tok,
        out_shape=jax.ShapeDtypeStruct((nb, seq, SB_WIDTH), BF16),
        scratch_shapes=[pltpu.VMEM((SB_HEADS * tq, LANES), BF16),
                        pltpu.VMEM((tq, SB_WIDTH), F32),
                        pltpu.VMEM((SB_HEADS * tq, LANES), F32)],
        compiler_params=_params(2),
        name="attn",
    )(q, k_new, v_new, k_past, v_past)


def _mlstm_kernel(mq_ref, mk_ref, mv_ref, mo_ref, gcol_ref, grow_ref, c0_ref, n0_ref, m0_ref, gn_ref,
                  yb_ref, c_out_ref, n_out_ref, m_out_ref,
                  c_sc, n_sc, m_sc, *, lc):
    step = pl.program_id(1)

    @pl.when(step == 0)
    def _():
        c_sc[...] = c0_ref[...]
        n_sc[...] = n0_ref[...]
        m_sc[...] = m0_ref[...]

    r = lax.broadcasted_iota(jnp.int32, (lc, lc), 0)
    c = lax.broadcasted_iota(jnp.int32, (lc, lc), 1)
    causal = c <= r
    incl = jnp.where(causal, 1.0, 0.0).astype(BF16)
    incl_t = jnp.where(r <= c, 1.0, 0.0).astype(BF16)

    gcol = gcol_ref[...]
    grow = grow_ref[...]
    bcol = _split_dot_left(incl, gcol, 3)
    brow = _split_dot(grow, incl_t, 3)

    for h in range(ML_HEADS):
        cols = slice(h * ML_HEAD_DIM, (h + 1) * ML_HEAD_DIM)
        q = mq_ref[:, cols]
        k = mk_ref[:, cols]
        v = mv_ref[:, cols]
        li_c = gcol[:, h:h + 1]
        b_c = bcol[:, ML_HEADS + h:ML_HEADS + h + 1]
        li_r = grow[h:h + 1, :]
        b_r = brow[ML_HEADS + h:ML_HEADS + h + 1, :]
        b_last = b_r[:, lc - 1:lc]
        m_prev = m_sc[h:h + 1, 0:1]
        c_prev = c_sc[h]
        n_prev = n_sc[h:h + 1, :]

        d_log = jnp.where(causal, b_c - b_r + li_r, -jnp.inf)
        inter = b_c + m_prev
        m_t = jnp.maximum(inter, jnp.max(d_log, axis=1, keepdims=True))
        w_intra = jnp.exp(d_log - m_t)
        w_inter = jnp.exp(inter - m_t)
        s = _dot_nt(q, k) * w_intra
        qf = q.astype(F32)
        num = _dot(s.astype(BF16), v) + w_inter * _dot_nt(q, c_prev.astype(BF16))
        den = jnp.sum(s, axis=1, keepdims=True) + w_inter * jnp.sum(qf * n_prev, axis=1, keepdims=True)
        hval = num / jnp.maximum(jnp.abs(den), jnp.exp(-m_t))
        hval = hval * lax.rsqrt(jnp.mean(hval * hval, axis=1, keepdims=True) + EPS)
        hval = hval * gn_ref[:, cols]
        yb_ref[:, cols] = (_sigmoid(mo_ref[:, cols]) * hval).astype(BF16)

        g = b_last - b_c + li_c
        m_new = jnp.maximum(b_last + m_prev, jnp.max(g, axis=0, keepdims=True))
        wg = jnp.exp(g - m_new)
        decay = jnp.exp(b_last + m_prev - m_new)
        vw = (wg * v.astype(F32)).astype(BF16)
        c_new = decay * c_prev + _dot_tn(vw, k)
        n_new = decay * n_prev + jnp.sum(wg * k.astype(F32), axis=0, keepdims=True)
        c_sc[h] = c_new
        n_sc[h:h + 1, :] = n_new
        m_sc[h:h + 1, :] = jnp.broadcast_to(m_new, (1, LANES))

    c_out_ref[...] = c_sc[...]
    n_out_ref[...] = n_sc[...]
    m_out_ref[...] = m_sc[...]


def _mlstm(mq, mk, mv, mo, gcol, grow, c0, n0, m0, gn, *, lc):
    nb, seq, _ = mq.shape
    tok = pl.BlockSpec((None, lc, ML_WIDTH), lambda b, i: (b, i, 0))
    c_spec = pl.BlockSpec((None, ML_HEADS, ML_HEAD_DIM, ML_HEAD_DIM), lambda b, i: (b, 0, 0, 0))
    v_spec = pl.BlockSpec((None, ML_HEADS, ML_HEAD_DIM), lambda b, i: (b, 0, 0))
    return pl.pallas_call(
        functools.partial(_mlstm_kernel, lc=lc),
        grid=(nb, seq // lc),
        in_specs=[tok, tok, tok, tok,
                  pl.BlockSpec((None, lc, LANES), lambda b, i: (b, i, 0)),
                  pl.BlockSpec((None, SUBLANES, lc), lambda b, i: (b, 0, i)),
                  c_spec, v_spec, v_spec, _const_spec((1, ML_WIDTH))],
        out_specs=[tok, c_spec, v_spec, v_spec],
        out_shape=[jax.ShapeDtypeStruct((nb, seq, ML_WIDTH), BF16),
                   jax.ShapeDtypeStruct((nb, ML_HEADS, ML_HEAD_DIM, ML_HEAD_DIM), F32),
                   jax.ShapeDtypeStruct((nb, ML_HEADS, ML_HEAD_DIM), F32),
                   jax.ShapeDtypeStruct((nb, ML_HEADS, LANES), F32)],
        scratch_shapes=[pltpu.VMEM((ML_HEADS, ML_HEAD_DIM, ML_HEAD_DIM), F32),
                        pltpu.VMEM((ML_HEADS, ML_HEAD_DIM), F32),
                        pltpu.VMEM((ML_HEADS, LANES), F32)],
        compiler_params=_params(2),
        name="mlstm",
    )(mq, mk, mv, mo, gcol, grow, c0, n0, m0, gn)


def _post_kernel(x_ref, ya_ref, yb_ref, mod_ref, g1_ref, g2_ref, gf_ref,
                 wgab_ref, wa_ref, wb_ref, wout_ref, wfg_ref, wfu_ref, wfd_ref,
                 y_ref, u_sc, *, bt, lt):
    g1 = g1_ref[...]
    for j in range(bt):
        u = _norm_mod(x_ref[j], g1, mod_ref[j, 1:2, :], mod_ref[j, 0:1, :])
        u_sc[j * lt:(j + 1) * lt, :] = u.astype(BF16)
    u = u_sc[...]
    ya = ya_ref[...].reshape(bt * lt, SB_WIDTH)
    yb = yb_ref[...].reshape(bt * lt, ML_WIDTH)
    ga = _dot(u, wgab_ref[:, :D_MODEL])
    gb = _dot(u, wgab_ref[:, D_MODEL:])
    merged = _sigmoid(ga) * _dot(ya, wa_ref[...]) + _sigmoid(gb) * _dot(yb, wb_ref[...])
    attn_out = _dot(merged.astype(BF16), wout_ref[...])

    g2 = g2_ref[...]
    for j in range(bt):
        rows = slice(j * lt, (j + 1) * lt)
        x1 = x_ref[j] + mod_ref[j, 2:3, :] * attn_out[rows]
        y_ref[j] = x1
        u2 = _norm_mod(x1, g2, mod_ref[j, 4:5, :], mod_ref[j, 3:4, :])
        u_sc[rows, :] = u2.astype(BF16)
    u2 = u_sc[...]
    hg = _dot(u2, wfg_ref[...])
    hu = _dot(u2, wfu_ref[...])
    act = (hg * _sigmoid(hg) * hu).astype(BF16)
    ff = _dot(act, wfd_ref[...])
    gf = gf_ref[...]
    for j in range(bt):
        rows = slice(j * lt, (j + 1) * lt)
        x2 = y_ref[j] + mod_ref[j, 5:6, :] * ff[rows]
        ms = jnp.mean(x2 * x2, axis=-1, keepdims=True)
        y_ref[j] = x2 * lax.rsqrt(ms + EPS) * gf


def _post(x, ya, yb, mod3, g1, g2, gf, wgab, wa, wb, wout, wfg, wfu, wfd, *, bt, lt):
    nb, seq, _ = x.shape
    assert nb % bt == 0 and seq % lt == 0

    def tok(width):
        return pl.BlockSpec((bt, lt, width), lambda b, l: (b, l, 0))

    in_specs = [tok(D_MODEL), tok(SB_WIDTH), tok(ML_WIDTH),
                pl.BlockSpec((bt, 6, D_MODEL), lambda b, l: (b, 0, 0)),
                _const_spec((1, D_MODEL)), _const_spec((1, D_MODEL)), _const_spec((1, D_MODEL)),
                _const_spec(wgab.shape), _const_spec(wa.shape), _const_spec(wb.shape), _const_spec(wout.shape),
                _const_spec(wfg.shape), _const_spec(wfu.shape), _const_spec(wfd.shape)]
    return pl.pallas_call(
        functools.partial(_post_kernel, bt=bt, lt=lt),
        grid=(nb // bt, seq // lt),
        in_specs=in_specs,
        out_specs=tok(D_MODEL),
        out_shape=jax.ShapeDtypeStruct((nb, seq, D_MODEL), F32),
        scratch_shapes=[pltpu.VMEM((bt * lt, D_MODEL), BF16)],
        compiler_params=_params(2),
        name="post",
    )(x, ya, yb, mod3, g1, g2, gf, wgab, wa, wb, wout, wfg, wfu, wfd)


def _group(x, mod3, k_past, v_past, c0, n0, m0, conv0, wts, *, bt, lt, tq, n_past, lc):
    (g1, g2, gf, w1, wif_col, wif_row, bif_col, bif_row, wconv, bconv, gn,
     wgab, wa, wb, wout, wfg, wfu, wfd) = wts
    nb, seq, _ = x.shape
    (q, k32, v32, kb, vb, mq, mk, mv, mo, gcol, grow, conv_new) = _inproj(
        x, mod3, conv0, g1, w1, wif_col, wif_row, bif_col, bif_row, wconv, bconv, bt=bt, lt=lt)
    if k_past is None:
        k_past, v_past = kb, vb
    ya = _attn(q, kb, vb, k_past, v_past, tq=tq, n_past=n_past)
    m0b = jnp.broadcast_to(m0[:, :, None], (nb, ML_HEADS, LANES))
    yb, c1, n1, m1 = _mlstm(mq, mk, mv, mo, gcol, grow, c0, n0, m0b, gn, lc=lc)
    y = _post(x, ya, yb, mod3, g1, g2, gf, wgab, wa, wb, wout, wfg, wfu, wfd, bt=bt, lt=lt)
    states = (k32.reshape(1, nb, seq, SB_HEADS, SB_HEAD_DIM), v32.reshape(1, nb, seq, SB_HEADS, SB_HEAD_DIM),
              c1[None], n1[None], m1[None, :, :, 0], conv_new[None])
    return y, states


def kernel(x_prompt, x_sample, c_prompt, c_sample, cache_sb_k, cache_sb_v, state_mlstm_C, state_mlstm_n, state_mlstm_m, state_conv, norm1_g, norm2_g, w_ada, b_ada, w_in, b_if, w_conv, b_conv, ml_norm_g, w_a, w_b, w_out, w_ff_gate, w_ff_up, w_ff_down, final_g):
    assert w_in.shape[0] == 1, "single layer"
    bp, seq_p, _ = x_prompt.shape
    bs, seq_s, _ = x_sample.shape
    past = cache_sb_k.shape[2]

    c_all = jnp.concatenate([c_prompt, c_sample], axis=0)
    mod3 = _ada(c_all, w_ada[0], b_ada[0]).reshape(bp + bs, 6, D_MODEL)

    w = w_in[0]
    w1 = w[:, :_C_IF].astype(BF16)
    wif = w[:, _C_IF:_C_GAB]
    wif_col = jnp.pad(wif, ((0, 0), (0, LANES - 2 * ML_HEADS))).astype(BF16)
    wif_row = wif.T.astype(BF16)
    bif_col = jnp.pad(b_if[0], (0, LANES - 2 * ML_HEADS)).reshape(1, LANES)
    bif_row = b_if[0].reshape(2 * ML_HEADS, 1)
    wts = (norm1_g[0].reshape(1, D_MODEL), norm2_g[0].reshape(1, D_MODEL), final_g.reshape(1, D_MODEL),
           w1, wif_col, wif_row, bif_col, bif_row, w_conv[0], b_conv[0].reshape(1, 2 * ML_WIDTH),
           ml_norm_g[0].reshape(1, ML_WIDTH),
           w[:, _C_GAB:_C_END].astype(BF16), w_a[0].astype(BF16), w_b[0].astype(BF16), w_out[0].astype(BF16),
           w_ff_gate[0].astype(BF16), w_ff_up[0].astype(BF16), w_ff_down[0].astype(BF16))

    zeros = functools.partial(jnp.zeros, dtype=F32)
    y_p, st_p = _group(
        x_prompt, mod3[:bp], None, None,
        zeros((bp, ML_HEADS, ML_HEAD_DIM, ML_HEAD_DIM)), zeros((bp, ML_HEADS, ML_HEAD_DIM)),
        zeros((bp, ML_HEADS)), zeros((bp, CONV_W - 1, 2 * ML_WIDTH)), wts,
        bt=1, lt=256, tq=ATTN_BLOCK, n_past=None, lc=ML_CHUNK)
    y_s, st_s = _group(
        x_sample, mod3[bp:],
        cache_sb_k[0].reshape(bs, past, SB_WIDTH), cache_sb_v[0].reshape(bs, past, SB_WIDTH),
        state_mlstm_C[0], state_mlstm_n[0], state_mlstm_m[0], state_conv[0], wts,
        bt=4, lt=seq_s, tq=seq_s, n_past=past // ATTN_BLOCK, lc=seq_s)
    return (y_p, y_s) + st_p + st_s
```

```python
import functools
import math

import jax
import jax.numpy as jnp
from jax import lax
from jax.experimental import pallas as pl
from jax.experimental.pallas import tpu as pltpu

D_MODEL = 1024
SB_HEADS = 8
SB_HEAD_DIM = 64
SB_WIDTH = SB_HEADS * SB_HEAD_DIM
ML_HEADS = 4
ML_HEAD_DIM = 128
ML_WIDTH = ML_HEADS * ML_HEAD_DIM
CONV_W = 4
D_FF = 2816
EPS = 1e-6
LANES = 128
SUBLANES = 8
ATTN_BLOCK = 128
ML_CHUNK = 128
EXP_UNDERFLOW = 104.0
VMEM_LIMIT = 56 * 1024 * 1024

BF16 = jnp.bfloat16
F32 = jnp.float32

_C_Q, _C_K, _C_V = 0, SB_WIDTH, 2 * SB_WIDTH
_C_MQK = 3 * SB_WIDTH
_C_MV = _C_MQK + 2 * ML_WIDTH
_C_MO = _C_MV + ML_WIDTH
_C_IF = _C_MO + ML_WIDTH
_C_GAB = _C_IF + 2 * ML_HEADS
_C_END = _C_GAB + 2 * D_MODEL


def _sigmoid(x):
    return 1.0 / (1.0 + jnp.exp(-x))


def _log_sigmoid(x):
    return jnp.minimum(x, 0.0) - jnp.log1p(jnp.exp(-jnp.abs(x)))


def _softplus(x):
    return jnp.maximum(x, 0.0) + jnp.log(1.0 + jnp.exp(-jnp.abs(x)))


def _dot(a, b):
    return jnp.dot(a, b, preferred_element_type=F32)


def _dot_nt(a, b):
    return lax.dot_general(a, b, (((1,), (1,)), ((), ())), preferred_element_type=F32)


def _dot_tn(a, b):
    return lax.dot_general(a, b, (((0,), (0,)), ((), ())), preferred_element_type=F32)


def _split_dot(x, tri, parts):
    acc = None
    rem = x
    for i in range(parts):
        piece = rem.astype(BF16)
        term = _dot(piece, tri)
        acc = term if acc is None else acc + term
        if i + 1 < parts:
            rem = rem - piece.astype(F32)
    return acc


def _split_dot_left(tri, x, parts):
    acc = None
    rem = x
    for i in range(parts):
        piece = rem.astype(BF16)
        term = _dot(tri, piece)
        acc = term if acc is None else acc + term
        if i + 1 < parts:
            rem = rem - piece.astype(F32)
    return acc


def _norm_mod(x, g, sc, sh):
    ms = jnp.mean(x * x, axis=-1, keepdims=True)
    return x * lax.rsqrt(ms + EPS) * g * (1.0 + sc) + sh


def _const_spec(shape):
    nd = len(shape)
    return pl.BlockSpec(shape, lambda *_: (0,) * nd, pipeline_mode=pl.Buffered(1))


def _params(n_axes):
    return pltpu.CompilerParams(dimension_semantics=("arbitrary",) * n_axes,
                                vmem_limit_bytes=VMEM_LIMIT)


def _ada_kernel(c_ref, w_ref, b_ref, o_ref):
    c = c_ref[...]
    s = c * _sigmoid(c)
    o_ref[...] = _dot(s.astype(BF16), w_ref[...].astype(BF16)) + b_ref[...]


def _ada(c_all, w_ada, b_ada):
    nb = c_all.shape[0]
    n_out = w_ada.shape[1]
    tn = D_MODEL
    return pl.pallas_call(
        _ada_kernel,
        grid=(n_out // tn,),
        in_specs=[pl.BlockSpec((nb, D_MODEL), lambda j: (0, 0)),
                  pl.BlockSpec((D_MODEL, tn), lambda j: (0, j)),
                  pl.BlockSpec((1, tn), lambda j: (0, j))],
        out_specs=pl.BlockSpec((nb, tn), lambda j: (0, j)),
        out_shape=jax.ShapeDtypeStruct((nb, n_out), F32),
        compiler_params=_params(1),
        name="ada",
    )(c_all, w_ada, b_ada.reshape(1, n_out))


def _inproj_kernel(x_ref, mod_ref, conv0_ref, g1_ref, w1_ref, wmvt_ref, wifr_ref, bifr_ref,
                   wconv_ref, bconv_ref,
                   q_ref, k32_ref, v32_ref, kb_ref, vb_ref, mq_ref, mk_ref, mv_ref, mvt_ref, mo_ref,
                   grow_ref, convn_ref,
                   u_sc, xp_sc, *, bt, lt):
    step = pl.program_id(1)
    g1 = g1_ref[...]
    for j in range(bt):
        u = _norm_mod(x_ref[j], g1, mod_ref[j, 1:2, :], mod_ref[j, 0:1, :])
        u_sc[j * lt:(j + 1) * lt, :] = u.astype(BF16)
    u = u_sc[...]

    def mm(lo, hi):
        return _dot(u, w1_ref[:, lo:hi])

    sq = mm(_C_Q, _C_K)
    sk = mm(_C_K, _C_V)
    sv = mm(_C_V, _C_MQK)
    mqk = mm(_C_MQK, _C_MV)
    mv = mm(_C_MV, _C_MO)
    mo = mm(_C_MO, _C_IF)

    wconv = wconv_ref[...]
    bconv = bconv_ref[...]
    k_scale = 1.0 / math.sqrt(ML_HEAD_DIM)
    for j in range(bt):
        rows = slice(j * lt, (j + 1) * lt)
        q_ref[j] = sq[rows].astype(BF16)
        k32_ref[j] = sk[rows]
        v32_ref[j] = sv[rows]
        kb_ref[j] = sk[rows].astype(BF16)
        vb_ref[j] = sv[rows].astype(BF16)
        mv_ref[j] = mv[rows].astype(BF16)
        mo_ref[j] = mo[rows]
        mvt_ref[j] = _dot_nt(wmvt_ref[...], u_sc[rows, :]).astype(BF16)
        gr = _dot_nt(wifr_ref[...], u_sc[rows, :]) + bifr_ref[...]
        row = lax.broadcasted_iota(jnp.int32, gr.shape, 0)
        grow_ref[j] = jnp.where(row >= SUBLANES, _log_sigmoid(gr), gr)

        @pl.when(step == 0)
        def _():
            xp_sc[j, SUBLANES - (CONV_W - 1):SUBLANES, :] = conv0_ref[j]

        xp_sc[j, SUBLANES:SUBLANES + lt, :] = mqk[rows]
        conv = bconv
        for i in range(CONV_W):
            off = SUBLANES - (CONV_W - 1) + i
            conv = conv + xp_sc[j, off:off + lt, :] * wconv[i:i + 1, :]
        convn_ref[j] = xp_sc[j, lt + SUBLANES - (CONV_W - 1):lt + SUBLANES, :]
        xp_sc[j, 0:SUBLANES, :] = xp_sc[j, lt:lt + SUBLANES, :]
        conv = conv * _sigmoid(conv)
        mq_ref[j] = conv[:, :ML_WIDTH].astype(BF16)
        mk_ref[j] = (conv[:, ML_WIDTH:] * k_scale).astype(BF16)


def _inproj(x, mod3, conv0, g1, w1, wmvt, wif_row, bif_row, wconv, bconv, *, bt, lt):
    nb, seq, _ = x.shape
    assert nb % bt == 0 and seq % lt == 0 and (bt == 1 or lt == seq)
    grid = (nb // bt, seq // lt)
    n1 = w1.shape[1]

    def tok(width):
        return pl.BlockSpec((bt, lt, width), lambda b, l: (b, l, 0))

    def per_batch(rows, width):
        return pl.BlockSpec((bt, rows, width), lambda b, l: (b, 0, 0))

    in_specs = [tok(D_MODEL), per_batch(6, D_MODEL), per_batch(CONV_W - 1, 2 * ML_WIDTH),
                _const_spec((1, D_MODEL)), _const_spec((D_MODEL, n1)),
                _const_spec((ML_WIDTH, D_MODEL)), _const_spec((2 * SUBLANES, D_MODEL)),
                _const_spec((2 * SUBLANES, 1)),
                _const_spec((CONV_W, 2 * ML_WIDTH)), _const_spec((1, 2 * ML_WIDTH))]

    def tok_on_lanes(rows):
        return pl.BlockSpec((bt, rows, lt), lambda b, l: (b, 0, l))

    out_specs = [tok(SB_WIDTH)] * 5 + [tok(ML_WIDTH)] * 3 + [
        tok_on_lanes(ML_WIDTH), tok(ML_WIDTH), tok_on_lanes(2 * SUBLANES),
        per_batch(CONV_W - 1, 2 * ML_WIDTH)]

    def sds(shape, dt):
        return jax.ShapeDtypeStruct(shape, dt)

    out_shape = [sds((nb, seq, SB_WIDTH), BF16), sds((nb, seq, SB_WIDTH), F32), sds((nb, seq, SB_WIDTH), F32),
                 sds((nb, seq, SB_WIDTH), BF16), sds((nb, seq, SB_WIDTH), BF16),
                 sds((nb, seq, ML_WIDTH), BF16), sds((nb, seq, ML_WIDTH), BF16), sds((nb, seq, ML_WIDTH), BF16),
                 sds((nb, ML_WIDTH, seq), BF16), sds((nb, seq, ML_WIDTH), F32),
                 sds((nb, 2 * SUBLANES, seq), F32),
                 sds((nb, CONV_W - 1, 2 * ML_WIDTH), F32)]
    return pl.pallas_call(
        functools.partial(_inproj_kernel, bt=bt, lt=lt),
        grid=grid,
        in_specs=in_specs,
        out_specs=out_specs,
        out_shape=out_shape,
        scratch_shapes=[pltpu.VMEM((bt * lt, D_MODEL), BF16),
                        pltpu.VMEM((bt, lt + SUBLANES, 2 * ML_WIDTH), F32)],
        compiler_params=_params(2),
        name="inproj",
    )(x, mod3, conv0, g1, w1, wmvt, wif_row, bif_row, wconv, bconv)


def _cum_matrix(t):
    r = lax.broadcasted_iota(jnp.int32, (2 * t, LANES + t), 0)
    c = lax.broadcasted_iota(jnp.int32, (2 * t, LANES + t), 1)
    return jnp.where((c < LANES) | ((r & (t - 1)) > (c - LANES)), 1.0, 0.0).astype(BF16)


def _attn_kernel(q_ref, kd_ref, vd_ref, kp_ref, vp_ref, o_ref, qm_sc, acc_sc, carry_sc, *, tq, tk, n_past):
    n_blocks = pl.program_id(1) if n_past is None else n_past
    scale = 1.0 / math.sqrt(SB_HEAD_DIM)
    n_pairs = SB_WIDTH // LANES
    lane = lax.broadcasted_iota(jnp.int32, (1, LANES), 1)
    first_head = lane < SB_HEAD_DIM

    for p in range(n_pairs):
        qp = q_ref[:, p * LANES:(p + 1) * LANES] * scale
        zero = jnp.zeros_like(qp)
        qm_sc[(2 * p) * tq:(2 * p + 1) * tq, :] = jnp.where(first_head, qp, zero)
        qm_sc[(2 * p + 1) * tq:(2 * p + 2) * tq, :] = jnp.where(first_head, zero, qp)

    def block(k_of, v_of, t, visible):
        z = jnp.concatenate(
            [_dot_nt(qm_sc[2 * p * tq:(2 * p + 2) * tq, :], k_of(p)) for p in range(n_pairs)], axis=0)
        sp = _softplus(z)
        if visible is not None:
            sp = jnp.where(visible, sp, 0.0)
        hi = sp.astype(BF16)
        lo = (sp - hi.astype(F32)).astype(BF16)
        sums = _dot(jnp.concatenate([hi, lo], axis=1), _cum_matrix(t))
        logit = z - sp - sums[:, LANES:]
        if visible is None:
            carry = carry_sc[...]
            a = jnp.exp(logit - carry)
            carry_sc[...] = carry + sums[:, :LANES]
        else:
            a = jnp.where(visible, jnp.exp(logit), 0.0)
            carry_sc[...] = sums[:, :LANES]
        a = a.astype(BF16)
        for p in range(n_pairs):
            cols = slice(p * LANES, (p + 1) * LANES)
            pv = _dot(a[2 * p * tq:(2 * p + 2) * tq, :], v_of(p))
            out = jnp.where(first_head, pv[:tq], pv[tq:])
            if visible is None:
                acc_sc[:, cols] += out
            else:
                acc_sc[:, cols] = out

    rd = lax.broadcasted_iota(jnp.int32, (SB_HEADS * tq, tq), 0) & (tq - 1)
    cd = lax.broadcasted_iota(jnp.int32, (SB_HEADS * tq, tq), 1)
    block(lambda p: kd_ref[:, p * LANES:(p + 1) * LANES],
          lambda p: vd_ref[:, p * LANES:(p + 1) * LANES],
          tq, cd < rd)

    def not_done():
        return jnp.min(carry_sc[...]) <= EXP_UNDERFLOW

    def cond(state):
        i, go = state
        return jnp.logical_and(i < n_blocks, go)

    def body(state):
        i, _ = state
        start = pl.multiple_of((n_blocks - 1 - i) * tk, tk)
        block(lambda p: kp_ref[pl.ds(start, tk), p * LANES:(p + 1) * LANES].astype(BF16),
              lambda p: vp_ref[pl.ds(start, tk), p * LANES:(p + 1) * LANES].astype(BF16),
              tk, None)
        return i + 1, not_done()

    lax.while_loop(cond, body, (jnp.int32(0), not_done()))
    o_ref[...] = acc_sc[...].astype(BF16)


def _attn(q, k_new, v_new, k_past, v_past, *, tq, n_past):
    nb, seq, _ = q.shape
    t_past = k_past.shape[1]
    tok = pl.BlockSpec((None, tq, SB_WIDTH), lambda b, i: (b, i, 0))
    past = pl.BlockSpec((None, t_past, SB_WIDTH), lambda b, i: (b, 0, 0))
    return pl.pallas_call(
        functools.partial(_attn_kernel, tq=tq, tk=ATTN_BLOCK, n_past=n_past),
        grid=(nb, seq // tq),
        in_specs=[tok, tok, tok, past, past],
        out_specs=tok,
        out_shape=jax.ShapeDtypeStruct((nb, seq, SB_WIDTH), BF16),
        scratch_shapes=[pltpu.VMEM((SB_HEADS * tq, LANES), BF16),
                        pltpu.VMEM((tq, SB_WIDTH), F32),
                        pltpu.VMEM((SB_HEADS * tq, LANES), F32)],
        compiler_params=_params(2),
        name="attn",
    )(q, k_new, v_new, k_past, v_past)


def _mlstm_kernel(mq_ref, mk_ref, mv_ref, mvt_ref, mo_ref, grow_ref, c0_ref, n0_ref, m0_ref, gn_ref,
                  yb_ref, c_out_ref, n_out_ref, m_out_ref,
                  c_sc, n_sc, m_sc, alpha_sc, beta_sc, r_sc, wg_sc, dec_sc, mprev_sc, *, lc, nc):
    step = pl.program_id(1)

    ri = lax.broadcasted_iota(jnp.int32, (lc, lc), 0)
    ci = lax.broadcasted_iota(jnp.int32, (lc, lc), 1)
    causal = ci <= ri
    eye = jnp.where(ri == ci, 1.0, 0.0).astype(BF16)
    ones_blk = jnp.ones((lc, ML_HEAD_DIM), BF16)
    zeros_blk = jnp.zeros((lc, ML_HEAD_DIM), BF16)

    @pl.when(step == 0)
    def _():
        c_sc[...] = c0_ref[...]
        n_sc[...] = n0_ref[...]
        incl_t = jnp.where(ri <= ci, 1.0, 0.0).astype(BF16)
        li = jnp.concatenate([grow_ref[0:SUBLANES, c * lc:(c + 1) * lc] for c in range(nc)], axis=0)
        lf = jnp.concatenate([grow_ref[SUBLANES:2 * SUBLANES, c * lc:(c + 1) * lc] for c in range(nc)], axis=0)
        b = _split_dot(lf, incl_t, 3)
        r = li - b
        lane = lax.broadcasted_iota(jnp.int32, r.shape, 1)
        run = r
        d = 1
        while d < lc:
            run = jnp.maximum(run, jnp.where(lane >= d, pltpu.roll(run, d, axis=1), -jnp.inf))
            d *= 2
        b_last = b[:, lc - 1:lc]
        g = b_last - b + li
        g_max = jnp.max(g, axis=1, keepdims=True)
        m = m0_ref[...]
        m_before = []
        m_after = []
        for c in range(nc):
            rows = slice(c * SUBLANES, (c + 1) * SUBLANES)
            m_before.append(m)
            m = jnp.maximum(b_last[rows] + m, g_max[rows])
            m_after.append(m)
        m_sc[...] = m
        m_prev = jnp.concatenate(m_before, axis=0)
        m_new = jnp.concatenate(m_after, axis=0)
        alpha = (-jnp.maximum(run, m_prev[:, :lc])).astype(BF16).astype(F32)
        alpha_sc[...] = alpha
        beta_sc[...] = alpha - b
        r_sc[...] = r
        wg_sc[...] = jnp.exp(g - m_new[:, :lc])
        dec_sc[...] = jnp.exp(b_last + m_prev - m_new)
        mprev_sc[...] = m_prev

    base = pl.multiple_of(step * SUBLANES, SUBLANES)
    chunk = pl.ds(base, SUBLANES)
    alpha = alpha_sc[chunk, :].astype(BF16)
    beta = beta_sc[chunk, :]
    beta_hi = beta.astype(BF16)
    beta_lo = (beta - beta_hi.astype(F32)).astype(BF16)
    r = r_sc[chunk, :]
    wg = wg_sc[chunk, :]
    decay = dec_sc[chunk, :]
    m_prev = mprev_sc[chunk, :]
    eye2 = jnp.concatenate([eye, eye], axis=1)

    heads = range(ML_HEADS)
    cols = [slice(h * ML_HEAD_DIM, (h + 1) * ML_HEAD_DIM) for h in heads]
    dk = ML_HEAD_DIM

    def rows_of(x, h):
        return jnp.broadcast_to(x[h:h + 1, :], (dk, lc))
    alpha_b = _dot_nt(eye, jnp.concatenate([rows_of(alpha, h) for h in heads], axis=0))
    beta_b = _dot_nt(eye2, jnp.concatenate(
        [jnp.concatenate([rows_of(beta_hi, h), rows_of(beta_lo, h)], axis=1) for h in heads], axis=0))
    qk = [_dot_nt(mq_ref[:, cols[h]], mk_ref[:, cols[h]]) for h in heads]
    c_prev = [c_sc[h] for h in heads]
    n_prev = [n_sc[h:h + 1, :] for h in heads]
    qc = [_dot_nt(mq_ref[:, cols[h]],
                  jnp.concatenate([c_prev[h].astype(BF16),
                                   jnp.broadcast_to(n_prev[h], (dk, dk)).astype(BF16)], axis=0))
          for h in heads]
    upd = []
    for h in heads:
        wg_h = wg[h:h + 1, :]
        lhs = jnp.concatenate([(mvt_ref[cols[h], :].astype(F32) * wg_h).astype(BF16),
                               jnp.broadcast_to(wg_h, (SUBLANES, lc)).astype(BF16)], axis=0)
        upd.append(_dot(lhs, mk_ref[:, cols[h]]))

    cat = []
    for h in heads:
        w = jnp.where(causal, jnp.exp(alpha_b[:, h * dk:h * dk + lc] + r[h:h + 1, :]), 0.0)
        s = qk[h] * w
        s_hi = s.astype(BF16)
        cat.append(jnp.concatenate([s_hi, (s - s_hi.astype(F32)).astype(BF16)], axis=1))
    nv = []
    for h in heads:
        rhs = jnp.concatenate([jnp.concatenate([mv_ref[:, cols[h]], ones_blk], axis=1),
                               jnp.concatenate([zeros_blk, ones_blk], axis=1)], axis=0)
        nv.append(_dot(cat[h], rhs))
    hval = []
    cat2 = []
    for h in heads:
        w_inter = jnp.exp(m_prev[h:h + 1, :] + alpha_b[:, cols[h]])
        num = nv[h][:, :dk] + w_inter * qc[h][:, :dk]
        den = nv[h][:, dk:] + w_inter * qc[h][:, dk:]
        hv = num / jnp.maximum(jnp.abs(den), jnp.exp(beta_b[:, cols[h]]))
        h2 = hv * hv
        h2_hi = h2.astype(BF16)
        hval.append(hv)
        cat2.append(jnp.concatenate([h2_hi, (h2 - h2_hi.astype(F32)).astype(BF16)], axis=1))
    ms = _dot(jnp.concatenate(cat2, axis=0), jnp.ones((2 * dk, dk), BF16)) * (1.0 / dk)
    for h in heads:
        y = hval[h] * lax.rsqrt(ms[h * lc:(h + 1) * lc] + EPS) * gn_ref[:, cols[h]]
        yb_ref[:, cols[h]] = (_sigmoid(mo_ref[:, cols[h]]) * y).astype(BF16)

    for h in heads:
        dec = decay[h:h + 1, :]
        c_sc[h] = dec * c_prev[h] + upd[h][:dk]
        n_sc[h:h + 1, :] = dec * n_prev[h] + upd[h][dk:dk + 1]

    c_out_ref[...] = c_sc[...]
    n_out_ref[...] = n_sc[...]
    m_out_ref[...] = m_sc[...]


def _mlstm(mq, mk, mv, mvt, mo, grow, c0, n0, m0, gn, *, lc):
    nb, seq, _ = mq.shape
    tok = pl.BlockSpec((None, lc, ML_WIDTH), lambda b, i: (b, i, 0))
    c_spec = pl.BlockSpec((None, ML_HEADS, ML_HEAD_DIM, ML_HEAD_DIM), lambda b, i: (b, 0, 0, 0))
    n_spec = pl.BlockSpec((None, ML_HEADS, ML_HEAD_DIM), lambda b, i: (b, 0, 0))
    m_spec = pl.BlockSpec((None, SUBLANES, LANES), lambda b, i: (b, 0, 0))
    nc = seq // lc
    table = pltpu.VMEM((nc * SUBLANES, lc), F32)
    table_m = pltpu.VMEM((nc * SUBLANES, LANES), F32)
    return pl.pallas_call(
        functools.partial(_mlstm_kernel, lc=lc, nc=nc),
        grid=(nb, nc),
        in_specs=[tok, tok, tok,
                  pl.BlockSpec((None, ML_WIDTH, lc), lambda b, i: (b, 0, i)),
                  tok,
                  pl.BlockSpec((None, 2 * SUBLANES, seq), lambda b, i: (b, 0, 0)),
                  c_spec, n_spec, m_spec, _const_spec((1, ML_WIDTH))],
        out_specs=[tok, c_spec, n_spec, m_spec],
        out_shape=[jax.ShapeDtypeStruct((nb, seq, ML_WIDTH), BF16),
                   jax.ShapeDtypeStruct((nb, ML_HEADS, ML_HEAD_DIM, ML_HEAD_DIM), F32),
                   jax.ShapeDtypeStruct((nb, ML_HEADS, ML_HEAD_DIM), F32),
                   jax.ShapeDtypeStruct((nb, SUBLANES, LANES), F32)],
        scratch_shapes=[pltpu.VMEM((ML_HEADS, ML_HEAD_DIM, ML_HEAD_DIM), F32),
                        pltpu.VMEM((ML_HEADS, ML_HEAD_DIM), F32),
                        pltpu.VMEM((SUBLANES, LANES), F32),
                        table, table, table, table, table_m, table_m],
        compiler_params=_params(2),
        name="mlstm",
    )(mq, mk, mv, mvt, mo, grow, c0, n0, m0, gn)


def _post_kernel(x_ref, ya_ref, yb_ref, mod_ref, g1_ref, g2_ref, gf_ref,
                 wgab_ref, wa_ref, wb_ref, wout_ref, wfg_ref, wfu_ref, wfd_ref,
                 y_ref, u_sc, *, bt, lt):
    g1 = g1_ref[...]
    for j in range(bt):
        u = _norm_mod(x_ref[j], g1, mod_ref[j, 1:2, :], mod_ref[j, 0:1, :])
        u_sc[j * lt:(j + 1) * lt, :] = u.astype(BF16)
    u = u_sc[...]
    ya = ya_ref[...].reshape(bt * lt, SB_WIDTH)
    yb = yb_ref[...].reshape(bt * lt, ML_WIDTH)
    ga = _dot(u, wgab_ref[:, :D_MODEL])
    gb = _dot(u, wgab_ref[:, D_MODEL:])
    merged = _sigmoid(ga) * _dot(ya, wa_ref[...]) + _sigmoid(gb) * _dot(yb, wb_ref[...])
    attn_out = _dot(merged.astype(BF16), wout_ref[...])

    g2 = g2_ref[...]
    for j in range(bt):
        rows = slice(j * lt, (j + 1) * lt)
        x1 = x_ref[j] + mod_ref[j, 2:3, :] * attn_out[rows]
        y_ref[j] = x1
        u2 = _norm_mod(x1, g2, mod_ref[j, 4:5, :], mod_ref[j, 3:4, :])
        u_sc[rows, :] = u2.astype(BF16)
    u2 = u_sc[...]
    hg = _dot(u2, wfg_ref[...])
    hu = _dot(u2, wfu_ref[...])
    act = (hg * _sigmoid(hg) * hu).astype(BF16)
    ff = _dot(act, wfd_ref[...])
    gf = gf_ref[...]
    for j in range(bt):
        rows = slice(j * lt, (j + 1) * lt)
        x2 = y_ref[j] + mod_ref[j, 5:6, :] * ff[rows]
        ms = jnp.mean(x2 * x2, axis=-1, keepdims=True)
        y_ref[j] = x2 * lax.rsqrt(ms + EPS) * gf


def _post(x, ya, yb, mod3, g1, g2, gf, wgab, wa, wb, wout, wfg, wfu, wfd, *, bt, lt):
    nb, seq, _ = x.shape
    assert nb % bt == 0 and seq % lt == 0

    def tok(width):
        return pl.BlockSpec((bt, lt, width), lambda b, l: (b, l, 0))

    in_specs = [tok(D_MODEL), tok(SB_WIDTH), tok(ML_WIDTH),
                pl.BlockSpec((bt, 6, D_MODEL), lambda b, l: (b, 0, 0)),
                _const_spec((1, D_MODEL)), _const_spec((1, D_MODEL)), _const_spec((1, D_MODEL)),
                _const_spec(wgab.shape), _const_spec(wa.shape), _const_spec(wb.shape), _const_spec(wout.shape),
                _const_spec(wfg.shape), _const_spec(wfu.shape), _const_spec(wfd.shape)]
    return pl.pallas_call(
        functools.partial(_post_kernel, bt=bt, lt=lt),
        grid=(nb // bt, seq // lt),
        in_specs=in_specs,
        out_specs=tok(D_MODEL),
        out_shape=jax.ShapeDtypeStruct((nb, seq, D_MODEL), F32),
        scratch_shapes=[pltpu.VMEM((bt * lt, D_MODEL), BF16)],
        compiler_params=_params(2),
        name="post",
    )(x, ya, yb, mod3, g1, g2, gf, wgab, wa, wb, wout, wfg, wfu, wfd)


def _group(x, mod3, k_past, v_past, c0, n0, m0, conv0, wts, *, bt, lt, tq, n_past, lc):
    (g1, g2, gf, w1, wmvt, wif_row, bif_row, wconv, bconv, gn,
     wgab, wa, wb, wout, wfg, wfu, wfd) = wts
    nb, seq, _ = x.shape
    (q, k32, v32, kb, vb, mq, mk, mv, mvt, mo, grow, conv_new) = _inproj(
        x, mod3, conv0, g1, w1, wmvt, wif_row, bif_row, wconv, bconv, bt=bt, lt=lt)
    if k_past is None:
        k_past, v_past = kb, vb
    ya = _attn(q, kb, vb, k_past, v_past, tq=tq, n_past=n_past)
    m0b = jnp.broadcast_to(jnp.pad(m0, ((0, 0), (0, SUBLANES - ML_HEADS)))[:, :, None], (nb, SUBLANES, LANES))
    yb, c1, n1, m1 = _mlstm(mq, mk, mv, mvt, mo, grow, c0, n0, m0b, gn, lc=lc)
    y = _post(x, ya, yb, mod3, g1, g2, gf, wgab, wa, wb, wout, wfg, wfu, wfd, bt=bt, lt=lt)
    states = (k32.reshape(1, nb, seq, SB_HEADS, SB_HEAD_DIM), v32.reshape(1, nb, seq, SB_HEADS, SB_HEAD_DIM),
              c1[None], n1[None], m1[None, :, :ML_HEADS, 0], conv_new[None])
    return y, states


def kernel(x_prompt, x_sample, c_prompt, c_sample, cache_sb_k, cache_sb_v, state_mlstm_C, state_mlstm_n, state_mlstm_m, state_conv, norm1_g, norm2_g, w_ada, b_ada, w_in, b_if, w_conv, b_conv, ml_norm_g, w_a, w_b, w_out, w_ff_gate, w_ff_up, w_ff_down, final_g):
    assert w_in.shape[0] == 1, "single layer"
    bp, seq_p, _ = x_prompt.shape
    bs, seq_s, _ = x_sample.shape
    past = cache_sb_k.shape[2]

    c_all = jnp.concatenate([c_prompt, c_sample], axis=0)
    mod3 = _ada(c_all, w_ada[0], b_ada[0]).reshape(bp + bs, 6, D_MODEL)

    w = w_in[0]
    w1 = w[:, :_C_IF].astype(BF16)
    wmvt = w[:, _C_MV:_C_MO].T.astype(BF16)
    pad = SUBLANES - ML_HEADS
    wif = w[:, _C_IF:_C_GAB].T
    wif_row = jnp.pad(wif.reshape(2, ML_HEADS, D_MODEL), ((0, 0), (0, pad), (0, 0)))
    wif_row = wif_row.reshape(2 * SUBLANES, D_MODEL).astype(BF16)
    bif_row = jnp.pad(b_if[0].reshape(2, ML_HEADS), ((0, 0), (0, pad))).reshape(2 * SUBLANES, 1)
    wts = (norm1_g[0].reshape(1, D_MODEL), norm2_g[0].reshape(1, D_MODEL), final_g.reshape(1, D_MODEL),
           w1, wmvt, wif_row, bif_row, w_conv[0], b_conv[0].reshape(1, 2 * ML_WIDTH),
           ml_norm_g[0].reshape(1, ML_WIDTH),
           w[:, _C_GAB:_C_END].astype(BF16), w_a[0].astype(BF16), w_b[0].astype(BF16), w_out[0].astype(BF16),
           w_ff_gate[0].astype(BF16), w_ff_up[0].astype(BF16), w_ff_down[0].astype(BF16))

    zeros = functools.partial(jnp.zeros, dtype=F32)
    y_p, st_p = _group(
        x_prompt, mod3[:bp], None, None,
        zeros((bp, ML_HEADS, ML_HEAD_DIM, ML_HEAD_DIM)), zeros((bp, ML_HEADS, ML_HEAD_DIM)),
        zeros((bp, ML_HEADS)), zeros((bp, CONV_W - 1, 2 * ML_WIDTH)), wts,
        bt=1, lt=256, tq=ATTN_BLOCK, n_past=None, lc=ML_CHUNK)
    y_s, st_s = _group(
        x_sample, mod3[bp:],
        cache_sb_k[0].reshape(bs, past, SB_WIDTH), cache_sb_v[0].reshape(bs, past, SB_WIDTH),
        state_mlstm_C[0], state_mlstm_n[0], state_mlstm_m[0], state_conv[0], wts,
        bt=4, lt=seq_s, tq=seq_s, n_past=past // ATTN_BLOCK, lc=seq_s)
    return (y_p, y_s) + st_p + st_s
```

```python
import functools
import math

import jax
import jax.numpy as jnp
from jax import lax
from jax.experimental import pallas as pl
from jax.experimental.pallas import tpu as pltpu

D_MODEL = 1024
SB_HEADS = 8
SB_HEAD_DIM = 64
SB_WIDTH = SB_HEADS * SB_HEAD_DIM
ML_HEADS = 4
ML_HEAD_DIM = 128
ML_WIDTH = ML_HEADS * ML_HEAD_DIM
CONV_W = 4
D_FF = 2816
EPS = 1e-6
LANES = 128
SUBLANES = 8
ATTN_BLOCK = 128
ML_CHUNK = 128
EXP_UNDERFLOW = 104.0
VMEM_LIMIT = 56 * 1024 * 1024

BF16 = jnp.bfloat16
F32 = jnp.float32

_C_Q, _C_K, _C_V = 0, SB_WIDTH, 2 * SB_WIDTH
_C_MQK = 3 * SB_WIDTH
_C_MV = _C_MQK + 2 * ML_WIDTH
_C_MO = _C_MV + ML_WIDTH
_C_IF = _C_MO + ML_WIDTH
_C_GAB = _C_IF + 2 * ML_HEADS
_C_END = _C_GAB + 2 * D_MODEL


def _sigmoid(x):
    return 1.0 / (1.0 + jnp.exp(-x))


def _log_sigmoid(x):
    return jnp.minimum(x, 0.0) - jnp.log1p(jnp.exp(-jnp.abs(x)))


def _softplus(x):
    return jnp.maximum(x, 0.0) + jnp.log(1.0 + jnp.exp(-jnp.abs(x)))


def _dot(a, b):
    return jnp.dot(a, b, preferred_element_type=F32)


def _dot_nt(a, b):
    return lax.dot_general(a, b, (((1,), (1,)), ((), ())), preferred_element_type=F32)


def _dot_tn(a, b):
    return lax.dot_general(a, b, (((0,), (0,)), ((), ())), preferred_element_type=F32)


def _split_dot(x, tri, parts):
    acc = None
    rem = x
    for i in range(parts):
        piece = rem.astype(BF16)
        term = _dot(piece, tri)
        acc = term if acc is None else acc + term
        if i + 1 < parts:
            rem = rem - piece.astype(F32)
    return acc


def _split_dot_left(tri, x, parts):
    acc = None
    rem = x
    for i in range(parts):
        piece = rem.astype(BF16)
        term = _dot(tri, piece)
        acc = term if acc is None else acc + term
        if i + 1 < parts:
            rem = rem - piece.astype(F32)
    return acc


def _norm_mod(x, g, sc, sh):
    ms = jnp.mean(x * x, axis=-1, keepdims=True)
    return x * lax.rsqrt(ms + EPS) * g * (1.0 + sc) + sh


def _const_spec(shape):
    nd = len(shape)
    return pl.BlockSpec(shape, lambda *_: (0,) * nd, pipeline_mode=pl.Buffered(1))


def _params(n_axes):
    return pltpu.CompilerParams(dimension_semantics=("arbitrary",) * n_axes,
                                vmem_limit_bytes=VMEM_LIMIT)


def _ada_kernel(c_ref, w_ref, b_ref, o_ref):
    c = c_ref[...]
    s = c * _sigmoid(c)
    o_ref[...] = _dot(s.astype(BF16), w_ref[...].astype(BF16)) + b_ref[...]


def _ada(c_all, w_ada, b_ada):
    nb = c_all.shape[0]
    n_out = w_ada.shape[1]
    tn = D_MODEL
    return pl.pallas_call(
        _ada_kernel,
        grid=(n_out // tn,),
        in_specs=[pl.BlockSpec((nb, D_MODEL), lambda j: (0, 0)),
                  pl.BlockSpec((D_MODEL, tn), lambda j: (0, j)),
                  pl.BlockSpec((1, tn), lambda j: (0, j))],
        out_specs=pl.BlockSpec((nb, tn), lambda j: (0, j)),
        out_shape=jax.ShapeDtypeStruct((nb, n_out), F32),
        compiler_params=_params(1),
        name="ada",
    )(c_all, w_ada, b_ada.reshape(1, n_out))


def _inproj_kernel(x_ref, mod_ref, conv0_ref, g1_ref, w1_ref, wmvt_ref, wifr_ref, bifr_ref,
                   wconv_ref, bconv_ref,
                   q_ref, k32_ref, v32_ref, kb_ref, vb_ref, mq_ref, mk_ref, mv_ref, mvt_ref, mo_ref,
                   grow_ref, convn_ref,
                   u_sc, xp_sc, *, bt, lt):
    step = pl.program_id(1)
    g1 = g1_ref[...]
    for j in range(bt):
        u = _norm_mod(x_ref[j], g1, mod_ref[j, 1:2, :], mod_ref[j, 0:1, :])
        u_sc[j * lt:(j + 1) * lt, :] = u.astype(BF16)
    u = u_sc[...]

    def mm(lo, hi):
        return _dot(u, w1_ref[:, lo:hi])

    sq = mm(_C_Q, _C_K)
    sk = mm(_C_K, _C_V)
    sv = mm(_C_V, _C_MQK)
    mqk = mm(_C_MQK, _C_MV)
    mv = mm(_C_MV, _C_MO)
    mo = mm(_C_MO, _C_IF)

    wconv = wconv_ref[...]
    bconv = bconv_ref[...]
    k_scale = 1.0 / math.sqrt(ML_HEAD_DIM)
    for j in range(bt):
        rows = slice(j * lt, (j + 1) * lt)
        q_ref[j] = sq[rows].astype(BF16)
        k32_ref[j] = sk[rows]
        v32_ref[j] = sv[rows]
        kb_ref[j] = sk[rows].astype(BF16)
        vb_ref[j] = sv[rows].astype(BF16)
        mv_ref[j] = mv[rows].astype(BF16)
        mo_ref[j] = mo[rows]
        mvt_ref[j] = _dot_nt(wmvt_ref[...], u_sc[rows, :]).astype(BF16)
        gr = _dot_nt(wifr_ref[...], u_sc[rows, :]) + bifr_ref[...]
        row = lax.broadcasted_iota(jnp.int32, gr.shape, 0)
        grow_ref[j] = jnp.where(row >= SUBLANES, _log_sigmoid(gr), gr)

        @pl.when(step == 0)
        def _():
            xp_sc[j, SUBLANES - (CONV_W - 1):SUBLANES, :] = conv0_ref[j]

        xp_sc[j, SUBLANES:SUBLANES + lt, :] = mqk[rows]
        conv = bconv
        for i in range(CONV_W):
            off = SUBLANES - (CONV_W - 1) + i
            conv = conv + xp_sc[j, off:off + lt, :] * wconv[i:i + 1, :]
        convn_ref[j] = xp_sc[j, lt + SUBLANES - (CONV_W - 1):lt + SUBLANES, :]
        xp_sc[j, 0:SUBLANES, :] = xp_sc[j, lt:lt + SUBLANES, :]
        conv = conv * _sigmoid(conv)
        mq_ref[j] = conv[:, :ML_WIDTH].astype(BF16)
        mk_ref[j] = (conv[:, ML_WIDTH:] * k_scale).astype(BF16)


def _inproj(x, mod3, conv0, g1, w1, wmvt, wif_row, bif_row, wconv, bconv, *, bt, lt):
    nb, seq, _ = x.shape
    assert nb % bt == 0 and seq % lt == 0 and (bt == 1 or lt == seq)
    grid = (nb // bt, seq // lt)
    n1 = w1.shape[1]

    def tok(width):
        return pl.BlockSpec((bt, lt, width), lambda b, l: (b, l, 0))

    def per_batch(rows, width):
        return pl.BlockSpec((bt, rows, width), lambda b, l: (b, 0, 0))

    in_specs = [tok(D_MODEL), per_batch(6, D_MODEL), per_batch(CONV_W - 1, 2 * ML_WIDTH),
                _const_spec((1, D_MODEL)), _const_spec((D_MODEL, n1)),
                _const_spec((ML_WIDTH, D_MODEL)), _const_spec((2 * SUBLANES, D_MODEL)),
                _const_spec((2 * SUBLANES, 1)),
                _const_spec((CONV_W, 2 * ML_WIDTH)), _const_spec((1, 2 * ML_WIDTH))]

    def tok_on_lanes(rows):
        return pl.BlockSpec((bt, rows, lt), lambda b, l: (b, 0, l))

    out_specs = [tok(SB_WIDTH)] * 5 + [tok(ML_WIDTH)] * 3 + [
        tok_on_lanes(ML_WIDTH), tok(ML_WIDTH), tok_on_lanes(2 * SUBLANES),
        per_batch(CONV_W - 1, 2 * ML_WIDTH)]

    def sds(shape, dt):
        return jax.ShapeDtypeStruct(shape, dt)

    out_shape = [sds((nb, seq, SB_WIDTH), BF16), sds((nb, seq, SB_WIDTH), F32), sds((nb, seq, SB_WIDTH), F32),
                 sds((nb, seq, SB_WIDTH), BF16), sds((nb, seq, SB_WIDTH), BF16),
                 sds((nb, seq, ML_WIDTH), BF16), sds((nb, seq, ML_WIDTH), BF16), sds((nb, seq, ML_WIDTH), BF16),
                 sds((nb, ML_WIDTH, seq), BF16), sds((nb, seq, ML_WIDTH), F32),
                 sds((nb, 2 * SUBLANES, seq), F32),
                 sds((nb, CONV_W - 1, 2 * ML_WIDTH), F32)]
    return pl.pallas_call(
        functools.partial(_inproj_kernel, bt=bt, lt=lt),
        grid=grid,
        in_specs=in_specs,
        out_specs=out_specs,
        out_shape=out_shape,
        scratch_shapes=[pltpu.VMEM((bt * lt, D_MODEL), BF16),
                        pltpu.VMEM((bt, lt + SUBLANES, 2 * ML_WIDTH), F32)],
        compiler_params=_params(2),
        name="inproj",
    )(x, mod3, conv0, g1, w1, wmvt, wif_row, bif_row, wconv, bconv)


def _cum_matrix(t):
    r = lax.broadcasted_iota(jnp.int32, (2 * t, LANES + t), 0)
    c = lax.broadcasted_iota(jnp.int32, (2 * t, LANES + t), 1)
    return jnp.where((c < LANES) | ((r & (t - 1)) > (c - LANES)), 1.0, 0.0).astype(BF16)


def _attn_kernel(q_ref, kd_ref, vd_ref, kp_ref, vp_ref, o_ref, qm_sc, acc_sc, carry_sc, *, tq, tk, n_past):
    cache_past = n_past is not None
    n_blocks = n_past if cache_past else pl.program_id(1)
    scale = 1.0 / math.sqrt(SB_HEAD_DIM)
    n_pairs = SB_WIDTH // LANES
    lane = lax.broadcasted_iota(jnp.int32, (1, LANES), 1)
    first_head = lane < SB_HEAD_DIM

    for p in range(n_pairs):
        qp = q_ref[:, p * LANES:(p + 1) * LANES] * scale
        zero = jnp.zeros_like(qp)
        qm_sc[(2 * p) * tq:(2 * p + 1) * tq, :] = jnp.where(first_head, qp, zero)
        qm_sc[(2 * p + 1) * tq:(2 * p + 2) * tq, :] = jnp.where(first_head, zero, qp)

    def block(k_of, v_of, t, visible, transposed=False):
        dot_k, dot_v = (_dot, _dot_nt) if transposed else (_dot_nt, _dot)
        z = jnp.concatenate(
            [dot_k(qm_sc[2 * p * tq:(2 * p + 2) * tq, :], k_of(p)) for p in range(n_pairs)], axis=0)
        sp = _softplus(z)
        if visible is not None:
            sp = jnp.where(visible, sp, 0.0)
        hi = sp.astype(BF16)
        lo = (sp - hi.astype(F32)).astype(BF16)
        sums = _dot(jnp.concatenate([hi, lo], axis=1), _cum_matrix(t))
        logit = z - sp - sums[:, LANES:]
        if visible is None:
            carry = carry_sc[...]
            a = jnp.exp(logit - carry)
            carry_sc[...] = carry + sums[:, :LANES]
        else:
            a = jnp.where(visible, jnp.exp(logit), 0.0)
            carry_sc[...] = sums[:, :LANES]
        a = a.astype(BF16)
        for p in range(n_pairs):
            cols = slice(p * LANES, (p + 1) * LANES)
            pv = dot_v(a[2 * p * tq:(2 * p + 2) * tq, :], v_of(p))
            out = jnp.where(first_head, pv[:tq], pv[tq:])
            if visible is None:
                acc_sc[:, cols] += out
            else:
                acc_sc[:, cols] = out

    rd = lax.broadcasted_iota(jnp.int32, (SB_HEADS * tq, tq), 0) & (tq - 1)
    cd = lax.broadcasted_iota(jnp.int32, (SB_HEADS * tq, tq), 1)
    block(lambda p: kd_ref[:, p * LANES:(p + 1) * LANES],
          lambda p: vd_ref[:, p * LANES:(p + 1) * LANES],
          tq, cd < rd)

    def not_done():
        return jnp.min(carry_sc[...]) <= EXP_UNDERFLOW

    def cond(state):
        i, go = state
        return jnp.logical_and(i < n_blocks, go)

    def body(state):
        i, _ = state
        start = pl.multiple_of((n_blocks - 1 - i) * tk, tk)
        if cache_past:
            block(lambda p: kp_ref[p * LANES:(p + 1) * LANES, pl.ds(start, tk)].astype(BF16),
                  lambda p: vp_ref[p * LANES:(p + 1) * LANES, pl.ds(start, tk)].astype(BF16),
                  tk, None, transposed=True)
        else:
            block(lambda p: kp_ref[pl.ds(start, tk), p * LANES:(p + 1) * LANES].astype(BF16),
                  lambda p: vp_ref[pl.ds(start, tk), p * LANES:(p + 1) * LANES].astype(BF16),
                  tk, None)
        return i + 1, not_done()

    lax.while_loop(cond, body, (jnp.int32(0), not_done()))
    o_ref[...] = acc_sc[...].astype(BF16)


def _attn(q, k_new, v_new, k_past, v_past, *, tq, n_past):
    nb, seq, _ = q.shape
    tok = pl.BlockSpec((None, tq, SB_WIDTH), lambda b, i: (b, i, 0))
    past = pl.BlockSpec((None,) + k_past.shape[1:], lambda b, i: (b, 0, 0))
    return pl.pallas_call(
        functools.partial(_attn_kernel, tq=tq, tk=ATTN_BLOCK, n_past=n_past),
        grid=(nb, seq // tq),
        in_specs=[tok, tok, tok, past, past],
        out_specs=tok,
        out_shape=jax.ShapeDtypeStruct((nb, seq, SB_WIDTH), BF16),
        scratch_shapes=[pltpu.VMEM((SB_HEADS * tq, LANES), BF16),
                        pltpu.VMEM((tq, SB_WIDTH), F32),
                        pltpu.VMEM((SB_HEADS * tq, LANES), F32)],
        compiler_params=_params(2),
        name="attn",
    )(q, k_new, v_new, k_past, v_past)


def _mlstm_kernel(mq_ref, mk_ref, mv_ref, mvt_ref, mo_ref, grow_ref, c0_ref, n0_ref, m0_ref, gn_ref,
                  yb_ref, c_out_ref, n_out_ref, m_out_ref,
                  c_sc, n_sc, m_sc, alpha_sc, beta_sc, r_sc, wg_sc, dec_sc, mprev_sc, *, lc, nc):
    step = pl.program_id(1)

    ri = lax.broadcasted_iota(jnp.int32, (lc, lc), 0)
    ci = lax.broadcasted_iota(jnp.int32, (lc, lc), 1)
    causal = ci <= ri
    eye = jnp.where(ri == ci, 1.0, 0.0).astype(BF16)
    ones_blk = jnp.ones((lc, ML_HEAD_DIM), BF16)
    zeros_blk = jnp.zeros((lc, ML_HEAD_DIM), BF16)

    @pl.when(step == 0)
    def _():
        c_sc[...] = c0_ref[...]
        n_sc[...] = n0_ref[...]
        incl_t = jnp.where(ri <= ci, 1.0, 0.0).astype(BF16)
        li = jnp.concatenate([grow_ref[0:SUBLANES, c * lc:(c + 1) * lc] for c in range(nc)], axis=0)
        lf = jnp.concatenate([grow_ref[SUBLANES:2 * SUBLANES, c * lc:(c + 1) * lc] for c in range(nc)], axis=0)
        b = _split_dot(lf, incl_t, 3)
        r = li - b
        lane = lax.broadcasted_iota(jnp.int32, r.shape, 1)
        run = r
        d = 1
        while d < lc:
            run = jnp.maximum(run, jnp.where(lane >= d, pltpu.roll(run, d, axis=1), -jnp.inf))
            d *= 2
        b_last = b[:, lc - 1:lc]
        g = b_last - b + li
        g_max = jnp.max(g, axis=1, keepdims=True)
        m = m0_ref[...]
        m_before = []
        m_after = []
        for c in range(nc):
            rows = slice(c * SUBLANES, (c + 1) * SUBLANES)
            m_before.append(m)
            m = jnp.maximum(b_last[rows] + m, g_max[rows])
            m_after.append(m)
        m_sc[...] = m
        m_prev = jnp.concatenate(m_before, axis=0)
        m_new = jnp.concatenate(m_after, axis=0)
        alpha = (-jnp.maximum(run, m_prev[:, :lc])).astype(BF16).astype(F32)
        alpha_sc[...] = alpha
        beta_sc[...] = alpha - b
        r_sc[...] = r
        wg_sc[...] = jnp.exp(g - m_new[:, :lc])
        dec_sc[...] = jnp.exp(b_last + m_prev - m_new)
        mprev_sc[...] = m_prev

    base = pl.multiple_of(step * SUBLANES, SUBLANES)
    chunk = pl.ds(base, SUBLANES)
    alpha = alpha_sc[chunk, :].astype(BF16)
    beta = beta_sc[chunk, :]
    beta_hi = beta.astype(BF16)
    beta_lo = (beta - beta_hi.astype(F32)).astype(BF16)
    r = r_sc[chunk, :]
    wg = wg_sc[chunk, :]
    decay = dec_sc[chunk, :]
    m_prev = mprev_sc[chunk, :]
    eye2 = jnp.concatenate([eye, eye], axis=1)

    heads = range(ML_HEADS)
    cols = [slice(h * ML_HEAD_DIM, (h + 1) * ML_HEAD_DIM) for h in heads]
    dk = ML_HEAD_DIM

    def rows_of(x, h):
        return jnp.broadcast_to(x[h:h + 1, :], (dk, lc))
    alpha_b = _dot_nt(eye, jnp.concatenate([rows_of(alpha, h) for h in heads], axis=0))
    beta_b = _dot_nt(eye2, jnp.concatenate(
        [jnp.concatenate([rows_of(beta_hi, h), rows_of(beta_lo, h)], axis=1) for h in heads], axis=0))
    qk = [_dot_nt(mq_ref[:, cols[h]], mk_ref[:, cols[h]]) for h in heads]
    c_prev = [c_sc[h] for h in heads]
    n_prev = [n_sc[h:h + 1, :] for h in heads]
    qc = [_dot_nt(mq_ref[:, cols[h]],
                  jnp.concatenate([c_prev[h].astype(BF16),
                                   jnp.broadcast_to(n_prev[h], (dk, dk)).astype(BF16)], axis=0))
          for h in heads]
    upd = []
    for h in heads:
        wg_h = wg[h:h + 1, :]
        lhs = jnp.concatenate([(mvt_ref[cols[h], :].astype(F32) * wg_h).astype(BF16),
                               jnp.broadcast_to(wg_h, (SUBLANES, lc)).astype(BF16)], axis=0)
        upd.append(_dot(lhs, mk_ref[:, cols[h]]))

    cat = []
    for h in heads:
        w = jnp.where(causal, jnp.exp(alpha_b[:, h * dk:h * dk + lc] + r[h:h + 1, :]), 0.0)
        s = qk[h] * w
        s_hi = s.astype(BF16)
        cat.append(jnp.concatenate([s_hi, (s - s_hi.astype(F32)).astype(BF16)], axis=1))
    nv = []
    for h in heads:
        rhs = jnp.concatenate([jnp.concatenate([mv_ref[:, cols[h]], ones_blk], axis=1),
                               jnp.concatenate([zeros_blk, ones_blk], axis=1)], axis=0)
        nv.append(_dot(cat[h], rhs))
    hval = []
    cat2 = []
    for h in heads:
        w_inter = jnp.exp(m_prev[h:h + 1, :] + alpha_b[:, cols[h]])
        num = nv[h][:, :dk] + w_inter * qc[h][:, :dk]
        den = nv[h][:, dk:] + w_inter * qc[h][:, dk:]
        hv = num / jnp.maximum(jnp.abs(den), jnp.exp(beta_b[:, cols[h]]))
        h2 = hv * hv
        h2_hi = h2.astype(BF16)
        hval.append(hv)
        cat2.append(jnp.concatenate([h2_hi, (h2 - h2_hi.astype(F32)).astype(BF16)], axis=1))
    ms = _dot(jnp.concatenate(cat2, axis=0), jnp.ones((2 * dk, dk), BF16)) * (1.0 / dk)
    for h in heads:
        y = hval[h] * lax.rsqrt(ms[h * lc:(h + 1) * lc] + EPS) * gn_ref[:, cols[h]]
        yb_ref[:, cols[h]] = (_sigmoid(mo_ref[:, cols[h]]) * y).astype(BF16)

    for h in heads:
        dec = decay[h:h + 1, :]
        c_sc[h] = dec * c_prev[h] + upd[h][:dk]
        n_sc[h:h + 1, :] = dec * n_prev[h] + upd[h][dk:dk + 1]

    c_out_ref[...] = c_sc[...]
    n_out_ref[...] = n_sc[...]
    m_out_ref[...] = m_sc[...]


def _mlstm(mq, mk, mv, mvt, mo, grow, c0, n0, m0, gn, *, lc):
    nb, seq, _ = mq.shape
    tok = pl.BlockSpec((None, lc, ML_WIDTH), lambda b, i: (b, i, 0))
    c_spec = pl.BlockSpec((None, ML_HEADS, ML_HEAD_DIM, ML_HEAD_DIM), lambda b, i: (b, 0, 0, 0))
    n_spec = pl.BlockSpec((None, ML_HEADS, ML_HEAD_DIM), lambda b, i: (b, 0, 0))
    m_spec = pl.BlockSpec((None, SUBLANES, LANES), lambda b, i: (b, 0, 0))
    nc = seq // lc
    table = pltpu.VMEM((nc * SUBLANES, lc), F32)
    table_m = pltpu.VMEM((nc * SUBLANES, LANES), F32)
    return pl.pallas_call(
        functools.partial(_mlstm_kernel, lc=lc, nc=nc),
        grid=(nb, nc),
        in_specs=[tok, tok, tok,
                  pl.BlockSpec((None, ML_WIDTH, lc), lambda b, i: (b, 0, i)),
                  tok,
                  pl.BlockSpec((None, 2 * SUBLANES, seq), lambda b, i: (b, 0, 0)),
                  c_spec, n_spec, m_spec, _const_spec((1, ML_WIDTH))],
        out_specs=[tok, c_spec, n_spec, m_spec],
        out_shape=[jax.ShapeDtypeStruct((nb, seq, ML_WIDTH), BF16),
                   jax.ShapeDtypeStruct((nb, ML_HEADS, ML_HEAD_DIM, ML_HEAD_DIM), F32),
                   jax.ShapeDtypeStruct((nb, ML_HEADS, ML_HEAD_DIM), F32),
                   jax.ShapeDtypeStruct((nb, SUBLANES, LANES), F32)],
        scratch_shapes=[pltpu.VMEM((ML_HEADS, ML_HEAD_DIM, ML_HEAD_DIM), F32),
                        pltpu.VMEM((ML_HEADS, ML_HEAD_DIM), F32),
                        pltpu.VMEM((SUBLANES, LANES), F32),
                        table, table, table, table, table_m, table_m],
        compiler_params=_params(2),
        name="mlstm",
    )(mq, mk, mv, mvt, mo, grow, c0, n0, m0, gn)


def _post_kernel(x_ref, ya_ref, yb_ref, mod_ref, g1_ref, g2_ref, gf_ref,
                 wgab_ref, wa_ref, wb_ref, wout_ref, wfg_ref, wfu_ref, wfd_ref,
                 y_ref, u_sc, *, bt, lt):
    g1 = g1_ref[...]
    for j in range(bt):
        u = _norm_mod(x_ref[j], g1, mod_ref[j, 1:2, :], mod_ref[j, 0:1, :])
        u_sc[j * lt:(j + 1) * lt, :] = u.astype(BF16)
    u = u_sc[...]
    ya = ya_ref[...].reshape(bt * lt, SB_WIDTH)
    yb = yb_ref[...].reshape(bt * lt, ML_WIDTH)
    ga = _dot(u, wgab_ref[:, :D_MODEL])
    gb = _dot(u, wgab_ref[:, D_MODEL:])
    merged = _sigmoid(ga) * _dot(ya, wa_ref[...]) + _sigmoid(gb) * _dot(yb, wb_ref[...])
    attn_out = _dot(merged.astype(BF16), wout_ref[...])

    g2 = g2_ref[...]
    for j in range(bt):
        rows = slice(j * lt, (j + 1) * lt)
        x1 = x_ref[j] + mod_ref[j, 2:3, :] * attn_out[rows]
        y_ref[j] = x1
        u2 = _norm_mod(x1, g2, mod_ref[j, 4:5, :], mod_ref[j, 3:4, :])
        u_sc[rows, :] = u2.astype(BF16)
    u2 = u_sc[...]
    hg = _dot(u2, wfg_ref[...])
    hu = _dot(u2, wfu_ref[...])
    act = (hg * _sigmoid(hg) * hu).astype(BF16)
    ff = _dot(act, wfd_ref[...])
    gf = gf_ref[...]
    for j in range(bt):
        rows = slice(j * lt, (j + 1) * lt)
        x2 = y_ref[j] + mod_ref[j, 5:6, :] * ff[rows]
        ms = jnp.mean(x2 * x2, axis=-1, keepdims=True)
        y_ref[j] = x2 * lax.rsqrt(ms + EPS) * gf


def _post(x, ya, yb, mod3, g1, g2, gf, wgab, wa, wb, wout, wfg, wfu, wfd, *, bt, lt):
    nb, seq, _ = x.shape
    assert nb % bt == 0 and seq % lt == 0

    def tok(width):
        return pl.BlockSpec((bt, lt, width), lambda b, l: (b, l, 0))

    in_specs = [tok(D_MODEL), tok(SB_WIDTH), tok(ML_WIDTH),
                pl.BlockSpec((bt, 6, D_MODEL), lambda b, l: (b, 0, 0)),
                _const_spec((1, D_MODEL)), _const_spec((1, D_MODEL)), _const_spec((1, D_MODEL)),
                _const_spec(wgab.shape), _const_spec(wa.shape), _const_spec(wb.shape), _const_spec(wout.shape),
                _const_spec(wfg.shape), _const_spec(wfu.shape), _const_spec(wfd.shape)]
    return pl.pallas_call(
        functools.partial(_post_kernel, bt=bt, lt=lt),
        grid=(nb // bt, seq // lt),
        in_specs=in_specs,
        out_specs=tok(D_MODEL),
        out_shape=jax.ShapeDtypeStruct((nb, seq, D_MODEL), F32),
        scratch_shapes=[pltpu.VMEM((bt * lt, D_MODEL), BF16)],
        compiler_params=_params(2),
        name="post",
    )(x, ya, yb, mod3, g1, g2, gf, wgab, wa, wb, wout, wfg, wfu, wfd)


def _group(x, mod3, k_past, v_past, c0, n0, m0, conv0, wts, *, bt, lt_in, lt, tq, n_past, lc):
    (g1, g2, gf, w1, wmvt, wif_row, bif_row, wconv, bconv, gn,
     wgab, wa, wb, wout, wfg, wfu, wfd) = wts
    nb, seq, _ = x.shape
    (q, k32, v32, kb, vb, mq, mk, mv, mvt, mo, grow, conv_new) = _inproj(
        x, mod3, conv0, g1, w1, wmvt, wif_row, bif_row, wconv, bconv, bt=bt, lt=lt_in)
    if k_past is None:
        k_past, v_past = kb, vb
    ya = _attn(q, kb, vb, k_past, v_past, tq=tq, n_past=n_past)
    m0b = jnp.broadcast_to(jnp.pad(m0, ((0, 0), (0, SUBLANES - ML_HEADS)))[:, :, None], (nb, SUBLANES, LANES))
    yb, c1, n1, m1 = _mlstm(mq, mk, mv, mvt, mo, grow, c0, n0, m0b, gn, lc=lc)
    y = _post(x, ya, yb, mod3, g1, g2, gf, wgab, wa, wb, wout, wfg, wfu, wfd, bt=bt, lt=lt)
    states = (k32.reshape(1, nb, seq, SB_HEADS, SB_HEAD_DIM), v32.reshape(1, nb, seq, SB_HEADS, SB_HEAD_DIM),
              c1[None], n1[None], m1[None, :, :ML_HEADS, 0], conv_new[None])
    return y, states


def kernel(x_prompt, x_sample, c_prompt, c_sample, cache_sb_k, cache_sb_v, state_mlstm_C, state_mlstm_n, state_mlstm_m, state_conv, norm1_g, norm2_g, w_ada, b_ada, w_in, b_if, w_conv, b_conv, ml_norm_g, w_a, w_b, w_out, w_ff_gate, w_ff_up, w_ff_down, final_g):
    assert w_in.shape[0] == 1, "single layer"
    bp, seq_p, _ = x_prompt.shape
    bs, seq_s, _ = x_sample.shape
    past = cache_sb_k.shape[2]

    c_all = jnp.concatenate([c_prompt, c_sample], axis=0)
    mod3 = _ada(c_all, w_ada[0], b_ada[0]).reshape(bp + bs, 6, D_MODEL)

    w = w_in[0]
    w1 = w[:, :_C_IF].astype(BF16)
    wmvt = w[:, _C_MV:_C_MO].T.astype(BF16)
    pad = SUBLANES - ML_HEADS
    wif = w[:, _C_IF:_C_GAB].T
    wif_row = jnp.pad(wif.reshape(2, ML_HEADS, D_MODEL), ((0, 0), (0, pad), (0, 0)))
    wif_row = wif_row.reshape(2 * SUBLANES, D_MODEL).astype(BF16)
    bif_row = jnp.pad(b_if[0].reshape(2, ML_HEADS), ((0, 0), (0, pad))).reshape(2 * SUBLANES, 1)
    wts = (norm1_g[0].reshape(1, D_MODEL), norm2_g[0].reshape(1, D_MODEL), final_g.reshape(1, D_MODEL),
           w1, wmvt, wif_row, bif_row, w_conv[0], b_conv[0].reshape(1, 2 * ML_WIDTH),
           ml_norm_g[0].reshape(1, ML_WIDTH),
           w[:, _C_GAB:_C_END].astype(BF16), w_a[0].astype(BF16), w_b[0].astype(BF16), w_out[0].astype(BF16),
           w_ff_gate[0].astype(BF16), w_ff_up[0].astype(BF16), w_ff_down[0].astype(BF16))

    zeros = functools.partial(jnp.zeros, dtype=F32)
    y_p, st_p = _group(
        x_prompt, mod3[:bp], None, None,
        zeros((bp, ML_HEADS, ML_HEAD_DIM, ML_HEAD_DIM)), zeros((bp, ML_HEADS, ML_HEAD_DIM)),
        zeros((bp, ML_HEADS)), zeros((bp, CONV_W - 1, 2 * ML_WIDTH)), wts,
        bt=1, lt_in=512, lt=256, tq=ATTN_BLOCK, n_past=None, lc=ML_CHUNK)
    def cache_t(c):
        return jnp.transpose(c[0], (0, 2, 3, 1)).reshape(bs, SB_WIDTH, past)

    y_s, st_s = _group(
        x_sample, mod3[bp:], cache_t(cache_sb_k), cache_t(cache_sb_v),
        state_mlstm_C[0], state_mlstm_n[0], state_mlstm_m[0], state_conv[0], wts,
        bt=4, lt_in=seq_s, lt=seq_s, tq=seq_s, n_past=past // ATTN_BLOCK, lc=seq_s)
    return (y_p, y_s) + st_p + st_s
```

```python
import functools
import math

import jax
import jax.numpy as jnp
from jax import lax
from jax.experimental import pallas as pl
from jax.experimental.pallas import tpu as pltpu

D_MODEL = 1024
SB_HEADS = 8
SB_HEAD_DIM = 64
SB_WIDTH = SB_HEADS * SB_HEAD_DIM
ML_HEADS = 4
ML_HEAD_DIM = 128
ML_WIDTH = ML_HEADS * ML_HEAD_DIM
CONV_W = 4
D_FF = 2816
EPS = 1e-6
LANES = 128
SUBLANES = 8
ATTN_BLOCK = 128
ML_CHUNK = 128
EXP_UNDERFLOW = 104.0
SOFTPLUS_CLAMP = 60.0
ATTN_SEQS = 2
ATTN_GROUPS = 4
VMEM_LIMIT = 56 * 1024 * 1024

BF16 = jnp.bfloat16
F32 = jnp.float32

_C_Q, _C_K, _C_V = 0, SB_WIDTH, 2 * SB_WIDTH
_C_MQK = 3 * SB_WIDTH
_C_MV = _C_MQK + 2 * ML_WIDTH
_C_MO = _C_MV + ML_WIDTH
_C_IF = _C_MO + ML_WIDTH
_C_GAB = _C_IF + 2 * ML_HEADS
_C_END = _C_GAB + 2 * D_MODEL


def _sigmoid(x):
    return 1.0 / (1.0 + jnp.exp(-x))


def _log_sigmoid(x):
    return jnp.minimum(x, 0.0) - jnp.log1p(jnp.exp(-jnp.abs(x)))


def _softplus_pos(x):
    return jnp.maximum(jnp.log(1.0 + jnp.exp(jnp.minimum(x, SOFTPLUS_CLAMP))), x)


def _dot(a, b):
    return jnp.dot(a, b, preferred_element_type=F32)


def _dot_nt(a, b):
    return lax.dot_general(a, b, (((1,), (1,)), ((), ())), preferred_element_type=F32)


def _dot_tn(a, b):
    return lax.dot_general(a, b, (((0,), (0,)), ((), ())), preferred_element_type=F32)


def _split_dot(x, tri, parts):
    acc = None
    rem = x
    for i in range(parts):
        piece = rem.astype(BF16)
        term = _dot(piece, tri)
        acc = term if acc is None else acc + term
        if i + 1 < parts:
            rem = rem - piece.astype(F32)
    return acc


def _split_dot_left(tri, x, parts):
    acc = None
    rem = x
    for i in range(parts):
        piece = rem.astype(BF16)
        term = _dot(tri, piece)
        acc = term if acc is None else acc + term
        if i + 1 < parts:
            rem = rem - piece.astype(F32)
    return acc


def _norm_mod(x, g, sc, sh):
    ms = jnp.mean(x * x, axis=-1, keepdims=True)
    return x * lax.rsqrt(ms + EPS) * g * (1.0 + sc) + sh


def _const_spec(shape):
    nd = len(shape)
    return pl.BlockSpec(shape, lambda *_: (0,) * nd, pipeline_mode=pl.Buffered(1))


def _params(n_axes):
    return pltpu.CompilerParams(dimension_semantics=("arbitrary",) * n_axes,
                                vmem_limit_bytes=VMEM_LIMIT)


def _ada_kernel(c_ref, w_ref, b_ref, o_ref):
    c = c_ref[...]
    s = c * _sigmoid(c)
    o_ref[...] = _dot(s.astype(BF16), w_ref[...].astype(BF16)) + b_ref[...]


def _ada(c_all, w_ada, b_ada):
    nb = c_all.shape[0]
    n_out = w_ada.shape[1]
    tn = D_MODEL
    return pl.pallas_call(
        _ada_kernel,
        grid=(n_out // tn,),
        in_specs=[pl.BlockSpec((nb, D_MODEL), lambda j: (0, 0)),
                  pl.BlockSpec((D_MODEL, tn), lambda j: (0, j)),
                  pl.BlockSpec((1, tn), lambda j: (0, j))],
        out_specs=pl.BlockSpec((nb, tn), lambda j: (0, j)),
        out_shape=jax.ShapeDtypeStruct((nb, n_out), F32),
        compiler_params=_params(1),
        name="ada",
    )(c_all, w_ada, b_ada.reshape(1, n_out))


def _inproj_kernel(x_ref, mod_ref, conv0_ref, g1_ref, w1_ref, wmvt_ref, wifr_ref, bifr_ref,
                   wconv_ref, bconv_ref,
                   q_ref, k32_ref, v32_ref, kb_ref, vb_ref, mq_ref, mk_ref, mv_ref, mvt_ref, mo_ref,
                   grow_ref, convn_ref,
                   u_sc, xp_sc, *, bt, lt):
    step = pl.program_id(1)
    g1 = g1_ref[...]
    for j in range(bt):
        u = _norm_mod(x_ref[j], g1, mod_ref[j, 1:2, :], mod_ref[j, 0:1, :])
        u_sc[j * lt:(j + 1) * lt, :] = u.astype(BF16)
    u = u_sc[...]

    def mm(lo, hi):
        return _dot(u, w1_ref[:, lo:hi])

    sq = mm(_C_Q, _C_K)
    sk = mm(_C_K, _C_V)
    sv = mm(_C_V, _C_MQK)
    mqk = mm(_C_MQK, _C_MV)
    mv = mm(_C_MV, _C_MO)
    mo = mm(_C_MO, _C_IF)

    wconv = wconv_ref[...]
    bconv = bconv_ref[...]
    k_scale = 1.0 / math.sqrt(ML_HEAD_DIM)
    for j in range(bt):
        rows = slice(j * lt, (j + 1) * lt)
        q_ref[j] = sq[rows].astype(BF16)
        k32_ref[j] = sk[rows]
        v32_ref[j] = sv[rows]
        kb_ref[j] = sk[rows].astype(BF16)
        vb_ref[j] = sv[rows].astype(BF16)
        mv_ref[j] = mv[rows].astype(BF16)
        mo_ref[j] = mo[rows]
        mvt_ref[j] = _dot_nt(wmvt_ref[...], u_sc[rows, :]).astype(BF16)
        gr = _dot_nt(wifr_ref[...], u_sc[rows, :]) + bifr_ref[...]
        row = lax.broadcasted_iota(jnp.int32, gr.shape, 0)
        grow_ref[j] = jnp.where(row >= SUBLANES, _log_sigmoid(gr), gr)

        @pl.when(step == 0)
        def _():
            xp_sc[j, SUBLANES - (CONV_W - 1):SUBLANES, :] = conv0_ref[j]

        xp_sc[j, SUBLANES:SUBLANES + lt, :] = mqk[rows]
        conv = bconv
        for i in range(CONV_W):
            off = SUBLANES - (CONV_W - 1) + i
            conv = conv + xp_sc[j, off:off + lt, :] * wconv[i:i + 1, :]
        convn_ref[j] = xp_sc[j, lt + SUBLANES - (CONV_W - 1):lt + SUBLANES, :]
        xp_sc[j, 0:SUBLANES, :] = xp_sc[j, lt:lt + SUBLANES, :]
        conv = conv * _sigmoid(conv)
        mq_ref[j] = conv[:, :ML_WIDTH].astype(BF16)
        mk_ref[j] = (conv[:, ML_WIDTH:] * k_scale).astype(BF16)


def _inproj(x, mod3, conv0, g1, w1, wmvt, wif_row, bif_row, wconv, bconv, *, bt, lt):
    nb, seq, _ = x.shape
    assert nb % bt == 0 and seq % lt == 0 and (bt == 1 or lt == seq)
    grid = (nb // bt, seq // lt)
    n1 = w1.shape[1]

    def tok(width):
        return pl.BlockSpec((bt, lt, width), lambda b, l: (b, l, 0))

    def per_batch(rows, width):
        return pl.BlockSpec((bt, rows, width), lambda b, l: (b, 0, 0))

    in_specs = [tok(D_MODEL), per_batch(6, D_MODEL), per_batch(CONV_W - 1, 2 * ML_WIDTH),
                _const_spec((1, D_MODEL)), _const_spec((D_MODEL, n1)),
                _const_spec((ML_WIDTH, D_MODEL)), _const_spec((2 * SUBLANES, D_MODEL)),
                _const_spec((2 * SUBLANES, 1)),
                _const_spec((CONV_W, 2 * ML_WIDTH)), _const_spec((1, 2 * ML_WIDTH))]

    def tok_on_lanes(rows):
        return pl.BlockSpec((bt, rows, lt), lambda b, l: (b, 0, l))

    out_specs = [tok(SB_WIDTH)] * 5 + [tok(ML_WIDTH)] * 3 + [
        tok_on_lanes(ML_WIDTH), tok(ML_WIDTH), tok_on_lanes(2 * SUBLANES),
        per_batch(CONV_W - 1, 2 * ML_WIDTH)]

    def sds(shape, dt):
        return jax.ShapeDtypeStruct(shape, dt)

    out_shape = [sds((nb, seq, SB_WIDTH), BF16), sds((nb, seq, SB_WIDTH), F32), sds((nb, seq, SB_WIDTH), F32),
                 sds((nb, seq, SB_WIDTH), BF16), sds((nb, seq, SB_WIDTH), BF16),
                 sds((nb, seq, ML_WIDTH), BF16), sds((nb, seq, ML_WIDTH), BF16), sds((nb, seq, ML_WIDTH), BF16),
                 sds((nb, ML_WIDTH, seq), BF16), sds((nb, seq, ML_WIDTH), F32),
                 sds((nb, 2 * SUBLANES, seq), F32),
                 sds((nb, CONV_W - 1, 2 * ML_WIDTH), F32)]
    return pl.pallas_call(
        functools.partial(_inproj_kernel, bt=bt, lt=lt),
        grid=grid,
        in_specs=in_specs,
        out_specs=out_specs,
        out_shape=out_shape,
        scratch_shapes=[pltpu.VMEM((bt * lt, D_MODEL), BF16),
                        pltpu.VMEM((bt, lt + SUBLANES, 2 * ML_WIDTH), F32)],
        compiler_params=_params(2),
        name="inproj",
    )(x, mod3, conv0, g1, w1, wmvt, wif_row, bif_row, wconv, bconv)


def _cum_matrix(t):
    r = lax.broadcasted_iota(jnp.int32, (2 * t, LANES + t), 0)
    c = lax.broadcasted_iota(jnp.int32, (2 * t, LANES + t), 1)
    return jnp.where((c < LANES) | ((r & (t - 1)) > (c - LANES)), 1.0, 0.0).astype(BF16)


def _attn_kernel(q_ref, kd_ref, vd_ref, kp_ref, vp_ref, o_ref, qm_sc, acc_sc, carry_sc, *, bt, tq, tk, n_past):
    cache_past = n_past is not None
    n_blocks = n_past if cache_past else pl.program_id(1)
    scale = 1.0 / math.sqrt(SB_HEAD_DIM)
    n_pairs = SB_WIDTH // LANES
    lane = lax.broadcasted_iota(jnp.int32, (1, LANES), 1)
    first_head = lane < SB_HEAD_DIM

    n_units = bt * n_pairs
    for u in range(n_units):
        s, p = divmod(u, n_pairs)
        qp = q_ref[s, :, p * LANES:(p + 1) * LANES] * scale
        zero = jnp.zeros_like(qp)
        qm_sc[(2 * u) * tq:(2 * u + 1) * tq, :] = jnp.where(first_head, qp, zero)
        qm_sc[(2 * u + 1) * tq:(2 * u + 2) * tq, :] = jnp.where(first_head, zero, qp)

    def block(k_of, v_of, t, visible, transposed=False):
        dot_k, dot_v = (_dot, _dot_nt) if transposed else (_dot_nt, _dot)
        cum = _cum_matrix(t)
        per_group = n_units // ATTN_GROUPS
        groups = [range(g, g + per_group) for g in range(0, n_units, per_group)]
        zs = [jnp.concatenate([dot_k(qm_sc[2 * u * tq:(2 * u + 2) * tq, :], k_of(u)) for u in grp], axis=0)
              for grp in groups]
        sps, sums = [], []
        for gi, grp in enumerate(groups):
            rows = slice(2 * grp[0] * tq, 2 * (grp[-1] + 1) * tq)
            sp = _softplus_pos(zs[gi])
            if visible is not None:
                sp = jnp.where(visible[rows], sp, 0.0)
            hi = sp.astype(BF16)
            lo = (sp - hi.astype(F32)).astype(BF16)
            sps.append(sp)
            sums.append(_dot(jnp.concatenate([hi, lo], axis=1), cum))
        for gi, grp in enumerate(groups):
            rows = slice(2 * grp[0] * tq, 2 * (grp[-1] + 1) * tq)
            logit = zs[gi] - sps[gi] - sums[gi][:, LANES:]
            if visible is None:
                carry = carry_sc[rows, :]
                a = jnp.exp(logit - carry)
                carry_sc[rows, :] = carry + sums[gi][:, :LANES]
            else:
                a = jnp.where(visible[rows], jnp.exp(logit), 0.0)
                carry_sc[rows, :] = sums[gi][:, :LANES]
            a = a.astype(BF16)
            for j, u in enumerate(grp):
                s, p = divmod(u, n_pairs)
                cols = slice(p * LANES, (p + 1) * LANES)
                pv = dot_v(a[2 * j * tq:(2 * j + 2) * tq, :], v_of(u))
                out = jnp.where(first_head, pv[:tq], pv[tq:])
                if visible is None:
                    acc_sc[s, :, cols] += out
                else:
                    acc_sc[s, :, cols] = out

    def unit_cols(ref):
        def of(u):
            s, p = divmod(u, n_pairs)
            return ref[s, :, p * LANES:(p + 1) * LANES]
        return of

    rd = lax.broadcasted_iota(jnp.int32, (2 * n_units * tq, tq), 0) & (tq - 1)
    cd = lax.broadcasted_iota(jnp.int32, (2 * n_units * tq, tq), 1)
    block(unit_cols(kd_ref), unit_cols(vd_ref), tq, cd < rd)

    def not_done():
        return jnp.min(carry_sc[...]) <= EXP_UNDERFLOW

    def cond(state):
        i, go = state
        return jnp.logical_and(i < n_blocks, go)

    def body(state):
        i, _ = state
        start = pl.multiple_of((n_blocks - 1 - i) * tk, tk)
        def past_of(ref):
            def of(u):
                s, p = divmod(u, n_pairs)
                if cache_past:
                    return ref[s, p * LANES:(p + 1) * LANES, pl.ds(start, tk)].astype(BF16)
                return ref[s, pl.ds(start, tk), p * LANES:(p + 1) * LANES].astype(BF16)
            return of

        block(past_of(kp_ref), past_of(vp_ref), tk, None, transposed=cache_past)
        return i + 1, not_done()

    lax.while_loop(cond, body, (jnp.int32(0), not_done()))
    o_ref[...] = acc_sc[...].astype(BF16)


def _attn(q, k_new, v_new, k_past, v_past, *, bt, tq, n_past):
    nb, seq, _ = q.shape
    assert nb % bt == 0
    tok = pl.BlockSpec((bt, tq, SB_WIDTH), lambda b, i: (b, i, 0))
    past = pl.BlockSpec((bt,) + k_past.shape[1:], lambda b, i: (b, 0, 0))
    rows = bt * SB_HEADS * tq
    return pl.pallas_call(
        functools.partial(_attn_kernel, bt=bt, tq=tq, tk=ATTN_BLOCK, n_past=n_past),
        grid=(nb // bt, seq // tq),
        in_specs=[tok, tok, tok, past, past],
        out_specs=tok,
        out_shape=jax.ShapeDtypeStruct((nb, seq, SB_WIDTH), BF16),
        scratch_shapes=[pltpu.VMEM((rows, LANES), BF16),
                        pltpu.VMEM((bt, tq, SB_WIDTH), F32),
                        pltpu.VMEM((rows, LANES), F32)],
        compiler_params=_params(2),
        name="attn",
    )(q, k_new, v_new, k_past, v_past)


def _mlstm_kernel(mq_ref, mk_ref, mv_ref, mvt_ref, mo_ref, grow_ref, c0_ref, n0_ref, m0_ref, gn_ref,
                  yb_ref, c_out_ref, n_out_ref, m_out_ref,
                  c_sc, n_sc, m_sc, alpha_sc, beta_sc, r_sc, wg_sc, dec_sc, mprev_sc, *, lc, nc):
    step = pl.program_id(1)

    ri = lax.broadcasted_iota(jnp.int32, (lc, lc), 0)
    ci = lax.broadcasted_iota(jnp.int32, (lc, lc), 1)
    causal = ci <= ri
    eye = jnp.where(ri == ci, 1.0, 0.0).astype(BF16)
    ones_blk = jnp.ones((lc, ML_HEAD_DIM), BF16)
    zeros_blk = jnp.zeros((lc, ML_HEAD_DIM), BF16)

    @pl.when(step == 0)
    def _():
        c_sc[...] = c0_ref[...]
        n_sc[...] = n0_ref[...]
        incl_t = jnp.where(ri <= ci, 1.0, 0.0).astype(BF16)
        li = jnp.concatenate([grow_ref[0:SUBLANES, c * lc:(c + 1) * lc] for c in range(nc)], axis=0)
        lf = jnp.concatenate([grow_ref[SUBLANES:2 * SUBLANES, c * lc:(c + 1) * lc] for c in range(nc)], axis=0)
        b = _split_dot(lf, incl_t, 3)
        r = li - b
        lane = lax.broadcasted_iota(jnp.int32, r.shape, 1)
        run = r
        d = 1
        while d < lc:
            run = jnp.maximum(run, jnp.where(lane >= d, pltpu.roll(run, d, axis=1), -jnp.inf))
            d *= 2
        b_last = b[:, lc - 1:lc]
        g = b_last - b + li
        g_max = jnp.max(g, axis=1, keepdims=True)
        m = m0_ref[...]
        m_before = []
        m_after = []
        for c in range(nc):
            rows = slice(c * SUBLANES, (c + 1) * SUBLANES)
            m_before.append(m)
            m = jnp.maximum(b_last[rows] + m, g_max[rows])
            m_after.append(m)
        m_sc[...] = m
        m_prev = jnp.concatenate(m_before, axis=0)
        m_new = jnp.concatenate(m_after, axis=0)
        alpha = (-jnp.maximum(run, m_prev[:, :lc])).astype(BF16).astype(F32)
        alpha_sc[...] = alpha
        beta_sc[...] = alpha - b
        r_sc[...] = r
        wg_sc[...] = jnp.exp(g - m_new[:, :lc])
        dec_sc[...] = jnp.exp(b_last + m_prev - m_new)
        mprev_sc[...] = m_prev

    base = pl.multiple_of(step * SUBLANES, SUBLANES)
    chunk = pl.ds(base, SUBLANES)
    alpha = alpha_sc[chunk, :].astype(BF16)
    beta = beta_sc[chunk, :]
    beta_hi = beta.astype(BF16)
    beta_lo = (beta - beta_hi.astype(F32)).astype(BF16)
    r = r_sc[chunk, :]
    wg = wg_sc[chunk, :]
    decay = dec_sc[chunk, :]
    m_prev = mprev_sc[chunk, :]
    eye2 = jnp.concatenate([eye, eye], axis=1)

    heads = range(ML_HEADS)
    cols = [slice(h * ML_HEAD_DIM, (h + 1) * ML_HEAD_DIM) for h in heads]
    dk = ML_HEAD_DIM

    def rows_of(x, h):
        return jnp.broadcast_to(x[h:h + 1, :], (dk, lc))
    alpha_b = _dot_nt(eye, jnp.concatenate([rows_of(alpha, h) for h in heads], axis=0))
    beta_b = _dot_nt(eye2, jnp.concatenate(
        [jnp.concatenate([rows_of(beta_hi, h), rows_of(beta_lo, h)], axis=1) for h in heads], axis=0))
    qk = [_dot_nt(mq_ref[:, cols[h]], mk_ref[:, cols[h]]) for h in heads]
    c_prev = [c_sc[h] for h in heads]
    n_prev = [n_sc[h:h + 1, :] for h in heads]
    qc = [_dot_nt(mq_ref[:, cols[h]],
                  jnp.concatenate([c_prev[h].astype(BF16),
                                   jnp.broadcast_to(n_prev[h], (dk, dk)).astype(BF16)], axis=0))
          for h in heads]
    upd = []
    for h in heads:
        wg_h = wg[h:h + 1, :]
        lhs = jnp.concatenate([(mvt_ref[cols[h], :].astype(F32) * wg_h).astype(BF16),
                               jnp.broadcast_to(wg_h, (SUBLANES, lc)).astype(BF16)], axis=0)
        upd.append(_dot(lhs, mk_ref[:, cols[h]]))

    cat = []
    for h in heads:
        w = jnp.where(causal, jnp.exp(alpha_b[:, h * dk:h * dk + lc] + r[h:h + 1, :]), 0.0)
        s = qk[h] * w
        s_hi = s.astype(BF16)
        cat.append(jnp.concatenate([s_hi, (s - s_hi.astype(F32)).astype(BF16)], axis=1))
    nv = []
    for h in heads:
        rhs = jnp.concatenate([jnp.concatenate([mv_ref[:, cols[h]], ones_blk], axis=1),
                               jnp.concatenate([zeros_blk, ones_blk], axis=1)], axis=0)
        nv.append(_dot(cat[h], rhs))
    hval = []
    cat2 = []
    for h in heads:
        w_inter = jnp.exp(m_prev[h:h + 1, :] + alpha_b[:, cols[h]])
        num = nv[h][:, :dk] + w_inter * qc[h][:, :dk]
        den = nv[h][:, dk:] + w_inter * qc[h][:, dk:]
        hv = num / jnp.maximum(jnp.abs(den), jnp.exp(beta_b[:, cols[h]]))
        h2 = hv * hv
        h2_hi = h2.astype(BF16)
        hval.append(hv)
        cat2.append(jnp.concatenate([h2_hi, (h2 - h2_hi.astype(F32)).astype(BF16)], axis=1))
    ms = _dot(jnp.concatenate(cat2, axis=0), jnp.ones((2 * dk, dk), BF16)) * (1.0 / dk)
    for h in heads:
        y = hval[h] * lax.rsqrt(ms[h * lc:(h + 1) * lc] + EPS) * gn_ref[:, cols[h]]
        yb_ref[:, cols[h]] = (_sigmoid(mo_ref[:, cols[h]]) * y).astype(BF16)

    for h in heads:
        dec = decay[h:h + 1, :]
        c_sc[h] = dec * c_prev[h] + upd[h][:dk]
        n_sc[h:h + 1, :] = dec * n_prev[h] + upd[h][dk:dk + 1]

    c_out_ref[...] = c_sc[...]
    n_out_ref[...] = n_sc[...]
    m_out_ref[...] = m_sc[...]


def _mlstm(mq, mk, mv, mvt, mo, grow, c0, n0, m0, gn, *, lc):
    nb, seq, _ = mq.shape
    tok = pl.BlockSpec((None, lc, ML_WIDTH), lambda b, i: (b, i, 0))
    c_spec = pl.BlockSpec((None, ML_HEADS, ML_HEAD_DIM, ML_HEAD_DIM), lambda b, i: (b, 0, 0, 0))
    n_spec = pl.BlockSpec((None, ML_HEADS, ML_HEAD_DIM), lambda b, i: (b, 0, 0))
    m_spec = pl.BlockSpec((None, SUBLANES, LANES), lambda b, i: (b, 0, 0))
    nc = seq // lc
    table = pltpu.VMEM((nc * SUBLANES, lc), F32)
    table_m = pltpu.VMEM((nc * SUBLANES, LANES), F32)
    return pl.pallas_call(
        functools.partial(_mlstm_kernel, lc=lc, nc=nc),
        grid=(nb, nc),
        in_specs=[tok, tok, tok,
                  pl.BlockSpec((None, ML_WIDTH, lc), lambda b, i: (b, 0, i)),
                  tok,
                  pl.BlockSpec((None, 2 * SUBLANES, seq), lambda b, i: (b, 0, 0)),
                  c_spec, n_spec, m_spec, _const_spec((1, ML_WIDTH))],
        out_specs=[tok, c_spec, n_spec, m_spec],
        out_shape=[jax.ShapeDtypeStruct((nb, seq, ML_WIDTH), BF16),
                   jax.ShapeDtypeStruct((nb, ML_HEADS, ML_HEAD_DIM, ML_HEAD_DIM), F32),
                   jax.ShapeDtypeStruct((nb, ML_HEADS, ML_HEAD_DIM), F32),
                   jax.ShapeDtypeStruct((nb, SUBLANES, LANES), F32)],
        scratch_shapes=[pltpu.VMEM((ML_HEADS, ML_HEAD_DIM, ML_HEAD_DIM), F32),
                        pltpu.VMEM((ML_HEADS, ML_HEAD_DIM), F32),
                        pltpu.VMEM((SUBLANES, LANES), F32),
                        table, table, table, table, table_m, table_m],
        compiler_params=_params(2),
        name="mlstm",
    )(mq, mk, mv, mvt, mo, grow, c0, n0, m0, gn)


def _post_kernel(x_ref, ya_ref, yb_ref, mod_ref, g1_ref, g2_ref, gf_ref,
                 wgab_ref, wa_ref, wb_ref, wout_ref, wfg_ref, wfu_ref, wfd_ref,
                 y_ref, u_sc, *, bt, lt):
    g1 = g1_ref[...]
    for j in range(bt):
        u = _norm_mod(x_ref[j], g1, mod_ref[j, 1:2, :], mod_ref[j, 0:1, :])
        u_sc[j * lt:(j + 1) * lt, :] = u.astype(BF16)
    u = u_sc[...]
    ya = ya_ref[...].reshape(bt * lt, SB_WIDTH)
    yb = yb_ref[...].reshape(bt * lt, ML_WIDTH)
    ga = _dot(u, wgab_ref[:, :D_MODEL])
    gb = _dot(u, wgab_ref[:, D_MODEL:])
    merged = _sigmoid(ga) * _dot(ya, wa_ref[...]) + _sigmoid(gb) * _dot(yb, wb_ref[...])
    attn_out = _dot(merged.astype(BF16), wout_ref[...])

    g2 = g2_ref[...]
    for j in range(bt):
        rows = slice(j * lt, (j + 1) * lt)
        x1 = x_ref[j] + mod_ref[j, 2:3, :] * attn_out[rows]
        y_ref[j] = x1
        u2 = _norm_mod(x1, g2, mod_ref[j, 4:5, :], mod_ref[j, 3:4, :])
        u_sc[rows, :] = u2.astype(BF16)
    u2 = u_sc[...]
    hg = _dot(u2, wfg_ref[...])
    hu = _dot(u2, wfu_ref[...])
    act = (hg * _sigmoid(hg) * hu).astype(BF16)
    ff = _dot(act, wfd_ref[...])
    gf = gf_ref[...]
    for j in range(bt):
        rows = slice(j * lt, (j + 1) * lt)
        x2 = y_ref[j] + mod_ref[j, 5:6, :] * ff[rows]
        ms = jnp.mean(x2 * x2, axis=-1, keepdims=True)
        y_ref[j] = x2 * lax.rsqrt(ms + EPS) * gf


def _post(x, ya, yb, mod3, g1, g2, gf, wgab, wa, wb, wout, wfg, wfu, wfd, *, bt, lt):
    nb, seq, _ = x.shape
    assert nb % bt == 0 and seq % lt == 0

    def tok(width):
        return pl.BlockSpec((bt, lt, width), lambda b, l: (b, l, 0))

    in_specs = [tok(D_MODEL), tok(SB_WIDTH), tok(ML_WIDTH),
                pl.BlockSpec((bt, 6, D_MODEL), lambda b, l: (b, 0, 0)),
                _const_spec((1, D_MODEL)), _const_spec((1, D_MODEL)), _const_spec((1, D_MODEL)),
                _const_spec(wgab.shape), _const_spec(wa.shape), _const_spec(wb.shape), _const_spec(wout.shape),
                _const_spec(wfg.shape), _const_spec(wfu.shape), _const_spec(wfd.shape)]
    return pl.pallas_call(
        functools.partial(_post_kernel, bt=bt, lt=lt),
        grid=(nb // bt, seq // lt),
        in_specs=in_specs,
        out_specs=tok(D_MODEL),
        out_shape=jax.ShapeDtypeStruct((nb, seq, D_MODEL), F32),
        scratch_shapes=[pltpu.VMEM((bt * lt, D_MODEL), BF16)],
        compiler_params=_params(2),
        name="post",
    )(x, ya, yb, mod3, g1, g2, gf, wgab, wa, wb, wout, wfg, wfu, wfd)


def _group(x, mod3, k_past, v_past, c0, n0, m0, conv0, wts, *, bt, lt_in, lt, tq, n_past, lc):
    (g1, g2, gf, w1, wmvt, wif_row, bif_row, wconv, bconv, gn,
     wgab, wa, wb, wout, wfg, wfu, wfd) = wts
    nb, seq, _ = x.shape
    (q, k32, v32, kb, vb, mq, mk, mv, mvt, mo, grow, conv_new) = _inproj(
        x, mod3, conv0, g1, w1, wmvt, wif_row, bif_row, wconv, bconv, bt=bt, lt=lt_in)
    if k_past is None:
        k_past, v_past = kb, vb
    ya = _attn(q, kb, vb, k_past, v_past, bt=ATTN_SEQS, tq=tq, n_past=n_past)
    m0b = jnp.broadcast_to(jnp.pad(m0, ((0, 0), (0, SUBLANES - ML_HEADS)))[:, :, None], (nb, SUBLANES, LANES))
    yb, c1, n1, m1 = _mlstm(mq, mk, mv, mvt, mo, grow, c0, n0, m0b, gn, lc=lc)
    y = _post(x, ya, yb, mod3, g1, g2, gf, wgab, wa, wb, wout, wfg, wfu, wfd, bt=bt, lt=lt)
    states = (k32.reshape(1, nb, seq, SB_HEADS, SB_HEAD_DIM), v32.reshape(1, nb, seq, SB_HEADS, SB_HEAD_DIM),
              c1[None], n1[None], m1[None, :, :ML_HEADS, 0], conv_new[None])
    return y, states


def kernel(x_prompt, x_sample, c_prompt, c_sample, cache_sb_k, cache_sb_v, state_mlstm_C, state_mlstm_n, state_mlstm_m, state_conv, norm1_g, norm2_g, w_ada, b_ada, w_in, b_if, w_conv, b_conv, ml_norm_g, w_a, w_b, w_out, w_ff_gate, w_ff_up, w_ff_down, final_g):
    assert w_in.shape[0] == 1, "single layer"
    bp, seq_p, _ = x_prompt.shape
    bs, seq_s, _ = x_sample.shape
    past = cache_sb_k.shape[2]

    c_all = jnp.concatenate([c_prompt, c_sample], axis=0)
    mod3 = _ada(c_all, w_ada[0], b_ada[0]).reshape(bp + bs, 6, D_MODEL)

    w = w_in[0]
    w1 = w[:, :_C_IF].astype(BF16)
    wmvt = w[:, _C_MV:_C_MO].T.astype(BF16)
    pad = SUBLANES - ML_HEADS
    wif = w[:, _C_IF:_C_GAB].T
    wif_row = jnp.pad(wif.reshape(2, ML_HEADS, D_MODEL), ((0, 0), (0, pad), (0, 0)))
    wif_row = wif_row.reshape(2 * SUBLANES, D_MODEL).astype(BF16)
    bif_row = jnp.pad(b_if[0].reshape(2, ML_HEADS), ((0, 0), (0, pad))).reshape(2 * SUBLANES, 1)
    wts = (norm1_g[0].reshape(1, D_MODEL), norm2_g[0].reshape(1, D_MODEL), final_g.reshape(1, D_MODEL),
           w1, wmvt, wif_row, bif_row, w_conv[0], b_conv[0].reshape(1, 2 * ML_WIDTH),
           ml_norm_g[0].reshape(1, ML_WIDTH),
           w[:, _C_GAB:_C_END].astype(BF16), w_a[0].astype(BF16), w_b[0].astype(BF16), w_out[0].astype(BF16),
           w_ff_gate[0].astype(BF16), w_ff_up[0].astype(BF16), w_ff_down[0].astype(BF16))

    zeros = functools.partial(jnp.zeros, dtype=F32)
    y_p, st_p = _group(
        x_prompt, mod3[:bp], None, None,
        zeros((bp, ML_HEADS, ML_HEAD_DIM, ML_HEAD_DIM)), zeros((bp, ML_HEADS, ML_HEAD_DIM)),
        zeros((bp, ML_HEADS)), zeros((bp, CONV_W - 1, 2 * ML_WIDTH)), wts,
        bt=1, lt_in=512, lt=256, tq=ATTN_BLOCK, n_past=None, lc=ML_CHUNK)
    def cache_t(c):
        return jnp.transpose(c[0], (0, 2, 3, 1)).reshape(bs, SB_WIDTH, past)

    y_s, st_s = _group(
        x_sample, mod3[bp:], cache_t(cache_sb_k), cache_t(cache_sb_v),
        state_mlstm_C[0], state_mlstm_n[0], state_mlstm_m[0], state_conv[0], wts,
        bt=4, lt_in=seq_s, lt=seq_s, tq=seq_s, n_past=past // ATTN_BLOCK, lc=seq_s)
    return (y_p, y_s) + st_p + st_s
```

```python
import functools
import math

import jax
import jax.numpy as jnp
from jax import lax
from jax.experimental import pallas as pl
from jax.experimental.pallas import tpu as pltpu

D_MODEL = 1024
SB_HEADS = 8
SB_HEAD_DIM = 64
SB_WIDTH = SB_HEADS * SB_HEAD_DIM
ML_HEADS = 4
ML_HEAD_DIM = 128
ML_WIDTH = ML_HEADS * ML_HEAD_DIM
CONV_W = 4
D_FF = 2816
EPS = 1e-6
LANES = 128
SUBLANES = 8
ATTN_BLOCK = 128
ML_CHUNK = 128
ML_SEQS = 4
EXP_UNDERFLOW = 104.0
SOFTPLUS_CLAMP = 60.0
CONV_CHUNKS = 4
ATTN_SEQS = 2
ATTN_GROUPS = 4
VMEM_LIMIT = 56 * 1024 * 1024

BF16 = jnp.bfloat16
F32 = jnp.float32

_C_Q, _C_K, _C_V = 0, SB_WIDTH, 2 * SB_WIDTH
_C_MQK = 3 * SB_WIDTH
_C_MV = _C_MQK + 2 * ML_WIDTH
_C_MO = _C_MV + ML_WIDTH
_C_IF = _C_MO + ML_WIDTH
_C_GAB = _C_IF + 2 * ML_HEADS
_C_END = _C_GAB + 2 * D_MODEL


def _sigmoid(x):
    return 1.0 / (1.0 + jnp.exp(-x))


def _log_sigmoid(x):
    return jnp.minimum(x, 0.0) - jnp.log1p(jnp.exp(-jnp.abs(x)))


def _softplus_pos(x):
    return jnp.maximum(jnp.log(1.0 + jnp.exp(jnp.minimum(x, SOFTPLUS_CLAMP))), x)


def _dot(a, b):
    return jnp.dot(a, b, preferred_element_type=F32)


def _dot_nt(a, b):
    return lax.dot_general(a, b, (((1,), (1,)), ((), ())), preferred_element_type=F32)


def _dot_tn(a, b):
    return lax.dot_general(a, b, (((0,), (0,)), ((), ())), preferred_element_type=F32)


def _split_dot(x, tri, parts):
    acc = None
    rem = x
    for i in range(parts):
        piece = rem.astype(BF16)
        term = _dot(piece, tri)
        acc = term if acc is None else acc + term
        if i + 1 < parts:
            rem = rem - piece.astype(F32)
    return acc


def _split_dot_left(tri, x, parts):
    acc = None
    rem = x
    for i in range(parts):
        piece = rem.astype(BF16)
        term = _dot(tri, piece)
        acc = term if acc is None else acc + term
        if i + 1 < parts:
            rem = rem - piece.astype(F32)
    return acc


def _norm_mod(x, g, sc, sh):
    ms = jnp.mean(x * x, axis=-1, keepdims=True)
    return x * lax.rsqrt(ms + EPS) * g * (1.0 + sc) + sh


def _const_spec(shape):
    nd = len(shape)
    return pl.BlockSpec(shape, lambda *_: (0,) * nd, pipeline_mode=pl.Buffered(1))


def _params(n_axes):
    return pltpu.CompilerParams(dimension_semantics=("arbitrary",) * n_axes,
                                vmem_limit_bytes=VMEM_LIMIT)


def _ada_kernel(c_ref, w_ref, b_ref, o_ref):
    c = c_ref[...]
    s = c * _sigmoid(c)
    o_ref[...] = _dot(s.astype(BF16), w_ref[...].astype(BF16)) + b_ref[...]


def _ada(c_all, w_ada, b_ada):
    nb = c_all.shape[0]
    n_out = w_ada.shape[1]
    tn = D_MODEL
    return pl.pallas_call(
        _ada_kernel,
        grid=(n_out // tn,),
        in_specs=[pl.BlockSpec((nb, D_MODEL), lambda j: (0, 0)),
                  pl.BlockSpec((D_MODEL, tn), lambda j: (0, j)),
                  pl.BlockSpec((1, tn), lambda j: (0, j))],
        out_specs=pl.BlockSpec((nb, tn), lambda j: (0, j)),
        out_shape=jax.ShapeDtypeStruct((nb, n_out), F32),
        compiler_params=_params(1),
        name="ada",
    )(c_all, w_ada, b_ada.reshape(1, n_out))


def _inproj_kernel(x_ref, mod_ref, conv0_ref, g1_ref, w1_ref, wmvt_ref, wifr_ref, bifr_ref,
                   wconv_ref, bconv_ref,
                   q_ref, k32_ref, v32_ref, kb_ref, vb_ref, mq_ref, mk_ref, mv_ref, mvt_ref, mo_ref,
                   grow_ref, convn_ref,
                   u_sc, xp_sc, *, bt, lt):
    @pl.when(pl.program_id(1) == 0)
    def _():
        for j in range(bt):
            xp_sc[j, SUBLANES - (CONV_W - 1):SUBLANES, :] = conv0_ref[j]

    g1 = g1_ref[...]
    for j in range(bt):
        u = _norm_mod(x_ref[j], g1, mod_ref[j, 1:2, :], mod_ref[j, 0:1, :])
        u_sc[j * lt:(j + 1) * lt, :] = u.astype(BF16)
    u = u_sc[...]

    def mm(lo, hi):
        return _dot(u, w1_ref[:, lo:hi])

    wconv = wconv_ref[...]
    bconv = bconv_ref[...]
    k_scale = 1.0 / math.sqrt(ML_HEAD_DIM)
    mqk = mm(_C_MQK, _C_MV)
    for j in range(bt):
        xp_sc[j, SUBLANES:SUBLANES + lt, :] = mqk[j * lt:(j + 1) * lt]
        convn_ref[j] = xp_sc[j, lt + SUBLANES - (CONV_W - 1):lt + SUBLANES, :]

    def conv_chunk(c):
        width = 2 * ML_WIDTH // CONV_CHUNKS
        cols = slice(c * width, (c + 1) * width)
        for j in range(bt):
            conv = bconv[:, cols]
            for i in range(CONV_W):
                off = SUBLANES - (CONV_W - 1) + i
                conv = conv + xp_sc[j, off:off + lt, cols] * wconv[i:i + 1, cols]
            conv = conv * _sigmoid(conv)
            if c < CONV_CHUNKS // 2:
                mq_ref[j, :, cols] = conv.astype(BF16)
            else:
                mk_ref[j, :, c * width - ML_WIDTH:(c + 1) * width - ML_WIDTH] = (conv * k_scale).astype(BF16)

    sq = mm(_C_Q, _C_K)
    conv_chunk(0)
    for j in range(bt):
        q_ref[j] = sq[j * lt:(j + 1) * lt].astype(BF16)
    sk = mm(_C_K, _C_V)
    conv_chunk(1)
    for j in range(bt):
        k32_ref[j] = sk[j * lt:(j + 1) * lt]
        kb_ref[j] = sk[j * lt:(j + 1) * lt].astype(BF16)
    sv = mm(_C_V, _C_MQK)
    conv_chunk(2)
    for j in range(bt):
        v32_ref[j] = sv[j * lt:(j + 1) * lt]
        vb_ref[j] = sv[j * lt:(j + 1) * lt].astype(BF16)
    mv = mm(_C_MV, _C_MO)
    conv_chunk(3)
    for j in range(bt):
        mv_ref[j] = mv[j * lt:(j + 1) * lt].astype(BF16)
        xp_sc[j, 0:SUBLANES, :] = xp_sc[j, lt:lt + SUBLANES, :]
    mo = mm(_C_MO, _C_IF)
    for j in range(bt):
        rows = slice(j * lt, (j + 1) * lt)
        mo_ref[j] = mo[rows]
        mvt_ref[j] = _dot_nt(wmvt_ref[...], u_sc[rows, :]).astype(BF16)
        gr = _dot_nt(wifr_ref[...], u_sc[rows, :]) + bifr_ref[...]
        row = lax.broadcasted_iota(jnp.int32, gr.shape, 0)
        grow_ref[j] = jnp.where(row >= SUBLANES, _log_sigmoid(gr), gr)


def _inproj(x, mod3, conv0, g1, w1, wmvt, wif_row, bif_row, wconv, bconv, *, bt, lt):
    nb, seq, _ = x.shape
    assert nb % bt == 0 and seq % lt == 0 and (bt == 1 or lt == seq)
    grid = (nb // bt, seq // lt)
    n1 = w1.shape[1]

    def tok(width):
        return pl.BlockSpec((bt, lt, width), lambda b, l: (b, l, 0))

    def per_batch(rows, width):
        return pl.BlockSpec((bt, rows, width), lambda b, l: (b, 0, 0))

    in_specs = [tok(D_MODEL), per_batch(6, D_MODEL), per_batch(CONV_W - 1, 2 * ML_WIDTH),
                _const_spec((1, D_MODEL)), _const_spec((D_MODEL, n1)),
                _const_spec((ML_WIDTH, D_MODEL)), _const_spec((2 * SUBLANES, D_MODEL)),
                _const_spec((2 * SUBLANES, 1)),
                _const_spec((CONV_W, 2 * ML_WIDTH)), _const_spec((1, 2 * ML_WIDTH))]

    def tok_on_lanes(rows):
        return pl.BlockSpec((bt, rows, lt), lambda b, l: (b, 0, l))

    out_specs = [tok(SB_WIDTH)] * 5 + [tok(ML_WIDTH)] * 3 + [
        tok_on_lanes(ML_WIDTH), tok(ML_WIDTH), tok_on_lanes(2 * SUBLANES),
        per_batch(CONV_W - 1, 2 * ML_WIDTH)]

    def sds(shape, dt):
        return jax.ShapeDtypeStruct(shape, dt)

    out_shape = [sds((nb, seq, SB_WIDTH), BF16), sds((nb, seq, SB_WIDTH), F32), sds((nb, seq, SB_WIDTH), F32),
                 sds((nb, seq, SB_WIDTH), BF16), sds((nb, seq, SB_WIDTH), BF16),
                 sds((nb, seq, ML_WIDTH), BF16), sds((nb, seq, ML_WIDTH), BF16), sds((nb, seq, ML_WIDTH), BF16),
                 sds((nb, ML_WIDTH, seq), BF16), sds((nb, seq, ML_WIDTH), F32),
                 sds((nb, 2 * SUBLANES, seq), F32),
                 sds((nb, CONV_W - 1, 2 * ML_WIDTH), F32)]
    return pl.pallas_call(
        functools.partial(_inproj_kernel, bt=bt, lt=lt),
        grid=grid,
        in_specs=in_specs,
        out_specs=out_specs,
        out_shape=out_shape,
        scratch_shapes=[pltpu.VMEM((bt * lt, D_MODEL), BF16),
                        pltpu.VMEM((bt, lt + SUBLANES, 2 * ML_WIDTH), F32)],
        compiler_params=_params(2),
        name="inproj",
    )(x, mod3, conv0, g1, w1, wmvt, wif_row, bif_row, wconv, bconv)


def _cum_matrix(t):
    r = lax.broadcasted_iota(jnp.int32, (2 * t, LANES + t), 0)
    c = lax.broadcasted_iota(jnp.int32, (2 * t, LANES + t), 1)
    return jnp.where((c < LANES) | ((r & (t - 1)) > (c - LANES)), 1.0, 0.0).astype(BF16)


def _attn_kernel(q_ref, kd_ref, vd_ref, kp_ref, vp_ref, o_ref, qm_sc, acc_sc, carry_sc, *, bt, tq, tk, n_past):
    cache_past = n_past is not None
    n_blocks = n_past if cache_past else pl.program_id(1)
    scale = 1.0 / math.sqrt(SB_HEAD_DIM)
    n_pairs = SB_WIDTH // LANES
    lane = lax.broadcasted_iota(jnp.int32, (1, LANES), 1)
    first_head = lane < SB_HEAD_DIM

    n_units = bt * n_pairs
    for u in range(n_units):
        s, p = divmod(u, n_pairs)
        qp = q_ref[s, :, p * LANES:(p + 1) * LANES] * scale
        zero = jnp.zeros_like(qp)
        qm_sc[(2 * u) * tq:(2 * u + 1) * tq, :] = jnp.where(first_head, qp, zero)
        qm_sc[(2 * u + 1) * tq:(2 * u + 2) * tq, :] = jnp.where(first_head, zero, qp)

    def block(k_of, v_of, t, visible, transposed=False):
        dot_k, dot_v = (_dot, _dot_nt) if transposed else (_dot_nt, _dot)
        cum = _cum_matrix(t)
        per_group = n_units // ATTN_GROUPS
        groups = [range(g, g + per_group) for g in range(0, n_units, per_group)]
        zs = [jnp.concatenate([dot_k(qm_sc[2 * u * tq:(2 * u + 2) * tq, :], k_of(u)) for u in grp], axis=0)
              for grp in groups]
        sps, sums = [], []
        for gi, grp in enumerate(groups):
            rows = slice(2 * grp[0] * tq, 2 * (grp[-1] + 1) * tq)
            sp = _softplus_pos(zs[gi])
            if visible is not None:
                sp = jnp.where(visible[rows], sp, 0.0)
            hi = sp.astype(BF16)
            lo = (sp - hi.astype(F32)).astype(BF16)
            sps.append(sp)
            sums.append(_dot(jnp.concatenate([hi, lo], axis=1), cum))
        for gi, grp in enumerate(groups):
            rows = slice(2 * grp[0] * tq, 2 * (grp[-1] + 1) * tq)
            logit = zs[gi] - sps[gi] - sums[gi][:, LANES:]
            if visible is None:
                carry = carry_sc[rows, :]
                a = jnp.exp(logit - carry)
                carry_sc[rows, :] = carry + sums[gi][:, :LANES]
            else:
                a = jnp.where(visible[rows], jnp.exp(logit), 0.0)
                carry_sc[rows, :] = sums[gi][:, :LANES]
            a = a.astype(BF16)
            for j, u in enumerate(grp):
                s, p = divmod(u, n_pairs)
                cols = slice(p * LANES, (p + 1) * LANES)
                pv = dot_v(a[2 * j * tq:(2 * j + 2) * tq, :], v_of(u))
                out = jnp.where(first_head, pv[:tq], pv[tq:])
                if visible is None:
                    acc_sc[s, :, cols] += out
                else:
                    acc_sc[s, :, cols] = out

    def unit_cols(ref):
        def of(u):
            s, p = divmod(u, n_pairs)
            return ref[s, :, p * LANES:(p + 1) * LANES]
        return of

    rd = lax.broadcasted_iota(jnp.int32, (2 * n_units * tq, tq), 0) & (tq - 1)
    cd = lax.broadcasted_iota(jnp.int32, (2 * n_units * tq, tq), 1)
    block(unit_cols(kd_ref), unit_cols(vd_ref), tq, cd < rd)

    def not_done():
        return jnp.min(carry_sc[...]) <= EXP_UNDERFLOW

    def cond(state):
        i, go = state
        return jnp.logical_and(i < n_blocks, go)

    def body(state):
        i, _ = state
        start = pl.multiple_of((n_blocks - 1 - i) * tk, tk)
        def past_of(ref):
            def of(u):
                s, p = divmod(u, n_pairs)
                if cache_past:
                    return ref[s, p * LANES:(p + 1) * LANES, pl.ds(start, tk)].astype(BF16)
                return ref[s, pl.ds(start, tk), p * LANES:(p + 1) * LANES].astype(BF16)
            return of

        block(past_of(kp_ref), past_of(vp_ref), tk, None, transposed=cache_past)
        return i + 1, not_done()

    lax.while_loop(cond, body, (jnp.int32(0), not_done()))
    o_ref[...] = acc_sc[...].astype(BF16)


def _attn(q, k_new, v_new, k_past, v_past, *, bt, tq, n_past):
    nb, seq, _ = q.shape
    assert nb % bt == 0
    tok = pl.BlockSpec((bt, tq, SB_WIDTH), lambda b, i: (b, i, 0))
    past = pl.BlockSpec((bt,) + k_past.shape[1:], lambda b, i: (b, 0, 0))
    rows = bt * SB_HEADS * tq
    return pl.pallas_call(
        functools.partial(_attn_kernel, bt=bt, tq=tq, tk=ATTN_BLOCK, n_past=n_past),
        grid=(nb // bt, seq // tq),
        in_specs=[tok, tok, tok, past, past],
        out_specs=tok,
        out_shape=jax.ShapeDtypeStruct((nb, seq, SB_WIDTH), BF16),
        scratch_shapes=[pltpu.VMEM((rows, LANES), BF16),
                        pltpu.VMEM((bt, tq, SB_WIDTH), F32),
                        pltpu.VMEM((rows, LANES), F32)],
        compiler_params=_params(2),
        name="attn",
    )(q, k_new, v_new, k_past, v_past)


def _mlstm_kernel(mq_ref, mk_ref, mv_ref, mvt_ref, mo_ref, grow_ref, c0_ref, n0_ref, m0_ref, gn_ref,
                  yb_ref, c_out_ref, n_out_ref, m_out_ref,
                  c_sc, n_sc, m_sc, alpha_sc, beta_sc, r_sc, wg_sc, dec_sc, mprev_sc, *, bt, lc, nc):
    step = pl.program_id(1)
    dk = ML_HEAD_DIM

    ri = lax.broadcasted_iota(jnp.int32, (lc, lc), 0)
    ci = lax.broadcasted_iota(jnp.int32, (lc, lc), 1)
    causal = ci <= ri
    eye = jnp.where(ri == ci, 1.0, 0.0).astype(BF16)
    ones_blk = jnp.ones((lc, dk), BF16)
    zeros_blk = jnp.zeros((lc, dk), BF16)

    @pl.when(step == 0)
    def _():
        c_sc[...] = c0_ref[...]
        n_sc[...] = n0_ref[...]
        incl_t = jnp.where(ri <= ci, 1.0, 0.0).astype(BF16)
        blocks = [(s, c) for s in range(bt) for c in range(nc)]
        li = jnp.concatenate([grow_ref[s, 0:SUBLANES, c * lc:(c + 1) * lc] for s, c in blocks], axis=0)
        lf = jnp.concatenate([grow_ref[s, SUBLANES:2 * SUBLANES, c * lc:(c + 1) * lc] for s, c in blocks],
                             axis=0)
        b = _split_dot(lf, incl_t, 3)
        r = li - b
        lane = lax.broadcasted_iota(jnp.int32, r.shape, 1)
        run = r
        d = 1
        while d < lc:
            run = jnp.maximum(run, jnp.where(lane >= d, pltpu.roll(run, d, axis=1), -jnp.inf))
            d *= 2
        b_last = b[:, lc - 1:lc]
        g = b_last - b + li
        g_max = jnp.max(g, axis=1, keepdims=True)
        m_before = []
        m_after = []
        for s in range(bt):
            m = m0_ref[s]
            for c in range(nc):
                rows = slice((s * nc + c) * SUBLANES, (s * nc + c + 1) * SUBLANES)
                m_before.append(m)
                m = jnp.maximum(b_last[rows] + m, g_max[rows])
                m_after.append(m)
            m_sc[s] = m
        m_prev = jnp.concatenate(m_before, axis=0)
        m_new = jnp.concatenate(m_after, axis=0)
        alpha = (-jnp.maximum(run, m_prev[:, :lc])).astype(BF16).astype(F32)
        alpha_sc[...] = alpha
        beta_sc[...] = alpha - b
        r_sc[...] = r
        wg_sc[...] = jnp.exp(g - m_new[:, :lc])
        dec_sc[...] = jnp.exp(b_last + m_prev - m_new)
        mprev_sc[...] = m_prev

    units = [(s, h) for s in range(bt) for h in range(ML_HEADS)]
    cols = [slice(h * dk, (h + 1) * dk) for h in range(ML_HEADS)]
    alpha, beta_hi, beta_lo, r, wg, decay, m_prev = [], [], [], [], [], [], []
    for s in range(bt):
        chunk = pl.ds(pl.multiple_of((s * nc + step) * SUBLANES, SUBLANES), SUBLANES)
        alpha.append(alpha_sc[chunk, :].astype(BF16))
        beta = beta_sc[chunk, :]
        beta_hi.append(beta.astype(BF16))
        beta_lo.append((beta - beta_hi[s].astype(F32)).astype(BF16))
        r.append(r_sc[chunk, :])
        wg.append(wg_sc[chunk, :])
        decay.append(dec_sc[chunk, :])
        m_prev.append(mprev_sc[chunk, :])
    eye2 = jnp.concatenate([eye, eye], axis=1)

    def rows_of(x, h):
        return jnp.broadcast_to(x[h:h + 1, :], (dk, lc))
    alpha_b = _dot_nt(eye, jnp.concatenate([rows_of(alpha[s], h) for s, h in units], axis=0))
    beta_b = _dot_nt(eye2, jnp.concatenate(
        [jnp.concatenate([rows_of(beta_hi[s], h), rows_of(beta_lo[s], h)], axis=1) for s, h in units], axis=0))
    qk = [_dot_nt(mq_ref[s, :, cols[h]], mk_ref[s, :, cols[h]]) for s, h in units]
    c_prev = [c_sc[s, h] for s, h in units]
    n_prev = [n_sc[s, h:h + 1, :] for s, h in units]
    qc = [_dot_nt(mq_ref[s, :, cols[h]],
                  jnp.concatenate([c_prev[u].astype(BF16),
                                   jnp.broadcast_to(n_prev[u], (dk, dk)).astype(BF16)], axis=0))
          for u, (s, h) in enumerate(units)]
    upd = []
    for s, h in units:
        wg_h = wg[s][h:h + 1, :]
        lhs = jnp.concatenate([(mvt_ref[s, cols[h], :].astype(F32) * wg_h).astype(BF16),
                               jnp.broadcast_to(wg_h, (SUBLANES, lc)).astype(BF16)], axis=0)
        upd.append(_dot(lhs, mk_ref[s, :, cols[h]]))

    cat = []
    for u, (s, h) in enumerate(units):
        w = jnp.where(causal, jnp.exp(alpha_b[:, u * dk:u * dk + lc] + r[s][h:h + 1, :]), 0.0)
        sw = qk[u] * w
        s_hi = sw.astype(BF16)
        cat.append(jnp.concatenate([s_hi, (sw - s_hi.astype(F32)).astype(BF16)], axis=1))
    nv = []
    for u, (s, h) in enumerate(units):
        rhs = jnp.concatenate([jnp.concatenate([mv_ref[s, :, cols[h]], ones_blk], axis=1),
                               jnp.concatenate([zeros_blk, ones_blk], axis=1)], axis=0)
        nv.append(_dot(cat[u], rhs))
    hval = []
    cat2 = []
    for u, (s, h) in enumerate(units):
        ucols = slice(u * dk, (u + 1) * dk)
        w_inter = jnp.exp(m_prev[s][h:h + 1, :] + alpha_b[:, ucols])
        num = nv[u][:, :dk] + w_inter * qc[u][:, :dk]
        den = nv[u][:, dk:] + w_inter * qc[u][:, dk:]
        hv = num / jnp.maximum(jnp.abs(den), jnp.exp(beta_b[:, ucols]))
        h2 = hv * hv
        h2_hi = h2.astype(BF16)
        hval.append(hv)
        cat2.append(jnp.concatenate([h2_hi, (h2 - h2_hi.astype(F32)).astype(BF16)], axis=1))
    ms = _dot(jnp.concatenate(cat2, axis=0), jnp.ones((2 * dk, dk), BF16)) * (1.0 / dk)
    for u, (s, h) in enumerate(units):
        y = hval[u] * lax.rsqrt(ms[u * lc:(u + 1) * lc] + EPS) * gn_ref[:, cols[h]]
        yb_ref[s, :, cols[h]] = (_sigmoid(mo_ref[s, :, cols[h]]) * y).astype(BF16)

    for u, (s, h) in enumerate(units):
        dec = decay[s][h:h + 1, :]
        c_sc[s, h] = dec * c_prev[u] + upd[u][:dk]
        n_sc[s, h:h + 1, :] = dec * n_prev[u] + upd[u][dk:dk + 1]

    c_out_ref[...] = c_sc[...]
    n_out_ref[...] = n_sc[...]
    m_out_ref[...] = m_sc[...]


def _mlstm(mq, mk, mv, mvt, mo, grow, c0, n0, m0, gn, *, bt, lc):
    nb, seq, _ = mq.shape
    assert nb % bt == 0
    tok = pl.BlockSpec((bt, lc, ML_WIDTH), lambda b, i: (b, i, 0))
    c_spec = pl.BlockSpec((bt, ML_HEADS, ML_HEAD_DIM, ML_HEAD_DIM), lambda b, i: (b, 0, 0, 0))
    n_spec = pl.BlockSpec((bt, ML_HEADS, ML_HEAD_DIM), lambda b, i: (b, 0, 0))
    m_spec = pl.BlockSpec((bt, SUBLANES, LANES), lambda b, i: (b, 0, 0))
    nc = seq // lc
    table = pltpu.VMEM((bt * nc * SUBLANES, lc), F32)
    table_m = pltpu.VMEM((bt * nc * SUBLANES, LANES), F32)
    return pl.pallas_call(
        functools.partial(_mlstm_kernel, bt=bt, lc=lc, nc=nc),
        grid=(nb // bt, nc),
        in_specs=[tok, tok, tok,
                  pl.BlockSpec((bt, ML_WIDTH, lc), lambda b, i: (b, 0, i)),
                  tok,
                  pl.BlockSpec((bt, 2 * SUBLANES, seq), lambda b, i: (b, 0, 0)),
                  c_spec, n_spec, m_spec, _const_spec((1, ML_WIDTH))],
        out_specs=[tok, c_spec, n_spec, m_spec],
        out_shape=[jax.ShapeDtypeStruct((nb, seq, ML_WIDTH), BF16),
                   jax.ShapeDtypeStruct((nb, ML_HEADS, ML_HEAD_DIM, ML_HEAD_DIM), F32),
                   jax.ShapeDtypeStruct((nb, ML_HEADS, ML_HEAD_DIM), F32),
                   jax.ShapeDtypeStruct((nb, SUBLANES, LANES), F32)],
        scratch_shapes=[pltpu.VMEM((bt, ML_HEADS, ML_HEAD_DIM, ML_HEAD_DIM), F32),
                        pltpu.VMEM((bt, ML_HEADS, ML_HEAD_DIM), F32),
                        pltpu.VMEM((bt, SUBLANES, LANES), F32),
                        table, table, table, table, table_m, table_m],
        compiler_params=_params(2),
        name="mlstm",
    )(mq, mk, mv, mvt, mo, grow, c0, n0, m0, gn)


def _post_kernel(x_ref, ya_ref, yb_ref, mod_ref, g1_ref, g2_ref, gf_ref,
                 wgab_ref, wa_ref, wb_ref, wout_ref, wfg_ref, wfu_ref, wfd_ref,
                 y_ref, u_sc, *, bt, lt):
    g1 = g1_ref[...]
    for j in range(bt):
        u = _norm_mod(x_ref[j], g1, mod_ref[j, 1:2, :], mod_ref[j, 0:1, :])
        u_sc[j * lt:(j + 1) * lt, :] = u.astype(BF16)
    u = u_sc[...]
    ya = ya_ref[...].reshape(bt * lt, SB_WIDTH)
    yb = yb_ref[...].reshape(bt * lt, ML_WIDTH)
    ga = _dot(u, wgab_ref[:, :D_MODEL])
    gb = _dot(u, wgab_ref[:, D_MODEL:])
    merged = _sigmoid(ga) * _dot(ya, wa_ref[...]) + _sigmoid(gb) * _dot(yb, wb_ref[...])
    attn_out = _dot(merged.astype(BF16), wout_ref[...])

    g2 = g2_ref[...]
    for j in range(bt):
        rows = slice(j * lt, (j + 1) * lt)
        x1 = x_ref[j] + mod_ref[j, 2:3, :] * attn_out[rows]
        y_ref[j] = x1
        u2 = _norm_mod(x1, g2, mod_ref[j, 4:5, :], mod_ref[j, 3:4, :])
        u_sc[rows, :] = u2.astype(BF16)
    u2 = u_sc[...]
    hg = _dot(u2, wfg_ref[...])
    hu = _dot(u2, wfu_ref[...])
    act = (hg * _sigmoid(hg) * hu).astype(BF16)
    ff = _dot(act, wfd_ref[...])
    gf = gf_ref[...]
    for j in range(bt):
        rows = slice(j * lt, (j + 1) * lt)
        x2 = y_ref[j] + mod_ref[j, 5:6, :] * ff[rows]
        ms = jnp.mean(x2 * x2, axis=-1, keepdims=True)
        y_ref[j] = x2 * lax.rsqrt(ms + EPS) * gf


def _post(x, ya, yb, mod3, g1, g2, gf, wgab, wa, wb, wout, wfg, wfu, wfd, *, bt, lt):
    nb, seq, _ = x.shape
    assert nb % bt == 0 and seq % lt == 0

    def tok(width):
        return pl.BlockSpec((bt, lt, width), lambda b, l: (b, l, 0))

    in_specs = [tok(D_MODEL), tok(SB_WIDTH), tok(ML_WIDTH),
                pl.BlockSpec((bt, 6, D_MODEL), lambda b, l: (b, 0, 0)),
                _const_spec((1, D_MODEL)), _const_spec((1, D_MODEL)), _const_spec((1, D_MODEL)),
                _const_spec(wgab.shape), _const_spec(wa.shape), _const_spec(wb.shape), _const_spec(wout.shape),
                _const_spec(wfg.shape), _const_spec(wfu.shape), _const_spec(wfd.shape)]
    return pl.pallas_call(
        functools.partial(_post_kernel, bt=bt, lt=lt),
        grid=(nb // bt, seq // lt),
        in_specs=in_specs,
        out_specs=tok(D_MODEL),
        out_shape=jax.ShapeDtypeStruct((nb, seq, D_MODEL), F32),
        scratch_shapes=[pltpu.VMEM((bt * lt, D_MODEL), BF16)],
        compiler_params=_params(2),
        name="post",
    )(x, ya, yb, mod3, g1, g2, gf, wgab, wa, wb, wout, wfg, wfu, wfd)


def _group(x, mod3, k_past, v_past, c0, n0, m0, conv0, wts, *, bt, lt_in, lt, tq, n_past, lc):
    (g1, g2, gf, w1, wmvt, wif_row, bif_row, wconv, bconv, gn,
     wgab, wa, wb, wout, wfg, wfu, wfd) = wts
    nb, seq, _ = x.shape
    (q, k32, v32, kb, vb, mq, mk, mv, mvt, mo, grow, conv_new) = _inproj(
        x, mod3, conv0, g1, w1, wmvt, wif_row, bif_row, wconv, bconv, bt=bt, lt=lt_in)
    if k_past is None:
        k_past, v_past = kb, vb
    ya = _attn(q, kb, vb, k_past, v_past, bt=ATTN_SEQS, tq=tq, n_past=n_past)
    m0b = jnp.broadcast_to(jnp.pad(m0, ((0, 0), (0, SUBLANES - ML_HEADS)))[:, :, None], (nb, SUBLANES, LANES))
    yb, c1, n1, m1 = _mlstm(mq, mk, mv, mvt, mo, grow, c0, n0, m0b, gn, bt=ML_SEQS, lc=lc)
    y = _post(x, ya, yb, mod3, g1, g2, gf, wgab, wa, wb, wout, wfg, wfu, wfd, bt=bt, lt=lt)
    states = (k32.reshape(1, nb, seq, SB_HEADS, SB_HEAD_DIM), v32.reshape(1, nb, seq, SB_HEADS, SB_HEAD_DIM),
              c1[None], n1[None], m1[None, :, :ML_HEADS, 0], conv_new[None])
    return y, states


def kernel(x_prompt, x_sample, c_prompt, c_sample, cache_sb_k, cache_sb_v, state_mlstm_C, state_mlstm_n, state_mlstm_m, state_conv, norm1_g, norm2_g, w_ada, b_ada, w_in, b_if, w_conv, b_conv, ml_norm_g, w_a, w_b, w_out, w_ff_gate, w_ff_up, w_ff_down, final_g):
    assert w_in.shape[0] == 1, "single layer"
    bp, seq_p, _ = x_prompt.shape
    bs, seq_s, _ = x_sample.shape
    past = cache_sb_k.shape[2]

    c_all = jnp.concatenate([c_prompt, c_sample], axis=0)
    mod3 = _ada(c_all, w_ada[0], b_ada[0]).reshape(bp + bs, 6, D_MODEL)

    w = w_in[0]
    w1 = w[:, :_C_IF].astype(BF16)
    wmvt = w[:, _C_MV:_C_MO].T.astype(BF16)
    pad = SUBLANES - ML_HEADS
    wif = w[:, _C_IF:_C_GAB].T
    wif_row = jnp.pad(wif.reshape(2, ML_HEADS, D_MODEL), ((0, 0), (0, pad), (0, 0)))
    wif_row = wif_row.reshape(2 * SUBLANES, D_MODEL).astype(BF16)
    bif_row = jnp.pad(b_if[0].reshape(2, ML_HEADS), ((0, 0), (0, pad))).reshape(2 * SUBLANES, 1)
    wts = (norm1_g[0].reshape(1, D_MODEL), norm2_g[0].reshape(1, D_MODEL), final_g.reshape(1, D_MODEL),
           w1, wmvt, wif_row, bif_row, w_conv[0], b_conv[0].reshape(1, 2 * ML_WIDTH),
           ml_norm_g[0].reshape(1, ML_WIDTH),
           w[:, _C_GAB:_C_END].astype(BF16), w_a[0].astype(BF16), w_b[0].astype(BF16), w_out[0].astype(BF16),
           w_ff_gate[0].astype(BF16), w_ff_up[0].astype(BF16), w_ff_down[0].astype(BF16))

    zeros = functools.partial(jnp.zeros, dtype=F32)
    y_p, st_p = _group(
        x_prompt, mod3[:bp], None, None,
        zeros((bp, ML_HEADS, ML_HEAD_DIM, ML_HEAD_DIM)), zeros((bp, ML_HEADS, ML_HEAD_DIM)),
        zeros((bp, ML_HEADS)), zeros((bp, CONV_W - 1, 2 * ML_WIDTH)), wts,
        bt=1, lt_in=512, lt=256, tq=ATTN_BLOCK, n_past=None, lc=ML_CHUNK)
    def cache_t(c):
        return jnp.transpose(c[0], (0, 2, 3, 1)).reshape(bs, SB_WIDTH, past)

    y_s, st_s = _group(
        x_sample, mod3[bp:], cache_t(cache_sb_k), cache_t(cache_sb_v),
        state_mlstm_C[0], state_mlstm_n[0], state_mlstm_m[0], state_conv[0], wts,
        bt=4, lt_in=seq_s, lt=seq_s, tq=seq_s, n_past=past // ATTN_BLOCK, lc=seq_s)
    return (y_p, y_s) + st_p + st_s
```

```python
import functools
import math

import jax
import jax.numpy as jnp
from jax import lax
from jax.experimental import pallas as pl
from jax.experimental.pallas import tpu as pltpu

D_MODEL = 1024
SB_HEADS = 8
SB_HEAD_DIM = 64
SB_WIDTH = SB_HEADS * SB_HEAD_DIM
ML_HEADS = 4
ML_HEAD_DIM = 128
ML_WIDTH = ML_HEADS * ML_HEAD_DIM
CONV_W = 4
D_FF = 2816
EPS = 1e-6
LANES = 128
SUBLANES = 8
ATTN_BLOCK = 128
ML_CHUNK = 128
ML_SEQS = 4
EXP_UNDERFLOW = 104.0
SOFTPLUS_CLAMP = 60.0
CONV_CHUNKS = 4
ATTN_SEQS = 2
ATTN_GROUPS = 4
VMEM_LIMIT = 56 * 1024 * 1024

BF16 = jnp.bfloat16
F32 = jnp.float32

_C_Q, _C_K, _C_V = 0, SB_WIDTH, 2 * SB_WIDTH
_C_MQK = 3 * SB_WIDTH
_C_MV = _C_MQK + 2 * ML_WIDTH
_C_MO = _C_MV + ML_WIDTH
_C_IF = _C_MO + ML_WIDTH
_C_GAB = _C_IF + 2 * ML_HEADS
_C_END = _C_GAB + 2 * D_MODEL


def _sigmoid(x):
    return 1.0 / (1.0 + jnp.exp(-x))


def _log_sigmoid(x):
    return jnp.minimum(x, 0.0) - jnp.log1p(jnp.exp(-jnp.abs(x)))


def _softplus_pos(x):
    return jnp.maximum(jnp.log(1.0 + jnp.exp(jnp.minimum(x, SOFTPLUS_CLAMP))), x)


def _dot(a, b):
    return jnp.dot(a, b, preferred_element_type=F32)


def _dot_nt(a, b):
    return lax.dot_general(a, b, (((1,), (1,)), ((), ())), preferred_element_type=F32)


def _dot_tn(a, b):
    return lax.dot_general(a, b, (((0,), (0,)), ((), ())), preferred_element_type=F32)


def _split_dot(x, tri, parts):
    acc = None
    rem = x
    for i in range(parts):
        piece = rem.astype(BF16)
        term = _dot(piece, tri)
        acc = term if acc is None else acc + term
        if i + 1 < parts:
            rem = rem - piece.astype(F32)
    return acc


def _split_dot_left(tri, x, parts):
    acc = None
    rem = x
    for i in range(parts):
        piece = rem.astype(BF16)
        term = _dot(tri, piece)
        acc = term if acc is None else acc + term
        if i + 1 < parts:
            rem = rem - piece.astype(F32)
    return acc


def _norm_mod(x, g, sc, sh):
    ms = jnp.mean(x * x, axis=-1, keepdims=True)
    return x * lax.rsqrt(ms + EPS) * g * (1.0 + sc) + sh


def _const_spec(shape):
    nd = len(shape)
    return pl.BlockSpec(shape, lambda *_: (0,) * nd, pipeline_mode=pl.Buffered(1))


def _params(n_axes):
    return pltpu.CompilerParams(dimension_semantics=("arbitrary",) * n_axes,
                                vmem_limit_bytes=VMEM_LIMIT)


def _ada_kernel(c_ref, w_ref, b_ref, o_ref):
    c = c_ref[...]
    s = c * _sigmoid(c)
    o_ref[...] = _dot(s.astype(BF16), w_ref[...].astype(BF16)) + b_ref[...]


def _ada(c_all, w_ada, b_ada):
    nb = c_all.shape[0]
    n_out = w_ada.shape[1]
    tn = D_MODEL
    return pl.pallas_call(
        _ada_kernel,
        grid=(n_out // tn,),
        in_specs=[pl.BlockSpec((nb, D_MODEL), lambda j: (0, 0)),
                  pl.BlockSpec((D_MODEL, tn), lambda j: (0, j)),
                  pl.BlockSpec((1, tn), lambda j: (0, j))],
        out_specs=pl.BlockSpec((nb, tn), lambda j: (0, j)),
        out_shape=jax.ShapeDtypeStruct((nb, n_out), F32),
        compiler_params=_params(1),
        name="ada",
    )(c_all, w_ada, b_ada.reshape(1, n_out))


def _prep_kernel(win_ref, wa_ref, wb_ref, wout_ref, wfg_ref, wfu_ref, wfd_ref,
                 w1_ref, wgab_ref, wmvt_ref, wa_o, wb_o, wout_o, wfg_o, wfu_o, wfd_o):
    w = win_ref[...]
    w1_ref[...] = w[:, :_C_IF].astype(BF16)
    wgab_ref[...] = w[:, _C_GAB:_C_END].astype(BF16)
    wmvt_ref[...] = w[:, _C_MV:_C_MO].T.astype(BF16)
    for src, dst in ((wa_ref, wa_o), (wb_ref, wb_o), (wout_ref, wout_o),
                     (wfg_ref, wfg_o), (wfu_ref, wfu_o), (wfd_ref, wfd_o)):
        dst[...] = src[...].astype(BF16)


def _prep(w_in, w_a, w_b, w_out, w_fg, w_fu, w_fd):
    steps = D_MODEL // LANES

    def rows(a):
        assert a.shape[0] % (steps * 2 * SUBLANES) == 0
        return pl.BlockSpec((a.shape[0] // steps, a.shape[1]), lambda i: (i, 0))

    def out(a):
        return jax.ShapeDtypeStruct(a.shape, BF16)

    ins = (w_in, w_a, w_b, w_out, w_fg, w_fu, w_fd)
    out_shape = [jax.ShapeDtypeStruct((D_MODEL, _C_IF), BF16),
                 jax.ShapeDtypeStruct((D_MODEL, _C_END - _C_GAB), BF16),
                 jax.ShapeDtypeStruct((ML_WIDTH, D_MODEL), BF16)] + [out(a) for a in ins[1:]]
    out_specs = [pl.BlockSpec((LANES, _C_IF), lambda i: (i, 0)),
                 pl.BlockSpec((LANES, _C_END - _C_GAB), lambda i: (i, 0)),
                 pl.BlockSpec((ML_WIDTH, LANES), lambda i: (0, i))] + [rows(a) for a in ins[1:]]
    return pl.pallas_call(
        _prep_kernel,
        grid=(steps,),
        in_specs=[rows(a) for a in ins],
        out_specs=out_specs,
        out_shape=out_shape,
        compiler_params=_params(1),
        name="prep",
    )(*ins)


def _inproj_kernel(x_ref, mod_ref, conv0_ref, g1_ref, w1_ref, wmvt_ref, wifr_ref, bifr_ref,
                   wconv_ref, bconv_ref,
                   q_ref, k32_ref, v32_ref, kb_ref, vb_ref, mq_ref, mk_ref, mv_ref, mvt_ref, mo_ref,
                   grow_ref, convn_ref,
                   u_sc, xp_sc, *, bt, lt):
    @pl.when(pl.program_id(1) == 0)
    def _():
        for j in range(bt):
            xp_sc[j] = jnp.concatenate(
                [jnp.zeros((SUBLANES - (CONV_W - 1), 2 * ML_WIDTH), F32), conv0_ref[j]], axis=0)

    g1 = g1_ref[...]
    for j in range(bt):
        u = _norm_mod(x_ref[j], g1, mod_ref[j, 1:2, :], mod_ref[j, 0:1, :])
        u_sc[j * lt:(j + 1) * lt, :] = u.astype(BF16)
    u = u_sc[...]

    def mm(lo, hi):
        return _dot(u, w1_ref[:, lo:hi])

    wconv = wconv_ref[...]
    bconv = bconv_ref[...]
    k_scale = 1.0 / math.sqrt(ML_HEAD_DIM)
    mqk = mm(_C_MQK, _C_MV)
    width = 2 * ML_WIDTH // CONV_CHUNKS
    row8 = lax.broadcasted_iota(jnp.int32, (SUBLANES, width), 0)

    def conv_chunk(c):
        cols = slice(c * width, (c + 1) * width)
        for j in range(bt):
            x = mqk[j * lt:(j + 1) * lt, cols]
            before = xp_sc[j, :, cols]
            conv = bconv[:, cols] + x * wconv[CONV_W - 1:CONV_W, cols]
            for k in range(1, CONV_W):
                xr = pltpu.roll(x, k, axis=0)
                head = jnp.where(row8 < k, pltpu.roll(before, k, axis=0), xr[:SUBLANES])
                xk = jnp.concatenate([head, xr[SUBLANES:]], axis=0)
                conv = conv + xk * wconv[CONV_W - 1 - k:CONV_W - k, cols]
            conv = conv * _sigmoid(conv)
            if c < CONV_CHUNKS // 2:
                mq_ref[j, :, cols] = conv.astype(BF16)
            else:
                mk_ref[j, :, c * width - ML_WIDTH:(c + 1) * width - ML_WIDTH] = (conv * k_scale).astype(BF16)

    sq = mm(_C_Q, _C_K)
    conv_chunk(0)
    for j in range(bt):
        q_ref[j] = sq[j * lt:(j + 1) * lt].astype(BF16)
    sk = mm(_C_K, _C_V)
    conv_chunk(1)
    for j in range(bt):
        k32_ref[j] = sk[j * lt:(j + 1) * lt]
        kb_ref[j] = sk[j * lt:(j + 1) * lt].astype(BF16)
    sv = mm(_C_V, _C_MQK)
    conv_chunk(2)
    for j in range(bt):
        v32_ref[j] = sv[j * lt:(j + 1) * lt]
        vb_ref[j] = sv[j * lt:(j + 1) * lt].astype(BF16)
    mv = mm(_C_MV, _C_MO)
    conv_chunk(3)
    for j in range(bt):
        mv_ref[j] = mv[j * lt:(j + 1) * lt].astype(BF16)
        convn_ref[j] = mqk[(j + 1) * lt - (CONV_W - 1):(j + 1) * lt]
        xp_sc[j] = mqk[(j + 1) * lt - SUBLANES:(j + 1) * lt]
    mo = mm(_C_MO, _C_IF)
    for j in range(bt):
        rows = slice(j * lt, (j + 1) * lt)
        mo_ref[j] = mo[rows]
        mvt_ref[j] = mv[rows].T.astype(BF16)
        gr = _dot_nt(wifr_ref[...], u_sc[rows, :]) + bifr_ref[...]
        row = lax.broadcasted_iota(jnp.int32, gr.shape, 0)
        grow_ref[j] = jnp.where(row >= SUBLANES, _log_sigmoid(gr), gr)


def _inproj(x, mod3, conv0, g1, w1, wmvt, wif_row, bif_row, wconv, bconv, *, bt, lt):
    nb, seq, _ = x.shape
    assert nb % bt == 0 and seq % lt == 0 and (bt == 1 or lt == seq)
    grid = (nb // bt, seq // lt)
    n1 = w1.shape[1]

    def tok(width):
        return pl.BlockSpec((bt, lt, width), lambda b, l: (b, l, 0))

    def per_batch(rows, width):
        return pl.BlockSpec((bt, rows, width), lambda b, l: (b, 0, 0))

    in_specs = [tok(D_MODEL), per_batch(6, D_MODEL), per_batch(CONV_W - 1, 2 * ML_WIDTH),
                _const_spec((1, D_MODEL)), _const_spec((D_MODEL, n1)),
                _const_spec((ML_WIDTH, D_MODEL)), _const_spec((2 * SUBLANES, D_MODEL)),
                _const_spec((2 * SUBLANES, 1)),
                _const_spec((CONV_W, 2 * ML_WIDTH)), _const_spec((1, 2 * ML_WIDTH))]

    def tok_on_lanes(rows):
        return pl.BlockSpec((bt, rows, lt), lambda b, l: (b, 0, l))

    out_specs = [tok(SB_WIDTH)] * 5 + [tok(ML_WIDTH)] * 3 + [
        tok_on_lanes(ML_WIDTH), tok(ML_WIDTH), tok_on_lanes(2 * SUBLANES),
        per_batch(CONV_W - 1, 2 * ML_WIDTH)]

    def sds(shape, dt):
        return jax.ShapeDtypeStruct(shape, dt)

    out_shape = [sds((nb, seq, SB_WIDTH), BF16), sds((nb, seq, SB_WIDTH), F32), sds((nb, seq, SB_WIDTH), F32),
                 sds((nb, seq, SB_WIDTH), BF16), sds((nb, seq, SB_WIDTH), BF16),
                 sds((nb, seq, ML_WIDTH), BF16), sds((nb, seq, ML_WIDTH), BF16), sds((nb, seq, ML_WIDTH), BF16),
                 sds((nb, ML_WIDTH, seq), BF16), sds((nb, seq, ML_WIDTH), F32),
                 sds((nb, 2 * SUBLANES, seq), F32),
                 sds((nb, CONV_W - 1, 2 * ML_WIDTH), F32)]
    return pl.pallas_call(
        functools.partial(_inproj_kernel, bt=bt, lt=lt),
        grid=grid,
        in_specs=in_specs,
        out_specs=out_specs,
        out_shape=out_shape,
        scratch_shapes=[pltpu.VMEM((bt * lt, D_MODEL), BF16),
                        pltpu.VMEM((bt, SUBLANES, 2 * ML_WIDTH), F32)],
        compiler_params=_params(2),
        name="inproj",
    )(x, mod3, conv0, g1, w1, wmvt, wif_row, bif_row, wconv, bconv)


def _cum_matrix(t):
    r = lax.broadcasted_iota(jnp.int32, (2 * t, LANES + t), 0)
    c = lax.broadcasted_iota(jnp.int32, (2 * t, LANES + t), 1)
    return jnp.where((c < LANES) | ((r & (t - 1)) > (c - LANES)), 1.0, 0.0).astype(BF16)


def _attn_kernel(q_ref, kd_ref, vd_ref, kp_ref, vp_ref, o_ref, qm_sc, acc_sc, carry_sc, *, bt, tq, tk, n_past):
    cache_past = n_past is not None
    n_blocks = n_past if cache_past else pl.program_id(1)
    scale = 1.0 / math.sqrt(SB_HEAD_DIM)
    n_pairs = SB_WIDTH // LANES
    lane = lax.broadcasted_iota(jnp.int32, (1, LANES), 1)
    first_head = lane < SB_HEAD_DIM

    n_units = bt * n_pairs
    for u in range(n_units):
        s, p = divmod(u, n_pairs)
        qp = q_ref[s, :, p * LANES:(p + 1) * LANES] * scale
        zero = jnp.zeros_like(qp)
        qm_sc[(2 * u) * tq:(2 * u + 1) * tq, :] = jnp.where(first_head, qp, zero)
        qm_sc[(2 * u + 1) * tq:(2 * u + 2) * tq, :] = jnp.where(first_head, zero, qp)

    per_group = n_units // ATTN_GROUPS
    groups = [range(g, g + per_group) for g in range(0, n_units, per_group)]
    group_rows = [slice(2 * grp[0] * tq, 2 * (grp[-1] + 1) * tq) for grp in groups]

    def scores(k_of, transposed):
        dot_k = _dot if transposed else _dot_nt
        return [jnp.concatenate([dot_k(qm_sc[2 * u * tq:(2 * u + 2) * tq, :], k_of(u)) for u in grp], axis=0)
                for grp in groups]

    def weights(zs, t, visible):
        cum = _cum_matrix(t)
        sps, sums = [], []
        for gi, rows in enumerate(group_rows):
            sp = _softplus_pos(zs[gi])
            if visible is not None:
                sp = jnp.where(visible[rows], sp, 0.0)
            hi = sp.astype(BF16)
            lo = (sp - hi.astype(F32)).astype(BF16)
            sps.append(sp)
            sums.append(_dot(jnp.concatenate([hi, lo], axis=1), cum))
        out = []
        for gi, rows in enumerate(group_rows):
            logit = zs[gi] - sps[gi] - sums[gi][:, LANES:]
            if visible is None:
                carry = carry_sc[rows, :]
                a = jnp.exp(logit - carry)
                carry_sc[rows, :] = carry + sums[gi][:, :LANES]
            else:
                a = jnp.where(visible[rows], jnp.exp(logit), 0.0)
                carry_sc[rows, :] = sums[gi][:, :LANES]
            out.append(a.astype(BF16))
        return out

    def accumulate(a_groups, v_of, transposed, assign=False):
        dot_v = _dot_nt if transposed else _dot
        for gi, grp in enumerate(groups):
            for j, u in enumerate(grp):
                s, p = divmod(u, n_pairs)
                cols = slice(p * LANES, (p + 1) * LANES)
                pv = dot_v(a_groups[gi][2 * j * tq:(2 * j + 2) * tq, :], v_of(u))
                out = jnp.where(first_head, pv[:tq], pv[tq:])
                if assign:
                    acc_sc[s, :, cols] = out
                else:
                    acc_sc[s, :, cols] += out

    def unit_cols(ref):
        def of(u):
            s, p = divmod(u, n_pairs)
            return ref[s, :, p * LANES:(p + 1) * LANES]
        return of

    def past_of(ref, block_index):
        start = pl.multiple_of(block_index * tk, tk)

        def of(u):
            s, p = divmod(u, n_pairs)
            if cache_past:
                return ref[s, p * LANES:(p + 1) * LANES, pl.ds(start, tk)].astype(BF16)
            return ref[s, pl.ds(start, tk), p * LANES:(p + 1) * LANES].astype(BF16)
        return of

    rd = lax.broadcasted_iota(jnp.int32, (2 * n_units * tq, tq), 0) & (tq - 1)
    cd = lax.broadcasted_iota(jnp.int32, (2 * n_units * tq, tq), 1)
    a_diag = weights(scores(unit_cols(kd_ref), False), tq, cd < rd)
    accumulate(a_diag, unit_cols(vd_ref), False, assign=True)

    def not_done():
        return jnp.min(carry_sc[...]) <= EXP_UNDERFLOW

    def cond(state):
        i, go = state
        return jnp.logical_and(i < n_blocks, go)

    def body(state):
        i, _ = state
        block_index = n_blocks - 1 - i
        a = weights(scores(past_of(kp_ref, block_index), cache_past), tk, None)
        accumulate(a, past_of(vp_ref, block_index), cache_past)
        return i + 1, not_done()

    lax.while_loop(cond, body, (jnp.int32(0), not_done()))
    o_ref[...] = acc_sc[...].astype(BF16)


def _attn(q, k_new, v_new, k_past, v_past, *, bt, tq, n_past):
    nb, seq, _ = q.shape
    assert nb % bt == 0
    tok = pl.BlockSpec((bt, tq, SB_WIDTH), lambda b, i: (b, i, 0))
    past = pl.BlockSpec((bt,) + k_past.shape[1:], lambda b, i: (b, 0, 0))
    rows = bt * SB_HEADS * tq
    return pl.pallas_call(
        functools.partial(_attn_kernel, bt=bt, tq=tq, tk=ATTN_BLOCK, n_past=n_past),
        grid=(nb // bt, seq // tq),
        in_specs=[tok, tok, tok, past, past],
        out_specs=tok,
        out_shape=jax.ShapeDtypeStruct((nb, seq, SB_WIDTH), BF16),
        scratch_shapes=[pltpu.VMEM((rows, LANES), BF16),
                        pltpu.VMEM((bt, tq, SB_WIDTH), F32),
                        pltpu.VMEM((rows, LANES), F32)],
        compiler_params=_params(2),
        name="attn",
    )(q, k_new, v_new, k_past, v_past)


def _mlstm_kernel(mq_ref, mk_ref, mv_ref, mvt_ref, mo_ref, grow_ref, c0_ref, n0_ref, m0_ref, gn_ref,
                  yb_ref, c_out_ref, n_out_ref, m_out_ref,
                  c_sc, n_sc, m_sc, alpha_sc, beta_sc, r_sc, wg_sc, dec_sc, mprev_sc, *, bt, lc, nc):
    step = pl.program_id(1)
    dk = ML_HEAD_DIM

    ri = lax.broadcasted_iota(jnp.int32, (lc, lc), 0)
    ci = lax.broadcasted_iota(jnp.int32, (lc, lc), 1)
    causal = ci <= ri
    eye = jnp.where(ri == ci, 1.0, 0.0).astype(BF16)
    ones_blk = jnp.ones((lc, dk), BF16)
    zeros_blk = jnp.zeros((lc, dk), BF16)

    @pl.when(step == 0)
    def _():
        c_sc[...] = c0_ref[...]
        n_sc[...] = n0_ref[...]
        incl_t = jnp.where(ri <= ci, 1.0, 0.0).astype(BF16)
        blocks = [(s, c) for s in range(bt) for c in range(nc)]
        li = jnp.concatenate([grow_ref[s, 0:SUBLANES, c * lc:(c + 1) * lc] for s, c in blocks], axis=0)
        lf = jnp.concatenate([grow_ref[s, SUBLANES:2 * SUBLANES, c * lc:(c + 1) * lc] for s, c in blocks],
                             axis=0)
        b = _split_dot(lf, incl_t, 3)
        r = li - b
        lane = lax.broadcasted_iota(jnp.int32, r.shape, 1)
        run = r
        d = 1
        while d < lc:
            run = jnp.maximum(run, jnp.where(lane >= d, pltpu.roll(run, d, axis=1), -jnp.inf))
            d *= 2
        b_last = b[:, lc - 1:lc]
        g = b_last - b + li
        g_max = jnp.max(g, axis=1, keepdims=True)
        m_before = []
        m_after = []
        for s in range(bt):
            m = m0_ref[s]
            for c in range(nc):
                rows = slice((s * nc + c) * SUBLANES, (s * nc + c + 1) * SUBLANES)
                m_before.append(m)
                m = jnp.maximum(b_last[rows] + m, g_max[rows])
                m_after.append(m)
            m_sc[s] = m
        m_prev = jnp.concatenate(m_before, axis=0)
        m_new = jnp.concatenate(m_after, axis=0)
        alpha = (-jnp.maximum(run, m_prev[:, :lc])).astype(BF16).astype(F32)
        alpha_sc[...] = alpha
        beta_sc[...] = alpha - b
        r_sc[...] = r
        wg_sc[...] = jnp.exp(g - m_new[:, :lc])
        dec_sc[...] = jnp.exp(b_last + m_prev - m_new)
        mprev_sc[...] = m_prev

    units = [(s, h) for s in range(bt) for h in range(ML_HEADS)]
    cols = [slice(h * dk, (h + 1) * dk) for h in range(ML_HEADS)]
    alpha, beta_hi, beta_lo, r, wg, decay, m_prev = [], [], [], [], [], [], []
    for s in range(bt):
        chunk = pl.ds(pl.multiple_of((s * nc + step) * SUBLANES, SUBLANES), SUBLANES)
        alpha.append(alpha_sc[chunk, :].astype(BF16))
        beta = beta_sc[chunk, :]
        beta_hi.append(beta.astype(BF16))
        beta_lo.append((beta - beta_hi[s].astype(F32)).astype(BF16))
        r.append(r_sc[chunk, :])
        wg.append(wg_sc[chunk, :])
        decay.append(dec_sc[chunk, :])
        m_prev.append(mprev_sc[chunk, :])
    eye2 = jnp.concatenate([eye, eye], axis=1)

    def rows_of(x, h):
        return jnp.broadcast_to(x[h:h + 1, :], (dk, lc))
    alpha_b = _dot_nt(eye, jnp.concatenate([rows_of(alpha[s], h) for s, h in units], axis=0))
    beta_b = _dot_nt(eye2, jnp.concatenate(
        [jnp.concatenate([rows_of(beta_hi[s], h), rows_of(beta_lo[s], h)], axis=1) for s, h in units], axis=0))
    qk = [_dot_nt(mq_ref[s, :, cols[h]], mk_ref[s, :, cols[h]]) for s, h in units]
    c_prev = [c_sc[s, h] for s, h in units]
    n_prev = [n_sc[s, h:h + 1, :] for s, h in units]
    qc = [_dot_nt(mq_ref[s, :, cols[h]],
                  jnp.concatenate([c_prev[u].astype(BF16),
                                   jnp.broadcast_to(n_prev[u], (dk, dk)).astype(BF16)], axis=0))
          for u, (s, h) in enumerate(units)]
    upd = []
    for s, h in units:
        wg_h = wg[s][h:h + 1, :]
        lhs = jnp.concatenate([(mvt_ref[s, cols[h], :].astype(F32) * wg_h).astype(BF16),
                               jnp.broadcast_to(wg_h, (SUBLANES, lc)).astype(BF16)], axis=0)
        upd.append(_dot(lhs, mk_ref[s, :, cols[h]]))

    cat = []
    for u, (s, h) in enumerate(units):
        w = jnp.where(causal, jnp.exp(alpha_b[:, u * dk:u * dk + lc] + r[s][h:h + 1, :]), 0.0)
        sw = qk[u] * w
        s_hi = sw.astype(BF16)
        cat.append(jnp.concatenate([s_hi, (sw - s_hi.astype(F32)).astype(BF16)], axis=1))
    nv = []
    for u, (s, h) in enumerate(units):
        rhs = jnp.concatenate([jnp.concatenate([mv_ref[s, :, cols[h]], ones_blk], axis=1),
                               jnp.concatenate([zeros_blk, ones_blk], axis=1)], axis=0)
        nv.append(_dot(cat[u], rhs))
    hval = []
    cat2 = []
    for u, (s, h) in enumerate(units):
        ucols = slice(u * dk, (u + 1) * dk)
        w_inter = jnp.exp(m_prev[s][h:h + 1, :] + alpha_b[:, ucols])
        num = nv[u][:, :dk] + w_inter * qc[u][:, :dk]
        den = nv[u][:, dk:] + w_inter * qc[u][:, dk:]
        hv = num / jnp.maximum(jnp.abs(den), jnp.exp(beta_b[:, ucols]))
        h2 = hv * hv
        h2_hi = h2.astype(BF16)
        hval.append(hv)
        cat2.append(jnp.concatenate([h2_hi, (h2 - h2_hi.astype(F32)).astype(BF16)], axis=1))
    ms = _dot(jnp.concatenate(cat2, axis=0), jnp.ones((2 * dk, dk), BF16)) * (1.0 / dk)
    for u, (s, h) in enumerate(units):
        y = hval[u] * lax.rsqrt(ms[u * lc:(u + 1) * lc] + EPS) * gn_ref[:, cols[h]]
        yb_ref[s, :, cols[h]] = (_sigmoid(mo_ref[s, :, cols[h]]) * y).astype(BF16)

    for u, (s, h) in enumerate(units):
        dec = decay[s][h:h + 1, :]
        c_sc[s, h] = dec * c_prev[u] + upd[u][:dk]
        n_sc[s, h:h + 1, :] = dec * n_prev[u] + upd[u][dk:dk + 1]

    c_out_ref[...] = c_sc[...]
    n_out_ref[...] = n_sc[...]
    m_out_ref[...] = m_sc[...]


def _mlstm(mq, mk, mv, mvt, mo, grow, c0, n0, m0, gn, *, bt, lc):
    nb, seq, _ = mq.shape
    assert nb % bt == 0
    tok = pl.BlockSpec((bt, lc, ML_WIDTH), lambda b, i: (b, i, 0))
    c_spec = pl.BlockSpec((bt, ML_HEADS, ML_HEAD_DIM, ML_HEAD_DIM), lambda b, i: (b, 0, 0, 0))
    n_spec = pl.BlockSpec((bt, ML_HEADS, ML_HEAD_DIM), lambda b, i: (b, 0, 0))
    m_spec = pl.BlockSpec((bt, SUBLANES, LANES), lambda b, i: (b, 0, 0))
    nc = seq // lc
    table = pltpu.VMEM((bt * nc * SUBLANES, lc), F32)
    table_m = pltpu.VMEM((bt * nc * SUBLANES, LANES), F32)
    return pl.pallas_call(
        functools.partial(_mlstm_kernel, bt=bt, lc=lc, nc=nc),
        grid=(nb // bt, nc),
        in_specs=[tok, tok, tok,
                  pl.BlockSpec((bt, ML_WIDTH, lc), lambda b, i: (b, 0, i)),
                  tok,
                  pl.BlockSpec((bt, 2 * SUBLANES, seq), lambda b, i: (b, 0, 0)),
                  c_spec, n_spec, m_spec, _const_spec((1, ML_WIDTH))],
        out_specs=[tok, c_spec, n_spec, m_spec],
        out_shape=[jax.ShapeDtypeStruct((nb, seq, ML_WIDTH), BF16),
                   jax.ShapeDtypeStruct((nb, ML_HEADS, ML_HEAD_DIM, ML_HEAD_DIM), F32),
                   jax.ShapeDtypeStruct((nb, ML_HEADS, ML_HEAD_DIM), F32),
                   jax.ShapeDtypeStruct((nb, SUBLANES, LANES), F32)],
        scratch_shapes=[pltpu.VMEM((bt, ML_HEADS, ML_HEAD_DIM, ML_HEAD_DIM), F32),
                        pltpu.VMEM((bt, ML_HEADS, ML_HEAD_DIM), F32),
                        pltpu.VMEM((bt, SUBLANES, LANES), F32),
                        table, table, table, table, table_m, table_m],
        compiler_params=_params(2),
        name="mlstm",
    )(mq, mk, mv, mvt, mo, grow, c0, n0, m0, gn)


def _post_kernel(x_ref, ya_ref, yb_ref, mod_ref, g1_ref, g2_ref, gf_ref,
                 wgab_ref, wa_ref, wb_ref, wout_ref, wfg_ref, wfu_ref, wfd_ref,
                 y_ref, u_sc, *, bt, lt):
    g1 = g1_ref[...]
    for j in range(bt):
        u = _norm_mod(x_ref[j], g1, mod_ref[j, 1:2, :], mod_ref[j, 0:1, :])
        u_sc[j * lt:(j + 1) * lt, :] = u.astype(BF16)
    u = u_sc[...]
    ya = ya_ref[...].reshape(bt * lt, SB_WIDTH)
    yb = yb_ref[...].reshape(bt * lt, ML_WIDTH)
    ga = _dot(u, wgab_ref[:, :D_MODEL])
    gb = _dot(u, wgab_ref[:, D_MODEL:])
    merged = _sigmoid(ga) * _dot(ya, wa_ref[...]) + _sigmoid(gb) * _dot(yb, wb_ref[...])
    attn_out = _dot(merged.astype(BF16), wout_ref[...])

    g2 = g2_ref[...]
    for j in range(bt):
        rows = slice(j * lt, (j + 1) * lt)
        x1 = x_ref[j] + mod_ref[j, 2:3, :] * attn_out[rows]
        y_ref[j] = x1
        u2 = _norm_mod(x1, g2, mod_ref[j, 4:5, :], mod_ref[j, 3:4, :])
        u_sc[rows, :] = u2.astype(BF16)
    u2 = u_sc[...]
    hg = _dot(u2, wfg_ref[...])
    hu = _dot(u2, wfu_ref[...])
    act = (hg * _sigmoid(hg) * hu).astype(BF16)
    ff = _dot(act, wfd_ref[...])
    gf = gf_ref[...]
    for j in range(bt):
        rows = slice(j * lt, (j + 1) * lt)
        x2 = y_ref[j] + mod_ref[j, 5:6, :] * ff[rows]
        ms = jnp.mean(x2 * x2, axis=-1, keepdims=True)
        y_ref[j] = x2 * lax.rsqrt(ms + EPS) * gf


def _post(x, ya, yb, mod3, g1, g2, gf, wgab, wa, wb, wout, wfg, wfu, wfd, *, bt, lt):
    nb, seq, _ = x.shape
    assert nb % bt == 0 and seq % lt == 0

    def tok(width):
        return pl.BlockSpec((bt, lt, width), lambda b, l: (b, l, 0))

    in_specs = [tok(D_MODEL), tok(SB_WIDTH), tok(ML_WIDTH),
                pl.BlockSpec((bt, 6, D_MODEL), lambda b, l: (b, 0, 0)),
                _const_spec((1, D_MODEL)), _const_spec((1, D_MODEL)), _const_spec((1, D_MODEL)),
                _const_spec(wgab.shape), _const_spec(wa.shape), _const_spec(wb.shape), _const_spec(wout.shape),
                _const_spec(wfg.shape), _const_spec(wfu.shape), _const_spec(wfd.shape)]
    return pl.pallas_call(
        functools.partial(_post_kernel, bt=bt, lt=lt),
        grid=(nb // bt, seq // lt),
        in_specs=in_specs,
        out_specs=tok(D_MODEL),
        out_shape=jax.ShapeDtypeStruct((nb, seq, D_MODEL), F32),
        scratch_shapes=[pltpu.VMEM((bt * lt, D_MODEL), BF16)],
        compiler_params=_params(2),
        name="post",
    )(x, ya, yb, mod3, g1, g2, gf, wgab, wa, wb, wout, wfg, wfu, wfd)


def _group(x, mod3, k_past, v_past, c0, n0, m0, conv0, wts, *, bt, lt_in, lt, tq, n_past, lc):
    (g1, g2, gf, w1, wmvt, wif_row, bif_row, wconv, bconv, gn,
     wgab, wa, wb, wout, wfg, wfu, wfd) = wts
    nb, seq, _ = x.shape
    (q, k32, v32, kb, vb, mq, mk, mv, mvt, mo, grow, conv_new) = _inproj(
        x, mod3, conv0, g1, w1, wmvt, wif_row, bif_row, wconv, bconv, bt=bt, lt=lt_in)
    if k_past is None:
        k_past, v_past = kb, vb
    ya = _attn(q, kb, vb, k_past, v_past, bt=ATTN_SEQS, tq=tq, n_past=n_past)
    m0b = jnp.broadcast_to(jnp.pad(m0, ((0, 0), (0, SUBLANES - ML_HEADS)))[:, :, None], (nb, SUBLANES, LANES))
    yb, c1, n1, m1 = _mlstm(mq, mk, mv, mvt, mo, grow, c0, n0, m0b, gn, bt=ML_SEQS, lc=lc)
    y = _post(x, ya, yb, mod3, g1, g2, gf, wgab, wa, wb, wout, wfg, wfu, wfd, bt=bt, lt=lt)
    states = (k32.reshape(1, nb, seq, SB_HEADS, SB_HEAD_DIM), v32.reshape(1, nb, seq, SB_HEADS, SB_HEAD_DIM),
              c1[None], n1[None], m1[None, :, :ML_HEADS, 0], conv_new[None])
    return y, states


def kernel(x_prompt, x_sample, c_prompt, c_sample, cache_sb_k, cache_sb_v, state_mlstm_C, state_mlstm_n, state_mlstm_m, state_conv, norm1_g, norm2_g, w_ada, b_ada, w_in, b_if, w_conv, b_conv, ml_norm_g, w_a, w_b, w_out, w_ff_gate, w_ff_up, w_ff_down, final_g):
    assert w_in.shape[0] == 1, "single layer"
    bp, seq_p, _ = x_prompt.shape
    bs, seq_s, _ = x_sample.shape
    past = cache_sb_k.shape[2]

    c_all = jnp.concatenate([c_prompt, c_sample], axis=0)
    mod3 = _ada(c_all, w_ada[0], b_ada[0]).reshape(bp + bs, 6, D_MODEL)

    w = w_in[0]
    w1, wgab, wmvt, wa, wb, wout, wfg, wfu, wfd = _prep(
        w, w_a[0], w_b[0], w_out[0], w_ff_gate[0], w_ff_up[0], w_ff_down[0])
    pad = SUBLANES - ML_HEADS
    wif = w[:, _C_IF:_C_GAB].T
    wif_row = jnp.pad(wif.reshape(2, ML_HEADS, D_MODEL), ((0, 0), (0, pad), (0, 0)))
    wif_row = wif_row.reshape(2 * SUBLANES, D_MODEL).astype(BF16)
    bif_row = jnp.pad(b_if[0].reshape(2, ML_HEADS), ((0, 0), (0, pad))).reshape(2 * SUBLANES, 1)
    wts = (norm1_g[0].reshape(1, D_MODEL), norm2_g[0].reshape(1, D_MODEL), final_g.reshape(1, D_MODEL),
           w1, wmvt, wif_row, bif_row, w_conv[0], b_conv[0].reshape(1, 2 * ML_WIDTH),
           ml_norm_g[0].reshape(1, ML_WIDTH),
           wgab, wa, wb, wout, wfg, wfu, wfd)

    zeros = functools.partial(jnp.zeros, dtype=F32)
    y_p, st_p = _group(
        x_prompt, mod3[:bp], None, None,
        zeros((bp, ML_HEADS, ML_HEAD_DIM, ML_HEAD_DIM)), zeros((bp, ML_HEADS, ML_HEAD_DIM)),
        zeros((bp, ML_HEADS)), zeros((bp, CONV_W - 1, 2 * ML_WIDTH)), wts,
        bt=1, lt_in=512, lt=256, tq=ATTN_BLOCK, n_past=None, lc=ML_CHUNK)
    def cache_t(c):
        return jnp.transpose(c[0], (0, 2, 3, 1)).reshape(bs, SB_WIDTH, past)

    y_s, st_s = _group(
        x_sample, mod3[bp:], cache_t(cache_sb_k), cache_t(cache_sb_v),
        state_mlstm_C[0], state_mlstm_n[0], state_mlstm_m[0], state_conv[0], wts,
        bt=4, lt_in=seq_s, lt=seq_s, tq=seq_s, n_past=past // ATTN_BLOCK, lc=seq_s)
    return (y_p, y_s) + st_p + st_s
```

```python
import functools
import math

import jax
import jax.numpy as jnp
from jax import lax
from jax.experimental import pallas as pl
from jax.experimental.pallas import tpu as pltpu

D_MODEL = 1024
SB_HEADS = 8
SB_HEAD_DIM = 64
SB_WIDTH = SB_HEADS * SB_HEAD_DIM
ML_HEADS = 4
ML_HEAD_DIM = 128
ML_WIDTH = ML_HEADS * ML_HEAD_DIM
CONV_W = 4
D_FF = 2816
EPS = 1e-6
LANES = 128
SUBLANES = 8
ATTN_BLOCK = 128
ML_CHUNK = 128
ML_SEQS = 4
EXP_UNDERFLOW = 104.0
SOFTPLUS_CLAMP = 60.0
CONV_CHUNKS = 4
ATTN_SEQS = 2
ATTN_GROUPS = 4
VMEM_LIMIT = 56 * 1024 * 1024

BF16 = jnp.bfloat16
F32 = jnp.float32

_C_Q, _C_K, _C_V = 0, SB_WIDTH, 2 * SB_WIDTH
_C_MQK = 3 * SB_WIDTH
_C_MV = _C_MQK + 2 * ML_WIDTH
_C_MO = _C_MV + ML_WIDTH
_C_IF = _C_MO + ML_WIDTH
_C_GAB = _C_IF + 2 * ML_HEADS
_C_END = _C_GAB + 2 * D_MODEL


def _sigmoid(x):
    return 1.0 / (1.0 + jnp.exp(-x))


def _log_sigmoid(x):
    return jnp.minimum(x, 0.0) - jnp.log1p(jnp.exp(-jnp.abs(x)))


def _softplus_pos(x):
    return jnp.maximum(jnp.log(1.0 + jnp.exp(jnp.minimum(x, SOFTPLUS_CLAMP))), x)


def _dot(a, b):
    return jnp.dot(a, b, preferred_element_type=F32)


def _dot_nt(a, b):
    return lax.dot_general(a, b, (((1,), (1,)), ((), ())), preferred_element_type=F32)


def _dot_tn(a, b):
    return lax.dot_general(a, b, (((0,), (0,)), ((), ())), preferred_element_type=F32)


def _split_dot(x, tri, parts):
    acc = None
    rem = x
    for i in range(parts):
        piece = rem.astype(BF16)
        term = _dot(piece, tri)
        acc = term if acc is None else acc + term
        if i + 1 < parts:
            rem = rem - piece.astype(F32)
    return acc


def _split_dot_left(tri, x, parts):
    acc = None
    rem = x
    for i in range(parts):
        piece = rem.astype(BF16)
        term = _dot(tri, piece)
        acc = term if acc is None else acc + term
        if i + 1 < parts:
            rem = rem - piece.astype(F32)
    return acc


def _norm_mod(x, g, sc, sh):
    ms = jnp.mean(x * x, axis=-1, keepdims=True)
    return x * lax.rsqrt(ms + EPS) * g * (1.0 + sc) + sh


def _const_spec(shape):
    nd = len(shape)
    return pl.BlockSpec(shape, lambda *_: (0,) * nd, pipeline_mode=pl.Buffered(1))


def _params(n_axes):
    return pltpu.CompilerParams(dimension_semantics=("arbitrary",) * n_axes,
                                vmem_limit_bytes=VMEM_LIMIT)


def _ada_kernel(c_ref, w_ref, b_ref, o_ref):
    c = c_ref[...]
    s = c * _sigmoid(c)
    o_ref[...] = _dot(s.astype(BF16), w_ref[...].astype(BF16)) + b_ref[...]


def _ada(c_all, w_ada, b_ada):
    nb = c_all.shape[0]
    n_out = w_ada.shape[1]
    tn = D_MODEL
    return pl.pallas_call(
        _ada_kernel,
        grid=(n_out // tn,),
        in_specs=[pl.BlockSpec((nb, D_MODEL), lambda j: (0, 0)),
                  pl.BlockSpec((D_MODEL, tn), lambda j: (0, j)),
                  pl.BlockSpec((1, tn), lambda j: (0, j))],
        out_specs=pl.BlockSpec((nb, tn), lambda j: (0, j)),
        out_shape=jax.ShapeDtypeStruct((nb, n_out), F32),
        compiler_params=_params(1),
        name="ada",
    )(c_all, w_ada, b_ada.reshape(1, n_out))


PREP_STEPS = 8


def _prep_kernel(w1_ref, wgab_ref, wif_ref, wa_ref, wb_ref, wout_ref, wfg_ref, wfu_ref, wfd_ref,
                 w1_o, wgab_o, wif_o, wa_o, wb_o, wout_o, wfg_o, wfu_o, wfd_o):
    for src, dst in ((w1_ref, w1_o), (wgab_ref, wgab_o), (wa_ref, wa_o), (wb_ref, wb_o), (wout_ref, wout_o),
                     (wfg_ref, wfg_o), (wfu_ref, wfu_o), (wfd_ref, wfd_o)):
        dst[...] = src[...].astype(BF16)
    gates = wif_ref[...]
    pad = jnp.zeros((SUBLANES - ML_HEADS, D_MODEL), F32)
    wif_o[...] = jnp.concatenate([gates[:ML_HEADS], pad, gates[ML_HEADS:], pad], axis=0).astype(BF16)


def _prep(w_in_t, w_a, w_b, w_out, w_fg, w_fu, w_fd):
    def rows(a, n_rows=None, first=0):
        n_rows = a.shape[0] if n_rows is None else n_rows
        step = n_rows // PREP_STEPS
        assert n_rows % (PREP_STEPS * 2 * SUBLANES) == 0 and first % SUBLANES == 0
        if first == 0:
            return pl.BlockSpec((step, a.shape[1]), lambda i: (i, 0))
        return pl.BlockSpec((pl.Element(step), pl.Element(a.shape[1])),
                            lambda i: (pl.multiple_of(first + i * step, SUBLANES), 0))

    def out(n_rows, n_cols):
        return (jax.ShapeDtypeStruct((n_rows, n_cols), BF16),
                pl.BlockSpec((n_rows // PREP_STEPS, n_cols), lambda i: (i, 0)))

    others = (w_a, w_b, w_out, w_fg, w_fu, w_fd)
    outs = [out(_C_IF, D_MODEL), out(_C_END - _C_GAB, D_MODEL),
            (jax.ShapeDtypeStruct((2 * SUBLANES, D_MODEL), BF16),
             pl.BlockSpec((2 * SUBLANES, D_MODEL), lambda i: (0, 0)))] + [out(*a.shape) for a in others]
    in_specs = [rows(w_in_t, _C_IF), rows(w_in_t, _C_END - _C_GAB, _C_GAB),
                pl.BlockSpec((pl.Element(2 * ML_HEADS), pl.Element(D_MODEL)), lambda i: (_C_IF, 0))
                ] + [rows(a) for a in others]
    return pl.pallas_call(
        _prep_kernel,
        grid=(PREP_STEPS,),
        in_specs=in_specs,
        out_specs=[o[1] for o in outs],
        out_shape=[o[0] for o in outs],
        compiler_params=_params(1),
        name="prep",
    )(w_in_t, w_in_t, w_in_t, *others)


def _inproj_kernel(x_ref, mod_ref, conv0_ref, g1_ref, w1_ref, wifr_ref, bifr_ref,
                   wconv_ref, bconv_ref,
                   q_ref, k32_ref, v32_ref, kb_ref, vb_ref, mq_ref, mk_ref, mv_ref, mvt_ref, mo_ref,
                   grow_ref, convn_ref,
                   u_sc, xp_sc, *, bt, lt):
    @pl.when(pl.program_id(1) == 0)
    def _():
        for j in range(bt):
            xp_sc[j] = jnp.concatenate(
                [jnp.zeros((SUBLANES - (CONV_W - 1), 2 * ML_WIDTH), F32), conv0_ref[j]], axis=0)

    g1 = g1_ref[...]
    for j in range(bt):
        u = _norm_mod(x_ref[j], g1, mod_ref[j, 1:2, :], mod_ref[j, 0:1, :])
        u_sc[j * lt:(j + 1) * lt, :] = u.astype(BF16)
    u = u_sc[...]

    def mm(lo, hi):
        return _dot_nt(u, w1_ref[lo:hi, :])

    wconv = wconv_ref[...]
    bconv = bconv_ref[...]
    k_scale = 1.0 / math.sqrt(ML_HEAD_DIM)
    mqk = mm(_C_MQK, _C_MV)
    width = 2 * ML_WIDTH // CONV_CHUNKS
    row8 = lax.broadcasted_iota(jnp.int32, (SUBLANES, width), 0)

    def conv_chunk(c):
        cols = slice(c * width, (c + 1) * width)
        for j in range(bt):
            x = mqk[j * lt:(j + 1) * lt, cols]
            before = xp_sc[j, :, cols]
            conv = bconv[:, cols] + x * wconv[CONV_W - 1:CONV_W, cols]
            for k in range(1, CONV_W):
                xr = pltpu.roll(x, k, axis=0)
                head = jnp.where(row8 < k, pltpu.roll(before, k, axis=0), xr[:SUBLANES])
                xk = jnp.concatenate([head, xr[SUBLANES:]], axis=0)
                conv = conv + xk * wconv[CONV_W - 1 - k:CONV_W - k, cols]
            conv = conv * _sigmoid(conv)
            if c < CONV_CHUNKS // 2:
                mq_ref[j, :, cols] = conv.astype(BF16)
            else:
                mk_ref[j, :, c * width - ML_WIDTH:(c + 1) * width - ML_WIDTH] = (conv * k_scale).astype(BF16)

    sq = mm(_C_Q, _C_K)
    conv_chunk(0)
    for j in range(bt):
        q_ref[j] = sq[j * lt:(j + 1) * lt].astype(BF16)
    sk = mm(_C_K, _C_V)
    conv_chunk(1)
    for j in range(bt):
        k32_ref[j] = sk[j * lt:(j + 1) * lt]
        kb_ref[j] = sk[j * lt:(j + 1) * lt].astype(BF16)
    sv = mm(_C_V, _C_MQK)
    conv_chunk(2)
    for j in range(bt):
        v32_ref[j] = sv[j * lt:(j + 1) * lt]
        vb_ref[j] = sv[j * lt:(j + 1) * lt].astype(BF16)
    mv = mm(_C_MV, _C_MO)
    conv_chunk(3)
    for j in range(bt):
        mv_ref[j] = mv[j * lt:(j + 1) * lt].astype(BF16)
        convn_ref[j] = mqk[(j + 1) * lt - (CONV_W - 1):(j + 1) * lt]
        xp_sc[j] = mqk[(j + 1) * lt - SUBLANES:(j + 1) * lt]
    mo = mm(_C_MO, _C_IF)
    for j in range(bt):
        rows = slice(j * lt, (j + 1) * lt)
        mo_ref[j] = mo[rows]
        mvt_ref[j] = mv[rows].T.astype(BF16)
        gr = _dot_nt(wifr_ref[...], u_sc[rows, :]) + bifr_ref[...]
        row = lax.broadcasted_iota(jnp.int32, gr.shape, 0)
        grow_ref[j] = jnp.where(row >= SUBLANES, _log_sigmoid(gr), gr)


def _inproj(x, mod3, conv0, g1, w1, wif_row, bif_row, wconv, bconv, *, bt, lt):
    nb, seq, _ = x.shape
    assert nb % bt == 0 and seq % lt == 0 and (bt == 1 or lt == seq)
    grid = (nb // bt, seq // lt)

    def tok(width):
        return pl.BlockSpec((bt, lt, width), lambda b, l: (b, l, 0))

    def per_batch(rows, width):
        return pl.BlockSpec((bt, rows, width), lambda b, l: (b, 0, 0))

    in_specs = [tok(D_MODEL), per_batch(6, D_MODEL), per_batch(CONV_W - 1, 2 * ML_WIDTH),
                _const_spec((1, D_MODEL)), _const_spec(w1.shape),
                _const_spec((2 * SUBLANES, D_MODEL)),
                _const_spec((2 * SUBLANES, 1)),
                _const_spec((CONV_W, 2 * ML_WIDTH)), _const_spec((1, 2 * ML_WIDTH))]

    def tok_on_lanes(rows):
        return pl.BlockSpec((bt, rows, lt), lambda b, l: (b, 0, l))

    out_specs = [tok(SB_WIDTH)] * 5 + [tok(ML_WIDTH)] * 3 + [
        tok_on_lanes(ML_WIDTH), tok(ML_WIDTH), tok_on_lanes(2 * SUBLANES),
        per_batch(CONV_W - 1, 2 * ML_WIDTH)]

    def sds(shape, dt):
        return jax.ShapeDtypeStruct(shape, dt)

    out_shape = [sds((nb, seq, SB_WIDTH), BF16), sds((nb, seq, SB_WIDTH), F32), sds((nb, seq, SB_WIDTH), F32),
                 sds((nb, seq, SB_WIDTH), BF16), sds((nb, seq, SB_WIDTH), BF16),
                 sds((nb, seq, ML_WIDTH), BF16), sds((nb, seq, ML_WIDTH), BF16), sds((nb, seq, ML_WIDTH), BF16),
                 sds((nb, ML_WIDTH, seq), BF16), sds((nb, seq, ML_WIDTH), F32),
                 sds((nb, 2 * SUBLANES, seq), F32),
                 sds((nb, CONV_W - 1, 2 * ML_WIDTH), F32)]
    return pl.pallas_call(
        functools.partial(_inproj_kernel, bt=bt, lt=lt),
        grid=grid,
        in_specs=in_specs,
        out_specs=out_specs,
        out_shape=out_shape,
        scratch_shapes=[pltpu.VMEM((bt * lt, D_MODEL), BF16),
                        pltpu.VMEM((bt, SUBLANES, 2 * ML_WIDTH), F32)],
        compiler_params=_params(2),
        name="inproj",
    )(x, mod3, conv0, g1, w1, wif_row, bif_row, wconv, bconv)


def _cum_matrix(t):
    r = lax.broadcasted_iota(jnp.int32, (2 * t, LANES + t), 0)
    c = lax.broadcasted_iota(jnp.int32, (2 * t, LANES + t), 1)
    return jnp.where((c < LANES) | ((r & (t - 1)) > (c - LANES)), 1.0, 0.0).astype(BF16)


def _attn_kernel(q_ref, kd_ref, vd_ref, kp_ref, vp_ref, o_ref, qm_sc, acc_sc, carry_sc, *, bt, tq, tk, n_past):
    cache_past = n_past is not None
    n_blocks = n_past if cache_past else pl.program_id(1)
    scale = 1.0 / math.sqrt(SB_HEAD_DIM)
    n_pairs = SB_WIDTH // LANES
    lane = lax.broadcasted_iota(jnp.int32, (1, LANES), 1)
    first_head = lane < SB_HEAD_DIM

    n_units = bt * n_pairs
    for u in range(n_units):
        s, p = divmod(u, n_pairs)
        qp = q_ref[s, :, p * LANES:(p + 1) * LANES] * scale
        zero = jnp.zeros_like(qp)
        qm_sc[(2 * u) * tq:(2 * u + 1) * tq, :] = jnp.where(first_head, qp, zero)
        qm_sc[(2 * u + 1) * tq:(2 * u + 2) * tq, :] = jnp.where(first_head, zero, qp)

    per_group = n_units // ATTN_GROUPS
    groups = [range(g, g + per_group) for g in range(0, n_units, per_group)]
    group_rows = [slice(2 * grp[0] * tq, 2 * (grp[-1] + 1) * tq) for grp in groups]

    def scores(k_of, transposed):
        dot_k = _dot if transposed else _dot_nt
        return [jnp.concatenate([dot_k(qm_sc[2 * u * tq:(2 * u + 2) * tq, :], k_of(u)) for u in grp], axis=0)
                for grp in groups]

    def weights(zs, t, visible):
        cum = _cum_matrix(t)
        sps, sums = [], []
        for gi, rows in enumerate(group_rows):
            sp = _softplus_pos(zs[gi])
            if visible is not None:
                sp = jnp.where(visible[rows], sp, 0.0)
            hi = sp.astype(BF16)
            lo = (sp - hi.astype(F32)).astype(BF16)
            sps.append(sp)
            sums.append(_dot(jnp.concatenate([hi, lo], axis=1), cum))
        out = []
        for gi, rows in enumerate(group_rows):
            logit = zs[gi] - sps[gi] - sums[gi][:, LANES:]
            if visible is None:
                carry = carry_sc[rows, :]
                a = jnp.exp(logit - carry)
                carry_sc[rows, :] = carry + sums[gi][:, :LANES]
            else:
                a = jnp.where(visible[rows], jnp.exp(logit), 0.0)
                carry_sc[rows, :] = sums[gi][:, :LANES]
            out.append(a.astype(BF16))
        return out

    def accumulate(a_groups, v_of, transposed, assign=False):
        dot_v = _dot_nt if transposed else _dot
        for gi, grp in enumerate(groups):
            for j, u in enumerate(grp):
                s, p = divmod(u, n_pairs)
                cols = slice(p * LANES, (p + 1) * LANES)
                pv = dot_v(a_groups[gi][2 * j * tq:(2 * j + 2) * tq, :], v_of(u))
                out = jnp.where(first_head, pv[:tq], pv[tq:])
                if assign:
                    acc_sc[s, :, cols] = out
                else:
                    acc_sc[s, :, cols] += out

    def unit_cols(ref):
        def of(u):
            s, p = divmod(u, n_pairs)
            return ref[s, :, p * LANES:(p + 1) * LANES]
        return of

    def past_of(ref, block_index):
        start = pl.multiple_of(block_index * tk, tk)

        def of(u):
            s, p = divmod(u, n_pairs)
            if cache_past:
                return ref[s, p * LANES:(p + 1) * LANES, pl.ds(start, tk)].astype(BF16)
            return ref[s, pl.ds(start, tk), p * LANES:(p + 1) * LANES].astype(BF16)
        return of

    rd = lax.broadcasted_iota(jnp.int32, (2 * n_units * tq, tq), 0) & (tq - 1)
    cd = lax.broadcasted_iota(jnp.int32, (2 * n_units * tq, tq), 1)
    a_diag = weights(scores(unit_cols(kd_ref), False), tq, cd < rd)
    accumulate(a_diag, unit_cols(vd_ref), False, assign=True)

    def not_done():
        return jnp.min(carry_sc[...]) <= EXP_UNDERFLOW

    def cond(state):
        i, go = state
        return jnp.logical_and(i < n_blocks, go)

    def body(state):
        i, _ = state
        block_index = n_blocks - 1 - i
        a = weights(scores(past_of(kp_ref, block_index), cache_past), tk, None)
        accumulate(a, past_of(vp_ref, block_index), cache_past)
        return i + 1, not_done()

    lax.while_loop(cond, body, (jnp.int32(0), not_done()))
    o_ref[...] = acc_sc[...].astype(BF16)


def _attn(q, k_new, v_new, k_past, v_past, *, bt, tq, n_past):
    nb, seq, _ = q.shape
    assert nb % bt == 0
    tok = pl.BlockSpec((bt, tq, SB_WIDTH), lambda b, i: (b, i, 0))
    past = pl.BlockSpec((bt,) + k_past.shape[1:], lambda b, i: (b, 0, 0))
    rows = bt * SB_HEADS * tq
    return pl.pallas_call(
        functools.partial(_attn_kernel, bt=bt, tq=tq, tk=ATTN_BLOCK, n_past=n_past),
        grid=(nb // bt, seq // tq),
        in_specs=[tok, tok, tok, past, past],
        out_specs=tok,
        out_shape=jax.ShapeDtypeStruct((nb, seq, SB_WIDTH), BF16),
        scratch_shapes=[pltpu.VMEM((rows, LANES), BF16),
                        pltpu.VMEM((bt, tq, SB_WIDTH), F32),
                        pltpu.VMEM((rows, LANES), F32)],
        compiler_params=_params(2),
        name="attn",
    )(q, k_new, v_new, k_past, v_past)


def _mlstm_kernel(mq_ref, mk_ref, mv_ref, mvt_ref, mo_ref, grow_ref, c0_ref, n0_ref, m0_ref, gn_ref,
                  yb_ref, c_out_ref, n_out_ref, m_out_ref,
                  c_sc, n_sc, m_sc, alpha_sc, beta_sc, r_sc, wg_sc, dec_sc, mprev_sc, *, bt, lc, nc):
    step = pl.program_id(1)
    dk = ML_HEAD_DIM

    ri = lax.broadcasted_iota(jnp.int32, (lc, lc), 0)
    ci = lax.broadcasted_iota(jnp.int32, (lc, lc), 1)
    causal = ci <= ri
    eye = jnp.where(ri == ci, 1.0, 0.0).astype(BF16)
    ones_blk = jnp.ones((lc, dk), BF16)
    zeros_blk = jnp.zeros((lc, dk), BF16)

    @pl.when(step == 0)
    def _():
        c_sc[...] = c0_ref[...]
        n_sc[...] = n0_ref[...]
        incl_t = jnp.where(ri <= ci, 1.0, 0.0).astype(BF16)
        blocks = [(s, c) for s in range(bt) for c in range(nc)]
        li = jnp.concatenate([grow_ref[s, 0:SUBLANES, c * lc:(c + 1) * lc] for s, c in blocks], axis=0)
        lf = jnp.concatenate([grow_ref[s, SUBLANES:2 * SUBLANES, c * lc:(c + 1) * lc] for s, c in blocks],
                             axis=0)
        b = _split_dot(lf, incl_t, 3)
        r = li - b
        lane = lax.broadcasted_iota(jnp.int32, r.shape, 1)
        run = r
        d = 1
        while d < lc:
            run = jnp.maximum(run, jnp.where(lane >= d, pltpu.roll(run, d, axis=1), -jnp.inf))
            d *= 2
        b_last = b[:, lc - 1:lc]
        g = b_last - b + li
        g_max = jnp.max(g, axis=1, keepdims=True)
        m_before = []
        m_after = []
        for s in range(bt):
            m = m0_ref[s]
            for c in range(nc):
                rows = slice((s * nc + c) * SUBLANES, (s * nc + c + 1) * SUBLANES)
                m_before.append(m)
                m = jnp.maximum(b_last[rows] + m, g_max[rows])
                m_after.append(m)
            m_sc[s] = m
        m_prev = jnp.concatenate(m_before, axis=0)
        m_new = jnp.concatenate(m_after, axis=0)
        alpha = (-jnp.maximum(run, m_prev[:, :lc])).astype(BF16).astype(F32)
        alpha_sc[...] = alpha
        beta_sc[...] = alpha - b
        r_sc[...] = r
        wg_sc[...] = jnp.exp(g - m_new[:, :lc])
        dec_sc[...] = jnp.exp(b_last + m_prev - m_new)
        mprev_sc[...] = m_prev

    units = [(s, h) for s in range(bt) for h in range(ML_HEADS)]
    cols = [slice(h * dk, (h + 1) * dk) for h in range(ML_HEADS)]
    alpha, beta_hi, beta_lo, r, wg, decay, m_prev = [], [], [], [], [], [], []
    for s in range(bt):
        chunk = pl.ds(pl.multiple_of((s * nc + step) * SUBLANES, SUBLANES), SUBLANES)
        alpha.append(alpha_sc[chunk, :].astype(BF16))
        beta = beta_sc[chunk, :]
        beta_hi.append(beta.astype(BF16))
        beta_lo.append((beta - beta_hi[s].astype(F32)).astype(BF16))
        r.append(r_sc[chunk, :])
        wg.append(wg_sc[chunk, :])
        decay.append(dec_sc[chunk, :])
        m_prev.append(mprev_sc[chunk, :])
    eye2 = jnp.concatenate([eye, eye], axis=1)

    def rows_of(x, h):
        return jnp.broadcast_to(x[h:h + 1, :], (dk, lc))
    alpha_b = _dot_nt(eye, jnp.concatenate([rows_of(alpha[s], h) for s, h in units], axis=0))
    beta_b = _dot_nt(eye2, jnp.concatenate(
        [jnp.concatenate([rows_of(beta_hi[s], h), rows_of(beta_lo[s], h)], axis=1) for s, h in units], axis=0))
    qk = [_dot_nt(mq_ref[s, :, cols[h]], mk_ref[s, :, cols[h]]) for s, h in units]
    c_prev = [c_sc[s, h] for s, h in units]
    n_prev = [n_sc[s, h:h + 1, :] for s, h in units]
    qc = [_dot_nt(mq_ref[s, :, cols[h]],
                  jnp.concatenate([c_prev[u].astype(BF16),
                                   jnp.broadcast_to(n_prev[u], (dk, dk)).astype(BF16)], axis=0))
          for u, (s, h) in enumerate(units)]
    upd = []
    for s, h in units:
        wg_h = wg[s][h:h + 1, :]
        lhs = jnp.concatenate([(mvt_ref[s, cols[h], :].astype(F32) * wg_h).astype(BF16),
                               jnp.broadcast_to(wg_h, (SUBLANES, lc)).astype(BF16)], axis=0)
        upd.append(_dot(lhs, mk_ref[s, :, cols[h]]))

    cat = []
    for u, (s, h) in enumerate(units):
        w = jnp.where(causal, jnp.exp(alpha_b[:, u * dk:u * dk + lc] + r[s][h:h + 1, :]), 0.0)
        sw = qk[u] * w
        s_hi = sw.astype(BF16)
        cat.append(jnp.concatenate([s_hi, (sw - s_hi.astype(F32)).astype(BF16)], axis=1))
    nv = []
    for u, (s, h) in enumerate(units):
        rhs = jnp.concatenate([jnp.concatenate([mv_ref[s, :, cols[h]], ones_blk], axis=1),
                               jnp.concatenate([zeros_blk, ones_blk], axis=1)], axis=0)
        nv.append(_dot(cat[u], rhs))
    hval = []
    cat2 = []
    for u, (s, h) in enumerate(units):
        ucols = slice(u * dk, (u + 1) * dk)
        w_inter = jnp.exp(m_prev[s][h:h + 1, :] + alpha_b[:, ucols])
        num = nv[u][:, :dk] + w_inter * qc[u][:, :dk]
        den = nv[u][:, dk:] + w_inter * qc[u][:, dk:]
        hv = num / jnp.maximum(jnp.abs(den), jnp.exp(beta_b[:, ucols]))
        h2 = hv * hv
        h2_hi = h2.astype(BF16)
        hval.append(hv)
        cat2.append(jnp.concatenate([h2_hi, (h2 - h2_hi.astype(F32)).astype(BF16)], axis=1))
    ms = _dot(jnp.concatenate(cat2, axis=0), jnp.ones((2 * dk, dk), BF16)) * (1.0 / dk)
    for u, (s, h) in enumerate(units):
        y = hval[u] * lax.rsqrt(ms[u * lc:(u + 1) * lc] + EPS) * gn_ref[:, cols[h]]
        yb_ref[s, :, cols[h]] = (_sigmoid(mo_ref[s, :, cols[h]]) * y).astype(BF16)

    for u, (s, h) in enumerate(units):
        dec = decay[s][h:h + 1, :]
        c_sc[s, h] = dec * c_prev[u] + upd[u][:dk]
        n_sc[s, h:h + 1, :] = dec * n_prev[u] + upd[u][dk:dk + 1]

    c_out_ref[...] = c_sc[...]
    n_out_ref[...] = n_sc[...]
    m_out_ref[...] = m_sc[...]


def _mlstm(mq, mk, mv, mvt, mo, grow, c0, n0, m0, gn, *, bt, lc):
    nb, seq, _ = mq.shape
    assert nb % bt == 0
    tok = pl.BlockSpec((bt, lc, ML_WIDTH), lambda b, i: (b, i, 0))
    c_spec = pl.BlockSpec((bt, ML_HEADS, ML_HEAD_DIM, ML_HEAD_DIM), lambda b, i: (b, 0, 0, 0))
    n_spec = pl.BlockSpec((bt, ML_HEADS, ML_HEAD_DIM), lambda b, i: (b, 0, 0))
    m_spec = pl.BlockSpec((bt, SUBLANES, LANES), lambda b, i: (b, 0, 0))
    nc = seq // lc
    table = pltpu.VMEM((bt * nc * SUBLANES, lc), F32)
    table_m = pltpu.VMEM((bt * nc * SUBLANES, LANES), F32)
    return pl.pallas_call(
        functools.partial(_mlstm_kernel, bt=bt, lc=lc, nc=nc),
        grid=(nb // bt, nc),
        in_specs=[tok, tok, tok,
                  pl.BlockSpec((bt, ML_WIDTH, lc), lambda b, i: (b, 0, i)),
                  tok,
                  pl.BlockSpec((bt, 2 * SUBLANES, seq), lambda b, i: (b, 0, 0)),
                  c_spec, n_spec, m_spec, _const_spec((1, ML_WIDTH))],
        out_specs=[tok, c_spec, n_spec, m_spec],
        out_shape=[jax.ShapeDtypeStruct((nb, seq, ML_WIDTH), BF16),
                   jax.ShapeDtypeStruct((nb, ML_HEADS, ML_HEAD_DIM, ML_HEAD_DIM), F32),
                   jax.ShapeDtypeStruct((nb, ML_HEADS, ML_HEAD_DIM), F32),
                   jax.ShapeDtypeStruct((nb, SUBLANES, LANES), F32)],
        scratch_shapes=[pltpu.VMEM((bt, ML_HEADS, ML_HEAD_DIM, ML_HEAD_DIM), F32),
                        pltpu.VMEM((bt, ML_HEADS, ML_HEAD_DIM), F32),
                        pltpu.VMEM((bt, SUBLANES, LANES), F32),
                        table, table, table, table, table_m, table_m],
        compiler_params=_params(2),
        name="mlstm",
    )(mq, mk, mv, mvt, mo, grow, c0, n0, m0, gn)


def _post_kernel(x_ref, ya_ref, yb_ref, mod_ref, g1_ref, g2_ref, gf_ref,
                 wgab_ref, wa_ref, wb_ref, wout_ref, wfg_ref, wfu_ref, wfd_ref,
                 y_ref, u_sc, *, bt, lt):
    g1 = g1_ref[...]
    for j in range(bt):
        u = _norm_mod(x_ref[j], g1, mod_ref[j, 1:2, :], mod_ref[j, 0:1, :])
        u_sc[j * lt:(j + 1) * lt, :] = u.astype(BF16)
    u = u_sc[...]
    ya = ya_ref[...].reshape(bt * lt, SB_WIDTH)
    yb = yb_ref[...].reshape(bt * lt, ML_WIDTH)
    ga = _dot_nt(u, wgab_ref[:D_MODEL, :])
    gb = _dot_nt(u, wgab_ref[D_MODEL:, :])
    merged = _sigmoid(ga) * _dot(ya, wa_ref[...]) + _sigmoid(gb) * _dot(yb, wb_ref[...])
    attn_out = _dot(merged.astype(BF16), wout_ref[...])

    g2 = g2_ref[...]
    for j in range(bt):
        rows = slice(j * lt, (j + 1) * lt)
        x1 = x_ref[j] + mod_ref[j, 2:3, :] * attn_out[rows]
        y_ref[j] = x1
        u2 = _norm_mod(x1, g2, mod_ref[j, 4:5, :], mod_ref[j, 3:4, :])
        u_sc[rows, :] = u2.astype(BF16)
    u2 = u_sc[...]
    hg = _dot(u2, wfg_ref[...])
    hu = _dot(u2, wfu_ref[...])
    act = (hg * _sigmoid(hg) * hu).astype(BF16)
    ff = _dot(act, wfd_ref[...])
    gf = gf_ref[...]
    for j in range(bt):
        rows = slice(j * lt, (j + 1) * lt)
        x2 = y_ref[j] + mod_ref[j, 5:6, :] * ff[rows]
        ms = jnp.mean(x2 * x2, axis=-1, keepdims=True)
        y_ref[j] = x2 * lax.rsqrt(ms + EPS) * gf


def _post(x, ya, yb, mod3, g1, g2, gf, wgab, wa, wb, wout, wfg, wfu, wfd, *, bt, lt):
    nb, seq, _ = x.shape
    assert nb % bt == 0 and seq % lt == 0

    def tok(width):
        return pl.BlockSpec((bt, lt, width), lambda b, l: (b, l, 0))

    in_specs = [tok(D_MODEL), tok(SB_WIDTH), tok(ML_WIDTH),
                pl.BlockSpec((bt, 6, D_MODEL), lambda b, l: (b, 0, 0)),
                _const_spec((1, D_MODEL)), _const_spec((1, D_MODEL)), _const_spec((1, D_MODEL)),
                _const_spec(wgab.shape), _const_spec(wa.shape), _const_spec(wb.shape), _const_spec(wout.shape),
                _const_spec(wfg.shape), _const_spec(wfu.shape), _const_spec(wfd.shape)]
    return pl.pallas_call(
        functools.partial(_post_kernel, bt=bt, lt=lt),
        grid=(nb // bt, seq // lt),
        in_specs=in_specs,
        out_specs=tok(D_MODEL),
        out_shape=jax.ShapeDtypeStruct((nb, seq, D_MODEL), F32),
        scratch_shapes=[pltpu.VMEM((bt * lt, D_MODEL), BF16)],
        compiler_params=_params(2),
        name="post",
    )(x, ya, yb, mod3, g1, g2, gf, wgab, wa, wb, wout, wfg, wfu, wfd)


def _group(x, mod3, k_past, v_past, c0, n0, m0, conv0, wts, *, bt, lt_in, lt, tq, n_past, lc):
    (g1, g2, gf, w1, wif_row, bif_row, wconv, bconv, gn,
     wgab, wa, wb, wout, wfg, wfu, wfd) = wts
    nb, seq, _ = x.shape
    (q, k32, v32, kb, vb, mq, mk, mv, mvt, mo, grow, conv_new) = _inproj(
        x, mod3, conv0, g1, w1, wif_row, bif_row, wconv, bconv, bt=bt, lt=lt_in)
    if k_past is None:
        k_past, v_past = kb, vb
    ya = _attn(q, kb, vb, k_past, v_past, bt=ATTN_SEQS, tq=tq, n_past=n_past)
    m0b = jnp.broadcast_to(jnp.pad(m0, ((0, 0), (0, SUBLANES - ML_HEADS)))[:, :, None], (nb, SUBLANES, LANES))
    yb, c1, n1, m1 = _mlstm(mq, mk, mv, mvt, mo, grow, c0, n0, m0b, gn, bt=ML_SEQS, lc=lc)
    y = _post(x, ya, yb, mod3, g1, g2, gf, wgab, wa, wb, wout, wfg, wfu, wfd, bt=bt, lt=lt)
    states = (k32.reshape(1, nb, seq, SB_HEADS, SB_HEAD_DIM), v32.reshape(1, nb, seq, SB_HEADS, SB_HEAD_DIM),
              c1[None], n1[None], m1[None, :, :ML_HEADS, 0], conv_new[None])
    return y, states


def kernel(x_prompt, x_sample, c_prompt, c_sample, cache_sb_k, cache_sb_v, state_mlstm_C, state_mlstm_n, state_mlstm_m, state_conv, norm1_g, norm2_g, w_ada, b_ada, w_in, b_if, w_conv, b_conv, ml_norm_g, w_a, w_b, w_out, w_ff_gate, w_ff_up, w_ff_down, final_g):
    assert w_in.shape[0] == 1, "single layer"
    bp, seq_p, _ = x_prompt.shape
    bs, seq_s, _ = x_sample.shape
    past = cache_sb_k.shape[2]

    c_all = jnp.concatenate([c_prompt, c_sample], axis=0)
    mod3 = _ada(c_all, w_ada[0], b_ada[0]).reshape(bp + bs, 6, D_MODEL)

    w1, wgab, wif_row, wa, wb, wout, wfg, wfu, wfd = _prep(
        jnp.transpose(w_in[0]), w_a[0], w_b[0], w_out[0], w_ff_gate[0], w_ff_up[0], w_ff_down[0])
    pad = SUBLANES - ML_HEADS
    bif_row = jnp.pad(b_if[0].reshape(2, ML_HEADS), ((0, 0), (0, pad))).reshape(2 * SUBLANES, 1)
    wts = (norm1_g[0].reshape(1, D_MODEL), norm2_g[0].reshape(1, D_MODEL), final_g.reshape(1, D_MODEL),
           w1, wif_row, bif_row, w_conv[0], b_conv[0].reshape(1, 2 * ML_WIDTH),
           ml_norm_g[0].reshape(1, ML_WIDTH),
           wgab, wa, wb, wout, wfg, wfu, wfd)

    zeros = functools.partial(jnp.zeros, dtype=F32)
    y_p, st_p = _group(
        x_prompt, mod3[:bp], None, None,
        zeros((bp, ML_HEADS, ML_HEAD_DIM, ML_HEAD_DIM)), zeros((bp, ML_HEADS, ML_HEAD_DIM)),
        zeros((bp, ML_HEADS)), zeros((bp, CONV_W - 1, 2 * ML_WIDTH)), wts,
        bt=1, lt_in=512, lt=256, tq=ATTN_BLOCK, n_past=None, lc=ML_CHUNK)
    def cache_t(c):
        return jnp.transpose(c[0], (0, 2, 3, 1)).reshape(bs, SB_WIDTH, past)

    y_s, st_s = _group(
        x_sample, mod3[bp:], cache_t(cache_sb_k), cache_t(cache_sb_v),
        state_mlstm_C[0], state_mlstm_n[0], state_mlstm_m[0], state_conv[0], wts,
        bt=4, lt_in=seq_s, lt=seq_s, tq=seq_s, n_past=past // ATTN_BLOCK, lc=seq_s)
    return (y_p, y_s) + st_p + st_s
```

```python
import functools
import math

import jax
import jax.numpy as jnp
from jax import lax
from jax.experimental import pallas as pl
from jax.experimental.pallas import tpu as pltpu

D_MODEL = 1024
SB_HEADS = 8
SB_HEAD_DIM = 64
SB_WIDTH = SB_HEADS * SB_HEAD_DIM
ML_HEADS = 4
ML_HEAD_DIM = 128
ML_WIDTH = ML_HEADS * ML_HEAD_DIM
CONV_W = 4
D_FF = 2816
EPS = 1e-6
LANES = 128
SUBLANES = 8
ATTN_BLOCK = 128
ML_CHUNK = 128
ML_SEQS = 4
EXP_UNDERFLOW = 88.0
SOFTPLUS_CLAMP = 60.0
CONV_CHUNKS = 4
ATTN_SEQS = 2
ATTN_GROUPS = 4
VMEM_LIMIT = 56 * 1024 * 1024

BF16 = jnp.bfloat16
F32 = jnp.float32

_C_Q, _C_K, _C_V = 0, SB_WIDTH, 2 * SB_WIDTH
_C_MQK = 3 * SB_WIDTH
_C_MV = _C_MQK + 2 * ML_WIDTH
_C_MO = _C_MV + ML_WIDTH
_C_IF = _C_MO + ML_WIDTH
_C_GAB = _C_IF + 2 * ML_HEADS
_C_END = _C_GAB + 2 * D_MODEL


def _sigmoid(x):
    return 1.0 / (1.0 + jnp.exp(-x))


def _log_sigmoid(x):
    return jnp.minimum(x, 0.0) - jnp.log1p(jnp.exp(-jnp.abs(x)))


def _softplus_pos(x):
    return jnp.maximum(jnp.log(1.0 + jnp.exp(jnp.minimum(x, SOFTPLUS_CLAMP))), x)


def _dot(a, b):
    return jnp.dot(a, b, preferred_element_type=F32)


def _dot_nt(a, b):
    return lax.dot_general(a, b, (((1,), (1,)), ((), ())), preferred_element_type=F32)


def _dot_tn(a, b):
    return lax.dot_general(a, b, (((0,), (0,)), ((), ())), preferred_element_type=F32)


def _split_dot(x, tri, parts):
    acc = None
    rem = x
    for i in range(parts):
        piece = rem.astype(BF16)
        term = _dot(piece, tri)
        acc = term if acc is None else acc + term
        if i + 1 < parts:
            rem = rem - piece.astype(F32)
    return acc


def _split_dot_left(tri, x, parts):
    acc = None
    rem = x
    for i in range(parts):
        piece = rem.astype(BF16)
        term = _dot(tri, piece)
        acc = term if acc is None else acc + term
        if i + 1 < parts:
            rem = rem - piece.astype(F32)
    return acc


def _norm_mod(x, g, sc, sh):
    ms = jnp.mean(x * x, axis=-1, keepdims=True)
    return x * lax.rsqrt(ms + EPS) * g * (1.0 + sc) + sh


def _const_spec(shape):
    nd = len(shape)
    return pl.BlockSpec(shape, lambda *_: (0,) * nd, pipeline_mode=pl.Buffered(1))


def _params(n_axes):
    return pltpu.CompilerParams(dimension_semantics=("arbitrary",) * n_axes,
                                vmem_limit_bytes=VMEM_LIMIT)


def _ada_kernel(c_ref, w_ref, b_ref, o_ref):
    c = c_ref[...]
    s = c * _sigmoid(c)
    o_ref[...] = _dot(s.astype(BF16), w_ref[...].astype(BF16)) + b_ref[...]


def _ada(c_all, w_ada, b_ada):
    nb = c_all.shape[0]
    n_out = w_ada.shape[1]
    tn = D_MODEL
    return pl.pallas_call(
        _ada_kernel,
        grid=(n_out // tn,),
        in_specs=[pl.BlockSpec((nb, D_MODEL), lambda j: (0, 0)),
                  pl.BlockSpec((D_MODEL, tn), lambda j: (0, j)),
                  pl.BlockSpec((1, tn), lambda j: (0, j))],
        out_specs=pl.BlockSpec((nb, tn), lambda j: (0, j)),
        out_shape=jax.ShapeDtypeStruct((nb, n_out), F32),
        compiler_params=_params(1),
        name="ada",
    )(c_all, w_ada, b_ada.reshape(1, n_out))


PREP_STEPS = 8


def _prep_kernel(w1_ref, wgab_ref, wif_ref, wa_ref, wb_ref, wout_ref, wfg_ref, wfu_ref, wfd_ref,
                 w1_o, wgab_o, wif_o, wa_o, wb_o, wout_o, wfg_o, wfu_o, wfd_o):
    for src, dst in ((w1_ref, w1_o), (wgab_ref, wgab_o), (wa_ref, wa_o), (wb_ref, wb_o), (wout_ref, wout_o),
                     (wfg_ref, wfg_o), (wfu_ref, wfu_o), (wfd_ref, wfd_o)):
        dst[...] = src[...].astype(BF16)
    gates = wif_ref[...]
    pad = jnp.zeros((SUBLANES - ML_HEADS, D_MODEL), F32)
    wif_o[...] = jnp.concatenate([gates[:ML_HEADS], pad, gates[ML_HEADS:], pad], axis=0).astype(BF16)


def _prep(w_in_t, w_a, w_b, w_out, w_fg, w_fu, w_fd):
    def rows(a, n_rows=None, first=0):
        n_rows = a.shape[0] if n_rows is None else n_rows
        step = n_rows // PREP_STEPS
        assert n_rows % (PREP_STEPS * 2 * SUBLANES) == 0 and first % SUBLANES == 0
        if first == 0:
            return pl.BlockSpec((step, a.shape[1]), lambda i: (i, 0))
        return pl.BlockSpec((pl.Element(step), pl.Element(a.shape[1])),
                            lambda i: (pl.multiple_of(first + i * step, SUBLANES), 0))

    def out(n_rows, n_cols):
        return (jax.ShapeDtypeStruct((n_rows, n_cols), BF16),
                pl.BlockSpec((n_rows // PREP_STEPS, n_cols), lambda i: (i, 0)))

    others = (w_a, w_b, w_out, w_fg, w_fu, w_fd)
    outs = [out(_C_IF, D_MODEL), out(_C_END - _C_GAB, D_MODEL),
            (jax.ShapeDtypeStruct((2 * SUBLANES, D_MODEL), BF16),
             pl.BlockSpec((2 * SUBLANES, D_MODEL), lambda i: (0, 0)))] + [out(*a.shape) for a in others]
    in_specs = [rows(w_in_t, _C_IF), rows(w_in_t, _C_END - _C_GAB, _C_GAB),
                pl.BlockSpec((pl.Element(2 * ML_HEADS), pl.Element(D_MODEL)), lambda i: (_C_IF, 0))
                ] + [rows(a) for a in others]
    return pl.pallas_call(
        _prep_kernel,
        grid=(PREP_STEPS,),
        in_specs=in_specs,
        out_specs=[o[1] for o in outs],
        out_shape=[o[0] for o in outs],
        compiler_params=_params(1),
        name="prep",
    )(w_in_t, w_in_t, w_in_t, *others)


def _inproj_kernel(x_ref, mod_ref, conv0_ref, g1_ref, w1_ref, wifr_ref, bifr_ref,
                   wconv_ref, bconv_ref,
                   q_ref, k32_ref, v32_ref, kb_ref, vb_ref, mq_ref, mk_ref, mv_ref, mvt_ref, mo_ref,
                   grow_ref, convn_ref,
                   u_sc, xp_sc, *, bt, lt):
    @pl.when(pl.program_id(1) == 0)
    def _():
        for j in range(bt):
            xp_sc[j] = jnp.concatenate(
                [jnp.zeros((SUBLANES - (CONV_W - 1), 2 * ML_WIDTH), F32), conv0_ref[j]], axis=0)

    g1 = g1_ref[...]
    for j in range(bt):
        u = _norm_mod(x_ref[j], g1, mod_ref[j, 1:2, :], mod_ref[j, 0:1, :])
        u_sc[j * lt:(j + 1) * lt, :] = u.astype(BF16)
    u = u_sc[...]

    def mm(lo, hi):
        return _dot_nt(u, w1_ref[lo:hi, :])

    wconv = wconv_ref[...]
    bconv = bconv_ref[...]
    k_scale = 1.0 / math.sqrt(ML_HEAD_DIM)
    mqk = mm(_C_MQK, _C_MV)
    width = 2 * ML_WIDTH // CONV_CHUNKS
    row8 = lax.broadcasted_iota(jnp.int32, (SUBLANES, width), 0)

    def conv_chunk(c):
        cols = slice(c * width, (c + 1) * width)
        for j in range(bt):
            x = mqk[j * lt:(j + 1) * lt, cols]
            before = xp_sc[j, :, cols]
            conv = bconv[:, cols] + x * wconv[CONV_W - 1:CONV_W, cols]
            for k in range(1, CONV_W):
                xr = pltpu.roll(x, k, axis=0)
                head = jnp.where(row8 < k, pltpu.roll(before, k, axis=0), xr[:SUBLANES])
                xk = jnp.concatenate([head, xr[SUBLANES:]], axis=0)
                conv = conv + xk * wconv[CONV_W - 1 - k:CONV_W - k, cols]
            conv = conv * _sigmoid(conv)
            if c < CONV_CHUNKS // 2:
                mq_ref[j, :, cols] = conv.astype(BF16)
            else:
                mk_ref[j, :, c * width - ML_WIDTH:(c + 1) * width - ML_WIDTH] = (conv * k_scale).astype(BF16)

    sq = mm(_C_Q, _C_K)
    conv_chunk(0)
    for j in range(bt):
        q_ref[j] = sq[j * lt:(j + 1) * lt].astype(BF16)
    sk = mm(_C_K, _C_V)
    conv_chunk(1)
    for j in range(bt):
        k32_ref[j] = sk[j * lt:(j + 1) * lt]
        kb_ref[j] = sk[j * lt:(j + 1) * lt].astype(BF16)
    sv = mm(_C_V, _C_MQK)
    conv_chunk(2)
    for j in range(bt):
        v32_ref[j] = sv[j * lt:(j + 1) * lt]
        vb_ref[j] = sv[j * lt:(j + 1) * lt].astype(BF16)
    mv = mm(_C_MV, _C_MO)
    conv_chunk(3)
    for j in range(bt):
        mv_ref[j] = mv[j * lt:(j + 1) * lt].astype(BF16)
        convn_ref[j] = mqk[(j + 1) * lt - (CONV_W - 1):(j + 1) * lt]
        xp_sc[j] = mqk[(j + 1) * lt - SUBLANES:(j + 1) * lt]
    mo = mm(_C_MO, _C_IF)
    for j in range(bt):
        rows = slice(j * lt, (j + 1) * lt)
        mo_ref[j] = mo[rows]
        mvt_ref[j] = mv[rows].T.astype(BF16)
        gr = _dot_nt(wifr_ref[...], u_sc[rows, :]) + bifr_ref[...]
        row = lax.broadcasted_iota(jnp.int32, gr.shape, 0)
        grow_ref[j] = jnp.where(row >= SUBLANES, _log_sigmoid(gr), gr)


def _inproj(x, mod3, conv0, g1, w1, wif_row, bif_row, wconv, bconv, *, bt, lt):
    nb, seq, _ = x.shape
    assert nb % bt == 0 and seq % lt == 0 and (bt == 1 or lt == seq)
    grid = (nb // bt, seq // lt)

    def tok(width):
        return pl.BlockSpec((bt, lt, width), lambda b, l: (b, l, 0))

    def per_batch(rows, width):
        return pl.BlockSpec((bt, rows, width), lambda b, l: (b, 0, 0))

    in_specs = [tok(D_MODEL), per_batch(6, D_MODEL), per_batch(CONV_W - 1, 2 * ML_WIDTH),
                _const_spec((1, D_MODEL)), _const_spec(w1.shape),
                _const_spec((2 * SUBLANES, D_MODEL)),
                _const_spec((2 * SUBLANES, 1)),
                _const_spec((CONV_W, 2 * ML_WIDTH)), _const_spec((1, 2 * ML_WIDTH))]

    def tok_on_lanes(rows):
        return pl.BlockSpec((bt, rows, lt), lambda b, l: (b, 0, l))

    out_specs = [tok(SB_WIDTH)] * 5 + [tok(ML_WIDTH)] * 3 + [
        tok_on_lanes(ML_WIDTH), tok(ML_WIDTH), tok_on_lanes(2 * SUBLANES),
        per_batch(CONV_W - 1, 2 * ML_WIDTH)]

    def sds(shape, dt):
        return jax.ShapeDtypeStruct(shape, dt)

    out_shape = [sds((nb, seq, SB_WIDTH), BF16), sds((nb, seq, SB_WIDTH), F32), sds((nb, seq, SB_WIDTH), F32),
                 sds((nb, seq, SB_WIDTH), BF16), sds((nb, seq, SB_WIDTH), BF16),
                 sds((nb, seq, ML_WIDTH), BF16), sds((nb, seq, ML_WIDTH), BF16), sds((nb, seq, ML_WIDTH), BF16),
                 sds((nb, ML_WIDTH, seq), BF16), sds((nb, seq, ML_WIDTH), F32),
                 sds((nb, 2 * SUBLANES, seq), F32),
                 sds((nb, CONV_W - 1, 2 * ML_WIDTH), F32)]
    return pl.pallas_call(
        functools.partial(_inproj_kernel, bt=bt, lt=lt),
        grid=grid,
        in_specs=in_specs,
        out_specs=out_specs,
        out_shape=out_shape,
        scratch_shapes=[pltpu.VMEM((bt * lt, D_MODEL), BF16),
                        pltpu.VMEM((bt, SUBLANES, 2 * ML_WIDTH), F32)],
        compiler_params=_params(2),
        name="inproj",
    )(x, mod3, conv0, g1, w1, wif_row, bif_row, wconv, bconv)


def _cum_matrix(t):
    r = lax.broadcasted_iota(jnp.int32, (2 * t, LANES + t), 0)
    c = lax.broadcasted_iota(jnp.int32, (2 * t, LANES + t), 1)
    return jnp.where((c < LANES) | ((r & (t - 1)) > (c - LANES)), 1.0, 0.0).astype(BF16)


def _attn_kernel(q_ref, kd_ref, vd_ref, kp_ref, vp_ref, o_ref, qm_sc, acc_sc, carry_sc, *, bt, tq, tk, n_past):
    cache_past = n_past is not None
    n_blocks = n_past if cache_past else pl.program_id(1)
    scale = 1.0 / math.sqrt(SB_HEAD_DIM)
    n_pairs = SB_WIDTH // LANES
    lane = lax.broadcasted_iota(jnp.int32, (1, LANES), 1)
    first_head = lane < SB_HEAD_DIM

    n_units = bt * n_pairs
    for u in range(n_units):
        s, p = divmod(u, n_pairs)
        qp = q_ref[s, :, p * LANES:(p + 1) * LANES] * scale
        zero = jnp.zeros_like(qp)
        qm_sc[(2 * u) * tq:(2 * u + 1) * tq, :] = jnp.where(first_head, qp, zero)
        qm_sc[(2 * u + 1) * tq:(2 * u + 2) * tq, :] = jnp.where(first_head, zero, qp)

    per_group = n_units // ATTN_GROUPS
    groups = [range(g, g + per_group) for g in range(0, n_units, per_group)]
    group_rows = [slice(2 * grp[0] * tq, 2 * (grp[-1] + 1) * tq) for grp in groups]

    def scores(k_of, transposed):
        dot_k = _dot if transposed else _dot_nt
        return [jnp.concatenate([dot_k(qm_sc[2 * u * tq:(2 * u + 2) * tq, :], k_of(u)) for u in grp], axis=0)
                for grp in groups]

    def weights(zs, t, visible):
        cum = _cum_matrix(t)
        sps, sums = [], []
        for gi, rows in enumerate(group_rows):
            sp = _softplus_pos(zs[gi])
            if visible is not None:
                sp = jnp.where(visible[rows], sp, 0.0)
            hi = sp.astype(BF16)
            lo = (sp - hi.astype(F32)).astype(BF16)
            sps.append(sp)
            sums.append(_dot(jnp.concatenate([hi, lo], axis=1), cum))
        out = []
        for gi, rows in enumerate(group_rows):
            logit = zs[gi] - sps[gi] - sums[gi][:, LANES:]
            if visible is None:
                carry = carry_sc[rows, :]
                a = jnp.exp(logit - carry)
                carry_sc[rows, :] = carry + sums[gi][:, :LANES]
            else:
                a = jnp.where(visible[rows], jnp.exp(logit), 0.0)
                carry_sc[rows, :] = sums[gi][:, :LANES]
            out.append(a.astype(BF16))
        return out

    def accumulate(a_groups, v_of, transposed, assign=False):
        dot_v = _dot_nt if transposed else _dot
        for gi, grp in enumerate(groups):
            for j, u in enumerate(grp):
                s, p = divmod(u, n_pairs)
                cols = slice(p * LANES, (p + 1) * LANES)
                pv = dot_v(a_groups[gi][2 * j * tq:(2 * j + 2) * tq, :], v_of(u))
                out = jnp.where(first_head, pv[:tq], pv[tq:])
                if assign:
                    acc_sc[s, :, cols] = out
                else:
                    acc_sc[s, :, cols] += out

    def unit_cols(ref):
        def of(u):
            s, p = divmod(u, n_pairs)
            return ref[s, :, p * LANES:(p + 1) * LANES]
        return of

    def past_of(ref, block_index):
        start = pl.multiple_of(block_index * tk, tk)

        def of(u):
            s, p = divmod(u, n_pairs)
            if cache_past:
                return ref[s, p * LANES:(p + 1) * LANES, pl.ds(start, tk)].astype(BF16)
            return ref[s, pl.ds(start, tk), p * LANES:(p + 1) * LANES].astype(BF16)
        return of

    rd = lax.broadcasted_iota(jnp.int32, (2 * n_units * tq, tq), 0) & (tq - 1)
    cd = lax.broadcasted_iota(jnp.int32, (2 * n_units * tq, tq), 1)
    a_diag = weights(scores(unit_cols(kd_ref), False), tq, cd < rd)
    accumulate(a_diag, unit_cols(vd_ref), False, assign=True)

    def not_done():
        return jnp.min(carry_sc[...]) <= EXP_UNDERFLOW

    def cond(state):
        i, go = state
        return jnp.logical_and(i < n_blocks, go)

    def body(state):
        i, _ = state
        block_index = n_blocks - 1 - i
        a = weights(scores(past_of(kp_ref, block_index), cache_past), tk, None)
        accumulate(a, past_of(vp_ref, block_index), cache_past)
        return i + 1, not_done()

    lax.while_loop(cond, body, (jnp.int32(0), not_done()))
    o_ref[...] = acc_sc[...].astype(BF16)


def _attn(q, k_new, v_new, k_past, v_past, *, bt, tq, n_past):
    nb, seq, _ = q.shape
    assert nb % bt == 0
    tok = pl.BlockSpec((bt, tq, SB_WIDTH), lambda b, i: (b, i, 0))
    past = pl.BlockSpec((bt,) + k_past.shape[1:], lambda b, i: (b, 0, 0))
    rows = bt * SB_HEADS * tq
    return pl.pallas_call(
        functools.partial(_attn_kernel, bt=bt, tq=tq, tk=ATTN_BLOCK, n_past=n_past),
        grid=(nb // bt, seq // tq),
        in_specs=[tok, tok, tok, past, past],
        out_specs=tok,
        out_shape=jax.ShapeDtypeStruct((nb, seq, SB_WIDTH), BF16),
        scratch_shapes=[pltpu.VMEM((rows, LANES), BF16),
                        pltpu.VMEM((bt, tq, SB_WIDTH), F32),
                        pltpu.VMEM((rows, LANES), F32)],
        compiler_params=_params(2),
        name="attn",
    )(q, k_new, v_new, k_past, v_past)


def _mlstm_kernel(mq_ref, mk_ref, mv_ref, mvt_ref, mo_ref, grow_ref, c0_ref, n0_ref, m0_ref, gn_ref,
                  yb_ref, c_out_ref, n_out_ref, m_out_ref,
                  c_sc, n_sc, m_sc, alpha_sc, beta_sc, r_sc, wg_sc, dec_sc, mprev_sc, *, bt, lc, nc):
    step = pl.program_id(1)
    dk = ML_HEAD_DIM

    ri = lax.broadcasted_iota(jnp.int32, (lc, lc), 0)
    ci = lax.broadcasted_iota(jnp.int32, (lc, lc), 1)
    causal = ci <= ri
    eye = jnp.where(ri == ci, 1.0, 0.0).astype(BF16)
    ones_blk = jnp.ones((lc, dk), BF16)
    zeros_blk = jnp.zeros((lc, dk), BF16)

    @pl.when(step == 0)
    def _():
        c_sc[...] = c0_ref[...]
        n_sc[...] = n0_ref[...]
        incl_t = jnp.where(ri <= ci, 1.0, 0.0).astype(BF16)
        blocks = [(s, c) for s in range(bt) for c in range(nc)]
        li = jnp.concatenate([grow_ref[s, 0:SUBLANES, c * lc:(c + 1) * lc] for s, c in blocks], axis=0)
        lf = jnp.concatenate([grow_ref[s, SUBLANES:2 * SUBLANES, c * lc:(c + 1) * lc] for s, c in blocks],
                             axis=0)
        b = _split_dot(lf, incl_t, 3)
        r = li - b
        lane = lax.broadcasted_iota(jnp.int32, r.shape, 1)
        run = r
        d = 1
        while d < lc:
            run = jnp.maximum(run, jnp.where(lane >= d, pltpu.roll(run, d, axis=1), -jnp.inf))
            d *= 2
        b_last = b[:, lc - 1:lc]
        g = b_last - b + li
        g_max = jnp.max(g, axis=1, keepdims=True)
        m_before = []
        m_after = []
        for s in range(bt):
            m = m0_ref[s]
            for c in range(nc):
                rows = slice((s * nc + c) * SUBLANES, (s * nc + c + 1) * SUBLANES)
                m_before.append(m)
                m = jnp.maximum(b_last[rows] + m, g_max[rows])
                m_after.append(m)
            m_sc[s] = m
        m_prev = jnp.concatenate(m_before, axis=0)
        m_new = jnp.concatenate(m_after, axis=0)
        alpha = (-jnp.maximum(run, m_prev[:, :lc])).astype(BF16).astype(F32)
        alpha_sc[...] = alpha
        beta_sc[...] = alpha - b
        r_sc[...] = r
        wg_sc[...] = jnp.exp(g - m_new[:, :lc])
        dec_sc[...] = jnp.exp(b_last + m_prev - m_new)
        mprev_sc[...] = m_prev

    units = [(s, h) for s in range(bt) for h in range(ML_HEADS)]
    cols = [slice(h * dk, (h + 1) * dk) for h in range(ML_HEADS)]
    alpha, beta_hi, beta_lo, r, wg, decay, m_prev = [], [], [], [], [], [], []
    for s in range(bt):
        chunk = pl.ds(pl.multiple_of((s * nc + step) * SUBLANES, SUBLANES), SUBLANES)
        alpha.append(alpha_sc[chunk, :].astype(BF16))
        beta = beta_sc[chunk, :]
        beta_hi.append(beta.astype(BF16))
        beta_lo.append((beta - beta_hi[s].astype(F32)).astype(BF16))
        r.append(r_sc[chunk, :])
        wg.append(wg_sc[chunk, :])
        decay.append(dec_sc[chunk, :])
        m_prev.append(mprev_sc[chunk, :])
    eye2 = jnp.concatenate([eye, eye], axis=1)

    def rows_of(x, h):
        return jnp.broadcast_to(x[h:h + 1, :], (dk, lc))
    alpha_b = _dot_nt(eye, jnp.concatenate([rows_of(alpha[s], h) for s, h in units], axis=0))
    beta_b = _dot_nt(eye2, jnp.concatenate(
        [jnp.concatenate([rows_of(beta_hi[s], h), rows_of(beta_lo[s], h)], axis=1) for s, h in units], axis=0))
    qk = [_dot_nt(mq_ref[s, :, cols[h]], mk_ref[s, :, cols[h]]) for s, h in units]
    c_prev = [c_sc[s, h] for s, h in units]
    n_prev = [n_sc[s, h:h + 1, :] for s, h in units]
    qc = [_dot_nt(mq_ref[s, :, cols[h]],
                  jnp.concatenate([c_prev[u].astype(BF16),
                                   jnp.broadcast_to(n_prev[u], (dk, dk)).astype(BF16)], axis=0))
          for u, (s, h) in enumerate(units)]
    upd = []
    for s, h in units:
        wg_h = wg[s][h:h + 1, :]
        lhs = jnp.concatenate([(mvt_ref[s, cols[h], :].astype(F32) * wg_h).astype(BF16),
                               jnp.broadcast_to(wg_h, (SUBLANES, lc)).astype(BF16)], axis=0)
        upd.append(_dot(lhs, mk_ref[s, :, cols[h]]))

    cat = []
    for u, (s, h) in enumerate(units):
        w = jnp.where(causal, jnp.exp(alpha_b[:, u * dk:u * dk + lc] + r[s][h:h + 1, :]), 0.0)
        sw = qk[u] * w
        s_hi = sw.astype(BF16)
        cat.append(jnp.concatenate([s_hi, (sw - s_hi.astype(F32)).astype(BF16)], axis=1))
    nv = []
    for u, (s, h) in enumerate(units):
        rhs = jnp.concatenate([jnp.concatenate([mv_ref[s, :, cols[h]], ones_blk], axis=1),
                               jnp.concatenate([zeros_blk, ones_blk], axis=1)], axis=0)
        nv.append(_dot(cat[u], rhs))
    hval = []
    cat2 = []
    for u, (s, h) in enumerate(units):
        ucols = slice(u * dk, (u + 1) * dk)
        w_inter = jnp.exp(m_prev[s][h:h + 1, :] + alpha_b[:, ucols])
        num = nv[u][:, :dk] + w_inter * qc[u][:, :dk]
        den = nv[u][:, dk:] + w_inter * qc[u][:, dk:]
        hv = num / jnp.maximum(jnp.abs(den), jnp.exp(beta_b[:, ucols]))
        h2 = hv * hv
        h2_hi = h2.astype(BF16)
        hval.append(hv)
        cat2.append(jnp.concatenate([h2_hi, (h2 - h2_hi.astype(F32)).astype(BF16)], axis=1))
    ms = _dot(jnp.concatenate(cat2, axis=0), jnp.ones((2 * dk, dk), BF16)) * (1.0 / dk)
    for u, (s, h) in enumerate(units):
        y = hval[u] * lax.rsqrt(ms[u * lc:(u + 1) * lc] + EPS) * gn_ref[:, cols[h]]
        yb_ref[s, :, cols[h]] = (_sigmoid(mo_ref[s, :, cols[h]]) * y).astype(BF16)

    for u, (s, h) in enumerate(units):
        dec = decay[s][h:h + 1, :]
        c_sc[s, h] = dec * c_prev[u] + upd[u][:dk]
        n_sc[s, h:h + 1, :] = dec * n_prev[u] + upd[u][dk:dk + 1]

    c_out_ref[...] = c_sc[...]
    n_out_ref[...] = n_sc[...]
    m_out_ref[...] = m_sc[...]


def _mlstm(mq, mk, mv, mvt, mo, grow, c0, n0, m0, gn, *, bt, lc):
    nb, seq, _ = mq.shape
    assert nb % bt == 0
    tok = pl.BlockSpec((bt, lc, ML_WIDTH), lambda b, i: (b, i, 0))
    c_spec = pl.BlockSpec((bt, ML_HEADS, ML_HEAD_DIM, ML_HEAD_DIM), lambda b, i: (b, 0, 0, 0))
    n_spec = pl.BlockSpec((bt, ML_HEADS, ML_HEAD_DIM), lambda b, i: (b, 0, 0))
    m_spec = pl.BlockSpec((bt, SUBLANES, LANES), lambda b, i: (b, 0, 0))
    nc = seq // lc
    table = pltpu.VMEM((bt * nc * SUBLANES, lc), F32)
    table_m = pltpu.VMEM((bt * nc * SUBLANES, LANES), F32)
    return pl.pallas_call(
        functools.partial(_mlstm_kernel, bt=bt, lc=lc, nc=nc),
        grid=(nb // bt, nc),
        in_specs=[tok, tok, tok,
                  pl.BlockSpec((bt, ML_WIDTH, lc), lambda b, i: (b, 0, i)),
                  tok,
                  pl.BlockSpec((bt, 2 * SUBLANES, seq), lambda b, i: (b, 0, 0)),
                  c_spec, n_spec, m_spec, _const_spec((1, ML_WIDTH))],
        out_specs=[tok, c_spec, n_spec, m_spec],
        out_shape=[jax.ShapeDtypeStruct((nb, seq, ML_WIDTH), BF16),
                   jax.ShapeDtypeStruct((nb, ML_HEADS, ML_HEAD_DIM, ML_HEAD_DIM), F32),
                   jax.ShapeDtypeStruct((nb, ML_HEADS, ML_HEAD_DIM), F32),
                   jax.ShapeDtypeStruct((nb, SUBLANES, LANES), F32)],
        scratch_shapes=[pltpu.VMEM((bt, ML_HEADS, ML_HEAD_DIM, ML_HEAD_DIM), F32),
                        pltpu.VMEM((bt, ML_HEADS, ML_HEAD_DIM), F32),
                        pltpu.VMEM((bt, SUBLANES, LANES), F32),
                        table, table, table, table, table_m, table_m],
        compiler_params=_params(2),
        name="mlstm",
    )(mq, mk, mv, mvt, mo, grow, c0, n0, m0, gn)


def _post_kernel(x_ref, ya_ref, yb_ref, mod_ref, g1_ref, g2_ref, gf_ref,
                 wgab_ref, wa_ref, wb_ref, wout_ref, wfg_ref, wfu_ref, wfd_ref,
                 y_ref, u_sc, *, bt, lt):
    g1 = g1_ref[...]
    for j in range(bt):
        u = _norm_mod(x_ref[j], g1, mod_ref[j, 1:2, :], mod_ref[j, 0:1, :])
        u_sc[j * lt:(j + 1) * lt, :] = u.astype(BF16)
    u = u_sc[...]
    ya = ya_ref[...].reshape(bt * lt, SB_WIDTH)
    yb = yb_ref[...].reshape(bt * lt, ML_WIDTH)
    ga = _dot_nt(u, wgab_ref[:D_MODEL, :])
    gb = _dot_nt(u, wgab_ref[D_MODEL:, :])
    merged = _sigmoid(ga) * _dot(ya, wa_ref[...]) + _sigmoid(gb) * _dot(yb, wb_ref[...])
    attn_out = _dot(merged.astype(BF16), wout_ref[...])

    g2 = g2_ref[...]
    for j in range(bt):
        rows = slice(j * lt, (j + 1) * lt)
        x1 = x_ref[j] + mod_ref[j, 2:3, :] * attn_out[rows]
        y_ref[j] = x1
        u2 = _norm_mod(x1, g2, mod_ref[j, 4:5, :], mod_ref[j, 3:4, :])
        u_sc[rows, :] = u2.astype(BF16)
    u2 = u_sc[...]
    hg = _dot(u2, wfg_ref[...])
    hu = _dot(u2, wfu_ref[...])
    act = (hg * _sigmoid(hg) * hu).astype(BF16)
    ff = _dot(act, wfd_ref[...])
    gf = gf_ref[...]
    for j in range(bt):
        rows = slice(j * lt, (j + 1) * lt)
        x2 = y_ref[j] + mod_ref[j, 5:6, :] * ff[rows]
        ms = jnp.mean(x2 * x2, axis=-1, keepdims=True)
        y_ref[j] = x2 * lax.rsqrt(ms + EPS) * gf


def _post(x, ya, yb, mod3, g1, g2, gf, wgab, wa, wb, wout, wfg, wfu, wfd, *, bt, lt):
    nb, seq, _ = x.shape
    assert nb % bt == 0 and seq % lt == 0

    def tok(width):
        return pl.BlockSpec((bt, lt, width), lambda b, l: (b, l, 0))

    in_specs = [tok(D_MODEL), tok(SB_WIDTH), tok(ML_WIDTH),
                pl.BlockSpec((bt, 6, D_MODEL), lambda b, l: (b, 0, 0)),
                _const_spec((1, D_MODEL)), _const_spec((1, D_MODEL)), _const_spec((1, D_MODEL)),
                _const_spec(wgab.shape), _const_spec(wa.shape), _const_spec(wb.shape), _const_spec(wout.shape),
                _const_spec(wfg.shape), _const_spec(wfu.shape), _const_spec(wfd.shape)]
    return pl.pallas_call(
        functools.partial(_post_kernel, bt=bt, lt=lt),
        grid=(nb // bt, seq // lt),
        in_specs=in_specs,
        out_specs=tok(D_MODEL),
        out_shape=jax.ShapeDtypeStruct((nb, seq, D_MODEL), F32),
        scratch_shapes=[pltpu.VMEM((bt * lt, D_MODEL), BF16)],
        compiler_params=_params(2),
        name="post",
    )(x, ya, yb, mod3, g1, g2, gf, wgab, wa, wb, wout, wfg, wfu, wfd)


def _group(x, mod3, k_past, v_past, c0, n0, m0, conv0, wts, *, bt, lt_in, lt, tq, n_past, lc):
    (g1, g2, gf, w1, wif_row, bif_row, wconv, bconv, gn,
     wgab, wa, wb, wout, wfg, wfu, wfd) = wts
    nb, seq, _ = x.shape
    (q, k32, v32, kb, vb, mq, mk, mv, mvt, mo, grow, conv_new) = _inproj(
        x, mod3, conv0, g1, w1, wif_row, bif_row, wconv, bconv, bt=bt, lt=lt_in)
    if k_past is None:
        k_past, v_past = kb, vb
    ya = _attn(q, kb, vb, k_past, v_past, bt=ATTN_SEQS, tq=tq, n_past=n_past)
    m0b = jnp.broadcast_to(jnp.pad(m0, ((0, 0), (0, SUBLANES - ML_HEADS)))[:, :, None], (nb, SUBLANES, LANES))
    yb, c1, n1, m1 = _mlstm(mq, mk, mv, mvt, mo, grow, c0, n0, m0b, gn, bt=ML_SEQS, lc=lc)
    y = _post(x, ya, yb, mod3, g1, g2, gf, wgab, wa, wb, wout, wfg, wfu, wfd, bt=bt, lt=lt)
    states = (k32.reshape(1, nb, seq, SB_HEADS, SB_HEAD_DIM), v32.reshape(1, nb, seq, SB_HEADS, SB_HEAD_DIM),
              c1[None], n1[None], m1[None, :, :ML_HEADS, 0], conv_new[None])
    return y, states


def kernel(x_prompt, x_sample, c_prompt, c_sample, cache_sb_k, cache_sb_v, state_mlstm_C, state_mlstm_n, state_mlstm_m, state_conv, norm1_g, norm2_g, w_ada, b_ada, w_in, b_if, w_conv, b_conv, ml_norm_g, w_a, w_b, w_out, w_ff_gate, w_ff_up, w_ff_down, final_g):
    assert w_in.shape[0] == 1, "single layer"
    bp, seq_p, _ = x_prompt.shape
    bs, seq_s, _ = x_sample.shape
    past = cache_sb_k.shape[2]

    c_all = jnp.concatenate([c_prompt, c_sample], axis=0)
    mod3 = _ada(c_all, w_ada[0], b_ada[0]).reshape(bp + bs, 6, D_MODEL)

    w1, wgab, wif_row, wa, wb, wout, wfg, wfu, wfd = _prep(
        jnp.transpose(w_in[0]), w_a[0], w_b[0], w_out[0], w_ff_gate[0], w_ff_up[0], w_ff_down[0])
    pad = SUBLANES - ML_HEADS
    bif_row = jnp.pad(b_if[0].reshape(2, ML_HEADS), ((0, 0), (0, pad))).reshape(2 * SUBLANES, 1)
    wts = (norm1_g[0].reshape(1, D_MODEL), norm2_g[0].reshape(1, D_MODEL), final_g.reshape(1, D_MODEL),
           w1, wif_row, bif_row, w_conv[0], b_conv[0].reshape(1, 2 * ML_WIDTH),
           ml_norm_g[0].reshape(1, ML_WIDTH),
           wgab, wa, wb, wout, wfg, wfu, wfd)

    zeros = functools.partial(jnp.zeros, dtype=F32)
    y_p, st_p = _group(
        x_prompt, mod3[:bp], None, None,
        zeros((bp, ML_HEADS, ML_HEAD_DIM, ML_HEAD_DIM)), zeros((bp, ML_HEADS, ML_HEAD_DIM)),
        zeros((bp, ML_HEADS)), zeros((bp, CONV_W - 1, 2 * ML_WIDTH)), wts,
        bt=1, lt_in=512, lt=256, tq=ATTN_BLOCK, n_past=None, lc=ML_CHUNK)
    def cache_t(c):
        return jnp.transpose(c[0], (0, 2, 3, 1)).reshape(bs, SB_WIDTH, past)

    y_s, st_s = _group(
        x_sample, mod3[bp:], cache_t(cache_sb_k), cache_t(cache_sb_v),
        state_mlstm_C[0], state_mlstm_n[0], state_mlstm_m[0], state_conv[0], wts,
        bt=4, lt_in=seq_s, lt=seq_s, tq=seq_s, n_past=past // ATTN_BLOCK, lc=seq_s)
    return (y_p, y_s) + st_p + st_s
```

```python
import functools
import math

import jax
import jax.numpy as jnp
from jax import lax
from jax.experimental import pallas as pl
from jax.experimental.pallas import tpu as pltpu

D_MODEL = 1024
SB_HEADS = 8
SB_HEAD_DIM = 64
SB_WIDTH = SB_HEADS * SB_HEAD_DIM
ML_HEADS = 4
ML_HEAD_DIM = 128
ML_WIDTH = ML_HEADS * ML_HEAD_DIM
CONV_W = 4
D_FF = 2816
EPS = 1e-6
LANES = 128
SUBLANES = 8
ATTN_BLOCK = 128
ML_CHUNK = 128
ML_SEQS = 4
EXP_UNDERFLOW = 88.0
SOFTPLUS_CLAMP = 60.0
CONV_CHUNKS = 4
ATTN_SEQS = 2
ATTN_GROUPS = 4
VMEM_LIMIT = 56 * 1024 * 1024

BF16 = jnp.bfloat16
F32 = jnp.float32

_C_Q, _C_K, _C_V = 0, SB_WIDTH, 2 * SB_WIDTH
_C_MQK = 3 * SB_WIDTH
_C_MV = _C_MQK + 2 * ML_WIDTH
_C_MO = _C_MV + ML_WIDTH
_C_IF = _C_MO + ML_WIDTH
_C_GAB = _C_IF + 2 * ML_HEADS
_C_END = _C_GAB + 2 * D_MODEL


def _sigmoid(x):
    return 1.0 / (1.0 + jnp.exp(-x))


def _log_sigmoid(x):
    return jnp.minimum(x, 0.0) - jnp.log1p(jnp.exp(-jnp.abs(x)))


def _softplus_pos(x):
    return jnp.maximum(jnp.log(1.0 + jnp.exp(jnp.minimum(x, SOFTPLUS_CLAMP))), x)


def _dot(a, b):
    return jnp.dot(a, b, preferred_element_type=F32)


def _dot_nt(a, b):
    return lax.dot_general(a, b, (((1,), (1,)), ((), ())), preferred_element_type=F32)


def _dot_tn(a, b):
    return lax.dot_general(a, b, (((0,), (0,)), ((), ())), preferred_element_type=F32)


def _split_dot(x, tri, parts):
    acc = None
    rem = x
    for i in range(parts):
        piece = rem.astype(BF16)
        term = _dot(piece, tri)
        acc = term if acc is None else acc + term
        if i + 1 < parts:
            rem = rem - piece.astype(F32)
    return acc


def _split_dot_left(tri, x, parts):
    acc = None
    rem = x
    for i in range(parts):
        piece = rem.astype(BF16)
        term = _dot(tri, piece)
        acc = term if acc is None else acc + term
        if i + 1 < parts:
            rem = rem - piece.astype(F32)
    return acc


def _norm_mod(x, g, sc, sh):
    ms = jnp.mean(x * x, axis=-1, keepdims=True)
    return x * lax.rsqrt(ms + EPS) * g * (1.0 + sc) + sh


def _const_spec(shape):
    nd = len(shape)
    return pl.BlockSpec(shape, lambda *_: (0,) * nd, pipeline_mode=pl.Buffered(1))


def _params(n_axes):
    return pltpu.CompilerParams(dimension_semantics=("arbitrary",) * n_axes,
                                vmem_limit_bytes=VMEM_LIMIT)


def _ada_kernel(c_ref, w_ref, b_ref, o_ref):
    c = c_ref[...]
    s = c * _sigmoid(c)
    o_ref[...] = _dot(s.astype(BF16), w_ref[...].astype(BF16)) + b_ref[...]


def _ada(c_all, w_ada, b_ada):
    nb = c_all.shape[0]
    n_out = w_ada.shape[1]
    tn = D_MODEL
    return pl.pallas_call(
        _ada_kernel,
        grid=(n_out // tn,),
        in_specs=[pl.BlockSpec((nb, D_MODEL), lambda j: (0, 0)),
                  pl.BlockSpec((D_MODEL, tn), lambda j: (0, j)),
                  pl.BlockSpec((1, tn), lambda j: (0, j))],
        out_specs=pl.BlockSpec((nb, tn), lambda j: (0, j)),
        out_shape=jax.ShapeDtypeStruct((nb, n_out), F32),
        compiler_params=_params(1),
        name="ada",
    )(c_all, w_ada, b_ada.reshape(1, n_out))


PREP_STEPS = 8


def _prep_kernel(w1_ref, wgab_ref, wif_ref, wa_ref, wb_ref, wout_ref, wfg_ref, wfu_ref, wfd_ref,
                 w1_o, wgab_o, wif_o, wa_o, wb_o, wout_o, wfg_o, wfu_o, wfd_o):
    for src, dst in ((w1_ref, w1_o), (wgab_ref, wgab_o), (wa_ref, wa_o), (wb_ref, wb_o), (wout_ref, wout_o),
                     (wfg_ref, wfg_o), (wfu_ref, wfu_o), (wfd_ref, wfd_o)):
        dst[...] = src[...].astype(BF16)
    gates = wif_ref[...]
    pad = jnp.zeros((SUBLANES - ML_HEADS, D_MODEL), F32)
    wif_o[...] = jnp.concatenate([gates[:ML_HEADS], pad, gates[ML_HEADS:], pad], axis=0).astype(BF16)


def _prep(w_in_t, w_a, w_b, w_out, w_fg, w_fu, w_fd):
    def rows(a, n_rows=None, first=0):
        n_rows = a.shape[0] if n_rows is None else n_rows
        step = n_rows // PREP_STEPS
        assert n_rows % (PREP_STEPS * 2 * SUBLANES) == 0 and first % SUBLANES == 0
        if first == 0:
            return pl.BlockSpec((step, a.shape[1]), lambda i: (i, 0))
        return pl.BlockSpec((pl.Element(step), pl.Element(a.shape[1])),
                            lambda i: (pl.multiple_of(first + i * step, SUBLANES), 0))

    def out(n_rows, n_cols):
        return (jax.ShapeDtypeStruct((n_rows, n_cols), BF16),
                pl.BlockSpec((n_rows // PREP_STEPS, n_cols), lambda i: (i, 0)))

    others = (w_a, w_b, w_out, w_fg, w_fu, w_fd)
    outs = [out(_C_IF, D_MODEL), out(_C_END - _C_GAB, D_MODEL),
            (jax.ShapeDtypeStruct((2 * SUBLANES, D_MODEL), BF16),
             pl.BlockSpec((2 * SUBLANES, D_MODEL), lambda i: (0, 0)))] + [out(*a.shape) for a in others]
    in_specs = [rows(w_in_t, _C_IF), rows(w_in_t, _C_END - _C_GAB, _C_GAB),
                pl.BlockSpec((pl.Element(2 * ML_HEADS), pl.Element(D_MODEL)), lambda i: (_C_IF, 0))
                ] + [rows(a) for a in others]
    return pl.pallas_call(
        _prep_kernel,
        grid=(PREP_STEPS,),
        in_specs=in_specs,
        out_specs=[o[1] for o in outs],
        out_shape=[o[0] for o in outs],
        compiler_params=_params(1),
        name="prep",
    )(w_in_t, w_in_t, w_in_t, *others)


def _inproj_kernel(x_ref, mod_ref, conv0_ref, g1_ref, w1_ref, wifr_ref, bifr_ref,
                   wconv_ref, bconv_ref,
                   q_ref, k32_ref, v32_ref, kb_ref, vb_ref, mq_ref, mk_ref, mv_ref, mo_ref,
                   grow_ref, convn_ref,
                   u_sc, xp_sc, *, bt, lt):
    @pl.when(pl.program_id(1) == 0)
    def _():
        for j in range(bt):
            xp_sc[j] = jnp.concatenate(
                [jnp.zeros((SUBLANES - (CONV_W - 1), 2 * ML_WIDTH), F32), conv0_ref[j]], axis=0)

    g1 = g1_ref[...]
    for j in range(bt):
        u = _norm_mod(x_ref[j], g1, mod_ref[j, 1:2, :], mod_ref[j, 0:1, :])
        u_sc[j * lt:(j + 1) * lt, :] = u.astype(BF16)
    u = u_sc[...]

    def mm(lo, hi):
        return _dot_nt(u, w1_ref[lo:hi, :])

    wconv = wconv_ref[...]
    bconv = bconv_ref[...]
    k_scale = 1.0 / math.sqrt(ML_HEAD_DIM)
    mqk = mm(_C_MQK, _C_MV)
    width = 2 * ML_WIDTH // CONV_CHUNKS
    row8 = lax.broadcasted_iota(jnp.int32, (SUBLANES, width), 0)

    def conv_chunk(c):
        cols = slice(c * width, (c + 1) * width)
        for j in range(bt):
            x = mqk[j * lt:(j + 1) * lt, cols]
            before = xp_sc[j, :, cols]
            conv = bconv[:, cols] + x * wconv[CONV_W - 1:CONV_W, cols]
            for k in range(1, CONV_W):
                xr = pltpu.roll(x, k, axis=0)
                head = jnp.where(row8 < k, pltpu.roll(before, k, axis=0), xr[:SUBLANES])
                xk = jnp.concatenate([head, xr[SUBLANES:]], axis=0)
                conv = conv + xk * wconv[CONV_W - 1 - k:CONV_W - k, cols]
            conv = conv * _sigmoid(conv)
            if c < CONV_CHUNKS // 2:
                mq_ref[j, :, cols] = conv.astype(BF16)
            else:
                mk_ref[j, :, c * width - ML_WIDTH:(c + 1) * width - ML_WIDTH] = (conv * k_scale).astype(BF16)

    sq = mm(_C_Q, _C_K)
    conv_chunk(0)
    for j in range(bt):
        q_ref[j] = sq[j * lt:(j + 1) * lt].astype(BF16)
    sk = mm(_C_K, _C_V)
    conv_chunk(1)
    for j in range(bt):
        k32_ref[j] = sk[j * lt:(j + 1) * lt]
        kb_ref[j] = sk[j * lt:(j + 1) * lt].astype(BF16)
    sv = mm(_C_V, _C_MQK)
    conv_chunk(2)
    for j in range(bt):
        v32_ref[j] = sv[j * lt:(j + 1) * lt]
        vb_ref[j] = sv[j * lt:(j + 1) * lt].astype(BF16)
    mv = mm(_C_MV, _C_MO)
    conv_chunk(3)
    for j in range(bt):
        mv_ref[j] = mv[j * lt:(j + 1) * lt].astype(BF16)
        convn_ref[j] = mqk[(j + 1) * lt - (CONV_W - 1):(j + 1) * lt]
        xp_sc[j] = mqk[(j + 1) * lt - SUBLANES:(j + 1) * lt]
    mo = mm(_C_MO, _C_IF)
    for j in range(bt):
        rows = slice(j * lt, (j + 1) * lt)
        mo_ref[j] = mo[rows]
        gr = _dot_nt(wifr_ref[...], u_sc[rows, :]) + bifr_ref[...]
        row = lax.broadcasted_iota(jnp.int32, gr.shape, 0)
        grow_ref[j] = jnp.where(row >= SUBLANES, _log_sigmoid(gr), gr)


def _inproj(x, mod3, conv0, g1, w1, wif_row, bif_row, wconv, bconv, *, bt, lt):
    nb, seq, _ = x.shape
    assert nb % bt == 0 and seq % lt == 0 and (bt == 1 or lt == seq)
    grid = (nb // bt, seq // lt)

    def tok(width):
        return pl.BlockSpec((bt, lt, width), lambda b, l: (b, l, 0))

    def per_batch(rows, width):
        return pl.BlockSpec((bt, rows, width), lambda b, l: (b, 0, 0))

    in_specs = [tok(D_MODEL), per_batch(6, D_MODEL), per_batch(CONV_W - 1, 2 * ML_WIDTH),
                _const_spec((1, D_MODEL)), _const_spec(w1.shape),
                _const_spec((2 * SUBLANES, D_MODEL)),
                _const_spec((2 * SUBLANES, 1)),
                _const_spec((CONV_W, 2 * ML_WIDTH)), _const_spec((1, 2 * ML_WIDTH))]

    def tok_on_lanes(rows):
        return pl.BlockSpec((bt, rows, lt), lambda b, l: (b, 0, l))

    out_specs = [tok(SB_WIDTH)] * 5 + [tok(ML_WIDTH)] * 4 + [
        tok_on_lanes(2 * SUBLANES), per_batch(CONV_W - 1, 2 * ML_WIDTH)]

    def sds(shape, dt):
        return jax.ShapeDtypeStruct(shape, dt)

    out_shape = [sds((nb, seq, SB_WIDTH), BF16), sds((nb, seq, SB_WIDTH), F32), sds((nb, seq, SB_WIDTH), F32),
                 sds((nb, seq, SB_WIDTH), BF16), sds((nb, seq, SB_WIDTH), BF16),
                 sds((nb, seq, ML_WIDTH), BF16), sds((nb, seq, ML_WIDTH), BF16), sds((nb, seq, ML_WIDTH), BF16),
                 sds((nb, seq, ML_WIDTH), F32),
                 sds((nb, 2 * SUBLANES, seq), F32),
                 sds((nb, CONV_W - 1, 2 * ML_WIDTH), F32)]
    return pl.pallas_call(
        functools.partial(_inproj_kernel, bt=bt, lt=lt),
        grid=grid,
        in_specs=in_specs,
        out_specs=out_specs,
        out_shape=out_shape,
        scratch_shapes=[pltpu.VMEM((bt * lt, D_MODEL), BF16),
                        pltpu.VMEM((bt, SUBLANES, 2 * ML_WIDTH), F32)],
        compiler_params=_params(2),
        name="inproj",
    )(x, mod3, conv0, g1, w1, wif_row, bif_row, wconv, bconv)


def _cum_matrix(t):
    r = lax.broadcasted_iota(jnp.int32, (2 * t, LANES + t), 0)
    c = lax.broadcasted_iota(jnp.int32, (2 * t, LANES + t), 1)
    return jnp.where((c < LANES) | ((r & (t - 1)) > (c - LANES)), 1.0, 0.0).astype(BF16)


def _attn_kernel(q_ref, kd_ref, vd_ref, kp_ref, vp_ref, o_ref, qm_sc, acc_sc, carry_sc, *, bt, tq, tk, n_past):
    cache_past = n_past is not None
    n_blocks = n_past if cache_past else pl.program_id(1)
    scale = 1.0 / math.sqrt(SB_HEAD_DIM)
    n_pairs = SB_WIDTH // LANES
    lane = lax.broadcasted_iota(jnp.int32, (1, LANES), 1)
    first_head = lane < SB_HEAD_DIM

    n_units = bt * n_pairs
    for u in range(n_units):
        s, p = divmod(u, n_pairs)
        qp = q_ref[s, :, p * LANES:(p + 1) * LANES] * scale
        zero = jnp.zeros_like(qp)
        qm_sc[(2 * u) * tq:(2 * u + 1) * tq, :] = jnp.where(first_head, qp, zero)
        qm_sc[(2 * u + 1) * tq:(2 * u + 2) * tq, :] = jnp.where(first_head, zero, qp)

    per_group = n_units // ATTN_GROUPS
    groups = [range(g, g + per_group) for g in range(0, n_units, per_group)]
    group_rows = [slice(2 * grp[0] * tq, 2 * (grp[-1] + 1) * tq) for grp in groups]

    def scores(k_of, transposed):
        dot_k = _dot if transposed else _dot_nt
        return [jnp.concatenate([dot_k(qm_sc[2 * u * tq:(2 * u + 2) * tq, :], k_of(u)) for u in grp], axis=0)
                for grp in groups]

    def weights(zs, t, visible):
        cum = _cum_matrix(t)
        sps, sums = [], []
        for gi, rows in enumerate(group_rows):
            sp = _softplus_pos(zs[gi])
            if visible is not None:
                sp = jnp.where(visible[rows], sp, 0.0)
            hi = sp.astype(BF16)
            lo = (sp - hi.astype(F32)).astype(BF16)
            sps.append(sp)
            sums.append(_dot(jnp.concatenate([hi, lo], axis=1), cum))
        out = []
        for gi, rows in enumerate(group_rows):
            logit = zs[gi] - sps[gi] - sums[gi][:, LANES:]
            if visible is None:
                carry = carry_sc[rows, :]
                a = jnp.exp(logit - carry)
                carry_sc[rows, :] = carry + sums[gi][:, :LANES]
            else:
                a = jnp.where(visible[rows], jnp.exp(logit), 0.0)
                carry_sc[rows, :] = sums[gi][:, :LANES]
            out.append(a.astype(BF16))
        return out

    def accumulate(a_groups, v_of, transposed, assign=False):
        dot_v = _dot_nt if transposed else _dot
        for gi, grp in enumerate(groups):
            for j, u in enumerate(grp):
                s, p = divmod(u, n_pairs)
                cols = slice(p * LANES, (p + 1) * LANES)
                pv = dot_v(a_groups[gi][2 * j * tq:(2 * j + 2) * tq, :], v_of(u))
                out = jnp.where(first_head, pv[:tq], pv[tq:])
                if assign:
                    acc_sc[s, :, cols] = out
                else:
                    acc_sc[s, :, cols] += out

    def unit_cols(ref):
        def of(u):
            s, p = divmod(u, n_pairs)
            return ref[s, :, p * LANES:(p + 1) * LANES]
        return of

    def past_of(ref, block_index):
        start = pl.multiple_of(block_index * tk, tk)

        def of(u):
            s, p = divmod(u, n_pairs)
            if cache_past:
                return ref[s, p * LANES:(p + 1) * LANES, pl.ds(start, tk)].astype(BF16)
            return ref[s, pl.ds(start, tk), p * LANES:(p + 1) * LANES].astype(BF16)
        return of

    rd = lax.broadcasted_iota(jnp.int32, (2 * n_units * tq, tq), 0) & (tq - 1)
    cd = lax.broadcasted_iota(jnp.int32, (2 * n_units * tq, tq), 1)
    a_diag = weights(scores(unit_cols(kd_ref), False), tq, cd < rd)
    accumulate(a_diag, unit_cols(vd_ref), False, assign=True)

    def not_done():
        return jnp.min(carry_sc[...]) <= EXP_UNDERFLOW

    def cond(state):
        i, go = state
        return jnp.logical_and(i < n_blocks, go)

    def body(state):
        i, _ = state
        block_index = n_blocks - 1 - i
        a = weights(scores(past_of(kp_ref, block_index), cache_past), tk, None)
        accumulate(a, past_of(vp_ref, block_index), cache_past)
        return i + 1, not_done()

    lax.while_loop(cond, body, (jnp.int32(0), not_done()))
    o_ref[...] = acc_sc[...].astype(BF16)


def _attn(q, k_new, v_new, k_past, v_past, *, bt, tq, n_past):
    nb, seq, _ = q.shape
    assert nb % bt == 0
    tok = pl.BlockSpec((bt, tq, SB_WIDTH), lambda b, i: (b, i, 0))
    past = pl.BlockSpec((bt,) + k_past.shape[1:], lambda b, i: (b, 0, 0))
    rows = bt * SB_HEADS * tq
    return pl.pallas_call(
        functools.partial(_attn_kernel, bt=bt, tq=tq, tk=ATTN_BLOCK, n_past=n_past),
        grid=(nb // bt, seq // tq),
        in_specs=[tok, tok, tok, past, past],
        out_specs=tok,
        out_shape=jax.ShapeDtypeStruct((nb, seq, SB_WIDTH), BF16),
        scratch_shapes=[pltpu.VMEM((rows, LANES), BF16),
                        pltpu.VMEM((bt, tq, SB_WIDTH), F32),
                        pltpu.VMEM((rows, LANES), F32)],
        compiler_params=_params(2),
        name="attn",
    )(q, k_new, v_new, k_past, v_past)


def _mlstm_kernel(mq_ref, mk_ref, mv_ref, mo_ref, grow_ref, c0_ref, n0_ref, m0_ref, gn_ref,
                  yb_ref, c_out_ref, n_out_ref, m_out_ref,
                  c_sc, n_sc, m_sc, alpha_sc, beta_sc, r_sc, wg_sc, dec_sc, mprev_sc, *, bt, lc, nc):
    step = pl.program_id(1)
    dk = ML_HEAD_DIM

    ri = lax.broadcasted_iota(jnp.int32, (lc, lc), 0)
    ci = lax.broadcasted_iota(jnp.int32, (lc, lc), 1)
    causal = ci <= ri
    eye = jnp.where(ri == ci, 1.0, 0.0).astype(BF16)
    ones_blk = jnp.ones((lc, dk), BF16)
    zeros_blk = jnp.zeros((lc, dk), BF16)

    @pl.when(step == 0)
    def _():
        c_sc[...] = c0_ref[...]
        n_sc[...] = n0_ref[...]
        incl_t = jnp.where(ri <= ci, 1.0, 0.0).astype(BF16)
        blocks = [(s, c) for s in range(bt) for c in range(nc)]
        li = jnp.concatenate([grow_ref[s, 0:SUBLANES, c * lc:(c + 1) * lc] for s, c in blocks], axis=0)
        lf = jnp.concatenate([grow_ref[s, SUBLANES:2 * SUBLANES, c * lc:(c + 1) * lc] for s, c in blocks],
                             axis=0)
        b = _split_dot(lf, incl_t, 3)
        r = li - b
        lane = lax.broadcasted_iota(jnp.int32, r.shape, 1)
        run = r
        d = 1
        while d < lc:
            run = jnp.maximum(run, jnp.where(lane >= d, pltpu.roll(run, d, axis=1), -jnp.inf))
            d *= 2
        b_last = b[:, lc - 1:lc]
        g = b_last - b + li
        g_max = jnp.max(g, axis=1, keepdims=True)
        m_before = []
        m_after = []
        for s in range(bt):
            m = m0_ref[s]
            for c in range(nc):
                rows = slice((s * nc + c) * SUBLANES, (s * nc + c + 1) * SUBLANES)
                m_before.append(m)
                m = jnp.maximum(b_last[rows] + m, g_max[rows])
                m_after.append(m)
            m_sc[s] = m
        m_prev = jnp.concatenate(m_before, axis=0)
        m_new = jnp.concatenate(m_after, axis=0)
        alpha = (-jnp.maximum(run, m_prev[:, :lc])).astype(BF16).astype(F32)
        alpha_sc[...] = alpha
        beta_sc[...] = alpha - b
        r_sc[...] = r
        wg_sc[...] = jnp.exp(g - m_new[:, :lc])
        dec_sc[...] = jnp.exp(b_last + m_prev - m_new)
        mprev_sc[...] = m_prev

    units = [(s, h) for s in range(bt) for h in range(ML_HEADS)]
    cols = [slice(h * dk, (h + 1) * dk) for h in range(ML_HEADS)]
    alpha, beta_hi, beta_lo, r, wg, decay, m_prev = [], [], [], [], [], [], []
    for s in range(bt):
        chunk = pl.ds(pl.multiple_of((s * nc + step) * SUBLANES, SUBLANES), SUBLANES)
        alpha.append(alpha_sc[chunk, :].astype(BF16))
        beta = beta_sc[chunk, :]
        beta_hi.append(beta.astype(BF16))
        beta_lo.append((beta - beta_hi[s].astype(F32)).astype(BF16))
        r.append(r_sc[chunk, :])
        wg.append(wg_sc[chunk, :])
        decay.append(dec_sc[chunk, :])
        m_prev.append(mprev_sc[chunk, :])
    eye2 = jnp.concatenate([eye, eye], axis=1)

    def rows_of(x, h):
        return jnp.broadcast_to(x[h:h + 1, :], (dk, lc))
    alpha_b = _dot_nt(eye, jnp.concatenate([rows_of(alpha[s], h) for s, h in units], axis=0))
    beta_b = _dot_nt(eye2, jnp.concatenate(
        [jnp.concatenate([rows_of(beta_hi[s], h), rows_of(beta_lo[s], h)], axis=1) for s, h in units], axis=0))
    qk = [_dot_nt(mq_ref[s, :, cols[h]], mk_ref[s, :, cols[h]]) for s, h in units]
    c_prev = [c_sc[s, h] for s, h in units]
    n_prev = [n_sc[s, h:h + 1, :] for s, h in units]
    qc = [_dot_nt(mq_ref[s, :, cols[h]],
                  jnp.concatenate([c_prev[u].astype(BF16),
                                   jnp.broadcast_to(n_prev[u], (dk, dk)).astype(BF16)], axis=0))
          for u, (s, h) in enumerate(units)]
    upd = []
    for s, h in units:
        wg_h = wg[s][h:h + 1, :]
        vt = mv_ref[s, :, cols[h]].astype(F32).T
        lhs = jnp.concatenate([(vt * wg_h).astype(BF16),
                               jnp.broadcast_to(wg_h, (SUBLANES, lc)).astype(BF16)], axis=0)
        upd.append(_dot(lhs, mk_ref[s, :, cols[h]]))

    cat = []
    for u, (s, h) in enumerate(units):
        w = jnp.where(causal, jnp.exp(alpha_b[:, u * dk:u * dk + lc] + r[s][h:h + 1, :]), 0.0)
        sw = qk[u] * w
        s_hi = sw.astype(BF16)
        cat.append(jnp.concatenate([s_hi, (sw - s_hi.astype(F32)).astype(BF16)], axis=1))
    nv = []
    for u, (s, h) in enumerate(units):
        rhs = jnp.concatenate([jnp.concatenate([mv_ref[s, :, cols[h]], ones_blk], axis=1),
                               jnp.concatenate([zeros_blk, ones_blk], axis=1)], axis=0)
        nv.append(_dot(cat[u], rhs))
    hval = []
    cat2 = []
    for u, (s, h) in enumerate(units):
        ucols = slice(u * dk, (u + 1) * dk)
        w_inter = jnp.exp(m_prev[s][h:h + 1, :] + alpha_b[:, ucols])
        num = nv[u][:, :dk] + w_inter * qc[u][:, :dk]
        den = nv[u][:, dk:] + w_inter * qc[u][:, dk:]
        hv = num / jnp.maximum(jnp.abs(den), jnp.exp(beta_b[:, ucols]))
        h2 = hv * hv
        h2_hi = h2.astype(BF16)
        hval.append(hv)
        cat2.append(jnp.concatenate([h2_hi, (h2 - h2_hi.astype(F32)).astype(BF16)], axis=1))
    ms = _dot(jnp.concatenate(cat2, axis=0), jnp.ones((2 * dk, dk), BF16)) * (1.0 / dk)
    for u, (s, h) in enumerate(units):
        y = hval[u] * lax.rsqrt(ms[u * lc:(u + 1) * lc] + EPS) * gn_ref[:, cols[h]]
        yb_ref[s, :, cols[h]] = (_sigmoid(mo_ref[s, :, cols[h]]) * y).astype(BF16)

    for u, (s, h) in enumerate(units):
        dec = decay[s][h:h + 1, :]
        c_sc[s, h] = dec * c_prev[u] + upd[u][:dk]
        n_sc[s, h:h + 1, :] = dec * n_prev[u] + upd[u][dk:dk + 1]

    c_out_ref[...] = c_sc[...]
    n_out_ref[...] = n_sc[...]
    m_out_ref[...] = m_sc[...]


def _mlstm(mq, mk, mv, mo, grow, c0, n0, m0, gn, *, bt, lc):
    nb, seq, _ = mq.shape
    assert nb % bt == 0
    tok = pl.BlockSpec((bt, lc, ML_WIDTH), lambda b, i: (b, i, 0))
    c_spec = pl.BlockSpec((bt, ML_HEADS, ML_HEAD_DIM, ML_HEAD_DIM), lambda b, i: (b, 0, 0, 0))
    n_spec = pl.BlockSpec((bt, ML_HEADS, ML_HEAD_DIM), lambda b, i: (b, 0, 0))
    m_spec = pl.BlockSpec((bt, SUBLANES, LANES), lambda b, i: (b, 0, 0))
    nc = seq // lc
    table = pltpu.VMEM((bt * nc * SUBLANES, lc), F32)
    table_m = pltpu.VMEM((bt * nc * SUBLANES, LANES), F32)
    return pl.pallas_call(
        functools.partial(_mlstm_kernel, bt=bt, lc=lc, nc=nc),
        grid=(nb // bt, nc),
        in_specs=[tok, tok, tok, tok,
                  pl.BlockSpec((bt, 2 * SUBLANES, seq), lambda b, i: (b, 0, 0)),
                  c_spec, n_spec, m_spec, _const_spec((1, ML_WIDTH))],
        out_specs=[tok, c_spec, n_spec, m_spec],
        out_shape=[jax.ShapeDtypeStruct((nb, seq, ML_WIDTH), BF16),
                   jax.ShapeDtypeStruct((nb, ML_HEADS, ML_HEAD_DIM, ML_HEAD_DIM), F32),
                   jax.ShapeDtypeStruct((nb, ML_HEADS, ML_HEAD_DIM), F32),
                   jax.ShapeDtypeStruct((nb, SUBLANES, LANES), F32)],
        scratch_shapes=[pltpu.VMEM((bt, ML_HEADS, ML_HEAD_DIM, ML_HEAD_DIM), F32),
                        pltpu.VMEM((bt, ML_HEADS, ML_HEAD_DIM), F32),
                        pltpu.VMEM((bt, SUBLANES, LANES), F32),
                        table, table, table, table, table_m, table_m],
        compiler_params=_params(2),
        name="mlstm",
    )(mq, mk, mv, mo, grow, c0, n0, m0, gn)


def _post_kernel(x_ref, ya_ref, yb_ref, mod_ref, g1_ref, g2_ref, gf_ref,
                 wgab_ref, wa_ref, wb_ref, wout_ref, wfg_ref, wfu_ref, wfd_ref,
                 y_ref, u_sc, *, bt, lt):
    g1 = g1_ref[...]
    for j in range(bt):
        u = _norm_mod(x_ref[j], g1, mod_ref[j, 1:2, :], mod_ref[j, 0:1, :])
        u_sc[j * lt:(j + 1) * lt, :] = u.astype(BF16)
    u = u_sc[...]
    ya = ya_ref[...].reshape(bt * lt, SB_WIDTH)
    yb = yb_ref[...].reshape(bt * lt, ML_WIDTH)
    ga = _dot_nt(u, wgab_ref[:D_MODEL, :])
    gb = _dot_nt(u, wgab_ref[D_MODEL:, :])
    merged = _sigmoid(ga) * _dot(ya, wa_ref[...]) + _sigmoid(gb) * _dot(yb, wb_ref[...])
    attn_out = _dot(merged.astype(BF16), wout_ref[...])

    g2 = g2_ref[...]
    for j in range(bt):
        rows = slice(j * lt, (j + 1) * lt)
        x1 = x_ref[j] + mod_ref[j, 2:3, :] * attn_out[rows]
        y_ref[j] = x1
        u2 = _norm_mod(x1, g2, mod_ref[j, 4:5, :], mod_ref[j, 3:4, :])
        u_sc[rows, :] = u2.astype(BF16)
    u2 = u_sc[...]
    hg = _dot(u2, wfg_ref[...])
    hu = _dot(u2, wfu_ref[...])
    act = (hg * _sigmoid(hg) * hu).astype(BF16)
    ff = _dot(act, wfd_ref[...])
    gf = gf_ref[...]
    for j in range(bt):
        rows = slice(j * lt, (j + 1) * lt)
        x2 = y_ref[j] + mod_ref[j, 5:6, :] * ff[rows]
        ms = jnp.mean(x2 * x2, axis=-1, keepdims=True)
        y_ref[j] = x2 * lax.rsqrt(ms + EPS) * gf


def _post(x, ya, yb, mod3, g1, g2, gf, wgab, wa, wb, wout, wfg, wfu, wfd, *, bt, lt):
    nb, seq, _ = x.shape
    assert nb % bt == 0 and seq % lt == 0

    def tok(width):
        return pl.BlockSpec((bt, lt, width), lambda b, l: (b, l, 0))

    in_specs = [tok(D_MODEL), tok(SB_WIDTH), tok(ML_WIDTH),
                pl.BlockSpec((bt, 6, D_MODEL), lambda b, l: (b, 0, 0)),
                _const_spec((1, D_MODEL)), _const_spec((1, D_MODEL)), _const_spec((1, D_MODEL)),
                _const_spec(wgab.shape), _const_spec(wa.shape), _const_spec(wb.shape), _const_spec(wout.shape),
                _const_spec(wfg.shape), _const_spec(wfu.shape), _const_spec(wfd.shape)]
    return pl.pallas_call(
        functools.partial(_post_kernel, bt=bt, lt=lt),
        grid=(nb // bt, seq // lt),
        in_specs=in_specs,
        out_specs=tok(D_MODEL),
        out_shape=jax.ShapeDtypeStruct((nb, seq, D_MODEL), F32),
        scratch_shapes=[pltpu.VMEM((bt * lt, D_MODEL), BF16)],
        compiler_params=_params(2),
        name="post",
    )(x, ya, yb, mod3, g1, g2, gf, wgab, wa, wb, wout, wfg, wfu, wfd)


def _group(x, mod3, k_past, v_past, c0, n0, m0, conv0, wts, *, bt, lt_in, lt, tq, n_past, lc):
    (g1, g2, gf, w1, wif_row, bif_row, wconv, bconv, gn,
     wgab, wa, wb, wout, wfg, wfu, wfd) = wts
    nb, seq, _ = x.shape
    (q, k32, v32, kb, vb, mq, mk, mv, mo, grow, conv_new) = _inproj(
        x, mod3, conv0, g1, w1, wif_row, bif_row, wconv, bconv, bt=bt, lt=lt_in)
    if k_past is None:
        k_past, v_past = kb, vb
    ya = _attn(q, kb, vb, k_past, v_past, bt=ATTN_SEQS, tq=tq, n_past=n_past)
    m0b = jnp.broadcast_to(jnp.pad(m0, ((0, 0), (0, SUBLANES - ML_HEADS)))[:, :, None], (nb, SUBLANES, LANES))
    yb, c1, n1, m1 = _mlstm(mq, mk, mv, mo, grow, c0, n0, m0b, gn, bt=ML_SEQS, lc=lc)
    y = _post(x, ya, yb, mod3, g1, g2, gf, wgab, wa, wb, wout, wfg, wfu, wfd, bt=bt, lt=lt)
    states = (k32.reshape(1, nb, seq, SB_HEADS, SB_HEAD_DIM), v32.reshape(1, nb, seq, SB_HEADS, SB_HEAD_DIM),
              c1[None], n1[None], m1[None, :, :ML_HEADS, 0], conv_new[None])
    return y, states


def kernel(x_prompt, x_sample, c_prompt, c_sample, cache_sb_k, cache_sb_v, state_mlstm_C, state_mlstm_n, state_mlstm_m, state_conv, norm1_g, norm2_g, w_ada, b_ada, w_in, b_if, w_conv, b_conv, ml_norm_g, w_a, w_b, w_out, w_ff_gate, w_ff_up, w_ff_down, final_g):
    assert w_in.shape[0] == 1, "single layer"
    bp, seq_p, _ = x_prompt.shape
    bs, seq_s, _ = x_sample.shape
    past = cache_sb_k.shape[2]

    c_all = jnp.concatenate([c_prompt, c_sample], axis=0)
    mod3 = _ada(c_all, w_ada[0], b_ada[0]).reshape(bp + bs, 6, D_MODEL)

    w1, wgab, wif_row, wa, wb, wout, wfg, wfu, wfd = _prep(
        jnp.transpose(w_in[0]), w_a[0], w_b[0], w_out[0], w_ff_gate[0], w_ff_up[0], w_ff_down[0])
    pad = SUBLANES - ML_HEADS
    bif_row = jnp.pad(b_if[0].reshape(2, ML_HEADS), ((0, 0), (0, pad))).reshape(2 * SUBLANES, 1)
    wts = (norm1_g[0].reshape(1, D_MODEL), norm2_g[0].reshape(1, D_MODEL), final_g.reshape(1, D_MODEL),
           w1, wif_row, bif_row, w_conv[0], b_conv[0].reshape(1, 2 * ML_WIDTH),
           ml_norm_g[0].reshape(1, ML_WIDTH),
           wgab, wa, wb, wout, wfg, wfu, wfd)

    zeros = functools.partial(jnp.zeros, dtype=F32)
    y_p, st_p = _group(
        x_prompt, mod3[:bp], None, None,
        zeros((bp, ML_HEADS, ML_HEAD_DIM, ML_HEAD_DIM)), zeros((bp, ML_HEADS, ML_HEAD_DIM)),
        zeros((bp, ML_HEADS)), zeros((bp, CONV_W - 1, 2 * ML_WIDTH)), wts,
        bt=1, lt_in=512, lt=256, tq=ATTN_BLOCK, n_past=None, lc=ML_CHUNK)
    def cache_t(c):
        return jnp.transpose(c[0], (0, 2, 3, 1)).reshape(bs, SB_WIDTH, past)

    y_s, st_s = _group(
        x_sample, mod3[bp:], cache_t(cache_sb_k), cache_t(cache_sb_v),
        state_mlstm_C[0], state_mlstm_n[0], state_mlstm_m[0], state_conv[0], wts,
        bt=4, lt_in=seq_s, lt=seq_s, tq=seq_s, n_past=past // ATTN_BLOCK, lc=seq_s)
    return (y_p, y_s) + st_p + st_s
```

```python
import functools
import math

import jax
import jax.numpy as jnp
from jax import lax
from jax.experimental import pallas as pl
from jax.experimental.pallas import tpu as pltpu

D_MODEL = 1024
SB_HEADS = 8
SB_HEAD_DIM = 64
SB_WIDTH = SB_HEADS * SB_HEAD_DIM
ML_HEADS = 4
ML_HEAD_DIM = 128
ML_WIDTH = ML_HEADS * ML_HEAD_DIM
CONV_W = 4
D_FF = 2816
EPS = 1e-6
LANES = 128
SUBLANES = 8
ATTN_BLOCK = 128
ML_CHUNK = 128
ML_SEQS = 4
EXP_UNDERFLOW = 88.0
SOFTPLUS_CLAMP = 60.0
CONV_CHUNKS = 4
ATTN_SEQS = 2
ATTN_GROUPS = 4
VMEM_LIMIT = 56 * 1024 * 1024

BF16 = jnp.bfloat16
F32 = jnp.float32

_C_Q, _C_K, _C_V = 0, SB_WIDTH, 2 * SB_WIDTH
_C_MQK = 3 * SB_WIDTH
_C_MV = _C_MQK + 2 * ML_WIDTH
_C_MO = _C_MV + ML_WIDTH
_C_IF = _C_MO + ML_WIDTH
_C_GAB = _C_IF + 2 * ML_HEADS
_C_END = _C_GAB + 2 * D_MODEL


def _sigmoid(x):
    return 1.0 / (1.0 + jnp.exp(-x))


def _log_sigmoid(x):
    return jnp.minimum(x, 0.0) - jnp.log1p(jnp.exp(-jnp.abs(x)))


def _softplus_pos(x):
    return jnp.maximum(jnp.log(1.0 + jnp.exp(jnp.minimum(x, SOFTPLUS_CLAMP))), x)


def _dot(a, b):
    return jnp.dot(a, b, preferred_element_type=F32)


def _dot_nt(a, b):
    return lax.dot_general(a, b, (((1,), (1,)), ((), ())), preferred_element_type=F32)


def _dot_tn(a, b):
    return lax.dot_general(a, b, (((0,), (0,)), ((), ())), preferred_element_type=F32)


def _split_dot(x, tri, parts):
    acc = None
    rem = x
    for i in range(parts):
        piece = rem.astype(BF16)
        term = _dot(piece, tri)
        acc = term if acc is None else acc + term
        if i + 1 < parts:
            rem = rem - piece.astype(F32)
    return acc


def _split_dot_left(tri, x, parts):
    acc = None
    rem = x
    for i in range(parts):
        piece = rem.astype(BF16)
        term = _dot(tri, piece)
        acc = term if acc is None else acc + term
        if i + 1 < parts:
            rem = rem - piece.astype(F32)
    return acc


def _norm_mod(x, g, sc, sh):
    ms = jnp.mean(x * x, axis=-1, keepdims=True)
    return x * lax.rsqrt(ms + EPS) * g * (1.0 + sc) + sh


def _const_spec(shape):
    nd = len(shape)
    return pl.BlockSpec(shape, lambda *_: (0,) * nd, pipeline_mode=pl.Buffered(1))


def _params(n_axes):
    return pltpu.CompilerParams(dimension_semantics=("arbitrary",) * n_axes,
                                vmem_limit_bytes=VMEM_LIMIT)


def _ada_kernel(c_ref, w_ref, b_ref, o_ref):
    c = c_ref[...]
    s = c * _sigmoid(c)
    o_ref[...] = _dot(s.astype(BF16), w_ref[...].astype(BF16)) + b_ref[...]


def _ada(c_all, w_ada, b_ada):
    nb = c_all.shape[0]
    n_out = w_ada.shape[1]
    tn = D_MODEL
    return pl.pallas_call(
        _ada_kernel,
        grid=(n_out // tn,),
        in_specs=[pl.BlockSpec((nb, D_MODEL), lambda j: (0, 0)),
                  pl.BlockSpec((D_MODEL, tn), lambda j: (0, j)),
                  pl.BlockSpec((1, tn), lambda j: (0, j))],
        out_specs=pl.BlockSpec((nb, tn), lambda j: (0, j)),
        out_shape=jax.ShapeDtypeStruct((nb, n_out), F32),
        compiler_params=_params(1),
        name="ada",
    )(c_all, w_ada, b_ada.reshape(1, n_out))


PREP_STEPS = 8


def _prep_kernel(w1_ref, wgab_ref, wif_ref, wa_ref, wb_ref, wout_ref, wfg_ref, wfu_ref, wfd_ref,
                 w1_o, wgab_o, wif_o, wa_o, wb_o, wout_o, wfg_o, wfu_o, wfd_o):
    for src, dst in ((w1_ref, w1_o), (wgab_ref, wgab_o), (wa_ref, wa_o), (wb_ref, wb_o), (wout_ref, wout_o),
                     (wfg_ref, wfg_o), (wfu_ref, wfu_o), (wfd_ref, wfd_o)):
        dst[...] = src[...].astype(BF16)
    gates = wif_ref[...]
    pad = jnp.zeros((SUBLANES - ML_HEADS, D_MODEL), F32)
    wif_o[...] = jnp.concatenate([gates[:ML_HEADS], pad, gates[ML_HEADS:], pad], axis=0).astype(BF16)


def _prep(w_in_t, w_a, w_b, w_out, w_fg, w_fu, w_fd):
    def rows(a, n_rows=None, first=0):
        n_rows = a.shape[0] if n_rows is None else n_rows
        step = n_rows // PREP_STEPS
        assert n_rows % (PREP_STEPS * 2 * SUBLANES) == 0 and first % SUBLANES == 0
        if first == 0:
            return pl.BlockSpec((step, a.shape[1]), lambda i: (i, 0))
        return pl.BlockSpec((pl.Element(step), pl.Element(a.shape[1])),
                            lambda i: (pl.multiple_of(first + i * step, SUBLANES), 0))

    def out(n_rows, n_cols):
        return (jax.ShapeDtypeStruct((n_rows, n_cols), BF16),
                pl.BlockSpec((n_rows // PREP_STEPS, n_cols), lambda i: (i, 0)))

    others = (w_a, w_b, w_out, w_fg, w_fu, w_fd)
    outs = [out(_C_IF, D_MODEL), out(_C_END - _C_GAB, D_MODEL),
            (jax.ShapeDtypeStruct((2 * SUBLANES, D_MODEL), BF16),
             pl.BlockSpec((2 * SUBLANES, D_MODEL), lambda i: (0, 0)))] + [out(*a.shape) for a in others]
    in_specs = [rows(w_in_t, _C_IF), rows(w_in_t, _C_END - _C_GAB, _C_GAB),
                pl.BlockSpec((pl.Element(2 * ML_HEADS), pl.Element(D_MODEL)), lambda i: (_C_IF, 0))
                ] + [rows(a) for a in others]
    return pl.pallas_call(
        _prep_kernel,
        grid=(PREP_STEPS,),
        in_specs=in_specs,
        out_specs=[o[1] for o in outs],
        out_shape=[o[0] for o in outs],
        compiler_params=_params(1),
        name="prep",
    )(w_in_t, w_in_t, w_in_t, *others)


def _tokens_on_lanes(tile_len):
    return tile_len % LANES == 0


def _inproj_kernel(x_ref, mod_ref, conv0_ref, g1_ref, w1_ref, wifr_ref, bifr_ref,
                   wconv_ref, bconv_ref,
                   q_ref, k32_ref, v32_ref, kb_ref, vb_ref, mq_ref, mk_ref, mv_ref, mo_ref,
                   grow_ref, convn_ref,
                   u_sc, xp_sc, *, bt, lt):
    @pl.when(pl.program_id(1) == 0)
    def _():
        for j in range(bt):
            xp_sc[j] = jnp.concatenate(
                [jnp.zeros((SUBLANES - (CONV_W - 1), 2 * ML_WIDTH), F32), conv0_ref[j]], axis=0)

    g1 = g1_ref[...]
    for j in range(bt):
        u = _norm_mod(x_ref[j], g1, mod_ref[j, 1:2, :], mod_ref[j, 0:1, :])
        u_sc[j * lt:(j + 1) * lt, :] = u.astype(BF16)
    u = u_sc[...]

    def mm(lo, hi):
        return _dot_nt(u, w1_ref[lo:hi, :])

    def new_rows(t):
        return t.T if _tokens_on_lanes(lt) else t

    wconv = wconv_ref[...]
    bconv = bconv_ref[...]
    k_scale = 1.0 / math.sqrt(ML_HEAD_DIM)
    mqk = mm(_C_MQK, _C_MV)
    width = 2 * ML_WIDTH // CONV_CHUNKS
    row8 = lax.broadcasted_iota(jnp.int32, (SUBLANES, width), 0)

    def conv_chunk(c):
        cols = slice(c * width, (c + 1) * width)
        for j in range(bt):
            x = mqk[j * lt:(j + 1) * lt, cols]
            before = xp_sc[j, :, cols]
            conv = bconv[:, cols] + x * wconv[CONV_W - 1:CONV_W, cols]
            for k in range(1, CONV_W):
                xr = pltpu.roll(x, k, axis=0)
                head = jnp.where(row8 < k, pltpu.roll(before, k, axis=0), xr[:SUBLANES])
                xk = jnp.concatenate([head, xr[SUBLANES:]], axis=0)
                conv = conv + xk * wconv[CONV_W - 1 - k:CONV_W - k, cols]
            conv = conv * _sigmoid(conv)
            if c < CONV_CHUNKS // 2:
                mq_ref[j, :, cols] = conv.astype(BF16)
            else:
                mk_ref[j, :, c * width - ML_WIDTH:(c + 1) * width - ML_WIDTH] = (conv * k_scale).astype(BF16)

    sq = mm(_C_Q, _C_K)
    conv_chunk(0)
    for j in range(bt):
        q_ref[j] = sq[j * lt:(j + 1) * lt].astype(BF16)
    sk = mm(_C_K, _C_V)
    conv_chunk(1)
    for j in range(bt):
        k32_ref[j] = new_rows(sk[j * lt:(j + 1) * lt])
        kb_ref[j] = sk[j * lt:(j + 1) * lt].astype(BF16)
    sv = mm(_C_V, _C_MQK)
    conv_chunk(2)
    for j in range(bt):
        v32_ref[j] = new_rows(sv[j * lt:(j + 1) * lt])
        vb_ref[j] = sv[j * lt:(j + 1) * lt].astype(BF16)
    mv = mm(_C_MV, _C_MO)
    conv_chunk(3)
    for j in range(bt):
        mv_ref[j] = mv[j * lt:(j + 1) * lt].astype(BF16)
        convn_ref[j] = mqk[(j + 1) * lt - (CONV_W - 1):(j + 1) * lt]
        xp_sc[j] = mqk[(j + 1) * lt - SUBLANES:(j + 1) * lt]
    mo = mm(_C_MO, _C_IF)
    for j in range(bt):
        rows = slice(j * lt, (j + 1) * lt)
        mo_ref[j] = mo[rows]
        gr = _dot_nt(wifr_ref[...], u_sc[rows, :]) + bifr_ref[...]
        row = lax.broadcasted_iota(jnp.int32, gr.shape, 0)
        grow_ref[j] = jnp.where(row >= SUBLANES, _log_sigmoid(gr), gr)


def _inproj(x, mod3, conv0, g1, w1, wif_row, bif_row, wconv, bconv, *, bt, lt):
    nb, seq, _ = x.shape
    assert nb % bt == 0 and seq % lt == 0 and (bt == 1 or lt == seq)
    grid = (nb // bt, seq // lt)

    def tok(width):
        return pl.BlockSpec((bt, lt, width), lambda b, l: (b, l, 0))

    def per_batch(rows, width):
        return pl.BlockSpec((bt, rows, width), lambda b, l: (b, 0, 0))

    in_specs = [tok(D_MODEL), per_batch(6, D_MODEL), per_batch(CONV_W - 1, 2 * ML_WIDTH),
                _const_spec((1, D_MODEL)), _const_spec(w1.shape),
                _const_spec((2 * SUBLANES, D_MODEL)),
                _const_spec((2 * SUBLANES, 1)),
                _const_spec((CONV_W, 2 * ML_WIDTH)), _const_spec((1, 2 * ML_WIDTH))]

    def tok_on_lanes(rows):
        return pl.BlockSpec((bt, rows, lt), lambda b, l: (b, 0, l))

    kv_spec = tok_on_lanes(SB_WIDTH) if _tokens_on_lanes(lt) else tok(SB_WIDTH)
    kv_shape = (nb, SB_WIDTH, seq) if _tokens_on_lanes(lt) else (nb, seq, SB_WIDTH)
    out_specs = [tok(SB_WIDTH), kv_spec, kv_spec, tok(SB_WIDTH), tok(SB_WIDTH)] + [tok(ML_WIDTH)] * 4 + [
        tok_on_lanes(2 * SUBLANES), per_batch(CONV_W - 1, 2 * ML_WIDTH)]

    def sds(shape, dt):
        return jax.ShapeDtypeStruct(shape, dt)

    out_shape = [sds((nb, seq, SB_WIDTH), BF16), sds(kv_shape, F32), sds(kv_shape, F32),
                 sds((nb, seq, SB_WIDTH), BF16), sds((nb, seq, SB_WIDTH), BF16),
                 sds((nb, seq, ML_WIDTH), BF16), sds((nb, seq, ML_WIDTH), BF16), sds((nb, seq, ML_WIDTH), BF16),
                 sds((nb, seq, ML_WIDTH), F32),
                 sds((nb, 2 * SUBLANES, seq), F32),
                 sds((nb, CONV_W - 1, 2 * ML_WIDTH), F32)]
    return pl.pallas_call(
        functools.partial(_inproj_kernel, bt=bt, lt=lt),
        grid=grid,
        in_specs=in_specs,
        out_specs=out_specs,
        out_shape=out_shape,
        scratch_shapes=[pltpu.VMEM((bt * lt, D_MODEL), BF16),
                        pltpu.VMEM((bt, SUBLANES, 2 * ML_WIDTH), F32)],
        compiler_params=_params(2),
        name="inproj",
    )(x, mod3, conv0, g1, w1, wif_row, bif_row, wconv, bconv)


def _cum_matrix(t):
    r = lax.broadcasted_iota(jnp.int32, (2 * t, LANES + t), 0)
    c = lax.broadcasted_iota(jnp.int32, (2 * t, LANES + t), 1)
    return jnp.where((c < LANES) | ((r & (t - 1)) > (c - LANES)), 1.0, 0.0).astype(BF16)


def _attn_kernel(q_ref, kd_ref, vd_ref, kp_ref, vp_ref, o_ref, qm_sc, acc_sc, carry_sc, *, bt, tq, tk, n_past):
    cache_past = n_past is not None
    n_blocks = n_past if cache_past else pl.program_id(1)
    scale = 1.0 / math.sqrt(SB_HEAD_DIM)
    n_pairs = SB_WIDTH // LANES
    lane = lax.broadcasted_iota(jnp.int32, (1, LANES), 1)
    first_head = lane < SB_HEAD_DIM

    n_units = bt * n_pairs
    for u in range(n_units):
        s, p = divmod(u, n_pairs)
        qp = q_ref[s, :, p * LANES:(p + 1) * LANES] * scale
        zero = jnp.zeros_like(qp)
        qm_sc[(2 * u) * tq:(2 * u + 1) * tq, :] = jnp.where(first_head, qp, zero)
        qm_sc[(2 * u + 1) * tq:(2 * u + 2) * tq, :] = jnp.where(first_head, zero, qp)

    per_group = n_units // ATTN_GROUPS
    groups = [range(g, g + per_group) for g in range(0, n_units, per_group)]
    group_rows = [slice(2 * grp[0] * tq, 2 * (grp[-1] + 1) * tq) for grp in groups]

    def scores(k_of, transposed):
        dot_k = _dot if transposed else _dot_nt
        return [jnp.concatenate([dot_k(qm_sc[2 * u * tq:(2 * u + 2) * tq, :], k_of(u)) for u in grp], axis=0)
                for grp in groups]

    def weights(zs, t, visible):
        cum = _cum_matrix(t)
        sps, sums = [], []
        for gi, rows in enumerate(group_rows):
            sp = _softplus_pos(zs[gi])
            if visible is not None:
                sp = jnp.where(visible[rows], sp, 0.0)
            hi = sp.astype(BF16)
            lo = (sp - hi.astype(F32)).astype(BF16)
            sps.append(sp)
            sums.append(_dot(jnp.concatenate([hi, lo], axis=1), cum))
        out = []
        for gi, rows in enumerate(group_rows):
            logit = zs[gi] - sps[gi] - sums[gi][:, LANES:]
            if visible is None:
                carry = carry_sc[rows, :]
                a = jnp.exp(logit - carry)
                carry_sc[rows, :] = carry + sums[gi][:, :LANES]
            else:
                a = jnp.where(visible[rows], jnp.exp(logit), 0.0)
                carry_sc[rows, :] = sums[gi][:, :LANES]
            out.append(a.astype(BF16))
        return out

    def accumulate(a_groups, v_of, transposed, assign=False):
        dot_v = _dot_nt if transposed else _dot
        for gi, grp in enumerate(groups):
            for j, u in enumerate(grp):
                s, p = divmod(u, n_pairs)
                cols = slice(p * LANES, (p + 1) * LANES)
                pv = dot_v(a_groups[gi][2 * j * tq:(2 * j + 2) * tq, :], v_of(u))
                out = jnp.where(first_head, pv[:tq], pv[tq:])
                if assign:
                    acc_sc[s, :, cols] = out
                else:
                    acc_sc[s, :, cols] += out

    def unit_cols(ref):
        def of(u):
            s, p = divmod(u, n_pairs)
            return ref[s, :, p * LANES:(p + 1) * LANES]
        return of

    def past_of(ref, block_index):
        start = pl.multiple_of(block_index * tk, tk)

        def of(u):
            s, p = divmod(u, n_pairs)
            if cache_past:
                return ref[s, p * LANES:(p + 1) * LANES, pl.ds(start, tk)].astype(BF16)
            return ref[s, pl.ds(start, tk), p * LANES:(p + 1) * LANES].astype(BF16)
        return of

    rd = lax.broadcasted_iota(jnp.int32, (2 * n_units * tq, tq), 0) & (tq - 1)
    cd = lax.broadcasted_iota(jnp.int32, (2 * n_units * tq, tq), 1)
    a_diag = weights(scores(unit_cols(kd_ref), False), tq, cd < rd)
    accumulate(a_diag, unit_cols(vd_ref), False, assign=True)

    def not_done():
        return jnp.min(carry_sc[...]) <= EXP_UNDERFLOW

    def cond(state):
        i, go = state
        return jnp.logical_and(i < n_blocks, go)

    def body(state):
        i, _ = state
        block_index = n_blocks - 1 - i
        a = weights(scores(past_of(kp_ref, block_index), cache_past), tk, None)
        accumulate(a, past_of(vp_ref, block_index), cache_past)
        return i + 1, not_done()

    lax.while_loop(cond, body, (jnp.int32(0), not_done()))
    o_ref[...] = acc_sc[...].astype(BF16)


def _attn(q, k_new, v_new, k_past, v_past, *, bt, tq, n_past):
    nb, seq, _ = q.shape
    assert nb % bt == 0
    tok = pl.BlockSpec((bt, tq, SB_WIDTH), lambda b, i: (b, i, 0))
    past = pl.BlockSpec((bt,) + k_past.shape[1:], lambda b, i: (b, 0, 0))
    rows = bt * SB_HEADS * tq
    return pl.pallas_call(
        functools.partial(_attn_kernel, bt=bt, tq=tq, tk=ATTN_BLOCK, n_past=n_past),
        grid=(nb // bt, seq // tq),
        in_specs=[tok, tok, tok, past, past],
        out_specs=tok,
        out_shape=jax.ShapeDtypeStruct((nb, seq, SB_WIDTH), BF16),
        scratch_shapes=[pltpu.VMEM((rows, LANES), BF16),
                        pltpu.VMEM((bt, tq, SB_WIDTH), F32),
                        pltpu.VMEM((rows, LANES), F32)],
        compiler_params=_params(2),
        name="attn",
    )(q, k_new, v_new, k_past, v_past)


def _mlstm_kernel(mq_ref, mk_ref, mv_ref, mo_ref, grow_ref, c0_ref, n0_ref, m0_ref, gn_ref,
                  yb_ref, c_out_ref, n_out_ref, m_out_ref,
                  c_sc, n_sc, m_sc, alpha_sc, beta_sc, r_sc, wg_sc, dec_sc, mprev_sc, *, bt, lc, nc):
    step = pl.program_id(1)
    dk = ML_HEAD_DIM

    ri = lax.broadcasted_iota(jnp.int32, (lc, lc), 0)
    ci = lax.broadcasted_iota(jnp.int32, (lc, lc), 1)
    causal = ci <= ri
    eye = jnp.where(ri == ci, 1.0, 0.0).astype(BF16)
    ones_blk = jnp.ones((lc, dk), BF16)
    zeros_blk = jnp.zeros((lc, dk), BF16)

    @pl.when(step == 0)
    def _():
        c_sc[...] = c0_ref[...]
        n_sc[...] = n0_ref[...]
        incl_t = jnp.where(ri <= ci, 1.0, 0.0).astype(BF16)
        blocks = [(s, c) for s in range(bt) for c in range(nc)]
        li = jnp.concatenate([grow_ref[s, 0:SUBLANES, c * lc:(c + 1) * lc] for s, c in blocks], axis=0)
        lf = jnp.concatenate([grow_ref[s, SUBLANES:2 * SUBLANES, c * lc:(c + 1) * lc] for s, c in blocks],
                             axis=0)
        b = _split_dot(lf, incl_t, 3)
        r = li - b
        lane = lax.broadcasted_iota(jnp.int32, r.shape, 1)
        run = r
        d = 1
        while d < lc:
            run = jnp.maximum(run, jnp.where(lane >= d, pltpu.roll(run, d, axis=1), -jnp.inf))
            d *= 2
        b_last = b[:, lc - 1:lc]
        g = b_last - b + li
        g_max = jnp.max(g, axis=1, keepdims=True)
        m_before = []
        m_after = []
        for s in range(bt):
            m = m0_ref[s]
            for c in range(nc):
                rows = slice((s * nc + c) * SUBLANES, (s * nc + c + 1) * SUBLANES)
                m_before.append(m)
                m = jnp.maximum(b_last[rows] + m, g_max[rows])
                m_after.append(m)
            m_sc[s] = m
        m_prev = jnp.concatenate(m_before, axis=0)
        m_new = jnp.concatenate(m_after, axis=0)
        alpha = (-jnp.maximum(run, m_prev[:, :lc])).astype(BF16).astype(F32)
        alpha_sc[...] = alpha
        beta_sc[...] = alpha - b
        r_sc[...] = r
        wg_sc[...] = jnp.exp(g - m_new[:, :lc])
        dec_sc[...] = jnp.exp(b_last + m_prev - m_new)
        mprev_sc[...] = m_prev

    units = [(s, h) for s in range(bt) for h in range(ML_HEADS)]
    cols = [slice(h * dk, (h + 1) * dk) for h in range(ML_HEADS)]
    alpha, beta_hi, beta_lo, r, wg, decay, m_prev = [], [], [], [], [], [], []
    for s in range(bt):
        chunk = pl.ds(pl.multiple_of((s * nc + step) * SUBLANES, SUBLANES), SUBLANES)
        alpha.append(alpha_sc[chunk, :].astype(BF16))
        beta = beta_sc[chunk, :]
        beta_hi.append(beta.astype(BF16))
        beta_lo.append((beta - beta_hi[s].astype(F32)).astype(BF16))
        r.append(r_sc[chunk, :])
        wg.append(wg_sc[chunk, :])
        decay.append(dec_sc[chunk, :])
        m_prev.append(mprev_sc[chunk, :])
    eye2 = jnp.concatenate([eye, eye], axis=1)

    def rows_of(x, h):
        return jnp.broadcast_to(x[h:h + 1, :], (dk, lc))
    alpha_b = _dot_nt(eye, jnp.concatenate([rows_of(alpha[s], h) for s, h in units], axis=0))
    beta_b = _dot_nt(eye2, jnp.concatenate(
        [jnp.concatenate([rows_of(beta_hi[s], h), rows_of(beta_lo[s], h)], axis=1) for s, h in units], axis=0))
    qk = [_dot_nt(mq_ref[s, :, cols[h]], mk_ref[s, :, cols[h]]) for s, h in units]
    c_prev = [c_sc[s, h] for s, h in units]
    n_prev = [n_sc[s, h:h + 1, :] for s, h in units]
    qc = [_dot_nt(mq_ref[s, :, cols[h]],
                  jnp.concatenate([c_prev[u].astype(BF16),
                                   jnp.broadcast_to(n_prev[u], (dk, dk)).astype(BF16)], axis=0))
          for u, (s, h) in enumerate(units)]
    upd = []
    for s, h in units:
        wg_h = wg[s][h:h + 1, :]
        vt = mv_ref[s, :, cols[h]].astype(F32).T
        lhs = jnp.concatenate([(vt * wg_h).astype(BF16),
                               jnp.broadcast_to(wg_h, (SUBLANES, lc)).astype(BF16)], axis=0)
        upd.append(_dot(lhs, mk_ref[s, :, cols[h]]))

    cat = []
    for u, (s, h) in enumerate(units):
        w = jnp.where(causal, jnp.exp(alpha_b[:, u * dk:u * dk + lc] + r[s][h:h + 1, :]), 0.0)
        sw = qk[u] * w
        s_hi = sw.astype(BF16)
        cat.append(jnp.concatenate([s_hi, (sw - s_hi.astype(F32)).astype(BF16)], axis=1))
    nv = []
    for u, (s, h) in enumerate(units):
        rhs = jnp.concatenate([jnp.concatenate([mv_ref[s, :, cols[h]], ones_blk], axis=1),
                               jnp.concatenate([zeros_blk, ones_blk], axis=1)], axis=0)
        nv.append(_dot(cat[u], rhs))
    hval = []
    cat2 = []
    for u, (s, h) in enumerate(units):
        ucols = slice(u * dk, (u + 1) * dk)
        w_inter = jnp.exp(m_prev[s][h:h + 1, :] + alpha_b[:, ucols])
        num = nv[u][:, :dk] + w_inter * qc[u][:, :dk]
        den = nv[u][:, dk:] + w_inter * qc[u][:, dk:]
        hv = num / jnp.maximum(jnp.abs(den), jnp.exp(beta_b[:, ucols]))
        h2 = hv * hv
        h2_hi = h2.astype(BF16)
        hval.append(hv)
        cat2.append(jnp.concatenate([h2_hi, (h2 - h2_hi.astype(F32)).astype(BF16)], axis=1))
    ms = _dot(jnp.concatenate(cat2, axis=0), jnp.ones((2 * dk, dk), BF16)) * (1.0 / dk)
    for u, (s, h) in enumerate(units):
        y = hval[u] * lax.rsqrt(ms[u * lc:(u + 1) * lc] + EPS) * gn_ref[:, cols[h]]
        yb_ref[s, :, cols[h]] = (_sigmoid(mo_ref[s, :, cols[h]]) * y).astype(BF16)

    for u, (s, h) in enumerate(units):
        dec = decay[s][h:h + 1, :]
        c_sc[s, h] = dec * c_prev[u] + upd[u][:dk]
        n_sc[s, h:h + 1, :] = dec * n_prev[u] + upd[u][dk:dk + 1]

    c_out_ref[...] = c_sc[...]
    n_out_ref[...] = n_sc[...]
    m_out_ref[...] = m_sc[...]


def _mlstm(mq, mk, mv, mo, grow, c0, n0, m0, gn, *, bt, lc):
    nb, seq, _ = mq.shape
    assert nb % bt == 0
    tok = pl.BlockSpec((bt, lc, ML_WIDTH), lambda b, i: (b, i, 0))
    c_spec = pl.BlockSpec((bt, ML_HEADS, ML_HEAD_DIM, ML_HEAD_DIM), lambda b, i: (b, 0, 0, 0))
    n_spec = pl.BlockSpec((bt, ML_HEADS, ML_HEAD_DIM), lambda b, i: (b, 0, 0))
    m_spec = pl.BlockSpec((bt, SUBLANES, LANES), lambda b, i: (b, 0, 0))
    nc = seq // lc
    table = pltpu.VMEM((bt * nc * SUBLANES, lc), F32)
    table_m = pltpu.VMEM((bt * nc * SUBLANES, LANES), F32)
    return pl.pallas_call(
        functools.partial(_mlstm_kernel, bt=bt, lc=lc, nc=nc),
        grid=(nb // bt, nc),
        in_specs=[tok, tok, tok, tok,
                  pl.BlockSpec((bt, 2 * SUBLANES, seq), lambda b, i: (b, 0, 0)),
                  c_spec, n_spec, m_spec, _const_spec((1, ML_WIDTH))],
        out_specs=[tok, c_spec, n_spec, m_spec],
        out_shape=[jax.ShapeDtypeStruct((nb, seq, ML_WIDTH), BF16),
                   jax.ShapeDtypeStruct((nb, ML_HEADS, ML_HEAD_DIM, ML_HEAD_DIM), F32),
                   jax.ShapeDtypeStruct((nb, ML_HEADS, ML_HEAD_DIM), F32),
                   jax.ShapeDtypeStruct((nb, SUBLANES, LANES), F32)],
        scratch_shapes=[pltpu.VMEM((bt, ML_HEADS, ML_HEAD_DIM, ML_HEAD_DIM), F32),
                        pltpu.VMEM((bt, ML_HEADS, ML_HEAD_DIM), F32),
                        pltpu.VMEM((bt, SUBLANES, LANES), F32),
                        table, table, table, table, table_m, table_m],
        compiler_params=_params(2),
        name="mlstm",
    )(mq, mk, mv, mo, grow, c0, n0, m0, gn)


def _post_kernel(x_ref, ya_ref, yb_ref, mod_ref, g1_ref, g2_ref, gf_ref,
                 wgab_ref, wa_ref, wb_ref, wout_ref, wfg_ref, wfu_ref, wfd_ref,
                 y_ref, u_sc, *, bt, lt):
    g1 = g1_ref[...]
    for j in range(bt):
        u = _norm_mod(x_ref[j], g1, mod_ref[j, 1:2, :], mod_ref[j, 0:1, :])
        u_sc[j * lt:(j + 1) * lt, :] = u.astype(BF16)
    u = u_sc[...]
    ya = ya_ref[...].reshape(bt * lt, SB_WIDTH)
    yb = yb_ref[...].reshape(bt * lt, ML_WIDTH)
    ga = _dot_nt(u, wgab_ref[:D_MODEL, :])
    gb = _dot_nt(u, wgab_ref[D_MODEL:, :])
    merged = _sigmoid(ga) * _dot(ya, wa_ref[...]) + _sigmoid(gb) * _dot(yb, wb_ref[...])
    attn_out = _dot(merged.astype(BF16), wout_ref[...])

    g2 = g2_ref[...]
    for j in range(bt):
        rows = slice(j * lt, (j + 1) * lt)
        x1 = x_ref[j] + mod_ref[j, 2:3, :] * attn_out[rows]
        y_ref[j] = x1
        u2 = _norm_mod(x1, g2, mod_ref[j, 4:5, :], mod_ref[j, 3:4, :])
        u_sc[rows, :] = u2.astype(BF16)
    u2 = u_sc[...]
    hg = _dot(u2, wfg_ref[...])
    hu = _dot(u2, wfu_ref[...])
    act = (hg * _sigmoid(hg) * hu).astype(BF16)
    ff = _dot(act, wfd_ref[...])
    gf = gf_ref[...]
    for j in range(bt):
        rows = slice(j * lt, (j + 1) * lt)
        x2 = y_ref[j] + mod_ref[j, 5:6, :] * ff[rows]
        ms = jnp.mean(x2 * x2, axis=-1, keepdims=True)
        y_ref[j] = x2 * lax.rsqrt(ms + EPS) * gf


def _post(x, ya, yb, mod3, g1, g2, gf, wgab, wa, wb, wout, wfg, wfu, wfd, *, bt, lt):
    nb, seq, _ = x.shape
    assert nb % bt == 0 and seq % lt == 0

    def tok(width):
        return pl.BlockSpec((bt, lt, width), lambda b, l: (b, l, 0))

    in_specs = [tok(D_MODEL), tok(SB_WIDTH), tok(ML_WIDTH),
                pl.BlockSpec((bt, 6, D_MODEL), lambda b, l: (b, 0, 0)),
                _const_spec((1, D_MODEL)), _const_spec((1, D_MODEL)), _const_spec((1, D_MODEL)),
                _const_spec(wgab.shape), _const_spec(wa.shape), _const_spec(wb.shape), _const_spec(wout.shape),
                _const_spec(wfg.shape), _const_spec(wfu.shape), _const_spec(wfd.shape)]
    return pl.pallas_call(
        functools.partial(_post_kernel, bt=bt, lt=lt),
        grid=(nb // bt, seq // lt),
        in_specs=in_specs,
        out_specs=tok(D_MODEL),
        out_shape=jax.ShapeDtypeStruct((nb, seq, D_MODEL), F32),
        scratch_shapes=[pltpu.VMEM((bt * lt, D_MODEL), BF16)],
        compiler_params=_params(2),
        name="post",
    )(x, ya, yb, mod3, g1, g2, gf, wgab, wa, wb, wout, wfg, wfu, wfd)


def _group(x, mod3, k_past, v_past, c0, n0, m0, conv0, wts, *, bt, lt_in, lt, tq, n_past, lc):
    (g1, g2, gf, w1, wif_row, bif_row, wconv, bconv, gn,
     wgab, wa, wb, wout, wfg, wfu, wfd) = wts
    nb, seq, _ = x.shape
    (q, k32, v32, kb, vb, mq, mk, mv, mo, grow, conv_new) = _inproj(
        x, mod3, conv0, g1, w1, wif_row, bif_row, wconv, bconv, bt=bt, lt=lt_in)
    if k_past is None:
        k_past, v_past = kb, vb
    ya = _attn(q, kb, vb, k_past, v_past, bt=ATTN_SEQS, tq=tq, n_past=n_past)
    m0b = jnp.broadcast_to(jnp.pad(m0, ((0, 0), (0, SUBLANES - ML_HEADS)))[:, :, None], (nb, SUBLANES, LANES))
    yb, c1, n1, m1 = _mlstm(mq, mk, mv, mo, grow, c0, n0, m0b, gn, bt=ML_SEQS, lc=lc)
    y = _post(x, ya, yb, mod3, g1, g2, gf, wgab, wa, wb, wout, wfg, wfu, wfd, bt=bt, lt=lt)
    def rows_out(t):
        if _tokens_on_lanes(lt_in):
            return jnp.transpose(t.reshape(nb, SB_HEADS, SB_HEAD_DIM, seq), (0, 3, 1, 2))[None]
        return t.reshape(1, nb, seq, SB_HEADS, SB_HEAD_DIM)

    states = (rows_out(k32), rows_out(v32),
              c1[None], n1[None], m1[None, :, :ML_HEADS, 0], conv_new[None])
    return y, states


def kernel(x_prompt, x_sample, c_prompt, c_sample, cache_sb_k, cache_sb_v, state_mlstm_C, state_mlstm_n, state_mlstm_m, state_conv, norm1_g, norm2_g, w_ada, b_ada, w_in, b_if, w_conv, b_conv, ml_norm_g, w_a, w_b, w_out, w_ff_gate, w_ff_up, w_ff_down, final_g):
    assert w_in.shape[0] == 1, "single layer"
    bp, seq_p, _ = x_prompt.shape
    bs, seq_s, _ = x_sample.shape
    past = cache_sb_k.shape[2]

    c_all = jnp.concatenate([c_prompt, c_sample], axis=0)
    mod3 = _ada(c_all, w_ada[0], b_ada[0]).reshape(bp + bs, 6, D_MODEL)

    w1, wgab, wif_row, wa, wb, wout, wfg, wfu, wfd = _prep(
        jnp.transpose(w_in[0]), w_a[0], w_b[0], w_out[0], w_ff_gate[0], w_ff_up[0], w_ff_down[0])
    pad = SUBLANES - ML_HEADS
    bif_row = jnp.pad(b_if[0].reshape(2, ML_HEADS), ((0, 0), (0, pad))).reshape(2 * SUBLANES, 1)
    wts = (norm1_g[0].reshape(1, D_MODEL), norm2_g[0].reshape(1, D_MODEL), final_g.reshape(1, D_MODEL),
           w1, wif_row, bif_row, w_conv[0], b_conv[0].reshape(1, 2 * ML_WIDTH),
           ml_norm_g[0].reshape(1, ML_WIDTH),
           wgab, wa, wb, wout, wfg, wfu, wfd)

    zeros = functools.partial(jnp.zeros, dtype=F32)
    y_p, st_p = _group(
        x_prompt, mod3[:bp], None, None,
        zeros((bp, ML_HEADS, ML_HEAD_DIM, ML_HEAD_DIM)), zeros((bp, ML_HEADS, ML_HEAD_DIM)),
        zeros((bp, ML_HEADS)), zeros((bp, CONV_W - 1, 2 * ML_WIDTH)), wts,
        bt=1, lt_in=512, lt=256, tq=ATTN_BLOCK, n_past=None, lc=ML_CHUNK)
    def cache_t(c):
        return jnp.transpose(c[0], (0, 2, 3, 1)).reshape(bs, SB_WIDTH, past)

    y_s, st_s = _group(
        x_sample, mod3[bp:], cache_t(cache_sb_k), cache_t(cache_sb_v),
        state_mlstm_C[0], state_mlstm_n[0], state_mlstm_m[0], state_conv[0], wts,
        bt=4, lt_in=seq_s, lt=seq_s, tq=seq_s, n_past=past // ATTN_BLOCK, lc=seq_s)
    return (y_p, y_s) + st_p + st_s
```

```python
import functools
import math

import jax
import jax.numpy as jnp
from jax import lax
from jax.experimental import pallas as pl
from jax.experimental.pallas import tpu as pltpu

D_MODEL = 1024
SB_HEADS = 8
SB_HEAD_DIM = 64
SB_WIDTH = SB_HEADS * SB_HEAD_DIM
ML_HEADS = 4
ML_HEAD_DIM = 128
ML_WIDTH = ML_HEADS * ML_HEAD_DIM
CONV_W = 4
D_FF = 2816
EPS = 1e-6
LANES = 128
SUBLANES = 8
ATTN_BLOCK = 128
ML_CHUNK = 128
ML_SEQS = 4
EXP_UNDERFLOW = 88.0
SOFTPLUS_CLAMP = 60.0
ATTN_SEQS = 2
ATTN_GROUPS = 4
V7X_VMEM_BYTES = 64 * 1024 * 1024
VMEM_LIMIT = V7X_VMEM_BYTES - 8 * 1024 * 1024

BF16 = jnp.bfloat16
F32 = jnp.float32

_C_Q, _C_K, _C_V = 0, SB_WIDTH, 2 * SB_WIDTH
_C_MQK = 3 * SB_WIDTH
_C_MV = _C_MQK + 2 * ML_WIDTH
_C_MO = _C_MV + ML_WIDTH
_C_IF = _C_MO + ML_WIDTH
_C_GAB = _C_IF + 2 * ML_HEADS
_C_END = _C_GAB + 2 * D_MODEL


def _sigmoid(x):
    return 1.0 / (1.0 + jnp.exp(-x))


def _log_sigmoid(x):
    return jnp.minimum(x, 0.0) - jnp.log1p(jnp.exp(-jnp.abs(x)))


def _softplus_pos(x):
    return jnp.maximum(jnp.log(1.0 + jnp.exp(jnp.minimum(x, SOFTPLUS_CLAMP))), x)


def _dot(a, b):
    return jnp.dot(a, b, preferred_element_type=F32)


def _dot_nt(a, b):
    return lax.dot_general(a, b, (((1,), (1,)), ((), ())), preferred_element_type=F32)


def _split_dot(x, tri, parts):
    acc = None
    rem = x
    for i in range(parts):
        piece = rem.astype(BF16)
        term = _dot(piece, tri)
        acc = term if acc is None else acc + term
        if i + 1 < parts:
            rem = rem - piece.astype(F32)
    return acc


def _norm_mod(x, g, sc, sh):
    ms = jnp.mean(x * x, axis=-1, keepdims=True)
    return (x * lax.rsqrt(ms + EPS)) * (g * (1.0 + sc)) + sh


def _const_spec(shape):
    nd = len(shape)
    return pl.BlockSpec(shape, lambda *_: (0,) * nd, pipeline_mode=pl.Buffered(1))


def _params(n_axes):
    return pltpu.CompilerParams(dimension_semantics=("arbitrary",) * n_axes,
                                vmem_limit_bytes=VMEM_LIMIT)


def _ada_kernel(c_ref, w_ref, b_ref, o_ref):
    c = c_ref[...]
    s = c * _sigmoid(c)
    o_ref[...] = _dot(s.astype(BF16), w_ref[...].astype(BF16)) + b_ref[...]


def _ada(c_all, w_ada, b_ada):
    nb = c_all.shape[0]
    n_out = w_ada.shape[1]
    tn = D_MODEL
    return pl.pallas_call(
        _ada_kernel,
        grid=(n_out // tn,),
        in_specs=[pl.BlockSpec((nb, D_MODEL), lambda j: (0, 0)),
                  pl.BlockSpec((D_MODEL, tn), lambda j: (0, j)),
                  pl.BlockSpec((1, tn), lambda j: (0, j))],
        out_specs=pl.BlockSpec((nb, tn), lambda j: (0, j)),
        out_shape=jax.ShapeDtypeStruct((nb, n_out), F32),
        compiler_params=_params(1),
        name="ada",
    )(c_all, w_ada, b_ada.reshape(1, n_out))


PREP_STEPS = 8


def _prep_kernel(w1_ref, wgab_ref, wif_ref, wa_ref, wb_ref, wout_ref, wfg_ref, wfu_ref, wfd_ref,
                 w1_o, wgab_o, wif_o, wa_o, wb_o, wout_o, wfg_o, wfu_o, wfd_o):
    for src, dst in ((w1_ref, w1_o), (wgab_ref, wgab_o), (wa_ref, wa_o), (wb_ref, wb_o), (wout_ref, wout_o),
                     (wfg_ref, wfg_o), (wfu_ref, wfu_o), (wfd_ref, wfd_o)):
        dst[...] = src[...].astype(BF16)
    gates = wif_ref[...]
    pad = jnp.zeros((SUBLANES - ML_HEADS, D_MODEL), F32)
    wif_o[...] = jnp.concatenate([gates[:ML_HEADS], pad, gates[ML_HEADS:], pad], axis=0).astype(BF16)


def _prep(w_in_t, w_a, w_b, w_out, w_fg, w_fu, w_fd):
    def rows(a, n_rows=None, first=0):
        n_rows = a.shape[0] if n_rows is None else n_rows
        step = n_rows // PREP_STEPS
        assert n_rows % (PREP_STEPS * 2 * SUBLANES) == 0 and first % SUBLANES == 0
        if first == 0:
            return pl.BlockSpec((step, a.shape[1]), lambda i: (i, 0))
        return pl.BlockSpec((pl.Element(step), pl.Element(a.shape[1])),
                            lambda i: (pl.multiple_of(first + i * step, SUBLANES), 0))

    def out(n_rows, n_cols):
        return (jax.ShapeDtypeStruct((n_rows, n_cols), BF16),
                pl.BlockSpec((n_rows // PREP_STEPS, n_cols), lambda i: (i, 0)))

    others = (w_a, w_b, w_out, w_fg, w_fu, w_fd)
    outs = [out(_C_IF, D_MODEL), out(_C_END - _C_GAB, D_MODEL),
            (jax.ShapeDtypeStruct((2 * SUBLANES, D_MODEL), BF16),
             pl.BlockSpec((2 * SUBLANES, D_MODEL), lambda i: (0, 0)))] + [out(*a.shape) for a in others]
    in_specs = [rows(w_in_t, _C_IF), rows(w_in_t, _C_END - _C_GAB, _C_GAB),
                pl.BlockSpec((pl.Element(2 * ML_HEADS), pl.Element(D_MODEL)), lambda i: (_C_IF, 0))
                ] + [rows(a) for a in others]
    return pl.pallas_call(
        _prep_kernel,
        grid=(PREP_STEPS,),
        in_specs=in_specs,
        out_specs=[o[1] for o in outs],
        out_shape=[o[0] for o in outs],
        compiler_params=_params(1),
        name="prep",
    )(w_in_t, w_in_t, w_in_t, *others)


def _tokens_on_lanes(tile_len):
    return tile_len % LANES == 0


def _inproj_kernel(x_ref, mod_ref, conv0_ref, g1_ref, w1_ref, wifr_ref, bifr_ref,
                   wconv_ref, bconv_ref,
                   q_ref, k32_ref, v32_ref, kb_ref, vb_ref, mq_ref, mk_ref, mv_ref, mo_ref,
                   grow_ref, convn_ref,
                   u_sc, xp_sc, *, bt, lt):
    @pl.when(pl.program_id(1) == 0)
    def _():
        for j in range(bt):
            xp_sc[j] = jnp.concatenate(
                [jnp.zeros((SUBLANES - (CONV_W - 1), 2 * ML_WIDTH), F32), conv0_ref[j]], axis=0)

    g1 = g1_ref[...]
    for j in range(bt):
        u = _norm_mod(x_ref[j], g1, mod_ref[j, 1:2, :], mod_ref[j, 0:1, :])
        u_sc[j * lt:(j + 1) * lt, :] = u.astype(BF16)
    u = u_sc[...]

    def mm(lo, hi):
        return _dot_nt(u, w1_ref[lo:hi, :])

    def new_rows(t):
        return t.T if _tokens_on_lanes(lt) else t

    wconv = wconv_ref[...]
    bconv = bconv_ref[...]
    k_scale = 1.0 / math.sqrt(ML_HEAD_DIM)
    mqk = mm(_C_MQK, _C_MV)
    row8 = lax.broadcasted_iota(jnp.int32, (SUBLANES, 2 * ML_WIDTH), 0)
    for j in range(bt):
        x = mqk[j * lt:(j + 1) * lt]
        before = xp_sc[j]
        conv = bconv + x * wconv[CONV_W - 1:CONV_W, :]
        for k in range(1, CONV_W):
            xr = pltpu.roll(x, k, axis=0)
            head = jnp.where(row8 < k, pltpu.roll(before, k, axis=0), xr[:SUBLANES])
            xk = jnp.concatenate([head, xr[SUBLANES:]], axis=0)
            conv = conv + xk * wconv[CONV_W - 1 - k:CONV_W - k, :]
        conv = conv * _sigmoid(conv)
        mq_ref[j] = conv[:, :ML_WIDTH].astype(BF16)
        mk_ref[j] = (conv[:, ML_WIDTH:] * k_scale).astype(BF16)
        convn_ref[j] = x[lt - (CONV_W - 1):]
        xp_sc[j] = x[lt - SUBLANES:]

    sq = mm(_C_Q, _C_K)
    sk = mm(_C_K, _C_V)
    sv = mm(_C_V, _C_MQK)
    mv = mm(_C_MV, _C_MO)
    mo = mm(_C_MO, _C_IF)
    for j in range(bt):
        rows = slice(j * lt, (j + 1) * lt)
        q_ref[j] = sq[rows].astype(BF16)
        k32_ref[j] = new_rows(sk[rows])
        kb_ref[j] = sk[rows].astype(BF16)
        v32_ref[j] = new_rows(sv[rows])
        vb_ref[j] = sv[rows].astype(BF16)
        mv_ref[j] = mv[rows].astype(BF16)
    for j in range(bt):
        rows = slice(j * lt, (j + 1) * lt)
        mo_ref[j] = mo[rows]
        gr = _dot_nt(wifr_ref[...], u_sc[rows, :]) + bifr_ref[...]
        row = lax.broadcasted_iota(jnp.int32, gr.shape, 0)
        grow_ref[j] = jnp.where(row >= SUBLANES, _log_sigmoid(gr), gr)


def _inproj(x, mod3, conv0, g1, w1, wif_row, bif_row, wconv, bconv, *, bt, lt):
    nb, seq, _ = x.shape
    assert nb % bt == 0 and seq % lt == 0 and (bt == 1 or lt == seq)
    grid = (nb // bt, seq // lt)

    def tok(width):
        return pl.BlockSpec((bt, lt, width), lambda b, l: (b, l, 0))

    def per_batch(rows, width):
        return pl.BlockSpec((bt, rows, width), lambda b, l: (b, 0, 0))

    in_specs = [tok(D_MODEL), per_batch(6, D_MODEL), per_batch(CONV_W - 1, 2 * ML_WIDTH),
                _const_spec((1, D_MODEL)), _const_spec(w1.shape),
                _const_spec((2 * SUBLANES, D_MODEL)),
                _const_spec((2 * SUBLANES, 1)),
                _const_spec((CONV_W, 2 * ML_WIDTH)), _const_spec((1, 2 * ML_WIDTH))]

    def tok_on_lanes(rows):
        return pl.BlockSpec((bt, rows, lt), lambda b, l: (b, 0, l))

    kv_spec = tok_on_lanes(SB_WIDTH) if _tokens_on_lanes(lt) else tok(SB_WIDTH)
    kv_shape = (nb, SB_WIDTH, seq) if _tokens_on_lanes(lt) else (nb, seq, SB_WIDTH)
    out_specs = [tok(SB_WIDTH), kv_spec, kv_spec, tok(SB_WIDTH), tok(SB_WIDTH)] + [tok(ML_WIDTH)] * 4 + [
        tok_on_lanes(2 * SUBLANES), per_batch(CONV_W - 1, 2 * ML_WIDTH)]

    def sds(shape, dt):
        return jax.ShapeDtypeStruct(shape, dt)

    out_shape = [sds((nb, seq, SB_WIDTH), BF16), sds(kv_shape, F32), sds(kv_shape, F32),
                 sds((nb, seq, SB_WIDTH), BF16), sds((nb, seq, SB_WIDTH), BF16),
                 sds((nb, seq, ML_WIDTH), BF16), sds((nb, seq, ML_WIDTH), BF16), sds((nb, seq, ML_WIDTH), BF16),
                 sds((nb, seq, ML_WIDTH), F32),
                 sds((nb, 2 * SUBLANES, seq), F32),
                 sds((nb, CONV_W - 1, 2 * ML_WIDTH), F32)]
    return pl.pallas_call(
        functools.partial(_inproj_kernel, bt=bt, lt=lt),
        grid=grid,
        in_specs=in_specs,
        out_specs=out_specs,
        out_shape=out_shape,
        scratch_shapes=[pltpu.VMEM((bt * lt, D_MODEL), BF16),
                        pltpu.VMEM((bt, SUBLANES, 2 * ML_WIDTH), F32)],
        compiler_params=_params(2),
        name="inproj",
    )(x, mod3, conv0, g1, w1, wif_row, bif_row, wconv, bconv)


def _cum_matrix(t):
    r = lax.broadcasted_iota(jnp.int32, (2 * t, LANES + t), 0)
    c = lax.broadcasted_iota(jnp.int32, (2 * t, LANES + t), 1)
    return jnp.where((c < LANES) | ((r & (t - 1)) > (c - LANES)), 1.0, 0.0).astype(BF16)


def _attn_kernel(q_ref, kd_ref, vd_ref, kp_ref, vp_ref, o_ref, qm_sc, acc_sc, carry_sc, *, bt, tq, tk, n_past):
    cache_past = n_past is not None
    n_blocks = n_past if cache_past else pl.program_id(1)
    scale = 1.0 / math.sqrt(SB_HEAD_DIM)
    n_pairs = SB_WIDTH // LANES
    lane = lax.broadcasted_iota(jnp.int32, (1, LANES), 1)
    first_head = lane < SB_HEAD_DIM

    n_units = bt * n_pairs
    for u in range(n_units):
        s, p = divmod(u, n_pairs)
        qp = q_ref[s, :, p * LANES:(p + 1) * LANES] * scale
        zero = jnp.zeros_like(qp)
        qm_sc[(2 * u) * tq:(2 * u + 1) * tq, :] = jnp.where(first_head, qp, zero)
        qm_sc[(2 * u + 1) * tq:(2 * u + 2) * tq, :] = jnp.where(first_head, zero, qp)

    per_group = n_units // ATTN_GROUPS
    groups = [range(g, g + per_group) for g in range(0, n_units, per_group)]
    group_rows = [slice(2 * grp[0] * tq, 2 * (grp[-1] + 1) * tq) for grp in groups]

    def scores(k_of, transposed):
        dot_k = _dot if transposed else _dot_nt
        return [jnp.concatenate([dot_k(qm_sc[2 * u * tq:(2 * u + 2) * tq, :], k_of(u)) for u in grp], axis=0)
                for grp in groups]

    def weights(zs, t, visible):
        cum = _cum_matrix(t)
        sps, sums = [], []
        for gi, rows in enumerate(group_rows):
            sp = _softplus_pos(zs[gi])
            if visible is not None:
                sp = jnp.where(visible[rows], sp, 0.0)
            hi = sp.astype(BF16)
            lo = (sp - hi.astype(F32)).astype(BF16)
            sps.append(sp)
            sums.append(_dot(jnp.concatenate([hi, lo], axis=1), cum))
        out = []
        for gi, rows in enumerate(group_rows):
            logit = zs[gi] - sps[gi] - sums[gi][:, LANES:]
            if visible is None:
                carry = carry_sc[rows, :]
                a = jnp.exp(logit - carry)
                carry_sc[rows, :] = carry + sums[gi][:, :LANES]
            else:
                a = jnp.where(visible[rows], jnp.exp(logit), 0.0)
                carry_sc[rows, :] = sums[gi][:, :LANES]
            out.append(a.astype(BF16))
        return out

    def accumulate(a_groups, v_of, transposed, assign=False):
        dot_v = _dot_nt if transposed else _dot
        for gi, grp in enumerate(groups):
            for j, u in enumerate(grp):
                s, p = divmod(u, n_pairs)
                cols = slice(p * LANES, (p + 1) * LANES)
                pv = dot_v(a_groups[gi][2 * j * tq:(2 * j + 2) * tq, :], v_of(u))
                out = jnp.where(first_head, pv[:tq], pv[tq:])
                if assign:
                    acc_sc[s, :, cols] = out
                else:
                    acc_sc[s, :, cols] += out

    def unit_cols(ref):
        def of(u):
            s, p = divmod(u, n_pairs)
            return ref[s, :, p * LANES:(p + 1) * LANES]
        return of

    def past_of(ref, block_index):
        start = pl.multiple_of(block_index * tk, tk)

        def of(u):
            s, p = divmod(u, n_pairs)
            if cache_past:
                return ref[s, p * LANES:(p + 1) * LANES, pl.ds(start, tk)].astype(BF16)
            return ref[s, pl.ds(start, tk), p * LANES:(p + 1) * LANES].astype(BF16)
        return of

    rd = lax.broadcasted_iota(jnp.int32, (2 * n_units * tq, tq), 0) & (tq - 1)
    cd = lax.broadcasted_iota(jnp.int32, (2 * n_units * tq, tq), 1)
    a_diag = weights(scores(unit_cols(kd_ref), False), tq, cd < rd)
    accumulate(a_diag, unit_cols(vd_ref), False, assign=True)

    def not_done():
        return jnp.min(carry_sc[...]) <= EXP_UNDERFLOW

    def past_block(i):
        block_index = n_blocks - 1 - i
        a = weights(scores(past_of(kp_ref, block_index), cache_past), tk, None)
        accumulate(a, past_of(vp_ref, block_index), cache_past)

    def pair_cond(state):
        i, go = state
        return jnp.logical_and(i + 1 < n_blocks, go)

    def pair_body(state):
        i, _ = state
        past_block(i)
        past_block(i + 1)
        return i + 2, not_done()

    def cond(state):
        i, go = state
        return jnp.logical_and(i < n_blocks, go)

    def body(state):
        i, _ = state
        past_block(i)
        return i + 1, not_done()

    state = lax.while_loop(pair_cond, pair_body, (jnp.int32(0), not_done()))
    lax.while_loop(cond, body, state)
    o_ref[...] = acc_sc[...].astype(BF16)


def _attn(q, k_new, v_new, k_past, v_past, *, bt, tq, n_past):
    nb, seq, _ = q.shape
    assert nb % bt == 0
    tok = pl.BlockSpec((bt, tq, SB_WIDTH), lambda b, i: (b, i, 0))
    past = pl.BlockSpec((bt,) + k_past.shape[1:], lambda b, i: (b, 0, 0))
    rows = bt * SB_HEADS * tq
    return pl.pallas_call(
        functools.partial(_attn_kernel, bt=bt, tq=tq, tk=ATTN_BLOCK, n_past=n_past),
        grid=(nb // bt, seq // tq),
        in_specs=[tok, tok, tok, past, past],
        out_specs=tok,
        out_shape=jax.ShapeDtypeStruct((nb, seq, SB_WIDTH), BF16),
        scratch_shapes=[pltpu.VMEM((rows, LANES), BF16),
                        pltpu.VMEM((bt, tq, SB_WIDTH), F32),
                        pltpu.VMEM((rows, LANES), F32)],
        compiler_params=_params(2),
        name="attn",
    )(q, k_new, v_new, k_past, v_past)


def _mlstm_kernel(mq_ref, mk_ref, mv_ref, mo_ref, grow_ref, c0_ref, n0_ref, m0_ref, gn_ref,
                  yb_ref, c_out_ref, n_out_ref, m_out_ref,
                  c_sc, n_sc, m_sc, alpha_sc, beta_sc, r_sc, wg_sc, dec_sc, mprev_sc, *, bt, lc, nc):
    step = pl.program_id(1)
    dk = ML_HEAD_DIM

    ri = lax.broadcasted_iota(jnp.int32, (lc, lc), 0)
    ci = lax.broadcasted_iota(jnp.int32, (lc, lc), 1)
    causal = ci <= ri
    eye = jnp.where(ri == ci, 1.0, 0.0).astype(BF16)
    ones_blk = jnp.ones((lc, dk), BF16)
    zeros_blk = jnp.zeros((lc, dk), BF16)

    @pl.when(step == 0)
    def _():
        c_sc[...] = c0_ref[...]
        n_sc[...] = n0_ref[...]
        incl_t = jnp.where(ri <= ci, 1.0, 0.0).astype(BF16)
        blocks = [(s, c) for s in range(bt) for c in range(nc)]
        li = jnp.concatenate([grow_ref[s, 0:SUBLANES, c * lc:(c + 1) * lc] for s, c in blocks], axis=0)
        lf = jnp.concatenate([grow_ref[s, SUBLANES:2 * SUBLANES, c * lc:(c + 1) * lc] for s, c in blocks],
                             axis=0)
        b = _split_dot(lf, incl_t, 3)
        r = li - b
        lane = lax.broadcasted_iota(jnp.int32, r.shape, 1)
        run = r
        d = 1
        while d < lc:
            run = jnp.maximum(run, jnp.where(lane >= d, pltpu.roll(run, d, axis=1), -jnp.inf))
            d *= 2
        b_last = b[:, lc - 1:lc]
        g = b_last - b + li
        g_max = jnp.max(g, axis=1, keepdims=True)
        m_before = []
        m_after = []
        for s in range(bt):
            m = m0_ref[s]
            for c in range(nc):
                rows = slice((s * nc + c) * SUBLANES, (s * nc + c + 1) * SUBLANES)
                m_before.append(m)
                m = jnp.maximum(b_last[rows] + m, g_max[rows])
                m_after.append(m)
            m_sc[s] = m
        m_prev = jnp.concatenate(m_before, axis=0)
        m_new = jnp.concatenate(m_after, axis=0)
        alpha = (-jnp.maximum(run, m_prev[:, :lc])).astype(BF16).astype(F32)
        alpha_sc[...] = alpha
        beta_sc[...] = alpha - b
        r_sc[...] = r
        wg_sc[...] = jnp.exp(g - m_new[:, :lc])
        dec_sc[...] = jnp.exp(b_last + m_prev - m_new)
        mprev_sc[...] = m_prev

    units = [(s, h) for s in range(bt) for h in range(ML_HEADS)]
    cols = [slice(h * dk, (h + 1) * dk) for h in range(ML_HEADS)]
    alpha, beta_hi, beta_lo, r, wg, decay, m_prev = [], [], [], [], [], [], []
    for s in range(bt):
        chunk = pl.ds(pl.multiple_of((s * nc + step) * SUBLANES, SUBLANES), SUBLANES)
        alpha.append(alpha_sc[chunk, :].astype(BF16))
        beta = beta_sc[chunk, :]
        beta_hi.append(beta.astype(BF16))
        beta_lo.append((beta - beta_hi[s].astype(F32)).astype(BF16))
        r.append(r_sc[chunk, :])
        wg.append(wg_sc[chunk, :])
        decay.append(dec_sc[chunk, :])
        m_prev.append(mprev_sc[chunk, :])
    eye2 = jnp.concatenate([eye, eye], axis=1)

    def rows_of(x, h):
        return jnp.broadcast_to(x[h:h + 1, :], (dk, lc))
    alpha_b = _dot_nt(eye, jnp.concatenate([rows_of(alpha[s], h) for s, h in units], axis=0))
    beta_b = _dot_nt(eye2, jnp.concatenate(
        [jnp.concatenate([rows_of(beta_hi[s], h), rows_of(beta_lo[s], h)], axis=1) for s, h in units], axis=0))
    qk = [_dot_nt(mq_ref[s, :, cols[h]], mk_ref[s, :, cols[h]]) for s, h in units]
    c_prev = [c_sc[s, h] for s, h in units]
    n_prev = [n_sc[s, h:h + 1, :] for s, h in units]
    qc = [_dot_nt(mq_ref[s, :, cols[h]],
                  jnp.concatenate([c_prev[u].astype(BF16),
                                   jnp.broadcast_to(n_prev[u], (dk, dk)).astype(BF16)], axis=0))
          for u, (s, h) in enumerate(units)]
    upd = []
    for s, h in units:
        wg_h = wg[s][h:h + 1, :]
        vt = mv_ref[s, :, cols[h]].astype(F32).T
        lhs = jnp.concatenate([(vt * wg_h).astype(BF16),
                               jnp.broadcast_to(wg_h, (SUBLANES, lc)).astype(BF16)], axis=0)
        upd.append(_dot(lhs, mk_ref[s, :, cols[h]]))

    cat = []
    for u, (s, h) in enumerate(units):
        w = jnp.where(causal, jnp.exp(alpha_b[:, u * dk:u * dk + lc] + r[s][h:h + 1, :]), 0.0)
        sw = qk[u] * w
        s_hi = sw.astype(BF16)
        cat.append(jnp.concatenate([s_hi, (sw - s_hi.astype(F32)).astype(BF16)], axis=1))
    nv = []
    for u, (s, h) in enumerate(units):
        rhs = jnp.concatenate([jnp.concatenate([mv_ref[s, :, cols[h]], ones_blk], axis=1),
                               jnp.concatenate([zeros_blk, ones_blk], axis=1)], axis=0)
        nv.append(_dot(cat[u], rhs))
    hval = []
    cat2 = []
    for u, (s, h) in enumerate(units):
        ucols = slice(u * dk, (u + 1) * dk)
        w_inter = jnp.exp(m_prev[s][h:h + 1, :] + alpha_b[:, ucols])
        num = nv[u][:, :dk] + w_inter * qc[u][:, :dk]
        den = nv[u][:, dk:] + w_inter * qc[u][:, dk:]
        hv = num / jnp.maximum(jnp.abs(den), jnp.exp(beta_b[:, ucols]))
        h2 = hv * hv
        h2_hi = h2.astype(BF16)
        hval.append(hv)
        cat2.append(jnp.concatenate([h2_hi, (h2 - h2_hi.astype(F32)).astype(BF16)], axis=1))
    ms = _dot(jnp.concatenate(cat2, axis=0), jnp.ones((2 * dk, dk), BF16)) * (1.0 / dk)
    for u, (s, h) in enumerate(units):
        y = hval[u] * lax.rsqrt(ms[u * lc:(u + 1) * lc] + EPS) * gn_ref[:, cols[h]]
        yb_ref[s, :, cols[h]] = (_sigmoid(mo_ref[s, :, cols[h]]) * y).astype(BF16)

    for u, (s, h) in enumerate(units):
        dec = decay[s][h:h + 1, :]
        c_sc[s, h] = dec * c_prev[u] + upd[u][:dk]
        n_sc[s, h:h + 1, :] = dec * n_prev[u] + upd[u][dk:dk + 1]

    c_out_ref[...] = c_sc[...]
    n_out_ref[...] = n_sc[...]
    m_out_ref[...] = m_sc[...]


def _mlstm(mq, mk, mv, mo, grow, c0, n0, m0, gn, *, bt, lc):
    nb, seq, _ = mq.shape
    assert nb % bt == 0
    tok = pl.BlockSpec((bt, lc, ML_WIDTH), lambda b, i: (b, i, 0))
    c_spec = pl.BlockSpec((bt, ML_HEADS, ML_HEAD_DIM, ML_HEAD_DIM), lambda b, i: (b, 0, 0, 0))
    n_spec = pl.BlockSpec((bt, ML_HEADS, ML_HEAD_DIM), lambda b, i: (b, 0, 0))
    m_spec = pl.BlockSpec((bt, SUBLANES, LANES), lambda b, i: (b, 0, 0))
    nc = seq // lc
    table = pltpu.VMEM((bt * nc * SUBLANES, lc), F32)
    table_m = pltpu.VMEM((bt * nc * SUBLANES, LANES), F32)
    return pl.pallas_call(
        functools.partial(_mlstm_kernel, bt=bt, lc=lc, nc=nc),
        grid=(nb // bt, nc),
        in_specs=[tok, tok, tok, tok,
                  pl.BlockSpec((bt, 2 * SUBLANES, seq), lambda b, i: (b, 0, 0)),
                  c_spec, n_spec, m_spec, _const_spec((1, ML_WIDTH))],
        out_specs=[tok, c_spec, n_spec, m_spec],
        out_shape=[jax.ShapeDtypeStruct((nb, seq, ML_WIDTH), BF16),
                   jax.ShapeDtypeStruct((nb, ML_HEADS, ML_HEAD_DIM, ML_HEAD_DIM), F32),
                   jax.ShapeDtypeStruct((nb, ML_HEADS, ML_HEAD_DIM), F32),
                   jax.ShapeDtypeStruct((nb, SUBLANES, LANES), F32)],
        scratch_shapes=[pltpu.VMEM((bt, ML_HEADS, ML_HEAD_DIM, ML_HEAD_DIM), F32),
                        pltpu.VMEM((bt, ML_HEADS, ML_HEAD_DIM), F32),
                        pltpu.VMEM((bt, SUBLANES, LANES), F32),
                        table, table, table, table, table_m, table_m],
        compiler_params=_params(2),
        name="mlstm",
    )(mq, mk, mv, mo, grow, c0, n0, m0, gn)


def _post_kernel(x_ref, ya_ref, yb_ref, mod_ref, g1_ref, g2_ref, gf_ref,
                 wgab_ref, wa_ref, wb_ref, wout_ref, wfg_ref, wfu_ref, wfd_ref,
                 y_ref, u_sc, *, bt, lt):
    g1 = g1_ref[...]
    for j in range(bt):
        u = _norm_mod(x_ref[j], g1, mod_ref[j, 1:2, :], mod_ref[j, 0:1, :])
        u_sc[j * lt:(j + 1) * lt, :] = u.astype(BF16)
    u = u_sc[...]
    ya = ya_ref[...].reshape(bt * lt, SB_WIDTH)
    yb = yb_ref[...].reshape(bt * lt, ML_WIDTH)
    ga = _dot_nt(u, wgab_ref[:D_MODEL, :])
    gb = _dot_nt(u, wgab_ref[D_MODEL:, :])
    merged = _sigmoid(ga) * _dot(ya, wa_ref[...]) + _sigmoid(gb) * _dot(yb, wb_ref[...])
    attn_out = _dot(merged.astype(BF16), wout_ref[...])

    g2 = g2_ref[...]
    for j in range(bt):
        rows = slice(j * lt, (j + 1) * lt)
        x1 = x_ref[j] + mod_ref[j, 2:3, :] * attn_out[rows]
        y_ref[j] = x1
        u2 = _norm_mod(x1, g2, mod_ref[j, 4:5, :], mod_ref[j, 3:4, :])
        u_sc[rows, :] = u2.astype(BF16)
    u2 = u_sc[...]
    hg = _dot(u2, wfg_ref[...])
    hu = _dot(u2, wfu_ref[...])
    act = (hg * _sigmoid(hg) * hu).astype(BF16)
    ff = _dot(act, wfd_ref[...])
    gf = gf_ref[...]
    for j in range(bt):
        rows = slice(j * lt, (j + 1) * lt)
        x2 = y_ref[j] + mod_ref[j, 5:6, :] * ff[rows]
        ms = jnp.mean(x2 * x2, axis=-1, keepdims=True)
        y_ref[j] = x2 * lax.rsqrt(ms + EPS) * gf


def _post(x, ya, yb, mod3, g1, g2, gf, wgab, wa, wb, wout, wfg, wfu, wfd, *, bt, lt):
    nb, seq, _ = x.shape
    assert nb % bt == 0 and seq % lt == 0

    def tok(width):
        return pl.BlockSpec((bt, lt, width), lambda b, l: (b, l, 0))

    in_specs = [tok(D_MODEL), tok(SB_WIDTH), tok(ML_WIDTH),
                pl.BlockSpec((bt, 6, D_MODEL), lambda b, l: (b, 0, 0)),
                _const_spec((1, D_MODEL)), _const_spec((1, D_MODEL)), _const_spec((1, D_MODEL)),
                _const_spec(wgab.shape), _const_spec(wa.shape), _const_spec(wb.shape), _const_spec(wout.shape),
                _const_spec(wfg.shape), _const_spec(wfu.shape), _const_spec(wfd.shape)]
    return pl.pallas_call(
        functools.partial(_post_kernel, bt=bt, lt=lt),
        grid=(nb // bt, seq // lt),
        in_specs=in_specs,
        out_specs=tok(D_MODEL),
        out_shape=jax.ShapeDtypeStruct((nb, seq, D_MODEL), F32),
        scratch_shapes=[pltpu.VMEM((bt * lt, D_MODEL), BF16)],
        compiler_params=_params(2),
        name="post",
    )(x, ya, yb, mod3, g1, g2, gf, wgab, wa, wb, wout, wfg, wfu, wfd)


def _group(x, mod3, k_past, v_past, c0, n0, m0, conv0, wts, *, bt, lt_in, lt, tq, n_past, lc):
    (g1, g2, gf, w1, wif_row, bif_row, wconv, bconv, gn,
     wgab, wa, wb, wout, wfg, wfu, wfd) = wts
    nb, seq, _ = x.shape
    (q, k32, v32, kb, vb, mq, mk, mv, mo, grow, conv_new) = _inproj(
        x, mod3, conv0, g1, w1, wif_row, bif_row, wconv, bconv, bt=bt, lt=lt_in)
    if k_past is None:
        k_past, v_past = kb, vb
    ya = _attn(q, kb, vb, k_past, v_past, bt=ATTN_SEQS, tq=tq, n_past=n_past)
    m0b = jnp.broadcast_to(jnp.pad(m0, ((0, 0), (0, SUBLANES - ML_HEADS)))[:, :, None], (nb, SUBLANES, LANES))
    yb, c1, n1, m1 = _mlstm(mq, mk, mv, mo, grow, c0, n0, m0b, gn, bt=ML_SEQS, lc=lc)
    y = _post(x, ya, yb, mod3, g1, g2, gf, wgab, wa, wb, wout, wfg, wfu, wfd, bt=bt, lt=lt)
    def rows_out(t):
        if _tokens_on_lanes(lt_in):
            return jnp.transpose(t.reshape(nb, SB_HEADS, SB_HEAD_DIM, seq), (0, 3, 1, 2))[None]
        return t.reshape(1, nb, seq, SB_HEADS, SB_HEAD_DIM)

    states = (rows_out(k32), rows_out(v32),
              c1[None], n1[None], m1[None, :, :ML_HEADS, 0], conv_new[None])
    return y, states


def kernel(x_prompt, x_sample, c_prompt, c_sample, cache_sb_k, cache_sb_v, state_mlstm_C, state_mlstm_n, state_mlstm_m, state_conv, norm1_g, norm2_g, w_ada, b_ada, w_in, b_if, w_conv, b_conv, ml_norm_g, w_a, w_b, w_out, w_ff_gate, w_ff_up, w_ff_down, final_g):
    assert w_in.shape[0] == 1, "single layer"
    bp, seq_p, _ = x_prompt.shape
    bs, seq_s, _ = x_sample.shape
    past = cache_sb_k.shape[2]

    c_all = jnp.concatenate([c_prompt, c_sample], axis=0)
    mod3 = _ada(c_all, w_ada[0], b_ada[0]).reshape(bp + bs, 6, D_MODEL)

    w1, wgab, wif_row, wa, wb, wout, wfg, wfu, wfd = _prep(
        jnp.transpose(w_in[0]), w_a[0], w_b[0], w_out[0], w_ff_gate[0], w_ff_up[0], w_ff_down[0])
    pad = SUBLANES - ML_HEADS
    bif_row = jnp.pad(b_if[0].reshape(2, ML_HEADS), ((0, 0), (0, pad))).reshape(2 * SUBLANES, 1)
    wts = (norm1_g[0].reshape(1, D_MODEL), norm2_g[0].reshape(1, D_MODEL), final_g.reshape(1, D_MODEL),
           w1, wif_row, bif_row, w_conv[0], b_conv[0].reshape(1, 2 * ML_WIDTH),
           ml_norm_g[0].reshape(1, ML_WIDTH),
           wgab, wa, wb, wout, wfg, wfu, wfd)

    zeros = functools.partial(jnp.zeros, dtype=F32)
    y_p, st_p = _group(
        x_prompt, mod3[:bp], None, None,
        zeros((bp, ML_HEADS, ML_HEAD_DIM, ML_HEAD_DIM)), zeros((bp, ML_HEADS, ML_HEAD_DIM)),
        zeros((bp, ML_HEADS)), zeros((bp, CONV_W - 1, 2 * ML_WIDTH)), wts,
        bt=1, lt_in=512, lt=256, tq=ATTN_BLOCK, n_past=None, lc=ML_CHUNK)
    def cache_t(c):
        return jnp.transpose(c[0], (0, 2, 3, 1)).reshape(bs, SB_WIDTH, past)

    y_s, st_s = _group(
        x_sample, mod3[bp:], cache_t(cache_sb_k), cache_t(cache_sb_v),
        state_mlstm_C[0], state_mlstm_n[0], state_mlstm_m[0], state_conv[0], wts,
        bt=4, lt_in=seq_s, lt=seq_s, tq=seq_s, n_past=past // ATTN_BLOCK, lc=seq_s)
    return (y_p, y_s) + st_p + st_s
```

```python
import functools
import math

import jax
import jax.numpy as jnp
from jax import lax
from jax.experimental import pallas as pl
from jax.experimental.pallas import tpu as pltpu

D_MODEL = 1024
SB_HEADS = 8
SB_HEAD_DIM = 64
SB_WIDTH = SB_HEADS * SB_HEAD_DIM
ML_HEADS = 4
ML_HEAD_DIM = 128
ML_WIDTH = ML_HEADS * ML_HEAD_DIM
CONV_W = 4
D_FF = 2816
EPS = 1e-6
LANES = 128
SUBLANES = 8
ATTN_BLOCK = 128
ML_CHUNK = 128
ML_SEQS = 4
EXP_UNDERFLOW = 88.0
SOFTPLUS_CLAMP = 60.0
ATTN_SEQS = 2
ATTN_GROUPS = 4
V7X_VMEM_BYTES = 64 * 1024 * 1024
VMEM_LIMIT = V7X_VMEM_BYTES - 8 * 1024 * 1024

BF16 = jnp.bfloat16
F32 = jnp.float32

_C_Q, _C_K, _C_V = 0, SB_WIDTH, 2 * SB_WIDTH
_C_MQK = 3 * SB_WIDTH
_C_MV = _C_MQK + 2 * ML_WIDTH
_C_MO = _C_MV + ML_WIDTH
_C_IF = _C_MO + ML_WIDTH
_C_GAB = _C_IF + 2 * ML_HEADS
_C_END = _C_GAB + 2 * D_MODEL


def _sigmoid(x):
    return 1.0 / (1.0 + jnp.exp(-x))


def _log_sigmoid(x):
    return jnp.minimum(x, 0.0) - jnp.log1p(jnp.exp(-jnp.abs(x)))


def _softplus_pos(x):
    return jnp.maximum(jnp.log(1.0 + jnp.exp(jnp.minimum(x, SOFTPLUS_CLAMP))), x)


def _dot(a, b):
    return jnp.dot(a, b, preferred_element_type=F32)


def _dot_nt(a, b):
    return lax.dot_general(a, b, (((1,), (1,)), ((), ())), preferred_element_type=F32)


def _split_dot(x, tri, parts):
    acc = None
    rem = x
    for i in range(parts):
        piece = rem.astype(BF16)
        term = _dot(piece, tri)
        acc = term if acc is None else acc + term
        if i + 1 < parts:
            rem = rem - piece.astype(F32)
    return acc


def _norm_mod(x, g, sc, sh):
    ms = jnp.mean(x * x, axis=-1, keepdims=True)
    return (x * lax.rsqrt(ms + EPS)) * (g * (1.0 + sc)) + sh


def _const_spec(shape):
    nd = len(shape)
    return pl.BlockSpec(shape, lambda *_: (0,) * nd, pipeline_mode=pl.Buffered(1))


def _params(n_axes):
    return pltpu.CompilerParams(dimension_semantics=("arbitrary",) * n_axes,
                                vmem_limit_bytes=VMEM_LIMIT)


def _ada_kernel(c_ref, w_ref, b_ref, o_ref):
    c = c_ref[...]
    s = c * _sigmoid(c)
    o_ref[...] = _dot(s.astype(BF16), w_ref[...].astype(BF16)) + b_ref[...]


def _ada(c_all, w_ada, b_ada):
    nb = c_all.shape[0]
    n_out = w_ada.shape[1]
    tn = D_MODEL
    return pl.pallas_call(
        _ada_kernel,
        grid=(n_out // tn,),
        in_specs=[pl.BlockSpec((nb, D_MODEL), lambda j: (0, 0)),
                  pl.BlockSpec((D_MODEL, tn), lambda j: (0, j)),
                  pl.BlockSpec((1, tn), lambda j: (0, j))],
        out_specs=pl.BlockSpec((nb, tn), lambda j: (0, j)),
        out_shape=jax.ShapeDtypeStruct((nb, n_out), F32),
        compiler_params=_params(1),
        name="ada",
    )(c_all, w_ada, b_ada.reshape(1, n_out))


PREP_STEPS = 8


def _prep_kernel(w1_ref, wgab_ref, wif_ref, wa_ref, wb_ref, wout_ref, wfg_ref, wfu_ref, wfd_ref,
                 w1_o, wgab_o, wif_o, wa_o, wb_o, wout_o, wfg_o, wfu_o, wfd_o):
    for src, dst in ((w1_ref, w1_o), (wgab_ref, wgab_o), (wa_ref, wa_o), (wb_ref, wb_o), (wout_ref, wout_o),
                     (wfg_ref, wfg_o), (wfu_ref, wfu_o), (wfd_ref, wfd_o)):
        dst[...] = src[...].astype(BF16)
    gates = wif_ref[...]
    pad = jnp.zeros((SUBLANES - ML_HEADS, D_MODEL), F32)
    wif_o[...] = jnp.concatenate([gates[:ML_HEADS], pad, gates[ML_HEADS:], pad], axis=0).astype(BF16)


def _prep(w_in_t, w_a, w_b, w_out, w_fg, w_fu, w_fd):
    def rows(a, n_rows=None, first=0):
        n_rows = a.shape[0] if n_rows is None else n_rows
        step = n_rows // PREP_STEPS
        assert n_rows % (PREP_STEPS * 2 * SUBLANES) == 0 and first % SUBLANES == 0
        if first == 0:
            return pl.BlockSpec((step, a.shape[1]), lambda i: (i, 0))
        return pl.BlockSpec((pl.Element(step), pl.Element(a.shape[1])),
                            lambda i: (pl.multiple_of(first + i * step, SUBLANES), 0))

    def out(n_rows, n_cols):
        return (jax.ShapeDtypeStruct((n_rows, n_cols), BF16),
                pl.BlockSpec((n_rows // PREP_STEPS, n_cols), lambda i: (i, 0)))

    others = (w_a, w_b, w_out, w_fg, w_fu, w_fd)
    outs = [out(_C_IF, D_MODEL), out(_C_END - _C_GAB, D_MODEL),
            (jax.ShapeDtypeStruct((2 * SUBLANES, D_MODEL), BF16),
             pl.BlockSpec((2 * SUBLANES, D_MODEL), lambda i: (0, 0)))] + [out(*a.shape) for a in others]
    in_specs = [rows(w_in_t, _C_IF), rows(w_in_t, _C_END - _C_GAB, _C_GAB),
                pl.BlockSpec((pl.Element(2 * ML_HEADS), pl.Element(D_MODEL)), lambda i: (_C_IF, 0))
                ] + [rows(a) for a in others]
    return pl.pallas_call(
        _prep_kernel,
        grid=(PREP_STEPS,),
        in_specs=in_specs,
        out_specs=[o[1] for o in outs],
        out_shape=[o[0] for o in outs],
        compiler_params=_params(1),
        name="prep",
    )(w_in_t, w_in_t, w_in_t, *others)


def _tokens_on_lanes(tile_len):
    return tile_len % LANES == 0


def _inproj_kernel(x_ref, mod_ref, conv0_ref, g1_ref, w1_ref, wifr_ref, bifr_ref,
                   wconv_ref, bconv_ref,
                   q_ref, k32_ref, v32_ref, kb_ref, vb_ref, mq_ref, mk_ref, mv_ref, mo_ref,
                   grow_ref, convn_ref,
                   u_sc, xp_sc, *, bt, lt):
    @pl.when(pl.program_id(1) == 0)
    def _():
        for j in range(bt):
            xp_sc[j] = jnp.concatenate(
                [jnp.zeros((SUBLANES - (CONV_W - 1), 2 * ML_WIDTH), F32), conv0_ref[j]], axis=0)

    g1 = g1_ref[...]
    for j in range(bt):
        u = _norm_mod(x_ref[j], g1, mod_ref[j, 1:2, :], mod_ref[j, 0:1, :])
        u_sc[j * lt:(j + 1) * lt, :] = u.astype(BF16)
    u = u_sc[...]

    def mm(lo, hi):
        return _dot_nt(u, w1_ref[lo:hi, :])

    def new_rows(t):
        return t.T if _tokens_on_lanes(lt) else t

    wconv = wconv_ref[...]
    bconv = bconv_ref[...]
    k_scale = 1.0 / math.sqrt(ML_HEAD_DIM)
    mqk = mm(_C_MQK, _C_MV)
    row8 = lax.broadcasted_iota(jnp.int32, (SUBLANES, 2 * ML_WIDTH), 0)
    for j in range(bt):
        x = mqk[j * lt:(j + 1) * lt]
        before = xp_sc[j]
        conv = bconv + x * wconv[CONV_W - 1:CONV_W, :]
        for k in range(1, CONV_W):
            xr = pltpu.roll(x, k, axis=0)
            head = jnp.where(row8 < k, pltpu.roll(before, k, axis=0), xr[:SUBLANES])
            xk = jnp.concatenate([head, xr[SUBLANES:]], axis=0)
            conv = conv + xk * wconv[CONV_W - 1 - k:CONV_W - k, :]
        conv = conv * _sigmoid(conv)
        mq_ref[j] = conv[:, :ML_WIDTH].astype(BF16)
        mk_ref[j] = (conv[:, ML_WIDTH:] * k_scale).astype(BF16)
        convn_ref[j] = x[lt - (CONV_W - 1):]
        xp_sc[j] = x[lt - SUBLANES:]

    sq = mm(_C_Q, _C_K)
    sk = mm(_C_K, _C_V)
    sv = mm(_C_V, _C_MQK)
    mv = mm(_C_MV, _C_MO)
    mo = mm(_C_MO, _C_IF)
    for j in range(bt):
        rows = slice(j * lt, (j + 1) * lt)
        q_ref[j] = sq[rows].astype(BF16)
        k32_ref[j] = new_rows(sk[rows])
        kb_ref[j] = sk[rows].astype(BF16)
        v32_ref[j] = new_rows(sv[rows])
        vb_ref[j] = sv[rows].astype(BF16)
        mv_ref[j] = mv[rows].astype(BF16)
    for j in range(bt):
        rows = slice(j * lt, (j + 1) * lt)
        mo_ref[j] = mo[rows]
        gr = _dot_nt(wifr_ref[...], u_sc[rows, :]) + bifr_ref[...]
        row = lax.broadcasted_iota(jnp.int32, gr.shape, 0)
        grow_ref[j] = jnp.where(row >= SUBLANES, _log_sigmoid(gr), gr)


def _inproj(x, mod3, conv0, g1, w1, wif_row, bif_row, wconv, bconv, *, bt, lt):
    nb, seq, _ = x.shape
    assert nb % bt == 0 and seq % lt == 0 and (bt == 1 or lt == seq)
    grid = (nb // bt, seq // lt)

    def tok(width):
        return pl.BlockSpec((bt, lt, width), lambda b, l: (b, l, 0))

    def per_batch(rows, width):
        return pl.BlockSpec((bt, rows, width), lambda b, l: (b, 0, 0))

    in_specs = [tok(D_MODEL), per_batch(6, D_MODEL), per_batch(CONV_W - 1, 2 * ML_WIDTH),
                _const_spec((1, D_MODEL)), _const_spec(w1.shape),
                _const_spec((2 * SUBLANES, D_MODEL)),
                _const_spec((2 * SUBLANES, 1)),
                _const_spec((CONV_W, 2 * ML_WIDTH)), _const_spec((1, 2 * ML_WIDTH))]

    def tok_on_lanes(rows):
        return pl.BlockSpec((bt, rows, lt), lambda b, l: (b, 0, l))

    kv_spec = tok_on_lanes(SB_WIDTH) if _tokens_on_lanes(lt) else tok(SB_WIDTH)
    kv_shape = (nb, SB_WIDTH, seq) if _tokens_on_lanes(lt) else (nb, seq, SB_WIDTH)
    out_specs = [tok(SB_WIDTH), kv_spec, kv_spec, tok(SB_WIDTH), tok(SB_WIDTH)] + [tok(ML_WIDTH)] * 4 + [
        tok_on_lanes(2 * SUBLANES), per_batch(CONV_W - 1, 2 * ML_WIDTH)]

    def sds(shape, dt):
        return jax.ShapeDtypeStruct(shape, dt)

    out_shape = [sds((nb, seq, SB_WIDTH), BF16), sds(kv_shape, F32), sds(kv_shape, F32),
                 sds((nb, seq, SB_WIDTH), BF16), sds((nb, seq, SB_WIDTH), BF16),
                 sds((nb, seq, ML_WIDTH), BF16), sds((nb, seq, ML_WIDTH), BF16), sds((nb, seq, ML_WIDTH), BF16),
                 sds((nb, seq, ML_WIDTH), F32),
                 sds((nb, 2 * SUBLANES, seq), F32),
                 sds((nb, CONV_W - 1, 2 * ML_WIDTH), F32)]
    return pl.pallas_call(
        functools.partial(_inproj_kernel, bt=bt, lt=lt),
        grid=grid,
        in_specs=in_specs,
        out_specs=out_specs,
        out_shape=out_shape,
        scratch_shapes=[pltpu.VMEM((bt * lt, D_MODEL), BF16),
                        pltpu.VMEM((bt, SUBLANES, 2 * ML_WIDTH), F32)],
        compiler_params=_params(2),
        name="inproj",
    )(x, mod3, conv0, g1, w1, wif_row, bif_row, wconv, bconv)


def _cum_matrix(t):
    r = lax.broadcasted_iota(jnp.int32, (2 * t, LANES + t), 0)
    c = lax.broadcasted_iota(jnp.int32, (2 * t, LANES + t), 1)
    return jnp.where((c < LANES) | ((r & (t - 1)) > (c - LANES)), 1.0, 0.0).astype(BF16)


def _attn_kernel(q_ref, kd_ref, vd_ref, kp_ref, vp_ref, o_ref, qm_sc, acc_sc, carry_sc, *, bt, tq, tk, n_past):
    cache_past = n_past is not None
    n_blocks = n_past if cache_past else pl.program_id(1)
    scale = 1.0 / math.sqrt(SB_HEAD_DIM)
    n_pairs = SB_WIDTH // LANES
    lane = lax.broadcasted_iota(jnp.int32, (1, LANES), 1)
    first_head = lane < SB_HEAD_DIM

    n_units = bt * n_pairs
    for u in range(n_units):
        s, p = divmod(u, n_pairs)
        qp = q_ref[s, :, p * LANES:(p + 1) * LANES] * scale
        zero = jnp.zeros_like(qp)
        qm_sc[(2 * u) * tq:(2 * u + 1) * tq, :] = jnp.where(first_head, qp, zero)
        qm_sc[(2 * u + 1) * tq:(2 * u + 2) * tq, :] = jnp.where(first_head, zero, qp)

    per_group = n_units // ATTN_GROUPS
    groups = [range(g, g + per_group) for g in range(0, n_units, per_group)]
    group_rows = [slice(2 * grp[0] * tq, 2 * (grp[-1] + 1) * tq) for grp in groups]

    def scores(k_of, transposed):
        dot_k = _dot if transposed else _dot_nt
        return [jnp.concatenate([dot_k(qm_sc[2 * u * tq:(2 * u + 2) * tq, :], k_of(u)) for u in grp], axis=0)
                for grp in groups]

    def weights(zs, t, visible):
        cum = _cum_matrix(t)
        sps, sums = [], []
        for gi, rows in enumerate(group_rows):
            sp = _softplus_pos(zs[gi])
            if visible is not None:
                sp = jnp.where(visible[rows], sp, 0.0)
            hi = sp.astype(BF16)
            lo = (sp - hi.astype(F32)).astype(BF16)
            sps.append(sp)
            sums.append(_dot(jnp.concatenate([hi, lo], axis=1), cum))
        out = []
        for gi, rows in enumerate(group_rows):
            logit = zs[gi] - sps[gi] - sums[gi][:, LANES:]
            if visible is None:
                carry = carry_sc[rows, :]
                a = jnp.exp(logit - carry)
                carry_sc[rows, :] = carry + sums[gi][:, :LANES]
            else:
                a = jnp.where(visible[rows], jnp.exp(logit), 0.0)
                carry_sc[rows, :] = sums[gi][:, :LANES]
            out.append(a.astype(BF16))
        return out

    def accumulate(a_groups, v_of, transposed, assign=False):
        dot_v = _dot_nt if transposed else _dot
        for gi, grp in enumerate(groups):
            for j, u in enumerate(grp):
                s, p = divmod(u, n_pairs)
                cols = slice(p * LANES, (p + 1) * LANES)
                pv = dot_v(a_groups[gi][2 * j * tq:(2 * j + 2) * tq, :], v_of(u))
                out = jnp.where(first_head, pv[:tq], pv[tq:])
                if assign:
                    acc_sc[s, :, cols] = out
                else:
                    acc_sc[s, :, cols] += out

    def unit_cols(ref):
        def of(u):
            s, p = divmod(u, n_pairs)
            return ref[s, :, p * LANES:(p + 1) * LANES]
        return of

    def past_of(ref, block_index):
        start = pl.multiple_of(block_index * tk, tk)

        def of(u):
            s, p = divmod(u, n_pairs)
            if cache_past:
                return ref[s, p * LANES:(p + 1) * LANES, pl.ds(start, tk)].astype(BF16)
            return ref[s, pl.ds(start, tk), p * LANES:(p + 1) * LANES].astype(BF16)
        return of

    def diagonal_block():
        rd = lax.broadcasted_iota(jnp.int32, (2 * n_units * tq, tq), 0) & (tq - 1)
        cd = lax.broadcasted_iota(jnp.int32, (2 * n_units * tq, tq), 1)
        a_diag = weights(scores(unit_cols(kd_ref), False), tq, cd < rd)
        accumulate(a_diag, unit_cols(vd_ref), False, assign=True)

    def not_done():
        return jnp.min(carry_sc[...]) <= EXP_UNDERFLOW

    def past_block(i):
        block_index = n_blocks - 1 - i
        a = weights(scores(past_of(kp_ref, block_index), cache_past), tk, None)
        accumulate(a, past_of(vp_ref, block_index), cache_past)

    def pair_cond(state):
        i, go = state
        return jnp.logical_and(i + 1 < n_blocks, go)

    def pair_body(state):
        i, _ = state
        past_block(i)
        past_block(i + 1)
        return i + 2, not_done()

    def cond(state):
        i, go = state
        return jnp.logical_and(i < n_blocks, go)

    def body(state):
        i, _ = state
        past_block(i)
        return i + 1, not_done()

    def opening(with_pair):
        diagonal_block()
        if with_pair:
            past_block(0)
            past_block(1)

    if cache_past:
        opening(n_blocks >= 2)
        first = 2 if n_blocks >= 2 else 0
    else:
        pl.when(n_blocks >= 2)(functools.partial(opening, True))
        pl.when(n_blocks < 2)(functools.partial(opening, False))
        first = jnp.where(n_blocks >= 2, 2, 0)
    state = lax.while_loop(pair_cond, pair_body, (jnp.int32(first), not_done()))
    lax.while_loop(cond, body, state)
    o_ref[...] = acc_sc[...].astype(BF16)


def _attn(q, k_new, v_new, k_past, v_past, *, bt, tq, n_past):
    nb, seq, _ = q.shape
    assert nb % bt == 0
    tok = pl.BlockSpec((bt, tq, SB_WIDTH), lambda b, i: (b, i, 0))
    past = pl.BlockSpec((bt,) + k_past.shape[1:], lambda b, i: (b, 0, 0))
    rows = bt * SB_HEADS * tq
    return pl.pallas_call(
        functools.partial(_attn_kernel, bt=bt, tq=tq, tk=ATTN_BLOCK, n_past=n_past),
        grid=(nb // bt, seq // tq),
        in_specs=[tok, tok, tok, past, past],
        out_specs=tok,
        out_shape=jax.ShapeDtypeStruct((nb, seq, SB_WIDTH), BF16),
        scratch_shapes=[pltpu.VMEM((rows, LANES), BF16),
                        pltpu.VMEM((bt, tq, SB_WIDTH), F32),
                        pltpu.VMEM((rows, LANES), F32)],
        compiler_params=_params(2),
        name="attn",
    )(q, k_new, v_new, k_past, v_past)


def _mlstm_kernel(mq_ref, mk_ref, mv_ref, mo_ref, grow_ref, c0_ref, n0_ref, m0_ref, gn_ref,
                  yb_ref, c_out_ref, n_out_ref, m_out_ref,
                  c_sc, n_sc, m_sc, alpha_sc, beta_sc, r_sc, wg_sc, dec_sc, mprev_sc, *, bt, lc, nc):
    step = pl.program_id(1)
    dk = ML_HEAD_DIM

    ri = lax.broadcasted_iota(jnp.int32, (lc, lc), 0)
    ci = lax.broadcasted_iota(jnp.int32, (lc, lc), 1)
    causal = ci <= ri
    eye = jnp.where(ri == ci, 1.0, 0.0).astype(BF16)
    ones_blk = jnp.ones((lc, dk), BF16)
    zeros_blk = jnp.zeros((lc, dk), BF16)

    @pl.when(step == 0)
    def _():
        c_sc[...] = c0_ref[...]
        n_sc[...] = n0_ref[...]
        incl_t = jnp.where(ri <= ci, 1.0, 0.0).astype(BF16)
        blocks = [(s, c) for s in range(bt) for c in range(nc)]
        li = jnp.concatenate([grow_ref[s, 0:SUBLANES, c * lc:(c + 1) * lc] for s, c in blocks], axis=0)
        lf = jnp.concatenate([grow_ref[s, SUBLANES:2 * SUBLANES, c * lc:(c + 1) * lc] for s, c in blocks],
                             axis=0)
        b = _split_dot(lf, incl_t, 3)
        r = li - b
        lane = lax.broadcasted_iota(jnp.int32, r.shape, 1)
        run = r
        d = 1
        while d < lc:
            run = jnp.maximum(run, jnp.where(lane >= d, pltpu.roll(run, d, axis=1), -jnp.inf))
            d *= 2
        b_last = b[:, lc - 1:lc]
        g = b_last - b + li
        g_max = jnp.max(g, axis=1, keepdims=True)
        m_before = []
        m_after = []
        for s in range(bt):
            m = m0_ref[s]
            for c in range(nc):
                rows = slice((s * nc + c) * SUBLANES, (s * nc + c + 1) * SUBLANES)
                m_before.append(m)
                m = jnp.maximum(b_last[rows] + m, g_max[rows])
                m_after.append(m)
            m_sc[s] = m
        m_prev = jnp.concatenate(m_before, axis=0)
        m_new = jnp.concatenate(m_after, axis=0)
        alpha = (-jnp.maximum(run, m_prev[:, :lc])).astype(BF16).astype(F32)
        alpha_sc[...] = alpha
        beta_sc[...] = alpha - b
        r_sc[...] = r
        wg_sc[...] = jnp.exp(g - m_new[:, :lc])
        dec_sc[...] = jnp.exp(b_last + m_prev - m_new)
        mprev_sc[...] = m_prev

    units = [(s, h) for s in range(bt) for h in range(ML_HEADS)]
    cols = [slice(h * dk, (h + 1) * dk) for h in range(ML_HEADS)]
    alpha, beta_hi, beta_lo, r, wg, decay, m_prev = [], [], [], [], [], [], []
    for s in range(bt):
        chunk = pl.ds(pl.multiple_of((s * nc + step) * SUBLANES, SUBLANES), SUBLANES)
        alpha.append(alpha_sc[chunk, :].astype(BF16))
        beta = beta_sc[chunk, :]
        beta_hi.append(beta.astype(BF16))
        beta_lo.append((beta - beta_hi[s].astype(F32)).astype(BF16))
        r.append(r_sc[chunk, :])
        wg.append(wg_sc[chunk, :])
        decay.append(dec_sc[chunk, :])
        m_prev.append(mprev_sc[chunk, :])
    eye2 = jnp.concatenate([eye, eye], axis=1)

    def rows_of(x, h):
        return jnp.broadcast_to(x[h:h + 1, :], (dk, lc))
    alpha_b = _dot_nt(eye, jnp.concatenate([rows_of(alpha[s], h) for s, h in units], axis=0))
    beta_b = _dot_nt(eye2, jnp.concatenate(
        [jnp.concatenate([rows_of(beta_hi[s], h), rows_of(beta_lo[s], h)], axis=1) for s, h in units], axis=0))
    qk = [_dot_nt(mq_ref[s, :, cols[h]], mk_ref[s, :, cols[h]]) for s, h in units]
    c_prev = [c_sc[s, h] for s, h in units]
    n_prev = [n_sc[s, h:h + 1, :] for s, h in units]
    qc = [_dot_nt(mq_ref[s, :, cols[h]],
                  jnp.concatenate([c_prev[u].astype(BF16),
                                   jnp.broadcast_to(n_prev[u], (dk, dk)).astype(BF16)], axis=0))
          for u, (s, h) in enumerate(units)]
    upd = []
    for s, h in units:
        wg_h = wg[s][h:h + 1, :]
        vt = mv_ref[s, :, cols[h]].astype(F32).T
        lhs = jnp.concatenate([(vt * wg_h).astype(BF16),
                               jnp.broadcast_to(wg_h, (SUBLANES, lc)).astype(BF16)], axis=0)
        upd.append(_dot(lhs, mk_ref[s, :, cols[h]]))

    cat = []
    for u, (s, h) in enumerate(units):
        w = jnp.where(causal, jnp.exp(alpha_b[:, u * dk:u * dk + lc] + r[s][h:h + 1, :]), 0.0)
        sw = qk[u] * w
        s_hi = sw.astype(BF16)
        cat.append(jnp.concatenate([s_hi, (sw - s_hi.astype(F32)).astype(BF16)], axis=1))
    nv = []
    for u, (s, h) in enumerate(units):
        rhs = jnp.concatenate([jnp.concatenate([mv_ref[s, :, cols[h]], ones_blk], axis=1),
                               jnp.concatenate([zeros_blk, ones_blk], axis=1)], axis=0)
        nv.append(_dot(cat[u], rhs))
    hval = []
    cat2 = []
    for u, (s, h) in enumerate(units):
        ucols = slice(u * dk, (u + 1) * dk)
        w_inter = jnp.exp(m_prev[s][h:h + 1, :] + alpha_b[:, ucols])
        num = nv[u][:, :dk] + w_inter * qc[u][:, :dk]
        den = nv[u][:, dk:] + w_inter * qc[u][:, dk:]
        hv = num / jnp.maximum(jnp.abs(den), jnp.exp(beta_b[:, ucols]))
        h2 = hv * hv
        h2_hi = h2.astype(BF16)
        hval.append(hv)
        cat2.append(jnp.concatenate([h2_hi, (h2 - h2_hi.astype(F32)).astype(BF16)], axis=1))
    ms = _dot(jnp.concatenate(cat2, axis=0), jnp.ones((2 * dk, dk), BF16)) * (1.0 / dk)
    for u, (s, h) in enumerate(units):
        y = hval[u] * lax.rsqrt(ms[u * lc:(u + 1) * lc] + EPS) * gn_ref[:, cols[h]]
        yb_ref[s, :, cols[h]] = (_sigmoid(mo_ref[s, :, cols[h]]) * y).astype(BF16)

    for u, (s, h) in enumerate(units):
        dec = decay[s][h:h + 1, :]
        c_sc[s, h] = dec * c_prev[u] + upd[u][:dk]
        n_sc[s, h:h + 1, :] = dec * n_prev[u] + upd[u][dk:dk + 1]

    c_out_ref[...] = c_sc[...]
    n_out_ref[...] = n_sc[...]
    m_out_ref[...] = m_sc[...]


def _mlstm(mq, mk, mv, mo, grow, c0, n0, m0, gn, *, bt, lc):
    nb, seq, _ = mq.shape
    assert nb % bt == 0
    tok = pl.BlockSpec((bt, lc, ML_WIDTH), lambda b, i: (b, i, 0))
    c_spec = pl.BlockSpec((bt, ML_HEADS, ML_HEAD_DIM, ML_HEAD_DIM), lambda b, i: (b, 0, 0, 0))
    n_spec = pl.BlockSpec((bt, ML_HEADS, ML_HEAD_DIM), lambda b, i: (b, 0, 0))
    m_spec = pl.BlockSpec((bt, SUBLANES, LANES), lambda b, i: (b, 0, 0))
    nc = seq // lc
    table = pltpu.VMEM((bt * nc * SUBLANES, lc), F32)
    table_m = pltpu.VMEM((bt * nc * SUBLANES, LANES), F32)
    return pl.pallas_call(
        functools.partial(_mlstm_kernel, bt=bt, lc=lc, nc=nc),
        grid=(nb // bt, nc),
        in_specs=[tok, tok, tok, tok,
                  pl.BlockSpec((bt, 2 * SUBLANES, seq), lambda b, i: (b, 0, 0)),
                  c_spec, n_spec, m_spec, _const_spec((1, ML_WIDTH))],
        out_specs=[tok, c_spec, n_spec, m_spec],
        out_shape=[jax.ShapeDtypeStruct((nb, seq, ML_WIDTH), BF16),
                   jax.ShapeDtypeStruct((nb, ML_HEADS, ML_HEAD_DIM, ML_HEAD_DIM), F32),
                   jax.ShapeDtypeStruct((nb, ML_HEADS, ML_HEAD_DIM), F32),
                   jax.ShapeDtypeStruct((nb, SUBLANES, LANES), F32)],
        scratch_shapes=[pltpu.VMEM((bt, ML_HEADS, ML_HEAD_DIM, ML_HEAD_DIM), F32),
                        pltpu.VMEM((bt, ML_HEADS, ML_HEAD_DIM), F32),
                        pltpu.VMEM((bt, SUBLANES, LANES), F32),
                        table, table, table, table, table_m, table_m],
        compiler_params=_params(2),
        name="mlstm",
    )(mq, mk, mv, mo, grow, c0, n0, m0, gn)


def _post_kernel(x_ref, ya_ref, yb_ref, mod_ref, g1_ref, g2_ref, gf_ref,
                 wgab_ref, wa_ref, wb_ref, wout_ref, wfg_ref, wfu_ref, wfd_ref,
                 y_ref, u_sc, *, bt, lt):
    g1 = g1_ref[...]
    for j in range(bt):
        u = _norm_mod(x_ref[j], g1, mod_ref[j, 1:2, :], mod_ref[j, 0:1, :])
        u_sc[j * lt:(j + 1) * lt, :] = u.astype(BF16)
    u = u_sc[...]
    ya = ya_ref[...].reshape(bt * lt, SB_WIDTH)
    yb = yb_ref[...].reshape(bt * lt, ML_WIDTH)
    ga = _dot_nt(u, wgab_ref[:D_MODEL, :])
    gb = _dot_nt(u, wgab_ref[D_MODEL:, :])
    merged = _sigmoid(ga) * _dot(ya, wa_ref[...]) + _sigmoid(gb) * _dot(yb, wb_ref[...])
    attn_out = _dot(merged.astype(BF16), wout_ref[...])

    g2 = g2_ref[...]
    for j in range(bt):
        rows = slice(j * lt, (j + 1) * lt)
        x1 = x_ref[j] + mod_ref[j, 2:3, :] * attn_out[rows]
        y_ref[j] = x1
        u2 = _norm_mod(x1, g2, mod_ref[j, 4:5, :], mod_ref[j, 3:4, :])
        u_sc[rows, :] = u2.astype(BF16)
    u2 = u_sc[...]
    hg = _dot(u2, wfg_ref[...])
    hu = _dot(u2, wfu_ref[...])
    act = (hg * _sigmoid(hg) * hu).astype(BF16)
    ff = _dot(act, wfd_ref[...])
    gf = gf_ref[...]
    for j in range(bt):
        rows = slice(j * lt, (j + 1) * lt)
        x2 = y_ref[j] + mod_ref[j, 5:6, :] * ff[rows]
        ms = jnp.mean(x2 * x2, axis=-1, keepdims=True)
        y_ref[j] = x2 * lax.rsqrt(ms + EPS) * gf


def _post(x, ya, yb, mod3, g1, g2, gf, wgab, wa, wb, wout, wfg, wfu, wfd, *, bt, lt):
    nb, seq, _ = x.shape
    assert nb % bt == 0 and seq % lt == 0

    def tok(width):
        return pl.BlockSpec((bt, lt, width), lambda b, l: (b, l, 0))

    in_specs = [tok(D_MODEL), tok(SB_WIDTH), tok(ML_WIDTH),
                pl.BlockSpec((bt, 6, D_MODEL), lambda b, l: (b, 0, 0)),
                _const_spec((1, D_MODEL)), _const_spec((1, D_MODEL)), _const_spec((1, D_MODEL)),
                _const_spec(wgab.shape), _const_spec(wa.shape), _const_spec(wb.shape), _const_spec(wout.shape),
                _const_spec(wfg.shape), _const_spec(wfu.shape), _const_spec(wfd.shape)]
    return pl.pallas_call(
        functools.partial(_post_kernel, bt=bt, lt=lt),
        grid=(nb // bt, seq // lt),
        in_specs=in_specs,
        out_specs=tok(D_MODEL),
        out_shape=jax.ShapeDtypeStruct((nb, seq, D_MODEL), F32),
        scratch_shapes=[pltpu.VMEM((bt * lt, D_MODEL), BF16)],
        compiler_params=_params(2),
        name="post",
    )(x, ya, yb, mod3, g1, g2, gf, wgab, wa, wb, wout, wfg, wfu, wfd)


def _group(x, mod3, k_past, v_past, c0, n0, m0, conv0, wts, *, bt, lt_in, lt, tq, n_past, lc):
    (g1, g2, gf, w1, wif_row, bif_row, wconv, bconv, gn,
     wgab, wa, wb, wout, wfg, wfu, wfd) = wts
    nb, seq, _ = x.shape
    (q, k32, v32, kb, vb, mq, mk, mv, mo, grow, conv_new) = _inproj(
        x, mod3, conv0, g1, w1, wif_row, bif_row, wconv, bconv, bt=bt, lt=lt_in)
    if k_past is None:
        k_past, v_past = kb, vb
    ya = _attn(q, kb, vb, k_past, v_past, bt=ATTN_SEQS, tq=tq, n_past=n_past)
    m0b = jnp.broadcast_to(jnp.pad(m0, ((0, 0), (0, SUBLANES - ML_HEADS)))[:, :, None], (nb, SUBLANES, LANES))
    yb, c1, n1, m1 = _mlstm(mq, mk, mv, mo, grow, c0, n0, m0b, gn, bt=ML_SEQS, lc=lc)
    y = _post(x, ya, yb, mod3, g1, g2, gf, wgab, wa, wb, wout, wfg, wfu, wfd, bt=bt, lt=lt)
    def rows_out(t):
        if _tokens_on_lanes(lt_in):
            return jnp.transpose(t.reshape(nb, SB_HEADS, SB_HEAD_DIM, seq), (0, 3, 1, 2))[None]
        return t.reshape(1, nb, seq, SB_HEADS, SB_HEAD_DIM)

    states = (rows_out(k32), rows_out(v32),
              c1[None], n1[None], m1[None, :, :ML_HEADS, 0], conv_new[None])
    return y, states


def kernel(x_prompt, x_sample, c_prompt, c_sample, cache_sb_k, cache_sb_v, state_mlstm_C, state_mlstm_n, state_mlstm_m, state_conv, norm1_g, norm2_g, w_ada, b_ada, w_in, b_if, w_conv, b_conv, ml_norm_g, w_a, w_b, w_out, w_ff_gate, w_ff_up, w_ff_down, final_g):
    assert w_in.shape[0] == 1, "single layer"
    bp, seq_p, _ = x_prompt.shape
    bs, seq_s, _ = x_sample.shape
    past = cache_sb_k.shape[2]

    c_all = jnp.concatenate([c_prompt, c_sample], axis=0)
    mod3 = _ada(c_all, w_ada[0], b_ada[0]).reshape(bp + bs, 6, D_MODEL)

    w1, wgab, wif_row, wa, wb, wout, wfg, wfu, wfd = _prep(
        jnp.transpose(w_in[0]), w_a[0], w_b[0], w_out[0], w_ff_gate[0], w_ff_up[0], w_ff_down[0])
    pad = SUBLANES - ML_HEADS
    bif_row = jnp.pad(b_if[0].reshape(2, ML_HEADS), ((0, 0), (0, pad))).reshape(2 * SUBLANES, 1)
    wts = (norm1_g[0].reshape(1, D_MODEL), norm2_g[0].reshape(1, D_MODEL), final_g.reshape(1, D_MODEL),
           w1, wif_row, bif_row, w_conv[0], b_conv[0].reshape(1, 2 * ML_WIDTH),
           ml_norm_g[0].reshape(1, ML_WIDTH),
           wgab, wa, wb, wout, wfg, wfu, wfd)

    zeros = functools.partial(jnp.zeros, dtype=F32)
    y_p, st_p = _group(
        x_prompt, mod3[:bp], None, None,
        zeros((bp, ML_HEADS, ML_HEAD_DIM, ML_HEAD_DIM)), zeros((bp, ML_HEADS, ML_HEAD_DIM)),
        zeros((bp, ML_HEADS)), zeros((bp, CONV_W - 1, 2 * ML_WIDTH)), wts,
        bt=1, lt_in=512, lt=256, tq=ATTN_BLOCK, n_past=None, lc=ML_CHUNK)
    def cache_t(c):
        return jnp.transpose(c[0], (0, 2, 3, 1)).reshape(bs, SB_WIDTH, past)

    y_s, st_s = _group(
        x_sample, mod3[bp:], cache_t(cache_sb_k), cache_t(cache_sb_v),
        state_mlstm_C[0], state_mlstm_n[0], state_mlstm_m[0], state_conv[0], wts,
        bt=4, lt_in=seq_s, lt=seq_s, tq=seq_s, n_past=past // ATTN_BLOCK, lc=seq_s)
    return (y_p, y_s) + st_p + st_s
```

```python
import functools
import math

import jax
import jax.numpy as jnp
from jax import lax
from jax.experimental import pallas as pl
from jax.experimental.pallas import tpu as pltpu

D_MODEL = 1024
SB_HEADS = 8
SB_HEAD_DIM = 64
SB_WIDTH = SB_HEADS * SB_HEAD_DIM
ML_HEADS = 4
ML_HEAD_DIM = 128
ML_WIDTH = ML_HEADS * ML_HEAD_DIM
CONV_W = 4
D_FF = 2816
EPS = 1e-6
LANES = 128
SUBLANES = 8
ATTN_BLOCK = 128
ML_CHUNK = 128
ML_SEQS = 4
EXP_UNDERFLOW = 88.0
SOFTPLUS_CLAMP = 60.0
ATTN_SEQS = 2
ATTN_GROUPS = 4
V7X_VMEM_BYTES = 64 * 1024 * 1024
VMEM_LIMIT = V7X_VMEM_BYTES - 8 * 1024 * 1024

BF16 = jnp.bfloat16
F32 = jnp.float32

_C_Q, _C_K, _C_V = 0, SB_WIDTH, 2 * SB_WIDTH
_C_MQK = 3 * SB_WIDTH
_C_MV = _C_MQK + 2 * ML_WIDTH
_C_MO = _C_MV + ML_WIDTH
_C_IF = _C_MO + ML_WIDTH
_C_GAB = _C_IF + 2 * ML_HEADS
_C_END = _C_GAB + 2 * D_MODEL


def _sigmoid(x):
    return 1.0 / (1.0 + jnp.exp(-x))


def _log_sigmoid(x):
    return jnp.minimum(x, 0.0) - jnp.log1p(jnp.exp(-jnp.abs(x)))


def _softplus_pos(x):
    return jnp.maximum(jnp.log(1.0 + jnp.exp(jnp.minimum(x, SOFTPLUS_CLAMP))), x)


def _dot(a, b):
    return jnp.dot(a, b, preferred_element_type=F32)


def _dot_nt(a, b):
    return lax.dot_general(a, b, (((1,), (1,)), ((), ())), preferred_element_type=F32)


def _split_dot(x, tri, parts):
    acc = None
    rem = x
    for i in range(parts):
        piece = rem.astype(BF16)
        term = _dot(piece, tri)
        acc = term if acc is None else acc + term
        if i + 1 < parts:
            rem = rem - piece.astype(F32)
    return acc


def _norm_mod(x, g, sc, sh):
    ms = jnp.mean(x * x, axis=-1, keepdims=True)
    return (x * lax.rsqrt(ms + EPS)) * (g * (1.0 + sc)) + sh


def _const_spec(shape):
    nd = len(shape)
    return pl.BlockSpec(shape, lambda *_: (0,) * nd, pipeline_mode=pl.Buffered(1))


def _params(n_axes):
    return pltpu.CompilerParams(dimension_semantics=("arbitrary",) * n_axes,
                                vmem_limit_bytes=VMEM_LIMIT)


def _ada_kernel(c_ref, w_ref, b_ref, o_ref):
    c = c_ref[...]
    s = c * _sigmoid(c)
    o_ref[...] = _dot(s.astype(BF16), w_ref[...].astype(BF16)) + b_ref[...]


def _ada(c_all, w_ada, b_ada):
    nb = c_all.shape[0]
    n_out = w_ada.shape[1]
    tn = 2 * D_MODEL
    return pl.pallas_call(
        _ada_kernel,
        grid=(n_out // tn,),
        in_specs=[pl.BlockSpec((nb, D_MODEL), lambda j: (0, 0)),
                  pl.BlockSpec((D_MODEL, tn), lambda j: (0, j)),
                  pl.BlockSpec((1, tn), lambda j: (0, j))],
        out_specs=pl.BlockSpec((nb, tn), lambda j: (0, j)),
        out_shape=jax.ShapeDtypeStruct((nb, n_out), F32),
        compiler_params=_params(1),
        name="ada",
    )(c_all, w_ada, b_ada.reshape(1, n_out))


PREP_STEPS = 8
W1_BLOCK = 512


def _prep_kernel(w1_ref, wgab_ref, wif_ref, wa_ref, wb_ref, wout_ref, wfg_ref, wfu_ref, wfd_ref,
                 w1_o, wgab_o, wif_o, wa_o, wb_o, wout_o, wfg_o, wfu_o, wfd_o):
    w1_o[...] = w1_ref[...].T.astype(BF16)
    for src, dst in ((wgab_ref, wgab_o), (wa_ref, wa_o), (wb_ref, wb_o), (wout_ref, wout_o),
                     (wfg_ref, wfg_o), (wfu_ref, wfu_o), (wfd_ref, wfd_o)):
        dst[...] = src[...].astype(BF16)
    gates = wif_ref[...]
    pad = jnp.zeros((SUBLANES - ML_HEADS, D_MODEL), F32)
    wif_o[...] = jnp.concatenate([gates[:ML_HEADS], pad, gates[ML_HEADS:], pad], axis=0).astype(BF16)


def _prep(w_in_t, w_a, w_b, w_out, w_fg, w_fu, w_fd):
    def rows(a, n_rows=None, first=0):
        n_rows = a.shape[0] if n_rows is None else n_rows
        step = n_rows // PREP_STEPS
        assert n_rows % (PREP_STEPS * 2 * SUBLANES) == 0 and first % SUBLANES == 0
        if first == 0:
            return pl.BlockSpec((step, a.shape[1]), lambda i: (i, 0))
        return pl.BlockSpec((pl.Element(step), pl.Element(a.shape[1])),
                            lambda i: (pl.multiple_of(first + i * step, SUBLANES), 0))

    def out(n_rows, n_cols):
        return (jax.ShapeDtypeStruct((n_rows, n_cols), BF16),
                pl.BlockSpec((n_rows // PREP_STEPS, n_cols), lambda i: (i, 0)))

    last = _C_IF // W1_BLOCK - 1
    assert _C_IF % W1_BLOCK == 0 and last < PREP_STEPS
    w1_in = pl.BlockSpec((W1_BLOCK, D_MODEL), lambda i: (jnp.minimum(i, last), 0))
    w1_out = (jax.ShapeDtypeStruct((D_MODEL, _C_IF), BF16),
              pl.BlockSpec((D_MODEL, W1_BLOCK), lambda i: (0, jnp.minimum(i, last))))

    others = (w_a, w_b, w_out, w_fg, w_fu, w_fd)
    outs = [w1_out, out(_C_END - _C_GAB, D_MODEL),
            (jax.ShapeDtypeStruct((2 * SUBLANES, D_MODEL), BF16),
             pl.BlockSpec((2 * SUBLANES, D_MODEL), lambda i: (0, 0)))] + [out(*a.shape) for a in others]
    in_specs = [w1_in, rows(w_in_t, _C_END - _C_GAB, _C_GAB),
                pl.BlockSpec((pl.Element(2 * ML_HEADS), pl.Element(D_MODEL)), lambda i: (_C_IF, 0))
                ] + [rows(a) for a in others]
    return pl.pallas_call(
        _prep_kernel,
        grid=(PREP_STEPS,),
        in_specs=in_specs,
        out_specs=[o[1] for o in outs],
        out_shape=[o[0] for o in outs],
        compiler_params=_params(1),
        name="prep",
    )(w_in_t, w_in_t, w_in_t, *others)


def _tokens_on_lanes(tile_len):
    return tile_len % LANES == 0


def _inproj_kernel(x_ref, mod_ref, conv0_ref, g1_ref, w1_ref, wifr_ref, bifr_ref,
                   wconv_ref, bconv_ref,
                   q_ref, k32_ref, v32_ref, kb_ref, vb_ref, mq_ref, mk_ref, mv_ref, mo_ref,
                   grow_ref, convn_ref,
                   u_sc, xp_sc, *, bt, lt):
    @pl.when(pl.program_id(1) == 0)
    def _():
        for j in range(bt):
            xp_sc[j] = jnp.concatenate(
                [jnp.zeros((SUBLANES - (CONV_W - 1), 2 * ML_WIDTH), F32), conv0_ref[j]], axis=0)

    g1 = g1_ref[...]
    for j in range(bt):
        u = _norm_mod(x_ref[j], g1, mod_ref[j, 1:2, :], mod_ref[j, 0:1, :])
        u_sc[j * lt:(j + 1) * lt, :] = u.astype(BF16)
    u = u_sc[...]

    def mm(lo, hi):
        return _dot(u, w1_ref[:, lo:hi])

    def new_rows(t):
        return t.T if _tokens_on_lanes(lt) else t

    wconv = wconv_ref[...]
    bconv = bconv_ref[...]
    k_scale = 1.0 / math.sqrt(ML_HEAD_DIM)
    mqk = mm(_C_MQK, _C_MV)
    row8 = lax.broadcasted_iota(jnp.int32, (SUBLANES, 2 * ML_WIDTH), 0)
    for j in range(bt):
        x = mqk[j * lt:(j + 1) * lt]
        before = xp_sc[j]
        conv = bconv + x * wconv[CONV_W - 1:CONV_W, :]
        for k in range(1, CONV_W):
            xr = pltpu.roll(x, k, axis=0)
            head = jnp.where(row8 < k, pltpu.roll(before, k, axis=0), xr[:SUBLANES])
            xk = jnp.concatenate([head, xr[SUBLANES:]], axis=0)
            conv = conv + xk * wconv[CONV_W - 1 - k:CONV_W - k, :]
        conv = conv * _sigmoid(conv)
        mq_ref[j] = conv[:, :ML_WIDTH].astype(BF16)
        mk_ref[j] = (conv[:, ML_WIDTH:] * k_scale).astype(BF16)
        convn_ref[j] = x[lt - (CONV_W - 1):]
        xp_sc[j] = x[lt - SUBLANES:]

    sq = mm(_C_Q, _C_K)
    sk = mm(_C_K, _C_V)
    sv = mm(_C_V, _C_MQK)
    mv = mm(_C_MV, _C_MO)
    mo = mm(_C_MO, _C_IF)
    for j in range(bt):
        rows = slice(j * lt, (j + 1) * lt)
        q_ref[j] = sq[rows].astype(BF16)
        k32_ref[j] = new_rows(sk[rows])
        kb_ref[j] = sk[rows].astype(BF16)
        v32_ref[j] = new_rows(sv[rows])
        vb_ref[j] = sv[rows].astype(BF16)
        mv_ref[j] = mv[rows].astype(BF16)
    for j in range(bt):
        rows = slice(j * lt, (j + 1) * lt)
        mo_ref[j] = mo[rows]
        gr = _dot_nt(wifr_ref[...], u_sc[rows, :]) + bifr_ref[...]
        row = lax.broadcasted_iota(jnp.int32, gr.shape, 0)
        grow_ref[j] = jnp.where(row >= SUBLANES, _log_sigmoid(gr), gr)


def _inproj(x, mod3, conv0, g1, w1, wif_row, bif_row, wconv, bconv, *, bt, lt):
    nb, seq, _ = x.shape
    assert nb % bt == 0 and seq % lt == 0 and (bt == 1 or lt == seq)
    grid = (nb // bt, seq // lt)

    def tok(width):
        return pl.BlockSpec((bt, lt, width), lambda b, l: (b, l, 0))

    def per_batch(rows, width):
        return pl.BlockSpec((bt, rows, width), lambda b, l: (b, 0, 0))

    in_specs = [tok(D_MODEL), per_batch(6, D_MODEL), per_batch(CONV_W - 1, 2 * ML_WIDTH),
                _const_spec((1, D_MODEL)), _const_spec(w1.shape),
                _const_spec((2 * SUBLANES, D_MODEL)),
                _const_spec((2 * SUBLANES, 1)),
                _const_spec((CONV_W, 2 * ML_WIDTH)), _const_spec((1, 2 * ML_WIDTH))]

    def tok_on_lanes(rows):
        return pl.BlockSpec((bt, rows, lt), lambda b, l: (b, 0, l))

    kv_spec = tok_on_lanes(SB_WIDTH) if _tokens_on_lanes(lt) else tok(SB_WIDTH)
    kv_shape = (nb, SB_WIDTH, seq) if _tokens_on_lanes(lt) else (nb, seq, SB_WIDTH)
    out_specs = [tok(SB_WIDTH), kv_spec, kv_spec, tok(SB_WIDTH), tok(SB_WIDTH)] + [tok(ML_WIDTH)] * 4 + [
        tok_on_lanes(2 * SUBLANES), per_batch(CONV_W - 1, 2 * ML_WIDTH)]

    def sds(shape, dt):
        return jax.ShapeDtypeStruct(shape, dt)

    out_shape = [sds((nb, seq, SB_WIDTH), BF16), sds(kv_shape, F32), sds(kv_shape, F32),
                 sds((nb, seq, SB_WIDTH), BF16), sds((nb, seq, SB_WIDTH), BF16),
                 sds((nb, seq, ML_WIDTH), BF16), sds((nb, seq, ML_WIDTH), BF16), sds((nb, seq, ML_WIDTH), BF16),
                 sds((nb, seq, ML_WIDTH), F32),
                 sds((nb, 2 * SUBLANES, seq), F32),
                 sds((nb, CONV_W - 1, 2 * ML_WIDTH), F32)]
    return pl.pallas_call(
        functools.partial(_inproj_kernel, bt=bt, lt=lt),
        grid=grid,
        in_specs=in_specs,
        out_specs=out_specs,
        out_shape=out_shape,
        scratch_shapes=[pltpu.VMEM((bt * lt, D_MODEL), BF16),
                        pltpu.VMEM((bt, SUBLANES, 2 * ML_WIDTH), F32)],
        compiler_params=_params(2),
        name="inproj",
    )(x, mod3, conv0, g1, w1, wif_row, bif_row, wconv, bconv)


def _cum_matrix(t):
    r = lax.broadcasted_iota(jnp.int32, (2 * t, LANES + t), 0)
    c = lax.broadcasted_iota(jnp.int32, (2 * t, LANES + t), 1)
    return jnp.where((c < LANES) | ((r & (t - 1)) > (c - LANES)), 1.0, 0.0).astype(BF16)


def _attn_kernel(q_ref, kd_ref, vd_ref, kp_ref, vp_ref, o_ref, qm_sc, acc_sc, carry_sc, *, bt, tq, tk, n_past):
    cache_past = n_past is not None
    n_blocks = n_past if cache_past else pl.program_id(1)
    scale = 1.0 / math.sqrt(SB_HEAD_DIM)
    n_pairs = SB_WIDTH // LANES
    lane = lax.broadcasted_iota(jnp.int32, (1, LANES), 1)
    first_head = lane < SB_HEAD_DIM

    n_units = bt * n_pairs
    for u in range(n_units):
        s, p = divmod(u, n_pairs)
        qp = q_ref[s, :, p * LANES:(p + 1) * LANES] * scale
        zero = jnp.zeros_like(qp)
        qm_sc[(2 * u) * tq:(2 * u + 1) * tq, :] = jnp.where(first_head, qp, zero)
        qm_sc[(2 * u + 1) * tq:(2 * u + 2) * tq, :] = jnp.where(first_head, zero, qp)

    per_group = n_units // ATTN_GROUPS
    groups = [range(g, g + per_group) for g in range(0, n_units, per_group)]
    group_rows = [slice(2 * grp[0] * tq, 2 * (grp[-1] + 1) * tq) for grp in groups]

    def scores(k_of, transposed):
        dot_k = _dot if transposed else _dot_nt
        return [jnp.concatenate([dot_k(qm_sc[2 * u * tq:(2 * u + 2) * tq, :], k_of(u)) for u in grp], axis=0)
                for grp in groups]

    def weights(zs, t, visible):
        cum = _cum_matrix(t)
        sps, sums = [], []
        for gi, rows in enumerate(group_rows):
            sp = _softplus_pos(zs[gi])
            if visible is not None:
                sp = jnp.where(visible[rows], sp, 0.0)
            hi = sp.astype(BF16)
            lo = (sp - hi.astype(F32)).astype(BF16)
            sps.append(sp)
            sums.append(_dot(jnp.concatenate([hi, lo], axis=1), cum))
        out = []
        for gi, rows in enumerate(group_rows):
            logit = zs[gi] - sps[gi] - sums[gi][:, LANES:]
            if visible is None:
                carry = carry_sc[rows, :]
                a = jnp.exp(logit - carry)
                carry_sc[rows, :] = carry + sums[gi][:, :LANES]
            else:
                a = jnp.where(visible[rows], jnp.exp(logit), 0.0)
                carry_sc[rows, :] = sums[gi][:, :LANES]
            out.append(a.astype(BF16))
        return out

    def accumulate(a_groups, v_of, transposed, assign=False):
        dot_v = _dot_nt if transposed else _dot
        for gi, grp in enumerate(groups):
            for j, u in enumerate(grp):
                s, p = divmod(u, n_pairs)
                cols = slice(p * LANES, (p + 1) * LANES)
                pv = dot_v(a_groups[gi][2 * j * tq:(2 * j + 2) * tq, :], v_of(u))
                out = jnp.where(first_head, pv[:tq], pv[tq:])
                if assign:
                    acc_sc[s, :, cols] = out
                else:
                    acc_sc[s, :, cols] += out

    def unit_cols(ref):
        def of(u):
            s, p = divmod(u, n_pairs)
            return ref[s, :, p * LANES:(p + 1) * LANES]
        return of

    def past_of(ref, block_index):
        start = pl.multiple_of(block_index * tk, tk)

        def of(u):
            s, p = divmod(u, n_pairs)
            if cache_past:
                return ref[s, p * LANES:(p + 1) * LANES, pl.ds(start, tk)].astype(BF16)
            return ref[s, pl.ds(start, tk), p * LANES:(p + 1) * LANES].astype(BF16)
        return of

    rd = lax.broadcasted_iota(jnp.int32, (2 * n_units * tq, tq), 0) & (tq - 1)
    cd = lax.broadcasted_iota(jnp.int32, (2 * n_units * tq, tq), 1)
    a_diag = weights(scores(unit_cols(kd_ref), False), tq, cd < rd)
    accumulate(a_diag, unit_cols(vd_ref), False, assign=True)

    def not_done():
        return jnp.min(carry_sc[...]) <= EXP_UNDERFLOW

    def past_block(i):
        block_index = n_blocks - 1 - i
        a = weights(scores(past_of(kp_ref, block_index), cache_past), tk, None)
        accumulate(a, past_of(vp_ref, block_index), cache_past)

    def pair_cond(state):
        i, go = state
        return jnp.logical_and(i + 1 < n_blocks, go)

    def pair_body(state):
        i, _ = state
        past_block(i)
        past_block(i + 1)
        return i + 2, not_done()

    def cond(state):
        i, go = state
        return jnp.logical_and(i < n_blocks, go)

    def body(state):
        i, _ = state
        past_block(i)
        return i + 1, not_done()

    state = lax.while_loop(pair_cond, pair_body, (jnp.int32(0), not_done()))
    lax.while_loop(cond, body, state)
    o_ref[...] = acc_sc[...].astype(BF16)


def _attn(q, k_new, v_new, k_past, v_past, *, bt, tq, n_past):
    nb, seq, _ = q.shape
    assert nb % bt == 0
    tok = pl.BlockSpec((bt, tq, SB_WIDTH), lambda b, i: (b, i, 0))
    past = pl.BlockSpec((bt,) + k_past.shape[1:], lambda b, i: (b, 0, 0))
    rows = bt * SB_HEADS * tq
    return pl.pallas_call(
        functools.partial(_attn_kernel, bt=bt, tq=tq, tk=ATTN_BLOCK, n_past=n_past),
        grid=(nb // bt, seq // tq),
        in_specs=[tok, tok, tok, past, past],
        out_specs=tok,
        out_shape=jax.ShapeDtypeStruct((nb, seq, SB_WIDTH), BF16),
        scratch_shapes=[pltpu.VMEM((rows, LANES), BF16),
                        pltpu.VMEM((bt, tq, SB_WIDTH), F32),
                        pltpu.VMEM((rows, LANES), F32)],
        compiler_params=_params(2),
        name="attn",
    )(q, k_new, v_new, k_past, v_past)


def _mlstm_kernel(mq_ref, mk_ref, mv_ref, mo_ref, grow_ref, c0_ref, n0_ref, m0_ref, gn_ref,
                  yb_ref, c_out_ref, n_out_ref, m_out_ref,
                  c_sc, n_sc, m_sc, alpha_sc, beta_sc, r_sc, wg_sc, dec_sc, mprev_sc, *, bt, lc, nc):
    step = pl.program_id(1)
    dk = ML_HEAD_DIM

    ri = lax.broadcasted_iota(jnp.int32, (lc, lc), 0)
    ci = lax.broadcasted_iota(jnp.int32, (lc, lc), 1)
    causal = ci <= ri
    eye = jnp.where(ri == ci, 1.0, 0.0).astype(BF16)
    ones_blk = jnp.ones((lc, dk), BF16)
    zeros_blk = jnp.zeros((lc, dk), BF16)

    @pl.when(step == 0)
    def _():
        c_sc[...] = c0_ref[...]
        n_sc[...] = n0_ref[...]
        incl_t = jnp.where(ri <= ci, 1.0, 0.0).astype(BF16)
        blocks = [(s, c) for s in range(bt) for c in range(nc)]
        li = jnp.concatenate([grow_ref[s, 0:SUBLANES, c * lc:(c + 1) * lc] for s, c in blocks], axis=0)
        lf = jnp.concatenate([grow_ref[s, SUBLANES:2 * SUBLANES, c * lc:(c + 1) * lc] for s, c in blocks],
                             axis=0)
        b = _split_dot(lf, incl_t, 3)
        r = li - b
        lane = lax.broadcasted_iota(jnp.int32, r.shape, 1)
        run = r
        d = 1
        while d < lc:
            run = jnp.maximum(run, jnp.where(lane >= d, pltpu.roll(run, d, axis=1), -jnp.inf))
            d *= 2
        b_last = b[:, lc - 1:lc]
        g = b_last - b + li
        g_max = jnp.max(g, axis=1, keepdims=True)
        m_before = []
        m_after = []
        for s in range(bt):
            m = m0_ref[s]
            for c in range(nc):
                rows = slice((s * nc + c) * SUBLANES, (s * nc + c + 1) * SUBLANES)
                m_before.append(m)
                m = jnp.maximum(b_last[rows] + m, g_max[rows])
                m_after.append(m)
            m_sc[s] = m
        m_prev = jnp.concatenate(m_before, axis=0)
        m_new = jnp.concatenate(m_after, axis=0)
        alpha = (-jnp.maximum(run, m_prev[:, :lc])).astype(BF16).astype(F32)
        alpha_sc[...] = alpha
        beta_sc[...] = alpha - b
        r_sc[...] = r
        wg_sc[...] = jnp.exp(g - m_new[:, :lc])
        dec_sc[...] = jnp.exp(b_last + m_prev - m_new)
        mprev_sc[...] = m_prev

    units = [(s, h) for s in range(bt) for h in range(ML_HEADS)]
    cols = [slice(h * dk, (h + 1) * dk) for h in range(ML_HEADS)]
    alpha, beta_hi, beta_lo, r, wg, decay, m_prev = [], [], [], [], [], [], []
    for s in range(bt):
        chunk = pl.ds(pl.multiple_of((s * nc + step) * SUBLANES, SUBLANES), SUBLANES)
        alpha.append(alpha_sc[chunk, :].astype(BF16))
        beta = beta_sc[chunk, :]
        beta_hi.append(beta.astype(BF16))
        beta_lo.append((beta - beta_hi[s].astype(F32)).astype(BF16))
        r.append(r_sc[chunk, :])
        wg.append(wg_sc[chunk, :])
        decay.append(dec_sc[chunk, :])
        m_prev.append(mprev_sc[chunk, :])
    eye2 = jnp.concatenate([eye, eye], axis=1)

    def rows_of(x, h):
        return jnp.broadcast_to(x[h:h + 1, :], (dk, lc))
    alpha_b = _dot_nt(eye, jnp.concatenate([rows_of(alpha[s], h) for s, h in units], axis=0))
    beta_b = _dot_nt(eye2, jnp.concatenate(
        [jnp.concatenate([rows_of(beta_hi[s], h), rows_of(beta_lo[s], h)], axis=1) for s, h in units], axis=0))
    qk = [_dot_nt(mq_ref[s, :, cols[h]], mk_ref[s, :, cols[h]]) for s, h in units]
    c_prev = [c_sc[s, h] for s, h in units]
    n_prev = [n_sc[s, h:h + 1, :] for s, h in units]
    qc = [_dot_nt(mq_ref[s, :, cols[h]],
                  jnp.concatenate([c_prev[u].astype(BF16),
                                   jnp.broadcast_to(n_prev[u], (dk, dk)).astype(BF16)], axis=0))
          for u, (s, h) in enumerate(units)]
    upd = []
    for s, h in units:
        wg_h = wg[s][h:h + 1, :]
        vt = mv_ref[s, :, cols[h]].astype(F32).T
        lhs = jnp.concatenate([(vt * wg_h).astype(BF16),
                               jnp.broadcast_to(wg_h, (SUBLANES, lc)).astype(BF16)], axis=0)
        upd.append(_dot(lhs, mk_ref[s, :, cols[h]]))

    cat = []
    for u, (s, h) in enumerate(units):
        w = jnp.where(causal, jnp.exp(alpha_b[:, u * dk:u * dk + lc] + r[s][h:h + 1, :]), 0.0)
        sw = qk[u] * w
        s_hi = sw.astype(BF16)
        cat.append(jnp.concatenate([s_hi, (sw - s_hi.astype(F32)).astype(BF16)], axis=1))
    nv = []
    for u, (s, h) in enumerate(units):
        rhs = jnp.concatenate([jnp.concatenate([mv_ref[s, :, cols[h]], ones_blk], axis=1),
                               jnp.concatenate([zeros_blk, ones_blk], axis=1)], axis=0)
        nv.append(_dot(cat[u], rhs))
    hval = []
    cat2 = []
    for u, (s, h) in enumerate(units):
        ucols = slice(u * dk, (u + 1) * dk)
        w_inter = jnp.exp(m_prev[s][h:h + 1, :] + alpha_b[:, ucols])
        num = nv[u][:, :dk] + w_inter * qc[u][:, :dk]
        den = nv[u][:, dk:] + w_inter * qc[u][:, dk:]
        hv = num / jnp.maximum(jnp.abs(den), jnp.exp(beta_b[:, ucols]))
        h2 = hv * hv
        h2_hi = h2.astype(BF16)
        hval.append(hv)
        cat2.append(jnp.concatenate([h2_hi, (h2 - h2_hi.astype(F32)).astype(BF16)], axis=1))
    ms = _dot(jnp.concatenate(cat2, axis=0), jnp.ones((2 * dk, dk), BF16)) * (1.0 / dk)
    for u, (s, h) in enumerate(units):
        y = hval[u] * lax.rsqrt(ms[u * lc:(u + 1) * lc] + EPS) * gn_ref[:, cols[h]]
        yb_ref[s, :, cols[h]] = (_sigmoid(mo_ref[s, :, cols[h]]) * y).astype(BF16)

    for u, (s, h) in enumerate(units):
        dec = decay[s][h:h + 1, :]
        c_sc[s, h] = dec * c_prev[u] + upd[u][:dk]
        n_sc[s, h:h + 1, :] = dec * n_prev[u] + upd[u][dk:dk + 1]

    c_out_ref[...] = c_sc[...]
    n_out_ref[...] = n_sc[...]
    m_out_ref[...] = m_sc[...]


def _mlstm(mq, mk, mv, mo, grow, c0, n0, m0, gn, *, bt, lc):
    nb, seq, _ = mq.shape
    assert nb % bt == 0
    tok = pl.BlockSpec((bt, lc, ML_WIDTH), lambda b, i: (b, i, 0))
    c_spec = pl.BlockSpec((bt, ML_HEADS, ML_HEAD_DIM, ML_HEAD_DIM), lambda b, i: (b, 0, 0, 0))
    n_spec = pl.BlockSpec((bt, ML_HEADS, ML_HEAD_DIM), lambda b, i: (b, 0, 0))
    m_spec = pl.BlockSpec((bt, SUBLANES, LANES), lambda b, i: (b, 0, 0))
    nc = seq // lc
    table = pltpu.VMEM((bt * nc * SUBLANES, lc), F32)
    table_m = pltpu.VMEM((bt * nc * SUBLANES, LANES), F32)
    return pl.pallas_call(
        functools.partial(_mlstm_kernel, bt=bt, lc=lc, nc=nc),
        grid=(nb // bt, nc),
        in_specs=[tok, tok, tok, tok,
                  pl.BlockSpec((bt, 2 * SUBLANES, seq), lambda b, i: (b, 0, 0)),
                  c_spec, n_spec, m_spec, _const_spec((1, ML_WIDTH))],
        out_specs=[tok, c_spec, n_spec, m_spec],
        out_shape=[jax.ShapeDtypeStruct((nb, seq, ML_WIDTH), BF16),
                   jax.ShapeDtypeStruct((nb, ML_HEADS, ML_HEAD_DIM, ML_HEAD_DIM), F32),
                   jax.ShapeDtypeStruct((nb, ML_HEADS, ML_HEAD_DIM), F32),
                   jax.ShapeDtypeStruct((nb, SUBLANES, LANES), F32)],
        scratch_shapes=[pltpu.VMEM((bt, ML_HEADS, ML_HEAD_DIM, ML_HEAD_DIM), F32),
                        pltpu.VMEM((bt, ML_HEADS, ML_HEAD_DIM), F32),
                        pltpu.VMEM((bt, SUBLANES, LANES), F32),
                        table, table, table, table, table_m, table_m],
        compiler_params=_params(2),
        name="mlstm",
    )(mq, mk, mv, mo, grow, c0, n0, m0, gn)


def _post_kernel(x_ref, ya_ref, yb_ref, mod_ref, g1_ref, g2_ref, gf_ref,
                 wgab_ref, wa_ref, wb_ref, wout_ref, wfg_ref, wfu_ref, wfd_ref,
                 y_ref, u_sc, *, bt, lt):
    g1 = g1_ref[...]
    for j in range(bt):
        u = _norm_mod(x_ref[j], g1, mod_ref[j, 1:2, :], mod_ref[j, 0:1, :])
        u_sc[j * lt:(j + 1) * lt, :] = u.astype(BF16)
    u = u_sc[...]
    ya = ya_ref[...].reshape(bt * lt, SB_WIDTH)
    yb = yb_ref[...].reshape(bt * lt, ML_WIDTH)
    ga = _dot_nt(u, wgab_ref[:D_MODEL, :])
    gb = _dot_nt(u, wgab_ref[D_MODEL:, :])
    merged = _sigmoid(ga) * _dot(ya, wa_ref[...]) + _sigmoid(gb) * _dot(yb, wb_ref[...])
    attn_out = _dot(merged.astype(BF16), wout_ref[...])

    g2 = g2_ref[...]
    for j in range(bt):
        rows = slice(j * lt, (j + 1) * lt)
        x1 = x_ref[j] + mod_ref[j, 2:3, :] * attn_out[rows]
        y_ref[j] = x1
        u2 = _norm_mod(x1, g2, mod_ref[j, 4:5, :], mod_ref[j, 3:4, :])
        u_sc[rows, :] = u2.astype(BF16)
    u2 = u_sc[...]
    hg = _dot(u2, wfg_ref[...])
    hu = _dot(u2, wfu_ref[...])
    act = (hg * _sigmoid(hg) * hu).astype(BF16)
    ff = _dot(act, wfd_ref[...])
    gf = gf_ref[...]
    for j in range(bt):
        rows = slice(j * lt, (j + 1) * lt)
        x2 = y_ref[j] + mod_ref[j, 5:6, :] * ff[rows]
        ms = jnp.mean(x2 * x2, axis=-1, keepdims=True)
        y_ref[j] = x2 * lax.rsqrt(ms + EPS) * gf


def _post(x, ya, yb, mod3, g1, g2, gf, wgab, wa, wb, wout, wfg, wfu, wfd, *, bt, lt):
    nb, seq, _ = x.shape
    assert nb % bt == 0 and seq % lt == 0

    def tok(width):
        return pl.BlockSpec((bt, lt, width), lambda b, l: (b, l, 0))

    in_specs = [tok(D_MODEL), tok(SB_WIDTH), tok(ML_WIDTH),
                pl.BlockSpec((bt, 6, D_MODEL), lambda b, l: (b, 0, 0)),
                _const_spec((1, D_MODEL)), _const_spec((1, D_MODEL)), _const_spec((1, D_MODEL)),
                _const_spec(wgab.shape), _const_spec(wa.shape), _const_spec(wb.shape), _const_spec(wout.shape),
                _const_spec(wfg.shape), _const_spec(wfu.shape), _const_spec(wfd.shape)]
    return pl.pallas_call(
        functools.partial(_post_kernel, bt=bt, lt=lt),
        grid=(nb // bt, seq // lt),
        in_specs=in_specs,
        out_specs=tok(D_MODEL),
        out_shape=jax.ShapeDtypeStruct((nb, seq, D_MODEL), F32),
        scratch_shapes=[pltpu.VMEM((bt * lt, D_MODEL), BF16)],
        compiler_params=_params(2),
        name="post",
    )(x, ya, yb, mod3, g1, g2, gf, wgab, wa, wb, wout, wfg, wfu, wfd)


def _group(x, mod3, k_past, v_past, c0, n0, m0, conv0, wts, *, bt, lt_in, lt, tq, n_past, lc):
    (g1, g2, gf, w1, wif_row, bif_row, wconv, bconv, gn,
     wgab, wa, wb, wout, wfg, wfu, wfd) = wts
    nb, seq, _ = x.shape
    (q, k32, v32, kb, vb, mq, mk, mv, mo, grow, conv_new) = _inproj(
        x, mod3, conv0, g1, w1, wif_row, bif_row, wconv, bconv, bt=bt, lt=lt_in)
    if k_past is None:
        k_past, v_past = kb, vb
    ya = _attn(q, kb, vb, k_past, v_past, bt=ATTN_SEQS, tq=tq, n_past=n_past)
    m0b = jnp.broadcast_to(jnp.pad(m0, ((0, 0), (0, SUBLANES - ML_HEADS)))[:, :, None], (nb, SUBLANES, LANES))
    yb, c1, n1, m1 = _mlstm(mq, mk, mv, mo, grow, c0, n0, m0b, gn, bt=ML_SEQS, lc=lc)
    y = _post(x, ya, yb, mod3, g1, g2, gf, wgab, wa, wb, wout, wfg, wfu, wfd, bt=bt, lt=lt)
    def rows_out(t):
        if _tokens_on_lanes(lt_in):
            return jnp.transpose(t.reshape(nb, SB_HEADS, SB_HEAD_DIM, seq), (0, 3, 1, 2))[None]
        return t.reshape(1, nb, seq, SB_HEADS, SB_HEAD_DIM)

    states = (rows_out(k32), rows_out(v32),
              c1[None], n1[None], m1[None, :, :ML_HEADS, 0], conv_new[None])
    return y, states


def kernel(x_prompt, x_sample, c_prompt, c_sample, cache_sb_k, cache_sb_v, state_mlstm_C, state_mlstm_n, state_mlstm_m, state_conv, norm1_g, norm2_g, w_ada, b_ada, w_in, b_if, w_conv, b_conv, ml_norm_g, w_a, w_b, w_out, w_ff_gate, w_ff_up, w_ff_down, final_g):
    assert w_in.shape[0] == 1, "single layer"
    bp, seq_p, _ = x_prompt.shape
    bs, seq_s, _ = x_sample.shape
    past = cache_sb_k.shape[2]

    c_all = jnp.concatenate([c_prompt, c_sample], axis=0)
    mod3 = _ada(c_all, w_ada[0], b_ada[0]).reshape(bp + bs, 6, D_MODEL)

    w1, wgab, wif_row, wa, wb, wout, wfg, wfu, wfd = _prep(
        jnp.transpose(w_in[0]), w_a[0], w_b[0], w_out[0], w_ff_gate[0], w_ff_up[0], w_ff_down[0])
    pad = SUBLANES - ML_HEADS
    bif_row = jnp.pad(b_if[0].reshape(2, ML_HEADS), ((0, 0), (0, pad))).reshape(2 * SUBLANES, 1)
    wts = (norm1_g[0].reshape(1, D_MODEL), norm2_g[0].reshape(1, D_MODEL), final_g.reshape(1, D_MODEL),
           w1, wif_row, bif_row, w_conv[0], b_conv[0].reshape(1, 2 * ML_WIDTH),
           ml_norm_g[0].reshape(1, ML_WIDTH),
           wgab, wa, wb, wout, wfg, wfu, wfd)

    zeros = functools.partial(jnp.zeros, dtype=F32)
    y_p, st_p = _group(
        x_prompt, mod3[:bp], None, None,
        zeros((bp, ML_HEADS, ML_HEAD_DIM, ML_HEAD_DIM)), zeros((bp, ML_HEADS, ML_HEAD_DIM)),
        zeros((bp, ML_HEADS)), zeros((bp, CONV_W - 1, 2 * ML_WIDTH)), wts,
        bt=1, lt_in=512, lt=256, tq=ATTN_BLOCK, n_past=None, lc=ML_CHUNK)
    def cache_t(c):
        return jnp.transpose(c[0], (0, 2, 3, 1)).reshape(bs, SB_WIDTH, past)

    y_s, st_s = _group(
        x_sample, mod3[bp:], cache_t(cache_sb_k), cache_t(cache_sb_v),
        state_mlstm_C[0], state_mlstm_n[0], state_mlstm_m[0], state_conv[0], wts,
        bt=4, lt_in=seq_s, lt=seq_s, tq=seq_s, n_past=past // ATTN_BLOCK, lc=seq_s)
    return (y_p, y_s) + st_p + st_s
```

```python
import functools
import math

import jax
import jax.numpy as jnp
from jax import lax
from jax.experimental import pallas as pl
from jax.experimental.pallas import tpu as pltpu

D_MODEL = 1024
SB_HEADS = 8
SB_HEAD_DIM = 64
SB_WIDTH = SB_HEADS * SB_HEAD_DIM
ML_HEADS = 4
ML_HEAD_DIM = 128
ML_WIDTH = ML_HEADS * ML_HEAD_DIM
CONV_W = 4
D_FF = 2816
EPS = 1e-6
LANES = 128
SUBLANES = 8
ATTN_BLOCK = 128
ML_CHUNK = 128
ML_SEQS = 8
EXP_UNDERFLOW = 88.0
SOFTPLUS_CLAMP = 60.0
ATTN_SEQS = 2
ATTN_GROUPS = 4
V7X_VMEM_BYTES = 64 * 1024 * 1024
VMEM_LIMIT = V7X_VMEM_BYTES - 8 * 1024 * 1024

BF16 = jnp.bfloat16
F32 = jnp.float32

_C_Q, _C_K, _C_V = 0, SB_WIDTH, 2 * SB_WIDTH
_C_MQK = 3 * SB_WIDTH
_C_MV = _C_MQK + 2 * ML_WIDTH
_C_MO = _C_MV + ML_WIDTH
_C_IF = _C_MO + ML_WIDTH
_C_GAB = _C_IF + 2 * ML_HEADS
_C_END = _C_GAB + 2 * D_MODEL


def _sigmoid(x):
    return 1.0 / (1.0 + jnp.exp(-x))


def _log_sigmoid(x):
    return jnp.minimum(x, 0.0) - jnp.log1p(jnp.exp(-jnp.abs(x)))


def _softplus_pos(x):
    return jnp.maximum(jnp.log(1.0 + jnp.exp(jnp.minimum(x, SOFTPLUS_CLAMP))), x)


def _dot(a, b):
    return jnp.dot(a, b, preferred_element_type=F32)


def _dot_nt(a, b):
    return lax.dot_general(a, b, (((1,), (1,)), ((), ())), preferred_element_type=F32)


def _split_dot(x, tri, parts):
    acc = None
    rem = x
    for i in range(parts):
        piece = rem.astype(BF16)
        term = _dot(piece, tri)
        acc = term if acc is None else acc + term
        if i + 1 < parts:
            rem = rem - piece.astype(F32)
    return acc


def _norm_mod(x, g, sc, sh):
    ms = jnp.mean(x * x, axis=-1, keepdims=True)
    return (x * lax.rsqrt(ms + EPS)) * (g * (1.0 + sc)) + sh


def _const_spec(shape):
    nd = len(shape)
    return pl.BlockSpec(shape, lambda *_: (0,) * nd, pipeline_mode=pl.Buffered(1))


def _params(n_axes):
    return pltpu.CompilerParams(dimension_semantics=("arbitrary",) * n_axes,
                                vmem_limit_bytes=VMEM_LIMIT)


def _ada_kernel(c_ref, w_ref, b_ref, o_ref):
    c = c_ref[...]
    s = c * _sigmoid(c)
    o_ref[...] = _dot(s.astype(BF16), w_ref[...].astype(BF16)) + b_ref[...]


def _ada(c_all, w_ada, b_ada):
    nb = c_all.shape[0]
    n_out = w_ada.shape[1]
    tn = 2 * D_MODEL
    return pl.pallas_call(
        _ada_kernel,
        grid=(n_out // tn,),
        in_specs=[pl.BlockSpec((nb, D_MODEL), lambda j: (0, 0)),
                  pl.BlockSpec((D_MODEL, tn), lambda j: (0, j)),
                  pl.BlockSpec((1, tn), lambda j: (0, j))],
        out_specs=pl.BlockSpec((nb, tn), lambda j: (0, j)),
        out_shape=jax.ShapeDtypeStruct((nb, n_out), F32),
        compiler_params=_params(1),
        name="ada",
    )(c_all, w_ada, b_ada.reshape(1, n_out))


PREP_STEPS = 8
W1_BLOCK = 512


def _prep_kernel(w1_ref, wgab_ref, wif_ref, wa_ref, wb_ref, wout_ref, wfg_ref, wfu_ref, wfd_ref,
                 w1_o, wgab_o, wif_o, wa_o, wb_o, wout_o, wfg_o, wfu_o, wfd_o):
    w1_o[...] = w1_ref[...].T.astype(BF16)
    for src, dst in ((wgab_ref, wgab_o), (wa_ref, wa_o), (wb_ref, wb_o), (wout_ref, wout_o),
                     (wfg_ref, wfg_o), (wfu_ref, wfu_o), (wfd_ref, wfd_o)):
        dst[...] = src[...].astype(BF16)
    gates = wif_ref[...]
    pad = jnp.zeros((SUBLANES - ML_HEADS, D_MODEL), F32)
    wif_o[...] = jnp.concatenate([gates[:ML_HEADS], pad, gates[ML_HEADS:], pad], axis=0).astype(BF16)


def _prep(w_in_t, w_a, w_b, w_out, w_fg, w_fu, w_fd):
    def rows(a, n_rows=None, first=0):
        n_rows = a.shape[0] if n_rows is None else n_rows
        step = n_rows // PREP_STEPS
        assert n_rows % (PREP_STEPS * 2 * SUBLANES) == 0 and first % SUBLANES == 0
        if first == 0:
            return pl.BlockSpec((step, a.shape[1]), lambda i: (i, 0))
        return pl.BlockSpec((pl.Element(step), pl.Element(a.shape[1])),
                            lambda i: (pl.multiple_of(first + i * step, SUBLANES), 0))

    def out(n_rows, n_cols):
        return (jax.ShapeDtypeStruct((n_rows, n_cols), BF16),
                pl.BlockSpec((n_rows // PREP_STEPS, n_cols), lambda i: (i, 0)))

    last = _C_IF // W1_BLOCK - 1
    assert _C_IF % W1_BLOCK == 0 and last < PREP_STEPS
    w1_in = pl.BlockSpec((W1_BLOCK, D_MODEL), lambda i: (jnp.minimum(i, last), 0))
    w1_out = (jax.ShapeDtypeStruct((D_MODEL, _C_IF), BF16),
              pl.BlockSpec((D_MODEL, W1_BLOCK), lambda i: (0, jnp.minimum(i, last))))

    others = (w_a, w_b, w_out, w_fg, w_fu, w_fd)
    outs = [w1_out, out(_C_END - _C_GAB, D_MODEL),
            (jax.ShapeDtypeStruct((2 * SUBLANES, D_MODEL), BF16),
             pl.BlockSpec((2 * SUBLANES, D_MODEL), lambda i: (0, 0)))] + [out(*a.shape) for a in others]
    in_specs = [w1_in, rows(w_in_t, _C_END - _C_GAB, _C_GAB),
                pl.BlockSpec((pl.Element(2 * ML_HEADS), pl.Element(D_MODEL)), lambda i: (_C_IF, 0))
                ] + [rows(a) for a in others]
    return pl.pallas_call(
        _prep_kernel,
        grid=(PREP_STEPS,),
        in_specs=in_specs,
        out_specs=[o[1] for o in outs],
        out_shape=[o[0] for o in outs],
        compiler_params=_params(1),
        name="prep",
    )(w_in_t, w_in_t, w_in_t, *others)


def _tokens_on_lanes(tile_len):
    return tile_len % LANES == 0


def _inproj_kernel(x_ref, mod_ref, conv0_ref, g1_ref, w1_ref, wifr_ref, bifr_ref,
                   wconv_ref, bconv_ref,
                   q_ref, k32_ref, v32_ref, kb_ref, vb_ref, mq_ref, mk_ref, mv_ref, mo_ref,
                   grow_ref, convn_ref,
                   u_sc, xp_sc, *, bt, lt):
    @pl.when(pl.program_id(1) == 0)
    def _():
        for j in range(bt):
            xp_sc[j] = jnp.concatenate(
                [jnp.zeros((SUBLANES - (CONV_W - 1), 2 * ML_WIDTH), F32), conv0_ref[j]], axis=0)

    g1 = g1_ref[...]
    for j in range(bt):
        u = _norm_mod(x_ref[j], g1, mod_ref[j, 1:2, :], mod_ref[j, 0:1, :])
        u_sc[j * lt:(j + 1) * lt, :] = u.astype(BF16)
    u = u_sc[...]

    def mm(lo, hi):
        return _dot(u, w1_ref[:, lo:hi])

    def new_rows(t):
        return t.T if _tokens_on_lanes(lt) else t

    wconv = wconv_ref[...]
    bconv = bconv_ref[...]
    k_scale = 1.0 / math.sqrt(ML_HEAD_DIM)
    mqk = mm(_C_MQK, _C_MV)
    row8 = lax.broadcasted_iota(jnp.int32, (SUBLANES, 2 * ML_WIDTH), 0)
    for j in range(bt):
        x = mqk[j * lt:(j + 1) * lt]
        before = xp_sc[j]
        conv = bconv + x * wconv[CONV_W - 1:CONV_W, :]
        for k in range(1, CONV_W):
            xr = pltpu.roll(x, k, axis=0)
            head = jnp.where(row8 < k, pltpu.roll(before, k, axis=0), xr[:SUBLANES])
            xk = jnp.concatenate([head, xr[SUBLANES:]], axis=0)
            conv = conv + xk * wconv[CONV_W - 1 - k:CONV_W - k, :]
        conv = conv * _sigmoid(conv)
        mq_ref[j] = conv[:, :ML_WIDTH].astype(BF16)
        mk_ref[j] = (conv[:, ML_WIDTH:] * k_scale).astype(BF16)
        convn_ref[j] = x[lt - (CONV_W - 1):]
        xp_sc[j] = x[lt - SUBLANES:]

    sq = mm(_C_Q, _C_K)
    sk = mm(_C_K, _C_V)
    sv = mm(_C_V, _C_MQK)
    mv = mm(_C_MV, _C_MO)
    mo = mm(_C_MO, _C_IF)
    for j in range(bt):
        rows = slice(j * lt, (j + 1) * lt)
        q_ref[j] = sq[rows].astype(BF16)
        k32_ref[j] = new_rows(sk[rows])
        kb_ref[j] = sk[rows].astype(BF16)
        v32_ref[j] = new_rows(sv[rows])
        vb_ref[j] = sv[rows].astype(BF16)
        mv_ref[j] = mv[rows].astype(BF16)
    for j in range(bt):
        rows = slice(j * lt, (j + 1) * lt)
        mo_ref[j] = mo[rows]
        gr = _dot_nt(wifr_ref[...], u_sc[rows, :]) + bifr_ref[...]
        row = lax.broadcasted_iota(jnp.int32, gr.shape, 0)
        grow_ref[j] = jnp.where(row >= SUBLANES, _log_sigmoid(gr), gr)


def _inproj(x, mod3, conv0, g1, w1, wif_row, bif_row, wconv, bconv, *, bt, lt):
    nb, seq, _ = x.shape
    assert nb % bt == 0 and seq % lt == 0 and (bt == 1 or lt == seq)
    grid = (nb // bt, seq // lt)

    def tok(width):
        return pl.BlockSpec((bt, lt, width), lambda b, l: (b, l, 0))

    def per_batch(rows, width):
        return pl.BlockSpec((bt, rows, width), lambda b, l: (b, 0, 0))

    in_specs = [tok(D_MODEL), per_batch(6, D_MODEL), per_batch(CONV_W - 1, 2 * ML_WIDTH),
                _const_spec((1, D_MODEL)), _const_spec(w1.shape),
                _const_spec((2 * SUBLANES, D_MODEL)),
                _const_spec((2 * SUBLANES, 1)),
                _const_spec((CONV_W, 2 * ML_WIDTH)), _const_spec((1, 2 * ML_WIDTH))]

    def tok_on_lanes(rows):
        return pl.BlockSpec((bt, rows, lt), lambda b, l: (b, 0, l))

    kv_spec = tok_on_lanes(SB_WIDTH) if _tokens_on_lanes(lt) else tok(SB_WIDTH)
    kv_shape = (nb, SB_WIDTH, seq) if _tokens_on_lanes(lt) else (nb, seq, SB_WIDTH)
    out_specs = [tok(SB_WIDTH), kv_spec, kv_spec, tok(SB_WIDTH), tok(SB_WIDTH)] + [tok(ML_WIDTH)] * 4 + [
        tok_on_lanes(2 * SUBLANES), per_batch(CONV_W - 1, 2 * ML_WIDTH)]

    def sds(shape, dt):
        return jax.ShapeDtypeStruct(shape, dt)

    out_shape = [sds((nb, seq, SB_WIDTH), BF16), sds(kv_shape, F32), sds(kv_shape, F32),
                 sds((nb, seq, SB_WIDTH), BF16), sds((nb, seq, SB_WIDTH), BF16),
                 sds((nb, seq, ML_WIDTH), BF16), sds((nb, seq, ML_WIDTH), BF16), sds((nb, seq, ML_WIDTH), BF16),
                 sds((nb, seq, ML_WIDTH), F32),
                 sds((nb, 2 * SUBLANES, seq), F32),
                 sds((nb, CONV_W - 1, 2 * ML_WIDTH), F32)]
    return pl.pallas_call(
        functools.partial(_inproj_kernel, bt=bt, lt=lt),
        grid=grid,
        in_specs=in_specs,
        out_specs=out_specs,
        out_shape=out_shape,
        scratch_shapes=[pltpu.VMEM((bt * lt, D_MODEL), BF16),
                        pltpu.VMEM((bt, SUBLANES, 2 * ML_WIDTH), F32)],
        compiler_params=_params(2),
        name="inproj",
    )(x, mod3, conv0, g1, w1, wif_row, bif_row, wconv, bconv)


def _cum_matrix(t):
    r = lax.broadcasted_iota(jnp.int32, (2 * t, LANES + t), 0)
    c = lax.broadcasted_iota(jnp.int32, (2 * t, LANES + t), 1)
    return jnp.where((c < LANES) | ((r & (t - 1)) > (c - LANES)), 1.0, 0.0).astype(BF16)


def _attn_kernel(q_ref, kd_ref, vd_ref, kp_ref, vp_ref, o_ref, qm_sc, acc_sc, carry_sc, *, bt, tq, tk, n_past):
    cache_past = n_past is not None
    n_blocks = n_past if cache_past else pl.program_id(1)
    scale = 1.0 / math.sqrt(SB_HEAD_DIM)
    n_pairs = SB_WIDTH // LANES
    lane = lax.broadcasted_iota(jnp.int32, (1, LANES), 1)
    first_head = lane < SB_HEAD_DIM

    n_units = bt * n_pairs
    for u in range(n_units):
        s, p = divmod(u, n_pairs)
        qp = q_ref[s, :, p * LANES:(p + 1) * LANES] * scale
        zero = jnp.zeros_like(qp)
        qm_sc[(2 * u) * tq:(2 * u + 1) * tq, :] = jnp.where(first_head, qp, zero)
        qm_sc[(2 * u + 1) * tq:(2 * u + 2) * tq, :] = jnp.where(first_head, zero, qp)

    per_group = n_units // ATTN_GROUPS
    groups = [range(g, g + per_group) for g in range(0, n_units, per_group)]
    group_rows = [slice(2 * grp[0] * tq, 2 * (grp[-1] + 1) * tq) for grp in groups]

    def scores(k_of, transposed):
        dot_k = _dot if transposed else _dot_nt
        return [jnp.concatenate([dot_k(qm_sc[2 * u * tq:(2 * u + 2) * tq, :], k_of(u)) for u in grp], axis=0)
                for grp in groups]

    def weights(zs, t, visible):
        cum = _cum_matrix(t)
        sps, sums = [], []
        for gi, rows in enumerate(group_rows):
            sp = _softplus_pos(zs[gi])
            if visible is not None:
                sp = jnp.where(visible[rows], sp, 0.0)
            hi = sp.astype(BF16)
            lo = (sp - hi.astype(F32)).astype(BF16)
            sps.append(sp)
            sums.append(_dot(jnp.concatenate([hi, lo], axis=1), cum))
        out = []
        for gi, rows in enumerate(group_rows):
            logit = zs[gi] - sps[gi] - sums[gi][:, LANES:]
            if visible is None:
                carry = carry_sc[rows, :]
                a = jnp.exp(logit - carry)
                carry_sc[rows, :] = carry + sums[gi][:, :LANES]
            else:
                a = jnp.where(visible[rows], jnp.exp(logit), 0.0)
                carry_sc[rows, :] = sums[gi][:, :LANES]
            out.append(a.astype(BF16))
        return out

    def accumulate(a_groups, v_of, transposed, assign=False):
        dot_v = _dot_nt if transposed else _dot
        for gi, grp in enumerate(groups):
            for j, u in enumerate(grp):
                s, p = divmod(u, n_pairs)
                cols = slice(p * LANES, (p + 1) * LANES)
                pv = dot_v(a_groups[gi][2 * j * tq:(2 * j + 2) * tq, :], v_of(u))
                out = jnp.where(first_head, pv[:tq], pv[tq:])
                if assign:
                    acc_sc[s, :, cols] = out
                else:
                    acc_sc[s, :, cols] += out

    def unit_cols(ref):
        def of(u):
            s, p = divmod(u, n_pairs)
            return ref[s, :, p * LANES:(p + 1) * LANES]
        return of

    def past_of(ref, block_index):
        start = pl.multiple_of(block_index * tk, tk)

        def of(u):
            s, p = divmod(u, n_pairs)
            if cache_past:
                return ref[s, p * LANES:(p + 1) * LANES, pl.ds(start, tk)].astype(BF16)
            return ref[s, pl.ds(start, tk), p * LANES:(p + 1) * LANES].astype(BF16)
        return of

    rd = lax.broadcasted_iota(jnp.int32, (2 * n_units * tq, tq), 0) & (tq - 1)
    cd = lax.broadcasted_iota(jnp.int32, (2 * n_units * tq, tq), 1)
    a_diag = weights(scores(unit_cols(kd_ref), False), tq, cd < rd)
    accumulate(a_diag, unit_cols(vd_ref), False, assign=True)

    def not_done():
        return jnp.min(carry_sc[...]) <= EXP_UNDERFLOW

    def past_block(i):
        block_index = n_blocks - 1 - i
        a = weights(scores(past_of(kp_ref, block_index), cache_past), tk, None)
        accumulate(a, past_of(vp_ref, block_index), cache_past)

    def pair_cond(state):
        i, go = state
        return jnp.logical_and(i + 1 < n_blocks, go)

    def pair_body(state):
        i, _ = state
        past_block(i)
        past_block(i + 1)
        return i + 2, not_done()

    def cond(state):
        i, go = state
        return jnp.logical_and(i < n_blocks, go)

    def body(state):
        i, _ = state
        past_block(i)
        return i + 1, not_done()

    state = lax.while_loop(pair_cond, pair_body, (jnp.int32(0), not_done()))
    lax.while_loop(cond, body, state)
    o_ref[...] = acc_sc[...].astype(BF16)


def _attn(q, k_new, v_new, k_past, v_past, *, bt, tq, n_past):
    nb, seq, _ = q.shape
    assert nb % bt == 0
    tok = pl.BlockSpec((bt, tq, SB_WIDTH), lambda b, i: (b, i, 0))
    past = pl.BlockSpec((bt,) + k_past.shape[1:], lambda b, i: (b, 0, 0))
    rows = bt * SB_HEADS * tq
    return pl.pallas_call(
        functools.partial(_attn_kernel, bt=bt, tq=tq, tk=ATTN_BLOCK, n_past=n_past),
        grid=(nb // bt, seq // tq),
        in_specs=[tok, tok, tok, past, past],
        out_specs=tok,
        out_shape=jax.ShapeDtypeStruct((nb, seq, SB_WIDTH), BF16),
        scratch_shapes=[pltpu.VMEM((rows, LANES), BF16),
                        pltpu.VMEM((bt, tq, SB_WIDTH), F32),
                        pltpu.VMEM((rows, LANES), F32)],
        compiler_params=_params(2),
        name="attn",
    )(q, k_new, v_new, k_past, v_past)


def _mlstm_kernel(mq_ref, mk_ref, mv_ref, mo_ref, grow_ref, c0_ref, n0_ref, m0_ref, gn_ref,
                  yb_ref, c_out_ref, n_out_ref, m_out_ref,
                  c_sc, n_sc, m_sc, alpha_sc, beta_sc, r_sc, wg_sc, dec_sc, mprev_sc, *, bt, lc, nc):
    step = pl.program_id(1)
    dk = ML_HEAD_DIM

    ri = lax.broadcasted_iota(jnp.int32, (lc, lc), 0)
    ci = lax.broadcasted_iota(jnp.int32, (lc, lc), 1)
    causal = ci <= ri
    eye = jnp.where(ri == ci, 1.0, 0.0).astype(BF16)
    ones_blk = jnp.ones((lc, dk), BF16)
    zeros_blk = jnp.zeros((lc, dk), BF16)

    @pl.when(step == 0)
    def _():
        c_sc[...] = c0_ref[...]
        n_sc[...] = n0_ref[...]
        incl_t = jnp.where(ri <= ci, 1.0, 0.0).astype(BF16)
        blocks = [(s, c) for s in range(bt) for c in range(nc)]
        li = jnp.concatenate([grow_ref[s, 0:SUBLANES, c * lc:(c + 1) * lc] for s, c in blocks], axis=0)
        lf = jnp.concatenate([grow_ref[s, SUBLANES:2 * SUBLANES, c * lc:(c + 1) * lc] for s, c in blocks],
                             axis=0)
        b = _split_dot(lf, incl_t, 3)
        r = li - b
        lane = lax.broadcasted_iota(jnp.int32, r.shape, 1)
        run = r
        d = 1
        while d < lc:
            run = jnp.maximum(run, jnp.where(lane >= d, pltpu.roll(run, d, axis=1), -jnp.inf))
            d *= 2
        b_last = b[:, lc - 1:lc]
        g = b_last - b + li
        g_max = jnp.max(g, axis=1, keepdims=True)
        m_before = []
        m_after = []
        for s in range(bt):
            m = m0_ref[s]
            for c in range(nc):
                rows = slice((s * nc + c) * SUBLANES, (s * nc + c + 1) * SUBLANES)
                m_before.append(m)
                m = jnp.maximum(b_last[rows] + m, g_max[rows])
                m_after.append(m)
            m_sc[s] = m
        m_prev = jnp.concatenate(m_before, axis=0)
        m_new = jnp.concatenate(m_after, axis=0)
        alpha = (-jnp.maximum(run, m_prev[:, :lc])).astype(BF16).astype(F32)
        alpha_sc[...] = alpha
        beta_sc[...] = alpha - b
        r_sc[...] = r
        wg_sc[...] = jnp.exp(g - m_new[:, :lc])
        dec_sc[...] = jnp.exp(b_last + m_prev - m_new)
        mprev_sc[...] = m_prev

    units = [(s, h) for s in range(bt) for h in range(ML_HEADS)]
    cols = [slice(h * dk, (h + 1) * dk) for h in range(ML_HEADS)]
    alpha, beta_hi, beta_lo, r, wg, decay, m_prev = [], [], [], [], [], [], []
    for s in range(bt):
        chunk = pl.ds(pl.multiple_of((s * nc + step) * SUBLANES, SUBLANES), SUBLANES)
        alpha.append(alpha_sc[chunk, :].astype(BF16))
        beta = beta_sc[chunk, :]
        beta_hi.append(beta.astype(BF16))
        beta_lo.append((beta - beta_hi[s].astype(F32)).astype(BF16))
        r.append(r_sc[chunk, :])
        wg.append(wg_sc[chunk, :])
        decay.append(dec_sc[chunk, :])
        m_prev.append(mprev_sc[chunk, :])
    eye2 = jnp.concatenate([eye, eye], axis=1)

    def rows_of(x, h):
        return jnp.broadcast_to(x[h:h + 1, :], (dk, lc))
    alpha_b = _dot_nt(eye, jnp.concatenate([rows_of(alpha[s], h) for s, h in units], axis=0))
    beta_b = _dot_nt(eye2, jnp.concatenate(
        [jnp.concatenate([rows_of(beta_hi[s], h), rows_of(beta_lo[s], h)], axis=1) for s, h in units], axis=0))
    qk = [_dot_nt(mq_ref[s, :, cols[h]], mk_ref[s, :, cols[h]]) for s, h in units]
    c_prev = [c_sc[s, h] for s, h in units]
    n_prev = [n_sc[s, h:h + 1, :] for s, h in units]
    qc = [_dot_nt(mq_ref[s, :, cols[h]],
                  jnp.concatenate([c_prev[u].astype(BF16),
                                   jnp.broadcast_to(n_prev[u], (dk, dk)).astype(BF16)], axis=0))
          for u, (s, h) in enumerate(units)]
    upd = []
    for s, h in units:
        wg_h = wg[s][h:h + 1, :]
        vt = mv_ref[s, :, cols[h]].astype(F32).T
        lhs = jnp.concatenate([(vt * wg_h).astype(BF16),
                               jnp.broadcast_to(wg_h, (SUBLANES, lc)).astype(BF16)], axis=0)
        upd.append(_dot(lhs, mk_ref[s, :, cols[h]]))

    cat = []
    for u, (s, h) in enumerate(units):
        w = jnp.where(causal, jnp.exp(alpha_b[:, u * dk:u * dk + lc] + r[s][h:h + 1, :]), 0.0)
        sw = qk[u] * w
        s_hi = sw.astype(BF16)
        cat.append(jnp.concatenate([s_hi, (sw - s_hi.astype(F32)).astype(BF16)], axis=1))
    nv = []
    for u, (s, h) in enumerate(units):
        rhs = jnp.concatenate([jnp.concatenate([mv_ref[s, :, cols[h]], ones_blk], axis=1),
                               jnp.concatenate([zeros_blk, ones_blk], axis=1)], axis=0)
        nv.append(_dot(cat[u], rhs))
    hval = []
    cat2 = []
    for u, (s, h) in enumerate(units):
        ucols = slice(u * dk, (u + 1) * dk)
        w_inter = jnp.exp(m_prev[s][h:h + 1, :] + alpha_b[:, ucols])
        num = nv[u][:, :dk] + w_inter * qc[u][:, :dk]
        den = nv[u][:, dk:] + w_inter * qc[u][:, dk:]
        hv = num / jnp.maximum(jnp.abs(den), jnp.exp(beta_b[:, ucols]))
        h2 = hv * hv
        h2_hi = h2.astype(BF16)
        hval.append(hv)
        cat2.append(jnp.concatenate([h2_hi, (h2 - h2_hi.astype(F32)).astype(BF16)], axis=1))
    ms = _dot(jnp.concatenate(cat2, axis=0), jnp.ones((2 * dk, dk), BF16)) * (1.0 / dk)
    for u, (s, h) in enumerate(units):
        y = hval[u] * lax.rsqrt(ms[u * lc:(u + 1) * lc] + EPS) * gn_ref[:, cols[h]]
        yb_ref[s, :, cols[h]] = (_sigmoid(mo_ref[s, :, cols[h]]) * y).astype(BF16)

    for u, (s, h) in enumerate(units):
        dec = decay[s][h:h + 1, :]
        c_sc[s, h] = dec * c_prev[u] + upd[u][:dk]
        n_sc[s, h:h + 1, :] = dec * n_prev[u] + upd[u][dk:dk + 1]

    c_out_ref[...] = c_sc[...]
    n_out_ref[...] = n_sc[...]
    m_out_ref[...] = m_sc[...]


def _mlstm(mq, mk, mv, mo, grow, c0, n0, m0, gn, *, bt, lc):
    nb, seq, _ = mq.shape
    assert nb % bt == 0
    tok = pl.BlockSpec((bt, lc, ML_WIDTH), lambda b, i: (b, i, 0))
    c_spec = pl.BlockSpec((bt, ML_HEADS, ML_HEAD_DIM, ML_HEAD_DIM), lambda b, i: (b, 0, 0, 0))
    n_spec = pl.BlockSpec((bt, ML_HEADS, ML_HEAD_DIM), lambda b, i: (b, 0, 0))
    m_spec = pl.BlockSpec((bt, SUBLANES, LANES), lambda b, i: (b, 0, 0))
    nc = seq // lc
    table = pltpu.VMEM((bt * nc * SUBLANES, lc), F32)
    table_m = pltpu.VMEM((bt * nc * SUBLANES, LANES), F32)
    return pl.pallas_call(
        functools.partial(_mlstm_kernel, bt=bt, lc=lc, nc=nc),
        grid=(nb // bt, nc),
        in_specs=[tok, tok, tok, tok,
                  pl.BlockSpec((bt, 2 * SUBLANES, seq), lambda b, i: (b, 0, 0)),
                  c_spec, n_spec, m_spec, _const_spec((1, ML_WIDTH))],
        out_specs=[tok, c_spec, n_spec, m_spec],
        out_shape=[jax.ShapeDtypeStruct((nb, seq, ML_WIDTH), BF16),
                   jax.ShapeDtypeStruct((nb, ML_HEADS, ML_HEAD_DIM, ML_HEAD_DIM), F32),
                   jax.ShapeDtypeStruct((nb, ML_HEADS, ML_HEAD_DIM), F32),
                   jax.ShapeDtypeStruct((nb, SUBLANES, LANES), F32)],
        scratch_shapes=[pltpu.VMEM((bt, ML_HEADS, ML_HEAD_DIM, ML_HEAD_DIM), F32),
                        pltpu.VMEM((bt, ML_HEADS, ML_HEAD_DIM), F32),
                        pltpu.VMEM((bt, SUBLANES, LANES), F32),
                        table, table, table, table, table_m, table_m],
        compiler_params=_params(2),
        name="mlstm",
    )(mq, mk, mv, mo, grow, c0, n0, m0, gn)


def _post_kernel(x_ref, ya_ref, yb_ref, mod_ref, g1_ref, g2_ref, gf_ref,
                 wgab_ref, wa_ref, wb_ref, wout_ref, wfg_ref, wfu_ref, wfd_ref,
                 y_ref, u_sc, *, bt, lt):
    g1 = g1_ref[...]
    for j in range(bt):
        u = _norm_mod(x_ref[j], g1, mod_ref[j, 1:2, :], mod_ref[j, 0:1, :])
        u_sc[j * lt:(j + 1) * lt, :] = u.astype(BF16)
    u = u_sc[...]
    ya = ya_ref[...].reshape(bt * lt, SB_WIDTH)
    yb = yb_ref[...].reshape(bt * lt, ML_WIDTH)
    ga = _dot_nt(u, wgab_ref[:D_MODEL, :])
    gb = _dot_nt(u, wgab_ref[D_MODEL:, :])
    merged = _sigmoid(ga) * _dot(ya, wa_ref[...]) + _sigmoid(gb) * _dot(yb, wb_ref[...])
    attn_out = _dot(merged.astype(BF16), wout_ref[...])

    g2 = g2_ref[...]
    for j in range(bt):
        rows = slice(j * lt, (j + 1) * lt)
        x1 = x_ref[j] + mod_ref[j, 2:3, :] * attn_out[rows]
        y_ref[j] = x1
        u2 = _norm_mod(x1, g2, mod_ref[j, 4:5, :], mod_ref[j, 3:4, :])
        u_sc[rows, :] = u2.astype(BF16)
    u2 = u_sc[...]
    hg = _dot(u2, wfg_ref[...])
    hu = _dot(u2, wfu_ref[...])
    act = (hg * _sigmoid(hg) * hu).astype(BF16)
    ff = _dot(act, wfd_ref[...])
    gf = gf_ref[...]
    for j in range(bt):
        rows = slice(j * lt, (j + 1) * lt)
        x2 = y_ref[j] + mod_ref[j, 5:6, :] * ff[rows]
        ms = jnp.mean(x2 * x2, axis=-1, keepdims=True)
        y_ref[j] = x2 * lax.rsqrt(ms + EPS) * gf


def _post(x, ya, yb, mod3, g1, g2, gf, wgab, wa, wb, wout, wfg, wfu, wfd, *, bt, lt):
    nb, seq, _ = x.shape
    assert nb % bt == 0 and seq % lt == 0

    def tok(width):
        return pl.BlockSpec((bt, lt, width), lambda b, l: (b, l, 0))

    in_specs = [tok(D_MODEL), tok(SB_WIDTH), tok(ML_WIDTH),
                pl.BlockSpec((bt, 6, D_MODEL), lambda b, l: (b, 0, 0)),
                _const_spec((1, D_MODEL)), _const_spec((1, D_MODEL)), _const_spec((1, D_MODEL)),
                _const_spec(wgab.shape), _const_spec(wa.shape), _const_spec(wb.shape), _const_spec(wout.shape),
                _const_spec(wfg.shape), _const_spec(wfu.shape), _const_spec(wfd.shape)]
    return pl.pallas_call(
        functools.partial(_post_kernel, bt=bt, lt=lt),
        grid=(nb // bt, seq // lt),
        in_specs=in_specs,
        out_specs=tok(D_MODEL),
        out_shape=jax.ShapeDtypeStruct((nb, seq, D_MODEL), F32),
        scratch_shapes=[pltpu.VMEM((bt * lt, D_MODEL), BF16)],
        compiler_params=_params(2),
        name="post",
    )(x, ya, yb, mod3, g1, g2, gf, wgab, wa, wb, wout, wfg, wfu, wfd)


def _group(x, mod3, k_past, v_past, c0, n0, m0, conv0, wts, *, bt, lt_in, lt, tq, n_past, lc):
    (g1, g2, gf, w1, wif_row, bif_row, wconv, bconv, gn,
     wgab, wa, wb, wout, wfg, wfu, wfd) = wts
    nb, seq, _ = x.shape
    (q, k32, v32, kb, vb, mq, mk, mv, mo, grow, conv_new) = _inproj(
        x, mod3, conv0, g1, w1, wif_row, bif_row, wconv, bconv, bt=bt, lt=lt_in)
    if k_past is None:
        k_past, v_past = kb, vb
    ya = _attn(q, kb, vb, k_past, v_past, bt=ATTN_SEQS, tq=tq, n_past=n_past)
    m0b = jnp.broadcast_to(jnp.pad(m0, ((0, 0), (0, SUBLANES - ML_HEADS)))[:, :, None], (nb, SUBLANES, LANES))
    yb, c1, n1, m1 = _mlstm(mq, mk, mv, mo, grow, c0, n0, m0b, gn, bt=ML_SEQS, lc=lc)
    y = _post(x, ya, yb, mod3, g1, g2, gf, wgab, wa, wb, wout, wfg, wfu, wfd, bt=bt, lt=lt)
    def rows_out(t):
        if _tokens_on_lanes(lt_in):
            return jnp.transpose(t.reshape(nb, SB_HEADS, SB_HEAD_DIM, seq), (0, 3, 1, 2))[None]
        return t.reshape(1, nb, seq, SB_HEADS, SB_HEAD_DIM)

    states = (rows_out(k32), rows_out(v32),
              c1[None], n1[None], m1[None, :, :ML_HEADS, 0], conv_new[None])
    return y, states


def kernel(x_prompt, x_sample, c_prompt, c_sample, cache_sb_k, cache_sb_v, state_mlstm_C, state_mlstm_n, state_mlstm_m, state_conv, norm1_g, norm2_g, w_ada, b_ada, w_in, b_if, w_conv, b_conv, ml_norm_g, w_a, w_b, w_out, w_ff_gate, w_ff_up, w_ff_down, final_g):
    assert w_in.shape[0] == 1, "single layer"
    bp, seq_p, _ = x_prompt.shape
    bs, seq_s, _ = x_sample.shape
    past = cache_sb_k.shape[2]

    c_all = jnp.concatenate([c_prompt, c_sample], axis=0)
    mod3 = _ada(c_all, w_ada[0], b_ada[0]).reshape(bp + bs, 6, D_MODEL)

    w1, wgab, wif_row, wa, wb, wout, wfg, wfu, wfd = _prep(
        jnp.transpose(w_in[0]), w_a[0], w_b[0], w_out[0], w_ff_gate[0], w_ff_up[0], w_ff_down[0])
    pad = SUBLANES - ML_HEADS
    bif_row = jnp.pad(b_if[0].reshape(2, ML_HEADS), ((0, 0), (0, pad))).reshape(2 * SUBLANES, 1)
    wts = (norm1_g[0].reshape(1, D_MODEL), norm2_g[0].reshape(1, D_MODEL), final_g.reshape(1, D_MODEL),
           w1, wif_row, bif_row, w_conv[0], b_conv[0].reshape(1, 2 * ML_WIDTH),
           ml_norm_g[0].reshape(1, ML_WIDTH),
           wgab, wa, wb, wout, wfg, wfu, wfd)

    zeros = functools.partial(jnp.zeros, dtype=F32)
    y_p, st_p = _group(
        x_prompt, mod3[:bp], None, None,
        zeros((bp, ML_HEADS, ML_HEAD_DIM, ML_HEAD_DIM)), zeros((bp, ML_HEADS, ML_HEAD_DIM)),
        zeros((bp, ML_HEADS)), zeros((bp, CONV_W - 1, 2 * ML_WIDTH)), wts,
        bt=1, lt_in=512, lt=256, tq=ATTN_BLOCK, n_past=None, lc=ML_CHUNK)
    def cache_t(c):
        return jnp.transpose(c[0], (0, 2, 3, 1)).reshape(bs, SB_WIDTH, past)

    y_s, st_s = _group(
        x_sample, mod3[bp:], cache_t(cache_sb_k), cache_t(cache_sb_v),
        state_mlstm_C[0], state_mlstm_n[0], state_mlstm_m[0], state_conv[0], wts,
        bt=4, lt_in=seq_s, lt=seq_s, tq=seq_s, n_past=past // ATTN_BLOCK, lc=seq_s)
    return (y_p, y_s) + st_p + st_s
```

```python
import functools
import math

import jax
import jax.numpy as jnp
from jax import lax
from jax.experimental import pallas as pl
from jax.experimental.pallas import tpu as pltpu

D_MODEL = 1024
SB_HEADS = 8
SB_HEAD_DIM = 64
SB_WIDTH = SB_HEADS * SB_HEAD_DIM
ML_HEADS = 4
ML_HEAD_DIM = 128
ML_WIDTH = ML_HEADS * ML_HEAD_DIM
CONV_W = 4
D_FF = 2816
EPS = 1e-6
LANES = 128
SUBLANES = 8
ATTN_BLOCK = 128
ML_CHUNK = 128
ML_SEQS = 8
STABILISER_MARGIN = 2.0 ** -7
EXP_UNDERFLOW = 88.0
SOFTPLUS_CLAMP = 60.0
ATTN_SEQS = 2
ATTN_GROUPS = 4
V7X_VMEM_BYTES = 64 * 1024 * 1024
VMEM_LIMIT = V7X_VMEM_BYTES - 8 * 1024 * 1024

BF16 = jnp.bfloat16
F32 = jnp.float32

_C_Q, _C_K, _C_V = 0, SB_WIDTH, 2 * SB_WIDTH
_C_MQK = 3 * SB_WIDTH
_C_MV = _C_MQK + 2 * ML_WIDTH
_C_MO = _C_MV + ML_WIDTH
_C_IF = _C_MO + ML_WIDTH
_C_GAB = _C_IF + 2 * ML_HEADS
_C_END = _C_GAB + 2 * D_MODEL


def _sigmoid(x):
    return 1.0 / (1.0 + jnp.exp(-x))


def _log_sigmoid(x):
    return jnp.minimum(x, 0.0) - jnp.log1p(jnp.exp(-jnp.abs(x)))


def _softplus_pos(x):
    return jnp.maximum(jnp.log(1.0 + jnp.exp(jnp.minimum(x, SOFTPLUS_CLAMP))), x)


def _dot(a, b):
    return jnp.dot(a, b, preferred_element_type=F32)


def _dot_nt(a, b):
    return lax.dot_general(a, b, (((1,), (1,)), ((), ())), preferred_element_type=F32)


def _split_dot(x, tri, parts):
    acc = None
    rem = x
    for i in range(parts):
        piece = rem.astype(BF16)
        term = _dot(piece, tri)
        acc = term if acc is None else acc + term
        if i + 1 < parts:
            rem = rem - piece.astype(F32)
    return acc


def _norm_mod(x, g, sc, sh):
    ms = jnp.mean(x * x, axis=-1, keepdims=True)
    return (x * lax.rsqrt(ms + EPS)) * (g * (1.0 + sc)) + sh


def _const_spec(shape):
    nd = len(shape)
    return pl.BlockSpec(shape, lambda *_: (0,) * nd, pipeline_mode=pl.Buffered(1))


def _params(n_axes):
    return pltpu.CompilerParams(dimension_semantics=("arbitrary",) * n_axes,
                                vmem_limit_bytes=VMEM_LIMIT)


def _ada_kernel(c_ref, w_ref, b_ref, o_ref):
    c = c_ref[...]
    s = c * _sigmoid(c)
    o_ref[...] = _dot(s.astype(BF16), w_ref[...].astype(BF16)) + b_ref[...]


def _ada(c_all, w_ada, b_ada):
    nb = c_all.shape[0]
    n_out = w_ada.shape[1]
    tn = 2 * D_MODEL
    return pl.pallas_call(
        _ada_kernel,
        grid=(n_out // tn,),
        in_specs=[pl.BlockSpec((nb, D_MODEL), lambda j: (0, 0)),
                  pl.BlockSpec((D_MODEL, tn), lambda j: (0, j)),
                  pl.BlockSpec((1, tn), lambda j: (0, j))],
        out_specs=pl.BlockSpec((nb, tn), lambda j: (0, j)),
        out_shape=jax.ShapeDtypeStruct((nb, n_out), F32),
        compiler_params=_params(1),
        name="ada",
    )(c_all, w_ada, b_ada.reshape(1, n_out))


PREP_STEPS = 8
W1_BLOCK = 512


def _prep_kernel(w1_ref, wgab_ref, wif_ref, wa_ref, wb_ref, wout_ref, wfg_ref, wfu_ref, wfd_ref,
                 w1_o, wgab_o, wif_o, wa_o, wb_o, wout_o, wfg_o, wfu_o, wfd_o):
    w1_o[...] = w1_ref[...].T.astype(BF16)
    for src, dst in ((wgab_ref, wgab_o), (wa_ref, wa_o), (wb_ref, wb_o), (wout_ref, wout_o),
                     (wfg_ref, wfg_o), (wfu_ref, wfu_o), (wfd_ref, wfd_o)):
        dst[...] = src[...].astype(BF16)
    gates = wif_ref[...]
    pad = jnp.zeros((SUBLANES - ML_HEADS, D_MODEL), F32)
    wif_o[...] = jnp.concatenate([gates[:ML_HEADS], pad, gates[ML_HEADS:], pad], axis=0).astype(BF16)


def _prep(w_in_t, w_a, w_b, w_out, w_fg, w_fu, w_fd):
    def rows(a, n_rows=None, first=0):
        n_rows = a.shape[0] if n_rows is None else n_rows
        step = n_rows // PREP_STEPS
        assert n_rows % (PREP_STEPS * 2 * SUBLANES) == 0 and first % SUBLANES == 0
        if first == 0:
            return pl.BlockSpec((step, a.shape[1]), lambda i: (i, 0))
        return pl.BlockSpec((pl.Element(step), pl.Element(a.shape[1])),
                            lambda i: (pl.multiple_of(first + i * step, SUBLANES), 0))

    def out(n_rows, n_cols):
        return (jax.ShapeDtypeStruct((n_rows, n_cols), BF16),
                pl.BlockSpec((n_rows // PREP_STEPS, n_cols), lambda i: (i, 0)))

    last = _C_IF // W1_BLOCK - 1
    assert _C_IF % W1_BLOCK == 0 and last < PREP_STEPS
    w1_in = pl.BlockSpec((W1_BLOCK, D_MODEL), lambda i: (jnp.minimum(i, last), 0))
    w1_out = (jax.ShapeDtypeStruct((D_MODEL, _C_IF), BF16),
              pl.BlockSpec((D_MODEL, W1_BLOCK), lambda i: (0, jnp.minimum(i, last))))

    others = (w_a, w_b, w_out, w_fg, w_fu, w_fd)
    outs = [w1_out, out(_C_END - _C_GAB, D_MODEL),
            (jax.ShapeDtypeStruct((2 * SUBLANES, D_MODEL), BF16),
             pl.BlockSpec((2 * SUBLANES, D_MODEL), lambda i: (0, 0)))] + [out(*a.shape) for a in others]
    in_specs = [w1_in, rows(w_in_t, _C_END - _C_GAB, _C_GAB),
                pl.BlockSpec((pl.Element(2 * ML_HEADS), pl.Element(D_MODEL)), lambda i: (_C_IF, 0))
                ] + [rows(a) for a in others]
    return pl.pallas_call(
        _prep_kernel,
        grid=(PREP_STEPS,),
        in_specs=in_specs,
        out_specs=[o[1] for o in outs],
        out_shape=[o[0] for o in outs],
        compiler_params=_params(1),
        name="prep",
    )(w_in_t, w_in_t, w_in_t, *others)


def _tokens_on_lanes(tile_len):
    return tile_len % LANES == 0


def _inproj_kernel(x_ref, mod_ref, conv0_ref, g1_ref, w1_ref, wifr_ref, bifr_ref,
                   wconv_ref, bconv_ref,
                   q_ref, k32_ref, v32_ref, kb_ref, vb_ref, mq_ref, mk_ref, mv_ref, mo_ref,
                   grow_ref, convn_ref,
                   u_sc, xp_sc, *, bt, lt):
    @pl.when(pl.program_id(1) == 0)
    def _():
        for j in range(bt):
            xp_sc[j] = jnp.concatenate(
                [jnp.zeros((SUBLANES - (CONV_W - 1), 2 * ML_WIDTH), F32), conv0_ref[j]], axis=0)

    g1 = g1_ref[...]
    for j in range(bt):
        u = _norm_mod(x_ref[j], g1, mod_ref[j, 1:2, :], mod_ref[j, 0:1, :])
        u_sc[j * lt:(j + 1) * lt, :] = u.astype(BF16)
    u = u_sc[...]

    def mm(lo, hi):
        return _dot(u, w1_ref[:, lo:hi])

    def new_rows(t):
        return t.T if _tokens_on_lanes(lt) else t

    wconv = wconv_ref[...]
    bconv = bconv_ref[...]
    k_scale = 1.0 / math.sqrt(ML_HEAD_DIM)
    mqk = mm(_C_MQK, _C_MV)
    row8 = lax.broadcasted_iota(jnp.int32, (SUBLANES, 2 * ML_WIDTH), 0)
    for j in range(bt):
        x = mqk[j * lt:(j + 1) * lt]
        before = xp_sc[j]
        conv = bconv + x * wconv[CONV_W - 1:CONV_W, :]
        for k in range(1, CONV_W):
            xr = pltpu.roll(x, k, axis=0)
            head = jnp.where(row8 < k, pltpu.roll(before, k, axis=0), xr[:SUBLANES])
            xk = jnp.concatenate([head, xr[SUBLANES:]], axis=0)
            conv = conv + xk * wconv[CONV_W - 1 - k:CONV_W - k, :]
        conv = conv * _sigmoid(conv)
        mq_ref[j] = conv[:, :ML_WIDTH].astype(BF16)
        mk_ref[j] = (conv[:, ML_WIDTH:] * k_scale).astype(BF16)
        convn_ref[j] = x[lt - (CONV_W - 1):]
        xp_sc[j] = x[lt - SUBLANES:]

    sq = mm(_C_Q, _C_K)
    sk = mm(_C_K, _C_V)
    sv = mm(_C_V, _C_MQK)
    mv = mm(_C_MV, _C_MO)
    mo = mm(_C_MO, _C_IF)
    for j in range(bt):
        rows = slice(j * lt, (j + 1) * lt)
        q_ref[j] = sq[rows].astype(BF16)
        k32_ref[j] = new_rows(sk[rows])
        kb_ref[j] = sk[rows].astype(BF16)
        v32_ref[j] = new_rows(sv[rows])
        vb_ref[j] = sv[rows].astype(BF16)
        mv_ref[j] = mv[rows].astype(BF16)
    for j in range(bt):
        rows = slice(j * lt, (j + 1) * lt)
        mo_ref[j] = mo[rows]
        gr = _dot_nt(wifr_ref[...], u_sc[rows, :]) + bifr_ref[...]
        row = lax.broadcasted_iota(jnp.int32, gr.shape, 0)
        grow_ref[j] = jnp.where(row >= SUBLANES, _log_sigmoid(gr), gr)


def _inproj(x, mod3, conv0, g1, w1, wif_row, bif_row, wconv, bconv, *, bt, lt):
    nb, seq, _ = x.shape
    assert nb % bt == 0 and seq % lt == 0 and (bt == 1 or lt == seq)
    grid = (nb // bt, seq // lt)

    def tok(width):
        return pl.BlockSpec((bt, lt, width), lambda b, l: (b, l, 0))

    def per_batch(rows, width):
        return pl.BlockSpec((bt, rows, width), lambda b, l: (b, 0, 0))

    in_specs = [tok(D_MODEL), per_batch(6, D_MODEL), per_batch(CONV_W - 1, 2 * ML_WIDTH),
                _const_spec((1, D_MODEL)), _const_spec(w1.shape),
                _const_spec((2 * SUBLANES, D_MODEL)),
                _const_spec((2 * SUBLANES, 1)),
                _const_spec((CONV_W, 2 * ML_WIDTH)), _const_spec((1, 2 * ML_WIDTH))]

    def tok_on_lanes(rows):
        return pl.BlockSpec((bt, rows, lt), lambda b, l: (b, 0, l))

    kv_spec = tok_on_lanes(SB_WIDTH) if _tokens_on_lanes(lt) else tok(SB_WIDTH)
    kv_shape = (nb, SB_WIDTH, seq) if _tokens_on_lanes(lt) else (nb, seq, SB_WIDTH)
    out_specs = [tok(SB_WIDTH), kv_spec, kv_spec, tok(SB_WIDTH), tok(SB_WIDTH)] + [tok(ML_WIDTH)] * 4 + [
        tok_on_lanes(2 * SUBLANES), per_batch(CONV_W - 1, 2 * ML_WIDTH)]

    def sds(shape, dt):
        return jax.ShapeDtypeStruct(shape, dt)

    out_shape = [sds((nb, seq, SB_WIDTH), BF16), sds(kv_shape, F32), sds(kv_shape, F32),
                 sds((nb, seq, SB_WIDTH), BF16), sds((nb, seq, SB_WIDTH), BF16),
                 sds((nb, seq, ML_WIDTH), BF16), sds((nb, seq, ML_WIDTH), BF16), sds((nb, seq, ML_WIDTH), BF16),
                 sds((nb, seq, ML_WIDTH), F32),
                 sds((nb, 2 * SUBLANES, seq), F32),
                 sds((nb, CONV_W - 1, 2 * ML_WIDTH), F32)]
    return pl.pallas_call(
        functools.partial(_inproj_kernel, bt=bt, lt=lt),
        grid=grid,
        in_specs=in_specs,
        out_specs=out_specs,
        out_shape=out_shape,
        scratch_shapes=[pltpu.VMEM((bt * lt, D_MODEL), BF16),
                        pltpu.VMEM((bt, SUBLANES, 2 * ML_WIDTH), F32)],
        compiler_params=_params(2),
        name="inproj",
    )(x, mod3, conv0, g1, w1, wif_row, bif_row, wconv, bconv)


def _cum_matrix(t):
    r = lax.broadcasted_iota(jnp.int32, (2 * t, LANES + t), 0)
    c = lax.broadcasted_iota(jnp.int32, (2 * t, LANES + t), 1)
    return jnp.where((c < LANES) | ((r & (t - 1)) > (c - LANES)), 1.0, 0.0).astype(BF16)


def _attn_kernel(q_ref, kd_ref, vd_ref, kp_ref, vp_ref, o_ref, qm_sc, acc_sc, carry_sc, *, bt, tq, tk, n_past):
    cache_past = n_past is not None
    n_blocks = n_past if cache_past else pl.program_id(1)
    scale = 1.0 / math.sqrt(SB_HEAD_DIM)
    n_pairs = SB_WIDTH // LANES
    lane = lax.broadcasted_iota(jnp.int32, (1, LANES), 1)
    first_head = lane < SB_HEAD_DIM

    n_units = bt * n_pairs
    for u in range(n_units):
        s, p = divmod(u, n_pairs)
        qp = q_ref[s, :, p * LANES:(p + 1) * LANES] * scale
        zero = jnp.zeros_like(qp)
        qm_sc[(2 * u) * tq:(2 * u + 1) * tq, :] = jnp.where(first_head, qp, zero)
        qm_sc[(2 * u + 1) * tq:(2 * u + 2) * tq, :] = jnp.where(first_head, zero, qp)

    per_group = n_units // ATTN_GROUPS
    groups = [range(g, g + per_group) for g in range(0, n_units, per_group)]
    group_rows = [slice(2 * grp[0] * tq, 2 * (grp[-1] + 1) * tq) for grp in groups]

    def scores(k_of, transposed):
        dot_k = _dot if transposed else _dot_nt
        return [jnp.concatenate([dot_k(qm_sc[2 * u * tq:(2 * u + 2) * tq, :], k_of(u)) for u in grp], axis=0)
                for grp in groups]

    def weights(zs, t, visible):
        cum = _cum_matrix(t)
        sps, sums = [], []
        for gi, rows in enumerate(group_rows):
            sp = _softplus_pos(zs[gi])
            if visible is not None:
                sp = jnp.where(visible[rows], sp, 0.0)
            hi = sp.astype(BF16)
            lo = (sp - hi.astype(F32)).astype(BF16)
            sps.append(sp)
            sums.append(_dot(jnp.concatenate([hi, lo], axis=1), cum))
        out = []
        for gi, rows in enumerate(group_rows):
            logit = zs[gi] - sps[gi] - sums[gi][:, LANES:]
            if visible is None:
                carry = carry_sc[rows, :]
                a = jnp.exp(logit - carry)
                carry_sc[rows, :] = carry + sums[gi][:, :LANES]
            else:
                a = jnp.where(visible[rows], jnp.exp(logit), 0.0)
                carry_sc[rows, :] = sums[gi][:, :LANES]
            out.append(a.astype(BF16))
        return out

    def accumulate(a_groups, v_of, transposed, assign=False):
        dot_v = _dot_nt if transposed else _dot
        for gi, grp in enumerate(groups):
            for j, u in enumerate(grp):
                s, p = divmod(u, n_pairs)
                cols = slice(p * LANES, (p + 1) * LANES)
                pv = dot_v(a_groups[gi][2 * j * tq:(2 * j + 2) * tq, :], v_of(u))
                out = jnp.where(first_head, pv[:tq], pv[tq:])
                if assign:
                    acc_sc[s, :, cols] = out
                else:
                    acc_sc[s, :, cols] += out

    def unit_cols(ref):
        def of(u):
            s, p = divmod(u, n_pairs)
            return ref[s, :, p * LANES:(p + 1) * LANES]
        return of

    def past_of(ref, block_index):
        start = pl.multiple_of(block_index * tk, tk)

        def of(u):
            s, p = divmod(u, n_pairs)
            if cache_past:
                return ref[s, p * LANES:(p + 1) * LANES, pl.ds(start, tk)].astype(BF16)
            return ref[s, pl.ds(start, tk), p * LANES:(p + 1) * LANES].astype(BF16)
        return of

    rd = lax.broadcasted_iota(jnp.int32, (2 * n_units * tq, tq), 0) & (tq - 1)
    cd = lax.broadcasted_iota(jnp.int32, (2 * n_units * tq, tq), 1)
    a_diag = weights(scores(unit_cols(kd_ref), False), tq, cd < rd)
    accumulate(a_diag, unit_cols(vd_ref), False, assign=True)

    def not_done():
        return jnp.min(carry_sc[...]) <= EXP_UNDERFLOW

    def past_block(i):
        block_index = n_blocks - 1 - i
        a = weights(scores(past_of(kp_ref, block_index), cache_past), tk, None)
        accumulate(a, past_of(vp_ref, block_index), cache_past)

    def pair_cond(state):
        i, go = state
        return jnp.logical_and(i + 1 < n_blocks, go)

    def pair_body(state):
        i, _ = state
        past_block(i)
        past_block(i + 1)
        return i + 2, not_done()

    def cond(state):
        i, go = state
        return jnp.logical_and(i < n_blocks, go)

    def body(state):
        i, _ = state
        past_block(i)
        return i + 1, not_done()

    state = lax.while_loop(pair_cond, pair_body, (jnp.int32(0), not_done()))
    lax.while_loop(cond, body, state)
    o_ref[...] = acc_sc[...].astype(BF16)


def _attn(q, k_new, v_new, k_past, v_past, *, bt, tq, n_past):
    nb, seq, _ = q.shape
    assert nb % bt == 0
    tok = pl.BlockSpec((bt, tq, SB_WIDTH), lambda b, i: (b, i, 0))
    past = pl.BlockSpec((bt,) + k_past.shape[1:], lambda b, i: (b, 0, 0))
    rows = bt * SB_HEADS * tq
    return pl.pallas_call(
        functools.partial(_attn_kernel, bt=bt, tq=tq, tk=ATTN_BLOCK, n_past=n_past),
        grid=(nb // bt, seq // tq),
        in_specs=[tok, tok, tok, past, past],
        out_specs=tok,
        out_shape=jax.ShapeDtypeStruct((nb, seq, SB_WIDTH), BF16),
        scratch_shapes=[pltpu.VMEM((rows, LANES), BF16),
                        pltpu.VMEM((bt, tq, SB_WIDTH), F32),
                        pltpu.VMEM((rows, LANES), F32)],
        compiler_params=_params(2),
        name="attn",
    )(q, k_new, v_new, k_past, v_past)


def _mlstm_kernel(mq_ref, mk_ref, mv_ref, mo_ref, grow_ref, c0_ref, n0_ref, m0_ref, gn_ref,
                  yb_ref, c_out_ref, n_out_ref, m_out_ref,
                  c_sc, n_sc, m_sc, alpha_sc, beta_sc, r_sc, wg_sc, dec_sc, mprev_sc, *, bt, lc, nc):
    step = pl.program_id(1)
    dk = ML_HEAD_DIM

    ri = lax.broadcasted_iota(jnp.int32, (lc, lc), 0)
    ci = lax.broadcasted_iota(jnp.int32, (lc, lc), 1)
    causal = ci <= ri
    eye = jnp.where(ri == ci, 1.0, 0.0).astype(BF16)
    ones_blk = jnp.ones((lc, dk), BF16)
    zeros_blk = jnp.zeros((lc, dk), BF16)

    @pl.when(step == 0)
    def _():
        c_sc[...] = c0_ref[...]
        n_sc[...] = n0_ref[...]
        incl_t = jnp.where(ri <= ci, 1.0, 0.0).astype(BF16)
        blocks = [(s, c) for s in range(bt) for c in range(nc)]
        li = jnp.concatenate([grow_ref[s, 0:SUBLANES, c * lc:(c + 1) * lc] for s, c in blocks], axis=0)
        lf = jnp.concatenate([grow_ref[s, SUBLANES:2 * SUBLANES, c * lc:(c + 1) * lc] for s, c in blocks],
                             axis=0)
        b = _split_dot(lf, incl_t, 3)
        r = li - b
        lane = lax.broadcasted_iota(jnp.int32, r.shape, 1)
        run = r
        d = 1
        while d < lc:
            run = jnp.maximum(run, jnp.where(lane >= d, pltpu.roll(run, d, axis=1), -jnp.inf))
            d *= 2
        b_last = b[:, lc - 1:lc]
        g = b_last - b + li
        g_max = jnp.max(g, axis=1, keepdims=True)
        m_before = []
        m_after = []
        for s in range(bt):
            m = m0_ref[s]
            for c in range(nc):
                rows = slice((s * nc + c) * SUBLANES, (s * nc + c + 1) * SUBLANES)
                m_before.append(m)
                m = jnp.maximum(b_last[rows] + m, g_max[rows])
                m_after.append(m)
            m_sc[s] = m
        m_prev = jnp.concatenate(m_before, axis=0)
        m_new = jnp.concatenate(m_after, axis=0)
        top = jnp.maximum(run, m_prev[:, :lc])
        alpha = (-(top + jnp.abs(top) * STABILISER_MARGIN)).astype(BF16).astype(F32)
        alpha_sc[...] = alpha
        beta_sc[...] = alpha - b
        r_sc[...] = r
        wg_sc[...] = jnp.exp(g - m_new[:, :lc])
        dec_sc[...] = jnp.exp(b_last + m_prev - m_new)
        mprev_sc[...] = m_prev

    units = [(s, h) for s in range(bt) for h in range(ML_HEADS)]
    cols = [slice(h * dk, (h + 1) * dk) for h in range(ML_HEADS)]
    alpha, beta_hi, beta_lo, r, wg, decay, m_prev = [], [], [], [], [], [], []
    for s in range(bt):
        chunk = pl.ds(pl.multiple_of((s * nc + step) * SUBLANES, SUBLANES), SUBLANES)
        alpha.append(alpha_sc[chunk, :].astype(BF16))
        beta = beta_sc[chunk, :]
        beta_hi.append(beta.astype(BF16))
        beta_lo.append((beta - beta_hi[s].astype(F32)).astype(BF16))
        r.append(r_sc[chunk, :])
        wg.append(wg_sc[chunk, :])
        decay.append(dec_sc[chunk, :])
        m_prev.append(mprev_sc[chunk, :])
    eye2 = jnp.concatenate([eye, eye], axis=1)

    def rows_of(x, h):
        return jnp.broadcast_to(x[h:h + 1, :], (dk, lc))
    alpha_b = _dot_nt(eye, jnp.concatenate([rows_of(alpha[s], h) for s, h in units], axis=0))
    beta_b = _dot_nt(eye2, jnp.concatenate(
        [jnp.concatenate([rows_of(beta_hi[s], h), rows_of(beta_lo[s], h)], axis=1) for s, h in units], axis=0))
    qk = [_dot_nt(mq_ref[s, :, cols[h]], mk_ref[s, :, cols[h]]) for s, h in units]
    c_prev = [c_sc[s, h] for s, h in units]
    n_prev = [n_sc[s, h:h + 1, :] for s, h in units]
    qc = [_dot_nt(mq_ref[s, :, cols[h]],
                  jnp.concatenate([c_prev[u].astype(BF16),
                                   jnp.broadcast_to(n_prev[u], (dk, dk)).astype(BF16)], axis=0))
          for u, (s, h) in enumerate(units)]
    upd = []
    for s, h in units:
        wg_h = wg[s][h:h + 1, :]
        vt = mv_ref[s, :, cols[h]].astype(F32).T
        lhs = jnp.concatenate([(vt * wg_h).astype(BF16),
                               jnp.broadcast_to(wg_h, (SUBLANES, lc)).astype(BF16)], axis=0)
        upd.append(_dot(lhs, mk_ref[s, :, cols[h]]))

    cat = []
    for u, (s, h) in enumerate(units):
        w = jnp.where(causal, jnp.exp(alpha_b[:, u * dk:u * dk + lc] + r[s][h:h + 1, :]), 0.0)
        sw = qk[u] * w
        s_hi = sw.astype(BF16)
        cat.append(jnp.concatenate([s_hi, (sw - s_hi.astype(F32)).astype(BF16)], axis=1))
    nv = []
    for u, (s, h) in enumerate(units):
        rhs = jnp.concatenate([jnp.concatenate([mv_ref[s, :, cols[h]], ones_blk], axis=1),
                               jnp.concatenate([zeros_blk, ones_blk], axis=1)], axis=0)
        nv.append(_dot(cat[u], rhs))
    hval = []
    cat2 = []
    for u, (s, h) in enumerate(units):
        ucols = slice(u * dk, (u + 1) * dk)
        w_inter = jnp.exp(m_prev[s][h:h + 1, :] + alpha_b[:, ucols])
        num = nv[u][:, :dk] + w_inter * qc[u][:, :dk]
        den = nv[u][:, dk:] + w_inter * qc[u][:, dk:]
        hv = num / jnp.maximum(jnp.abs(den), jnp.exp(beta_b[:, ucols]))
        h2 = hv * hv
        h2_hi = h2.astype(BF16)
        hval.append(hv)
        cat2.append(jnp.concatenate([h2_hi, (h2 - h2_hi.astype(F32)).astype(BF16)], axis=1))
    ms = _dot(jnp.concatenate(cat2, axis=0), jnp.ones((2 * dk, dk), BF16)) * (1.0 / dk)
    for u, (s, h) in enumerate(units):
        y = hval[u] * lax.rsqrt(ms[u * lc:(u + 1) * lc] + EPS) * gn_ref[:, cols[h]]
        yb_ref[s, :, cols[h]] = (_sigmoid(mo_ref[s, :, cols[h]]) * y).astype(BF16)

    for u, (s, h) in enumerate(units):
        dec = decay[s][h:h + 1, :]
        c_sc[s, h] = dec * c_prev[u] + upd[u][:dk]
        n_sc[s, h:h + 1, :] = dec * n_prev[u] + upd[u][dk:dk + 1]

    c_out_ref[...] = c_sc[...]
    n_out_ref[...] = n_sc[...]
    m_out_ref[...] = m_sc[...]


def _mlstm(mq, mk, mv, mo, grow, c0, n0, m0, gn, *, bt, lc):
    nb, seq, _ = mq.shape
    assert nb % bt == 0
    tok = pl.BlockSpec((bt, lc, ML_WIDTH), lambda b, i: (b, i, 0))
    c_spec = pl.BlockSpec((bt, ML_HEADS, ML_HEAD_DIM, ML_HEAD_DIM), lambda b, i: (b, 0, 0, 0))
    n_spec = pl.BlockSpec((bt, ML_HEADS, ML_HEAD_DIM), lambda b, i: (b, 0, 0))
    m_spec = pl.BlockSpec((bt, SUBLANES, LANES), lambda b, i: (b, 0, 0))
    nc = seq // lc
    table = pltpu.VMEM((bt * nc * SUBLANES, lc), F32)
    table_m = pltpu.VMEM((bt * nc * SUBLANES, LANES), F32)
    return pl.pallas_call(
        functools.partial(_mlstm_kernel, bt=bt, lc=lc, nc=nc),
        grid=(nb // bt, nc),
        in_specs=[tok, tok, tok, tok,
                  pl.BlockSpec((bt, 2 * SUBLANES, seq), lambda b, i: (b, 0, 0)),
                  c_spec, n_spec, m_spec, _const_spec((1, ML_WIDTH))],
        out_specs=[tok, c_spec, n_spec, m_spec],
        out_shape=[jax.ShapeDtypeStruct((nb, seq, ML_WIDTH), BF16),
                   jax.ShapeDtypeStruct((nb, ML_HEADS, ML_HEAD_DIM, ML_HEAD_DIM), F32),
                   jax.ShapeDtypeStruct((nb, ML_HEADS, ML_HEAD_DIM), F32),
                   jax.ShapeDtypeStruct((nb, SUBLANES, LANES), F32)],
        scratch_shapes=[pltpu.VMEM((bt, ML_HEADS, ML_HEAD_DIM, ML_HEAD_DIM), F32),
                        pltpu.VMEM((bt, ML_HEADS, ML_HEAD_DIM), F32),
                        pltpu.VMEM((bt, SUBLANES, LANES), F32),
                        table, table, table, table, table_m, table_m],
        compiler_params=_params(2),
        name="mlstm",
    )(mq, mk, mv, mo, grow, c0, n0, m0, gn)


def _post_kernel(x_ref, ya_ref, yb_ref, mod_ref, g1_ref, g2_ref, gf_ref,
                 wgab_ref, wa_ref, wb_ref, wout_ref, wfg_ref, wfu_ref, wfd_ref,
                 y_ref, u_sc, *, bt, lt):
    g1 = g1_ref[...]
    for j in range(bt):
        u = _norm_mod(x_ref[j], g1, mod_ref[j, 1:2, :], mod_ref[j, 0:1, :])
        u_sc[j * lt:(j + 1) * lt, :] = u.astype(BF16)
    u = u_sc[...]
    ya = ya_ref[...].reshape(bt * lt, SB_WIDTH)
    yb = yb_ref[...].reshape(bt * lt, ML_WIDTH)
    ga = _dot_nt(u, wgab_ref[:D_MODEL, :])
    gb = _dot_nt(u, wgab_ref[D_MODEL:, :])
    merged = _sigmoid(ga) * _dot(ya, wa_ref[...]) + _sigmoid(gb) * _dot(yb, wb_ref[...])
    attn_out = _dot(merged.astype(BF16), wout_ref[...])

    g2 = g2_ref[...]
    for j in range(bt):
        rows = slice(j * lt, (j + 1) * lt)
        x1 = x_ref[j] + mod_ref[j, 2:3, :] * attn_out[rows]
        y_ref[j] = x1
        u2 = _norm_mod(x1, g2, mod_ref[j, 4:5, :], mod_ref[j, 3:4, :])
        u_sc[rows, :] = u2.astype(BF16)
    u2 = u_sc[...]
    hg = _dot(u2, wfg_ref[...])
    hu = _dot(u2, wfu_ref[...])
    act = (hg * _sigmoid(hg) * hu).astype(BF16)
    ff = _dot(act, wfd_ref[...])
    gf = gf_ref[...]
    for j in range(bt):
        rows = slice(j * lt, (j + 1) * lt)
        x2 = y_ref[j] + mod_ref[j, 5:6, :] * ff[rows]
        ms = jnp.mean(x2 * x2, axis=-1, keepdims=True)
        y_ref[j] = x2 * lax.rsqrt(ms + EPS) * gf


def _post(x, ya, yb, mod3, g1, g2, gf, wgab, wa, wb, wout, wfg, wfu, wfd, *, bt, lt):
    nb, seq, _ = x.shape
    assert nb % bt == 0 and seq % lt == 0

    def tok(width):
        return pl.BlockSpec((bt, lt, width), lambda b, l: (b, l, 0))

    in_specs = [tok(D_MODEL), tok(SB_WIDTH), tok(ML_WIDTH),
                pl.BlockSpec((bt, 6, D_MODEL), lambda b, l: (b, 0, 0)),
                _const_spec((1, D_MODEL)), _const_spec((1, D_MODEL)), _const_spec((1, D_MODEL)),
                _const_spec(wgab.shape), _const_spec(wa.shape), _const_spec(wb.shape), _const_spec(wout.shape),
                _const_spec(wfg.shape), _const_spec(wfu.shape), _const_spec(wfd.shape)]
    return pl.pallas_call(
        functools.partial(_post_kernel, bt=bt, lt=lt),
        grid=(nb // bt, seq // lt),
        in_specs=in_specs,
        out_specs=tok(D_MODEL),
        out_shape=jax.ShapeDtypeStruct((nb, seq, D_MODEL), F32),
        scratch_shapes=[pltpu.VMEM((bt * lt, D_MODEL), BF16)],
        compiler_params=_params(2),
        name="post",
    )(x, ya, yb, mod3, g1, g2, gf, wgab, wa, wb, wout, wfg, wfu, wfd)


def _group(x, mod3, k_past, v_past, c0, n0, m0, conv0, wts, *, bt, lt_in, lt, tq, n_past, lc):
    (g1, g2, gf, w1, wif_row, bif_row, wconv, bconv, gn,
     wgab, wa, wb, wout, wfg, wfu, wfd) = wts
    nb, seq, _ = x.shape
    (q, k32, v32, kb, vb, mq, mk, mv, mo, grow, conv_new) = _inproj(
        x, mod3, conv0, g1, w1, wif_row, bif_row, wconv, bconv, bt=bt, lt=lt_in)
    if k_past is None:
        k_past, v_past = kb, vb
    ya = _attn(q, kb, vb, k_past, v_past, bt=ATTN_SEQS, tq=tq, n_past=n_past)
    m0b = jnp.broadcast_to(jnp.pad(m0, ((0, 0), (0, SUBLANES - ML_HEADS)))[:, :, None], (nb, SUBLANES, LANES))
    yb, c1, n1, m1 = _mlstm(mq, mk, mv, mo, grow, c0, n0, m0b, gn, bt=ML_SEQS, lc=lc)
    y = _post(x, ya, yb, mod3, g1, g2, gf, wgab, wa, wb, wout, wfg, wfu, wfd, bt=bt, lt=lt)
    def rows_out(t):
        if _tokens_on_lanes(lt_in):
            return jnp.transpose(t.reshape(nb, SB_HEADS, SB_HEAD_DIM, seq), (0, 3, 1, 2))[None]
        return t.reshape(1, nb, seq, SB_HEADS, SB_HEAD_DIM)

    states = (rows_out(k32), rows_out(v32),
              c1[None], n1[None], m1[None, :, :ML_HEADS, 0], conv_new[None])
    return y, states


def kernel(x_prompt, x_sample, c_prompt, c_sample, cache_sb_k, cache_sb_v, state_mlstm_C, state_mlstm_n, state_mlstm_m, state_conv, norm1_g, norm2_g, w_ada, b_ada, w_in, b_if, w_conv, b_conv, ml_norm_g, w_a, w_b, w_out, w_ff_gate, w_ff_up, w_ff_down, final_g):
    assert w_in.shape[0] == 1, "single layer"
    bp, seq_p, _ = x_prompt.shape
    bs, seq_s, _ = x_sample.shape
    past = cache_sb_k.shape[2]

    c_all = jnp.concatenate([c_prompt, c_sample], axis=0)
    mod3 = _ada(c_all, w_ada[0], b_ada[0]).reshape(bp + bs, 6, D_MODEL)

    w1, wgab, wif_row, wa, wb, wout, wfg, wfu, wfd = _prep(
        jnp.transpose(w_in[0]), w_a[0], w_b[0], w_out[0], w_ff_gate[0], w_ff_up[0], w_ff_down[0])
    pad = SUBLANES - ML_HEADS
    bif_row = jnp.pad(b_if[0].reshape(2, ML_HEADS), ((0, 0), (0, pad))).reshape(2 * SUBLANES, 1)
    wts = (norm1_g[0].reshape(1, D_MODEL), norm2_g[0].reshape(1, D_MODEL), final_g.reshape(1, D_MODEL),
           w1, wif_row, bif_row, w_conv[0], b_conv[0].reshape(1, 2 * ML_WIDTH),
           ml_norm_g[0].reshape(1, ML_WIDTH),
           wgab, wa, wb, wout, wfg, wfu, wfd)

    zeros = functools.partial(jnp.zeros, dtype=F32)
    y_p, st_p = _group(
        x_prompt, mod3[:bp], None, None,
        zeros((bp, ML_HEADS, ML_HEAD_DIM, ML_HEAD_DIM)), zeros((bp, ML_HEADS, ML_HEAD_DIM)),
        zeros((bp, ML_HEADS)), zeros((bp, CONV_W - 1, 2 * ML_WIDTH)), wts,
        bt=1, lt_in=512, lt=256, tq=ATTN_BLOCK, n_past=None, lc=ML_CHUNK)
    def cache_t(c):
        return jnp.transpose(c[0], (0, 2, 3, 1)).reshape(bs, SB_WIDTH, past)

    y_s, st_s = _group(
        x_sample, mod3[bp:], cache_t(cache_sb_k), cache_t(cache_sb_v),
        state_mlstm_C[0], state_mlstm_n[0], state_mlstm_m[0], state_conv[0], wts,
        bt=4, lt_in=seq_s, lt=seq_s, tq=seq_s, n_past=past // ATTN_BLOCK, lc=seq_s)
    return (y_p, y_s) + st_p + st_s
```

```python
import functools
import math

import jax
import jax.numpy as jnp
from jax import lax
from jax.experimental import pallas as pl
from jax.experimental.pallas import tpu as pltpu

D_MODEL = 1024
SB_HEADS = 8
SB_HEAD_DIM = 64
SB_WIDTH = SB_HEADS * SB_HEAD_DIM
ML_HEADS = 4
ML_HEAD_DIM = 128
ML_WIDTH = ML_HEADS * ML_HEAD_DIM
CONV_W = 4
D_FF = 2816
EPS = 1e-6
LANES = 128
SUBLANES = 8
ATTN_BLOCK = 128
ML_CHUNK = 128
ML_SEQS = 8
STABILISER_MARGIN = 2.0 ** -7
EXP_UNDERFLOW = 88.0
SOFTPLUS_CLAMP = 60.0
ATTN_ROWS = 256
ATTN_GROUPS = 4
V7X_VMEM_BYTES = 64 * 1024 * 1024
VMEM_LIMIT = V7X_VMEM_BYTES - 8 * 1024 * 1024

BF16 = jnp.bfloat16
F32 = jnp.float32

_C_Q, _C_K, _C_V = 0, SB_WIDTH, 2 * SB_WIDTH
_C_MQK = 3 * SB_WIDTH
_C_MV = _C_MQK + 2 * ML_WIDTH
_C_MO = _C_MV + ML_WIDTH
_C_IF = _C_MO + ML_WIDTH
_C_GAB = _C_IF + 2 * ML_HEADS
_C_END = _C_GAB + 2 * D_MODEL


def _sigmoid(x):
    return 1.0 / (1.0 + jnp.exp(-x))


def _log_sigmoid(x):
    return jnp.minimum(x, 0.0) - jnp.log1p(jnp.exp(-jnp.abs(x)))


def _softplus_pos(x):
    return jnp.maximum(jnp.log(1.0 + jnp.exp(jnp.minimum(x, SOFTPLUS_CLAMP))), x)


def _dot(a, b):
    return jnp.dot(a, b, preferred_element_type=F32)


def _dot_nt(a, b):
    return lax.dot_general(a, b, (((1,), (1,)), ((), ())), preferred_element_type=F32)


def _split_dot(x, tri, parts):
    acc = None
    rem = x
    for i in range(parts):
        piece = rem.astype(BF16)
        term = _dot(piece, tri)
        acc = term if acc is None else acc + term
        if i + 1 < parts:
            rem = rem - piece.astype(F32)
    return acc


def _norm_mod(x, g, sc, sh):
    ms = jnp.mean(x * x, axis=-1, keepdims=True)
    return (x * lax.rsqrt(ms + EPS)) * (g * (1.0 + sc)) + sh


def _const_spec(shape):
    nd = len(shape)
    return pl.BlockSpec(shape, lambda *_: (0,) * nd, pipeline_mode=pl.Buffered(1))


def _params(n_axes):
    return pltpu.CompilerParams(dimension_semantics=("arbitrary",) * n_axes,
                                vmem_limit_bytes=VMEM_LIMIT)


def _ada_kernel(c_ref, w_ref, b_ref, o_ref):
    c = c_ref[...]
    s = c * _sigmoid(c)
    o_ref[...] = _dot(s.astype(BF16), w_ref[...].astype(BF16)) + b_ref[...]


def _ada(c_all, w_ada, b_ada):
    nb = c_all.shape[0]
    n_out = w_ada.shape[1]
    tn = 2 * D_MODEL
    return pl.pallas_call(
        _ada_kernel,
        grid=(n_out // tn,),
        in_specs=[pl.BlockSpec((nb, D_MODEL), lambda j: (0, 0)),
                  pl.BlockSpec((D_MODEL, tn), lambda j: (0, j)),
                  pl.BlockSpec((1, tn), lambda j: (0, j))],
        out_specs=pl.BlockSpec((nb, tn), lambda j: (0, j)),
        out_shape=jax.ShapeDtypeStruct((nb, n_out), F32),
        compiler_params=_params(1),
        name="ada",
    )(c_all, w_ada, b_ada.reshape(1, n_out))


PREP_STEPS = 8
W1_BLOCK = 512


def _prep_kernel(w1_ref, wgab_ref, wif_ref, wa_ref, wb_ref, wout_ref, wfg_ref, wfu_ref, wfd_ref,
                 w1_o, wgab_o, wif_o, wa_o, wb_o, wout_o, wfg_o, wfu_o, wfd_o):
    w1_o[...] = w1_ref[...].T.astype(BF16)
    for src, dst in ((wgab_ref, wgab_o), (wa_ref, wa_o), (wb_ref, wb_o), (wout_ref, wout_o),
                     (wfg_ref, wfg_o), (wfu_ref, wfu_o), (wfd_ref, wfd_o)):
        dst[...] = src[...].astype(BF16)
    gates = wif_ref[...]
    pad = jnp.zeros((SUBLANES - ML_HEADS, D_MODEL), F32)
    wif_o[...] = jnp.concatenate([gates[:ML_HEADS], pad, gates[ML_HEADS:], pad], axis=0).astype(BF16)


def _prep(w_in_t, w_a, w_b, w_out, w_fg, w_fu, w_fd):
    def rows(a, n_rows=None, first=0):
        n_rows = a.shape[0] if n_rows is None else n_rows
        step = n_rows // PREP_STEPS
        assert n_rows % (PREP_STEPS * 2 * SUBLANES) == 0 and first % SUBLANES == 0
        if first == 0:
            return pl.BlockSpec((step, a.shape[1]), lambda i: (i, 0))
        return pl.BlockSpec((pl.Element(step), pl.Element(a.shape[1])),
                            lambda i: (pl.multiple_of(first + i * step, SUBLANES), 0))

    def out(n_rows, n_cols):
        return (jax.ShapeDtypeStruct((n_rows, n_cols), BF16),
                pl.BlockSpec((n_rows // PREP_STEPS, n_cols), lambda i: (i, 0)))

    last = _C_IF // W1_BLOCK - 1
    assert _C_IF % W1_BLOCK == 0 and last < PREP_STEPS
    w1_in = pl.BlockSpec((W1_BLOCK, D_MODEL), lambda i: (jnp.minimum(i, last), 0))
    w1_out = (jax.ShapeDtypeStruct((D_MODEL, _C_IF), BF16),
              pl.BlockSpec((D_MODEL, W1_BLOCK), lambda i: (0, jnp.minimum(i, last))))

    others = (w_a, w_b, w_out, w_fg, w_fu, w_fd)
    outs = [w1_out, out(_C_END - _C_GAB, D_MODEL),
            (jax.ShapeDtypeStruct((2 * SUBLANES, D_MODEL), BF16),
             pl.BlockSpec((2 * SUBLANES, D_MODEL), lambda i: (0, 0)))] + [out(*a.shape) for a in others]
    in_specs = [w1_in, rows(w_in_t, _C_END - _C_GAB, _C_GAB),
                pl.BlockSpec((pl.Element(2 * ML_HEADS), pl.Element(D_MODEL)), lambda i: (_C_IF, 0))
                ] + [rows(a) for a in others]
    return pl.pallas_call(
        _prep_kernel,
        grid=(PREP_STEPS,),
        in_specs=in_specs,
        out_specs=[o[1] for o in outs],
        out_shape=[o[0] for o in outs],
        compiler_params=_params(1),
        name="prep",
    )(w_in_t, w_in_t, w_in_t, *others)


def _tokens_on_lanes(tile_len):
    return tile_len % LANES == 0


def _inproj_kernel(x_ref, mod_ref, conv0_ref, g1_ref, w1_ref, wifr_ref, bifr_ref,
                   wconv_ref, bconv_ref,
                   q_ref, k32_ref, v32_ref, kb_ref, vb_ref, mq_ref, mk_ref, mv_ref, mo_ref,
                   grow_ref, convn_ref,
                   u_sc, xp_sc, *, bt, lt):
    @pl.when(pl.program_id(1) == 0)
    def _():
        for j in range(bt):
            xp_sc[j] = jnp.concatenate(
                [jnp.zeros((SUBLANES - (CONV_W - 1), 2 * ML_WIDTH), F32), conv0_ref[j]], axis=0)

    g1 = g1_ref[...]
    for j in range(bt):
        u = _norm_mod(x_ref[j], g1, mod_ref[j, 1:2, :], mod_ref[j, 0:1, :])
        u_sc[j * lt:(j + 1) * lt, :] = u.astype(BF16)
    u = u_sc[...]

    def mm(lo, hi):
        return _dot(u, w1_ref[:, lo:hi])

    def new_rows(t):
        return t.T if _tokens_on_lanes(lt) else t

    wconv = wconv_ref[...]
    bconv = bconv_ref[...]
    k_scale = 1.0 / math.sqrt(ML_HEAD_DIM)
    mqk = mm(_C_MQK, _C_MV)
    row8 = lax.broadcasted_iota(jnp.int32, (SUBLANES, 2 * ML_WIDTH), 0)
    for j in range(bt):
        x = mqk[j * lt:(j + 1) * lt]
        before = xp_sc[j]
        conv = bconv + x * wconv[CONV_W - 1:CONV_W, :]
        for k in range(1, CONV_W):
            xr = pltpu.roll(x, k, axis=0)
            head = jnp.where(row8 < k, pltpu.roll(before, k, axis=0), xr[:SUBLANES])
            xk = jnp.concatenate([head, xr[SUBLANES:]], axis=0)
            conv = conv + xk * wconv[CONV_W - 1 - k:CONV_W - k, :]
        conv = conv * _sigmoid(conv)
        mq_ref[j] = conv[:, :ML_WIDTH].astype(BF16)
        mk_ref[j] = (conv[:, ML_WIDTH:] * k_scale).astype(BF16)
        convn_ref[j] = x[lt - (CONV_W - 1):]
        xp_sc[j] = x[lt - SUBLANES:]

    sq = mm(_C_Q, _C_K)
    sk = mm(_C_K, _C_V)
    sv = mm(_C_V, _C_MQK)
    mv = mm(_C_MV, _C_MO)
    mo = mm(_C_MO, _C_IF)
    for j in range(bt):
        rows = slice(j * lt, (j + 1) * lt)
        q_ref[j] = sq[rows].astype(BF16)
        k32_ref[j] = new_rows(sk[rows])
        kb_ref[j] = sk[rows].astype(BF16)
        v32_ref[j] = new_rows(sv[rows])
        vb_ref[j] = sv[rows].astype(BF16)
        mv_ref[j] = mv[rows].astype(BF16)
    for j in range(bt):
        rows = slice(j * lt, (j + 1) * lt)
        mo_ref[j] = mo[rows]
        gr = _dot_nt(wifr_ref[...], u_sc[rows, :]) + bifr_ref[...]
        row = lax.broadcasted_iota(jnp.int32, gr.shape, 0)
        grow_ref[j] = jnp.where(row >= SUBLANES, _log_sigmoid(gr), gr)


def _inproj(x, mod3, conv0, g1, w1, wif_row, bif_row, wconv, bconv, *, bt, lt):
    nb, seq, _ = x.shape
    assert nb % bt == 0 and seq % lt == 0 and (bt == 1 or lt == seq)
    grid = (nb // bt, seq // lt)

    def tok(width):
        return pl.BlockSpec((bt, lt, width), lambda b, l: (b, l, 0))

    def per_batch(rows, width):
        return pl.BlockSpec((bt, rows, width), lambda b, l: (b, 0, 0))

    in_specs = [tok(D_MODEL), per_batch(6, D_MODEL), per_batch(CONV_W - 1, 2 * ML_WIDTH),
                _const_spec((1, D_MODEL)), _const_spec(w1.shape),
                _const_spec((2 * SUBLANES, D_MODEL)),
                _const_spec((2 * SUBLANES, 1)),
                _const_spec((CONV_W, 2 * ML_WIDTH)), _const_spec((1, 2 * ML_WIDTH))]

    def tok_on_lanes(rows):
        return pl.BlockSpec((bt, rows, lt), lambda b, l: (b, 0, l))

    kv_spec = tok_on_lanes(SB_WIDTH) if _tokens_on_lanes(lt) else tok(SB_WIDTH)
    kv_shape = (nb, SB_WIDTH, seq) if _tokens_on_lanes(lt) else (nb, seq, SB_WIDTH)
    out_specs = [tok(SB_WIDTH), kv_spec, kv_spec, tok(SB_WIDTH), tok(SB_WIDTH)] + [tok(ML_WIDTH)] * 4 + [
        tok_on_lanes(2 * SUBLANES), per_batch(CONV_W - 1, 2 * ML_WIDTH)]

    def sds(shape, dt):
        return jax.ShapeDtypeStruct(shape, dt)

    out_shape = [sds((nb, seq, SB_WIDTH), BF16), sds(kv_shape, F32), sds(kv_shape, F32),
                 sds((nb, seq, SB_WIDTH), BF16), sds((nb, seq, SB_WIDTH), BF16),
                 sds((nb, seq, ML_WIDTH), BF16), sds((nb, seq, ML_WIDTH), BF16), sds((nb, seq, ML_WIDTH), BF16),
                 sds((nb, seq, ML_WIDTH), F32),
                 sds((nb, 2 * SUBLANES, seq), F32),
                 sds((nb, CONV_W - 1, 2 * ML_WIDTH), F32)]
    return pl.pallas_call(
        functools.partial(_inproj_kernel, bt=bt, lt=lt),
        grid=grid,
        in_specs=in_specs,
        out_specs=out_specs,
        out_shape=out_shape,
        scratch_shapes=[pltpu.VMEM((bt * lt, D_MODEL), BF16),
                        pltpu.VMEM((bt, SUBLANES, 2 * ML_WIDTH), F32)],
        compiler_params=_params(2),
        name="inproj",
    )(x, mod3, conv0, g1, w1, wif_row, bif_row, wconv, bconv)


def _cum_matrix(t):
    r = lax.broadcasted_iota(jnp.int32, (2 * t, LANES + t), 0)
    c = lax.broadcasted_iota(jnp.int32, (2 * t, LANES + t), 1)
    return jnp.where((c < LANES) | ((r & (t - 1)) >= (c - LANES)), 1.0, 0.0).astype(BF16)


def _attn_kernel(q_ref, kd_ref, vd_ref, kp_ref, vp_ref, o_ref, qm_sc, acc_sc, carry_sc, *, bt, tq, tk, n_past):
    cache_past = n_past is not None
    n_blocks = n_past if cache_past else pl.program_id(1)
    scale = 1.0 / math.sqrt(SB_HEAD_DIM)
    n_pairs = SB_WIDTH // LANES
    lane = lax.broadcasted_iota(jnp.int32, (1, LANES), 1)
    first_head = lane < SB_HEAD_DIM

    n_units = bt * n_pairs
    for u in range(n_units):
        s, p = divmod(u, n_pairs)
        qp = q_ref[s, :, p * LANES:(p + 1) * LANES] * scale
        zero = jnp.zeros_like(qp)
        qm_sc[(2 * u) * tq:(2 * u + 1) * tq, :] = jnp.where(first_head, qp, zero)
        qm_sc[(2 * u + 1) * tq:(2 * u + 2) * tq, :] = jnp.where(first_head, zero, qp)

    per_group = n_units // ATTN_GROUPS
    groups = [range(g, g + per_group) for g in range(0, n_units, per_group)]
    group_rows = [slice(2 * grp[0] * tq, 2 * (grp[-1] + 1) * tq) for grp in groups]

    def scores(k_of, transposed):
        dot_k = _dot if transposed else _dot_nt
        return [jnp.concatenate([dot_k(qm_sc[2 * u * tq:(2 * u + 2) * tq, :], k_of(u)) for u in grp], axis=0)
                for grp in groups]

    def weights(zs, t, visible):
        cum = _cum_matrix(t)
        sums = []
        for gi, rows in enumerate(group_rows):
            sp = _softplus_pos(zs[gi])
            if visible is not None:
                sp = jnp.where(visible[rows], sp, 0.0)
            hi = sp.astype(BF16)
            lo = (sp - hi.astype(F32)).astype(BF16)
            sums.append(_dot(jnp.concatenate([hi, lo], axis=1), cum))
        out = []
        for gi, rows in enumerate(group_rows):
            logit = zs[gi] - sums[gi][:, LANES:]
            if visible is None:
                carry = carry_sc[rows, :]
                a = jnp.exp(logit - carry)
                carry_sc[rows, :] = carry + sums[gi][:, :LANES]
            else:
                a = jnp.where(visible[rows], jnp.exp(logit), 0.0)
                carry_sc[rows, :] = sums[gi][:, :LANES]
            out.append(a.astype(BF16))
        return out

    def accumulate(a_groups, v_of, transposed, assign=False):
        dot_v = _dot_nt if transposed else _dot
        for gi, grp in enumerate(groups):
            for j, u in enumerate(grp):
                s, p = divmod(u, n_pairs)
                cols = slice(p * LANES, (p + 1) * LANES)
                pv = dot_v(a_groups[gi][2 * j * tq:(2 * j + 2) * tq, :], v_of(u))
                out = jnp.where(first_head, pv[:tq], pv[tq:])
                if assign:
                    acc_sc[s, :, cols] = out
                else:
                    acc_sc[s, :, cols] += out

    def unit_cols(ref):
        def of(u):
            s, p = divmod(u, n_pairs)
            return ref[s, :, p * LANES:(p + 1) * LANES]
        return of

    def past_of(ref, block_index):
        start = pl.multiple_of(block_index * tk, tk)

        def of(u):
            s, p = divmod(u, n_pairs)
            if cache_past:
                return ref[s, p * LANES:(p + 1) * LANES, pl.ds(start, tk)].astype(BF16)
            return ref[s, pl.ds(start, tk), p * LANES:(p + 1) * LANES].astype(BF16)
        return of

    rd = lax.broadcasted_iota(jnp.int32, (2 * n_units * tq, tq), 0) & (tq - 1)
    cd = lax.broadcasted_iota(jnp.int32, (2 * n_units * tq, tq), 1)
    a_diag = weights(scores(unit_cols(kd_ref), False), tq, cd < rd)
    accumulate(a_diag, unit_cols(vd_ref), False, assign=True)

    def not_done():
        return jnp.min(carry_sc[...]) <= EXP_UNDERFLOW

    def past_block(i):
        block_index = n_blocks - 1 - i
        a = weights(scores(past_of(kp_ref, block_index), cache_past), tk, None)
        accumulate(a, past_of(vp_ref, block_index), cache_past)

    def pair_cond(state):
        i, go = state
        return jnp.logical_and(i + 1 < n_blocks, go)

    def pair_body(state):
        i, _ = state
        past_block(i)
        past_block(i + 1)
        return i + 2, not_done()

    def cond(state):
        i, go = state
        return jnp.logical_and(i < n_blocks, go)

    def body(state):
        i, _ = state
        past_block(i)
        return i + 1, not_done()

    state = lax.while_loop(pair_cond, pair_body, (jnp.int32(0), not_done()))
    lax.while_loop(cond, body, state)
    o_ref[...] = acc_sc[...].astype(BF16)


def _attn(q, k_new, v_new, k_past, v_past, *, bt, tq, n_past):
    nb, seq, _ = q.shape
    assert nb % bt == 0
    tok = pl.BlockSpec((bt, tq, SB_WIDTH), lambda b, i: (b, i, 0))
    past = pl.BlockSpec((bt,) + k_past.shape[1:], lambda b, i: (b, 0, 0))
    rows = bt * SB_HEADS * tq
    return pl.pallas_call(
        functools.partial(_attn_kernel, bt=bt, tq=tq, tk=ATTN_BLOCK, n_past=n_past),
        grid=(nb // bt, seq // tq),
        in_specs=[tok, tok, tok, past, past],
        out_specs=tok,
        out_shape=jax.ShapeDtypeStruct((nb, seq, SB_WIDTH), BF16),
        scratch_shapes=[pltpu.VMEM((rows, LANES), BF16),
                        pltpu.VMEM((bt, tq, SB_WIDTH), F32),
                        pltpu.VMEM((rows, LANES), F32)],
        compiler_params=_params(2),
        name="attn",
    )(q, k_new, v_new, k_past, v_past)


def _mlstm_kernel(mq_ref, mk_ref, mv_ref, mo_ref, grow_ref, c0_ref, n0_ref, m0_ref, gn_ref,
                  yb_ref, c_out_ref, n_out_ref, m_out_ref,
                  c_sc, n_sc, m_sc, alpha_sc, beta_sc, r_sc, wg_sc, dec_sc, mprev_sc, *, bt, lc, nc):
    step = pl.program_id(1)
    dk = ML_HEAD_DIM

    ri = lax.broadcasted_iota(jnp.int32, (lc, lc), 0)
    ci = lax.broadcasted_iota(jnp.int32, (lc, lc), 1)
    causal = ci <= ri
    eye = jnp.where(ri == ci, 1.0, 0.0).astype(BF16)
    ones_blk = jnp.ones((lc, dk), BF16)
    zeros_blk = jnp.zeros((lc, dk), BF16)

    @pl.when(step == 0)
    def _():
        c_sc[...] = c0_ref[...]
        n_sc[...] = n0_ref[...]
        incl_t = jnp.where(ri <= ci, 1.0, 0.0).astype(BF16)
        blocks = [(s, c) for s in range(bt) for c in range(nc)]
        li = jnp.concatenate([grow_ref[s, 0:SUBLANES, c * lc:(c + 1) * lc] for s, c in blocks], axis=0)
        lf = jnp.concatenate([grow_ref[s, SUBLANES:2 * SUBLANES, c * lc:(c + 1) * lc] for s, c in blocks],
                             axis=0)
        b = _split_dot(lf, incl_t, 3)
        r = li - b
        lane = lax.broadcasted_iota(jnp.int32, r.shape, 1)
        run = r
        d = 1
        while d < lc:
            run = jnp.maximum(run, jnp.where(lane >= d, pltpu.roll(run, d, axis=1), -jnp.inf))
            d *= 2
        b_last = b[:, lc - 1:lc]
        g = b_last - b + li
        g_max = jnp.max(g, axis=1, keepdims=True)
        m_before = []
        m_after = []
        for s in range(bt):
            m = m0_ref[s]
            for c in range(nc):
                rows = slice((s * nc + c) * SUBLANES, (s * nc + c + 1) * SUBLANES)
                m_before.append(m)
                m = jnp.maximum(b_last[rows] + m, g_max[rows])
                m_after.append(m)
            m_sc[s] = m
        m_prev = jnp.concatenate(m_before, axis=0)
        m_new = jnp.concatenate(m_after, axis=0)
        top = jnp.maximum(run, m_prev[:, :lc])
        alpha = (-(top + jnp.abs(top) * STABILISER_MARGIN)).astype(BF16).astype(F32)
        alpha_sc[...] = alpha
        beta_sc[...] = alpha - b
        r_sc[...] = r
        wg_sc[...] = jnp.exp(g - m_new[:, :lc])
        dec_sc[...] = jnp.exp(b_last + m_prev - m_new)
        mprev_sc[...] = m_prev

    units = [(s, h) for s in range(bt) for h in range(ML_HEADS)]
    cols = [slice(h * dk, (h + 1) * dk) for h in range(ML_HEADS)]
    alpha, beta_hi, beta_lo, r, wg, decay, m_prev = [], [], [], [], [], [], []
    for s in range(bt):
        chunk = pl.ds(pl.multiple_of((s * nc + step) * SUBLANES, SUBLANES), SUBLANES)
        alpha.append(alpha_sc[chunk, :].astype(BF16))
        beta = beta_sc[chunk, :]
        beta_hi.append(beta.astype(BF16))
        beta_lo.append((beta - beta_hi[s].astype(F32)).astype(BF16))
        r.append(r_sc[chunk, :])
        wg.append(wg_sc[chunk, :])
        decay.append(dec_sc[chunk, :])
        m_prev.append(mprev_sc[chunk, :])
    eye2 = jnp.concatenate([eye, eye], axis=1)

    def rows_of(x, h):
        return jnp.broadcast_to(x[h:h + 1, :], (dk, lc))
    alpha_b = _dot_nt(eye, jnp.concatenate([rows_of(alpha[s], h) for s, h in units], axis=0))
    beta_b = _dot_nt(eye2, jnp.concatenate(
        [jnp.concatenate([rows_of(beta_hi[s], h), rows_of(beta_lo[s], h)], axis=1) for s, h in units], axis=0))
    qk = [_dot_nt(mq_ref[s, :, cols[h]], mk_ref[s, :, cols[h]]) for s, h in units]
    c_prev = [c_sc[s, h] for s, h in units]
    n_prev = [n_sc[s, h:h + 1, :] for s, h in units]
    qc = [_dot_nt(mq_ref[s, :, cols[h]],
                  jnp.concatenate([c_prev[u].astype(BF16),
                                   jnp.broadcast_to(n_prev[u], (dk, dk)).astype(BF16)], axis=0))
          for u, (s, h) in enumerate(units)]
    upd = []
    for s, h in units:
        wg_h = wg[s][h:h + 1, :]
        vt = mv_ref[s, :, cols[h]].astype(F32).T
        lhs = jnp.concatenate([(vt * wg_h).astype(BF16),
                               jnp.broadcast_to(wg_h, (SUBLANES, lc)).astype(BF16)], axis=0)
        upd.append(_dot(lhs, mk_ref[s, :, cols[h]]))

    cat = []
    for u, (s, h) in enumerate(units):
        w = jnp.where(causal, jnp.exp(alpha_b[:, u * dk:u * dk + lc] + r[s][h:h + 1, :]), 0.0)
        sw = qk[u] * w
        s_hi = sw.astype(BF16)
        cat.append(jnp.concatenate([s_hi, (sw - s_hi.astype(F32)).astype(BF16)], axis=1))
    nv = []
    for u, (s, h) in enumerate(units):
        rhs = jnp.concatenate([jnp.concatenate([mv_ref[s, :, cols[h]], ones_blk], axis=1),
                               jnp.concatenate([zeros_blk, ones_blk], axis=1)], axis=0)
        nv.append(_dot(cat[u], rhs))
    hval = []
    cat2 = []
    for u, (s, h) in enumerate(units):
        ucols = slice(u * dk, (u + 1) * dk)
        w_inter = jnp.exp(m_prev[s][h:h + 1, :] + alpha_b[:, ucols])
        num = nv[u][:, :dk] + w_inter * qc[u][:, :dk]
        den = nv[u][:, dk:] + w_inter * qc[u][:, dk:]
        hv = num / jnp.maximum(jnp.abs(den), jnp.exp(beta_b[:, ucols]))
        h2 = hv * hv
        h2_hi = h2.astype(BF16)
        hval.append(hv)
        cat2.append(jnp.concatenate([h2_hi, (h2 - h2_hi.astype(F32)).astype(BF16)], axis=1))
    ms = _dot(jnp.concatenate(cat2, axis=0), jnp.ones((2 * dk, dk), BF16)) * (1.0 / dk)
    for u, (s, h) in enumerate(units):
        y = hval[u] * lax.rsqrt(ms[u * lc:(u + 1) * lc] + EPS) * gn_ref[:, cols[h]]
        yb_ref[s, :, cols[h]] = (_sigmoid(mo_ref[s, :, cols[h]]) * y).astype(BF16)

    for u, (s, h) in enumerate(units):
        dec = decay[s][h:h + 1, :]
        c_sc[s, h] = dec * c_prev[u] + upd[u][:dk]
        n_sc[s, h:h + 1, :] = dec * n_prev[u] + upd[u][dk:dk + 1]

    c_out_ref[...] = c_sc[...]
    n_out_ref[...] = n_sc[...]
    m_out_ref[...] = m_sc[...]


def _mlstm(mq, mk, mv, mo, grow, c0, n0, m0, gn, *, bt, lc):
    nb, seq, _ = mq.shape
    assert nb % bt == 0
    tok = pl.BlockSpec((bt, lc, ML_WIDTH), lambda b, i: (b, i, 0))
    c_spec = pl.BlockSpec((bt, ML_HEADS, ML_HEAD_DIM, ML_HEAD_DIM), lambda b, i: (b, 0, 0, 0))
    n_spec = pl.BlockSpec((bt, ML_HEADS, ML_HEAD_DIM), lambda b, i: (b, 0, 0))
    m_spec = pl.BlockSpec((bt, SUBLANES, LANES), lambda b, i: (b, 0, 0))
    nc = seq // lc
    table = pltpu.VMEM((bt * nc * SUBLANES, lc), F32)
    table_m = pltpu.VMEM((bt * nc * SUBLANES, LANES), F32)
    return pl.pallas_call(
        functools.partial(_mlstm_kernel, bt=bt, lc=lc, nc=nc),
        grid=(nb // bt, nc),
        in_specs=[tok, tok, tok, tok,
                  pl.BlockSpec((bt, 2 * SUBLANES, seq), lambda b, i: (b, 0, 0)),
                  c_spec, n_spec, m_spec, _const_spec((1, ML_WIDTH))],
        out_specs=[tok, c_spec, n_spec, m_spec],
        out_shape=[jax.ShapeDtypeStruct((nb, seq, ML_WIDTH), BF16),
                   jax.ShapeDtypeStruct((nb, ML_HEADS, ML_HEAD_DIM, ML_HEAD_DIM), F32),
                   jax.ShapeDtypeStruct((nb, ML_HEADS, ML_HEAD_DIM), F32),
                   jax.ShapeDtypeStruct((nb, SUBLANES, LANES), F32)],
        scratch_shapes=[pltpu.VMEM((bt, ML_HEADS, ML_HEAD_DIM, ML_HEAD_DIM), F32),
                        pltpu.VMEM((bt, ML_HEADS, ML_HEAD_DIM), F32),
                        pltpu.VMEM((bt, SUBLANES, LANES), F32),
                        table, table, table, table, table_m, table_m],
        compiler_params=_params(2),
        name="mlstm",
    )(mq, mk, mv, mo, grow, c0, n0, m0, gn)


def _post_kernel(x_ref, ya_ref, yb_ref, mod_ref, g1_ref, g2_ref, gf_ref,
                 wgab_ref, wa_ref, wb_ref, wout_ref, wfg_ref, wfu_ref, wfd_ref,
                 y_ref, u_sc, *, bt, lt):
    g1 = g1_ref[...]
    for j in range(bt):
        u = _norm_mod(x_ref[j], g1, mod_ref[j, 1:2, :], mod_ref[j, 0:1, :])
        u_sc[j * lt:(j + 1) * lt, :] = u.astype(BF16)
    u = u_sc[...]
    ya = ya_ref[...].reshape(bt * lt, SB_WIDTH)
    yb = yb_ref[...].reshape(bt * lt, ML_WIDTH)
    ga = _dot_nt(u, wgab_ref[:D_MODEL, :])
    gb = _dot_nt(u, wgab_ref[D_MODEL:, :])
    merged = _sigmoid(ga) * _dot(ya, wa_ref[...]) + _sigmoid(gb) * _dot(yb, wb_ref[...])
    attn_out = _dot(merged.astype(BF16), wout_ref[...])

    g2 = g2_ref[...]
    for j in range(bt):
        rows = slice(j * lt, (j + 1) * lt)
        x1 = x_ref[j] + mod_ref[j, 2:3, :] * attn_out[rows]
        y_ref[j] = x1
        u2 = _norm_mod(x1, g2, mod_ref[j, 4:5, :], mod_ref[j, 3:4, :])
        u_sc[rows, :] = u2.astype(BF16)
    u2 = u_sc[...]
    hg = _dot(u2, wfg_ref[...])
    hu = _dot(u2, wfu_ref[...])
    act = (hg * _sigmoid(hg) * hu).astype(BF16)
    ff = _dot(act, wfd_ref[...])
    gf = gf_ref[...]
    for j in range(bt):
        rows = slice(j * lt, (j + 1) * lt)
        x2 = y_ref[j] + mod_ref[j, 5:6, :] * ff[rows]
        ms = jnp.mean(x2 * x2, axis=-1, keepdims=True)
        y_ref[j] = x2 * lax.rsqrt(ms + EPS) * gf


def _post(x, ya, yb, mod3, g1, g2, gf, wgab, wa, wb, wout, wfg, wfu, wfd, *, bt, lt):
    nb, seq, _ = x.shape
    assert nb % bt == 0 and seq % lt == 0

    def tok(width):
        return pl.BlockSpec((bt, lt, width), lambda b, l: (b, l, 0))

    in_specs = [tok(D_MODEL), tok(SB_WIDTH), tok(ML_WIDTH),
                pl.BlockSpec((bt, 6, D_MODEL), lambda b, l: (b, 0, 0)),
                _const_spec((1, D_MODEL)), _const_spec((1, D_MODEL)), _const_spec((1, D_MODEL)),
                _const_spec(wgab.shape), _const_spec(wa.shape), _const_spec(wb.shape), _const_spec(wout.shape),
                _const_spec(wfg.shape), _const_spec(wfu.shape), _const_spec(wfd.shape)]
    return pl.pallas_call(
        functools.partial(_post_kernel, bt=bt, lt=lt),
        grid=(nb // bt, seq // lt),
        in_specs=in_specs,
        out_specs=tok(D_MODEL),
        out_shape=jax.ShapeDtypeStruct((nb, seq, D_MODEL), F32),
        scratch_shapes=[pltpu.VMEM((bt * lt, D_MODEL), BF16)],
        compiler_params=_params(2),
        name="post",
    )(x, ya, yb, mod3, g1, g2, gf, wgab, wa, wb, wout, wfg, wfu, wfd)


def _group(x, mod3, k_past, v_past, c0, n0, m0, conv0, wts, *, bt, lt_in, lt, tq, n_past, lc):
    (g1, g2, gf, w1, wif_row, bif_row, wconv, bconv, gn,
     wgab, wa, wb, wout, wfg, wfu, wfd) = wts
    nb, seq, _ = x.shape
    (q, k32, v32, kb, vb, mq, mk, mv, mo, grow, conv_new) = _inproj(
        x, mod3, conv0, g1, w1, wif_row, bif_row, wconv, bconv, bt=bt, lt=lt_in)
    if k_past is None:
        k_past, v_past = kb, vb
    ya = _attn(q, kb, vb, k_past, v_past, bt=ATTN_ROWS // tq, tq=tq, n_past=n_past)
    m0b = jnp.broadcast_to(jnp.pad(m0, ((0, 0), (0, SUBLANES - ML_HEADS)))[:, :, None], (nb, SUBLANES, LANES))
    yb, c1, n1, m1 = _mlstm(mq, mk, mv, mo, grow, c0, n0, m0b, gn, bt=ML_SEQS, lc=lc)
    y = _post(x, ya, yb, mod3, g1, g2, gf, wgab, wa, wb, wout, wfg, wfu, wfd, bt=bt, lt=lt)
    def rows_out(t):
        if _tokens_on_lanes(lt_in):
            return jnp.transpose(t.reshape(nb, SB_HEADS, SB_HEAD_DIM, seq), (0, 3, 1, 2))[None]
        return t.reshape(1, nb, seq, SB_HEADS, SB_HEAD_DIM)

    states = (rows_out(k32), rows_out(v32),
              c1[None], n1[None], m1[None, :, :ML_HEADS, 0], conv_new[None])
    return y, states


def kernel(x_prompt, x_sample, c_prompt, c_sample, cache_sb_k, cache_sb_v, state_mlstm_C, state_mlstm_n, state_mlstm_m, state_conv, norm1_g, norm2_g, w_ada, b_ada, w_in, b_if, w_conv, b_conv, ml_norm_g, w_a, w_b, w_out, w_ff_gate, w_ff_up, w_ff_down, final_g):
    assert w_in.shape[0] == 1, "single layer"
    bp, seq_p, _ = x_prompt.shape
    bs, seq_s, _ = x_sample.shape
    past = cache_sb_k.shape[2]

    c_all = jnp.concatenate([c_prompt, c_sample], axis=0)
    mod3 = _ada(c_all, w_ada[0], b_ada[0]).reshape(bp + bs, 6, D_MODEL)

    w1, wgab, wif_row, wa, wb, wout, wfg, wfu, wfd = _prep(
        jnp.transpose(w_in[0]), w_a[0], w_b[0], w_out[0], w_ff_gate[0], w_ff_up[0], w_ff_down[0])
    pad = SUBLANES - ML_HEADS
    bif_row = jnp.pad(b_if[0].reshape(2, ML_HEADS), ((0, 0), (0, pad))).reshape(2 * SUBLANES, 1)
    wts = (norm1_g[0].reshape(1, D_MODEL), norm2_g[0].reshape(1, D_MODEL), final_g.reshape(1, D_MODEL),
           w1, wif_row, bif_row, w_conv[0], b_conv[0].reshape(1, 2 * ML_WIDTH),
           ml_norm_g[0].reshape(1, ML_WIDTH),
           wgab, wa, wb, wout, wfg, wfu, wfd)

    zeros = functools.partial(jnp.zeros, dtype=F32)
    y_p, st_p = _group(
        x_prompt, mod3[:bp], None, None,
        zeros((bp, ML_HEADS, ML_HEAD_DIM, ML_HEAD_DIM)), zeros((bp, ML_HEADS, ML_HEAD_DIM)),
        zeros((bp, ML_HEADS)), zeros((bp, CONV_W - 1, 2 * ML_WIDTH)), wts,
        bt=1, lt_in=512, lt=256, tq=ATTN_BLOCK, n_past=None, lc=ML_CHUNK)
    def cache_t(c):
        return jnp.transpose(c[0], (0, 2, 3, 1)).reshape(bs, SB_WIDTH, past)

    y_s, st_s = _group(
        x_sample, mod3[bp:], cache_t(cache_sb_k), cache_t(cache_sb_v),
        state_mlstm_C[0], state_mlstm_n[0], state_mlstm_m[0], state_conv[0], wts,
        bt=4, lt_in=seq_s, lt=seq_s, tq=seq_s, n_past=past // ATTN_BLOCK, lc=seq_s)
    return (y_p, y_s) + st_p + st_s
```

```python
import functools
import math

import jax
import jax.numpy as jnp
from jax import lax
from jax.experimental import pallas as pl
from jax.experimental.pallas import tpu as pltpu

D_MODEL = 1024
SB_HEADS = 8
SB_HEAD_DIM = 64
SB_WIDTH = SB_HEADS * SB_HEAD_DIM
ML_HEADS = 4
ML_HEAD_DIM = 128
ML_WIDTH = ML_HEADS * ML_HEAD_DIM
CONV_W = 4
D_FF = 2816
EPS = 1e-6
LANES = 128
SUBLANES = 8
ATTN_BLOCK = 128
ML_CHUNK = 128
ML_SEQS = 8
STABILISER_MARGIN = 2.0 ** -7
EXP_UNDERFLOW = 88.0
SOFTPLUS_CLAMP = 60.0
MASKED_SCORE = -1e30
ATTN_ROWS = 256
ATTN_GROUPS = 4
V7X_VMEM_BYTES = 64 * 1024 * 1024
VMEM_LIMIT = V7X_VMEM_BYTES - 8 * 1024 * 1024

BF16 = jnp.bfloat16
F32 = jnp.float32

_C_Q, _C_K, _C_V = 0, SB_WIDTH, 2 * SB_WIDTH
_C_MQK = 3 * SB_WIDTH
_C_MV = _C_MQK + 2 * ML_WIDTH
_C_MO = _C_MV + ML_WIDTH
_C_IF = _C_MO + ML_WIDTH
_C_GAB = _C_IF + 2 * ML_HEADS
_C_END = _C_GAB + 2 * D_MODEL


def _sigmoid(x):
    return 1.0 / (1.0 + jnp.exp(-x))


def _log_sigmoid(x):
    return jnp.minimum(x, 0.0) - jnp.log1p(jnp.exp(-jnp.abs(x)))


def _softplus_pos(x):
    return jnp.maximum(jnp.log(1.0 + jnp.exp(jnp.minimum(x, SOFTPLUS_CLAMP))), x)


def _dot(a, b):
    return jnp.dot(a, b, preferred_element_type=F32)


def _dot_nt(a, b):
    return lax.dot_general(a, b, (((1,), (1,)), ((), ())), preferred_element_type=F32)


def _split_dot(x, tri, parts):
    acc = None
    rem = x
    for i in range(parts):
        piece = rem.astype(BF16)
        term = _dot(piece, tri)
        acc = term if acc is None else acc + term
        if i + 1 < parts:
            rem = rem - piece.astype(F32)
    return acc


def _norm_mod(x, g, sc, sh):
    ms = jnp.mean(x * x, axis=-1, keepdims=True)
    return (x * lax.rsqrt(ms + EPS)) * (g * (1.0 + sc)) + sh


def _const_spec(shape):
    nd = len(shape)
    return pl.BlockSpec(shape, lambda *_: (0,) * nd, pipeline_mode=pl.Buffered(1))


def _params(n_axes):
    return pltpu.CompilerParams(dimension_semantics=("arbitrary",) * n_axes,
                                vmem_limit_bytes=VMEM_LIMIT)


def _ada_kernel(c_ref, w_ref, b_ref, o_ref):
    c = c_ref[...]
    s = c * _sigmoid(c)
    o_ref[...] = _dot(s.astype(BF16), w_ref[...].astype(BF16)) + b_ref[...]


def _ada(c_all, w_ada, b_ada):
    nb = c_all.shape[0]
    n_out = w_ada.shape[1]
    tn = D_MODEL // 2
    return pl.pallas_call(
        _ada_kernel,
        grid=(n_out // tn,),
        in_specs=[pl.BlockSpec((nb, D_MODEL), lambda j: (0, 0)),
                  pl.BlockSpec((D_MODEL, tn), lambda j: (0, j)),
                  pl.BlockSpec((1, tn), lambda j: (0, j))],
        out_specs=pl.BlockSpec((nb, tn), lambda j: (0, j)),
        out_shape=jax.ShapeDtypeStruct((nb, n_out), F32),
        compiler_params=_params(1),
        name="ada",
    )(c_all, w_ada, b_ada.reshape(1, n_out))


PREP_STEPS = 8
W1_BLOCK = 512


def _prep_kernel(w1_ref, wgab_ref, wif_ref, wa_ref, wb_ref, wout_ref, wfg_ref, wfu_ref, wfd_ref,
                 w1_o, wgab_o, wif_o, wa_o, wb_o, wout_o, wfg_o, wfu_o, wfd_o):
    w1_o[...] = w1_ref[...].T.astype(BF16)
    for src, dst in ((wgab_ref, wgab_o), (wa_ref, wa_o), (wb_ref, wb_o), (wout_ref, wout_o),
                     (wfg_ref, wfg_o), (wfu_ref, wfu_o), (wfd_ref, wfd_o)):
        dst[...] = src[...].astype(BF16)
    gates = wif_ref[...]
    pad = jnp.zeros((SUBLANES - ML_HEADS, D_MODEL), F32)
    wif_o[...] = jnp.concatenate([gates[:ML_HEADS], pad, gates[ML_HEADS:], pad], axis=0).astype(BF16)


def _prep(w_in_t, w_a, w_b, w_out, w_fg, w_fu, w_fd):
    def rows(a, n_rows=None, first=0):
        n_rows = a.shape[0] if n_rows is None else n_rows
        step = n_rows // PREP_STEPS
        assert n_rows % (PREP_STEPS * 2 * SUBLANES) == 0 and first % SUBLANES == 0
        if first == 0:
            return pl.BlockSpec((step, a.shape[1]), lambda i: (i, 0))
        return pl.BlockSpec((pl.Element(step), pl.Element(a.shape[1])),
                            lambda i: (pl.multiple_of(first + i * step, SUBLANES), 0))

    def out(n_rows, n_cols):
        return (jax.ShapeDtypeStruct((n_rows, n_cols), BF16),
                pl.BlockSpec((n_rows // PREP_STEPS, n_cols), lambda i: (i, 0)))

    last = _C_IF // W1_BLOCK - 1
    assert _C_IF % W1_BLOCK == 0 and last < PREP_STEPS
    w1_in = pl.BlockSpec((W1_BLOCK, D_MODEL), lambda i: (jnp.minimum(i, last), 0))
    w1_out = (jax.ShapeDtypeStruct((D_MODEL, _C_IF), BF16),
              pl.BlockSpec((D_MODEL, W1_BLOCK), lambda i: (0, jnp.minimum(i, last))))

    others = (w_a, w_b, w_out, w_fg, w_fu, w_fd)
    outs = [w1_out, out(_C_END - _C_GAB, D_MODEL),
            (jax.ShapeDtypeStruct((2 * SUBLANES, D_MODEL), BF16),
             pl.BlockSpec((2 * SUBLANES, D_MODEL), lambda i: (0, 0)))] + [out(*a.shape) for a in others]
    in_specs = [w1_in, rows(w_in_t, _C_END - _C_GAB, _C_GAB),
                pl.BlockSpec((pl.Element(2 * ML_HEADS), pl.Element(D_MODEL)), lambda i: (_C_IF, 0))
                ] + [rows(a) for a in others]
    return pl.pallas_call(
        _prep_kernel,
        grid=(PREP_STEPS,),
        in_specs=in_specs,
        out_specs=[o[1] for o in outs],
        out_shape=[o[0] for o in outs],
        compiler_params=_params(1),
        name="prep",
    )(w_in_t, w_in_t, w_in_t, *others)


def _tokens_on_lanes(tile_len):
    return tile_len % LANES == 0


def _inproj_kernel(x_ref, mod_ref, conv0_ref, g1_ref, w1_ref, wifr_ref, bifr_ref,
                   wconv_ref, bconv_ref,
                   q_ref, k32_ref, v32_ref, kb_ref, vb_ref, mq_ref, mk_ref, mv_ref, mo_ref,
                   grow_ref, convn_ref,
                   u_sc, xp_sc, *, bt, lt):
    @pl.when(pl.program_id(1) == 0)
    def _():
        for j in range(bt):
            xp_sc[j] = jnp.concatenate(
                [jnp.zeros((SUBLANES - (CONV_W - 1), 2 * ML_WIDTH), F32), conv0_ref[j]], axis=0)

    g1 = g1_ref[...]
    for j in range(bt):
        u = _norm_mod(x_ref[j], g1, mod_ref[j, 1:2, :], mod_ref[j, 0:1, :])
        u_sc[j * lt:(j + 1) * lt, :] = u.astype(BF16)
    u = u_sc[...]

    def mm(lo, hi):
        return _dot(u, w1_ref[:, lo:hi])

    def new_rows(t):
        return t.T if _tokens_on_lanes(lt) else t

    wconv = wconv_ref[...]
    bconv = bconv_ref[...]
    k_scale = 1.0 / math.sqrt(ML_HEAD_DIM)
    mqk = mm(_C_MQK, _C_MV)
    row8 = lax.broadcasted_iota(jnp.int32, (SUBLANES, 2 * ML_WIDTH), 0)
    for j in range(bt):
        x = mqk[j * lt:(j + 1) * lt]
        before = xp_sc[j]
        conv = bconv + x * wconv[CONV_W - 1:CONV_W, :]
        for k in range(1, CONV_W):
            xr = pltpu.roll(x, k, axis=0)
            head = jnp.where(row8 < k, pltpu.roll(before, k, axis=0), xr[:SUBLANES])
            xk = jnp.concatenate([head, xr[SUBLANES:]], axis=0)
            conv = conv + xk * wconv[CONV_W - 1 - k:CONV_W - k, :]
        conv = conv * _sigmoid(conv)
        mq_ref[j] = conv[:, :ML_WIDTH].astype(BF16)
        mk_ref[j] = (conv[:, ML_WIDTH:] * k_scale).astype(BF16)
        convn_ref[j] = x[lt - (CONV_W - 1):]
        xp_sc[j] = x[lt - SUBLANES:]

    sq = mm(_C_Q, _C_K)
    sk = mm(_C_K, _C_V)
    sv = mm(_C_V, _C_MQK)
    mv = mm(_C_MV, _C_MO)
    mo = mm(_C_MO, _C_IF)
    for j in range(bt):
        rows = slice(j * lt, (j + 1) * lt)
        q_ref[j] = sq[rows].astype(BF16)
        k32_ref[j] = new_rows(sk[rows])
        kb_ref[j] = sk[rows].astype(BF16)
        v32_ref[j] = new_rows(sv[rows])
        vb_ref[j] = sv[rows].astype(BF16)
        mv_ref[j] = mv[rows].astype(BF16)
    for j in range(bt):
        rows = slice(j * lt, (j + 1) * lt)
        mo_ref[j] = mo[rows]
        gr = _dot_nt(wifr_ref[...], u_sc[rows, :]) + bifr_ref[...]
        row = lax.broadcasted_iota(jnp.int32, gr.shape, 0)
        grow_ref[j] = jnp.where(row >= SUBLANES, _log_sigmoid(gr), gr)


def _inproj(x, mod3, conv0, g1, w1, wif_row, bif_row, wconv, bconv, *, bt, lt):
    nb, seq, _ = x.shape
    assert nb % bt == 0 and seq % lt == 0 and (bt == 1 or lt == seq)
    grid = (nb // bt, seq // lt)

    def tok(width):
        return pl.BlockSpec((bt, lt, width), lambda b, l: (b, l, 0))

    def per_batch(rows, width):
        return pl.BlockSpec((bt, rows, width), lambda b, l: (b, 0, 0))

    in_specs = [tok(D_MODEL), per_batch(6, D_MODEL), per_batch(CONV_W - 1, 2 * ML_WIDTH),
                _const_spec((1, D_MODEL)), _const_spec(w1.shape),
                _const_spec((2 * SUBLANES, D_MODEL)),
                _const_spec((2 * SUBLANES, 1)),
                _const_spec((CONV_W, 2 * ML_WIDTH)), _const_spec((1, 2 * ML_WIDTH))]

    def tok_on_lanes(rows):
        return pl.BlockSpec((bt, rows, lt), lambda b, l: (b, 0, l))

    kv_spec = tok_on_lanes(SB_WIDTH) if _tokens_on_lanes(lt) else tok(SB_WIDTH)
    kv_shape = (nb, SB_WIDTH, seq) if _tokens_on_lanes(lt) else (nb, seq, SB_WIDTH)
    out_specs = [tok(SB_WIDTH), kv_spec, kv_spec, tok(SB_WIDTH), tok(SB_WIDTH)] + [tok(ML_WIDTH)] * 4 + [
        tok_on_lanes(2 * SUBLANES), per_batch(CONV_W - 1, 2 * ML_WIDTH)]

    def sds(shape, dt):
        return jax.ShapeDtypeStruct(shape, dt)

    out_shape = [sds((nb, seq, SB_WIDTH), BF16), sds(kv_shape, F32), sds(kv_shape, F32),
                 sds((nb, seq, SB_WIDTH), BF16), sds((nb, seq, SB_WIDTH), BF16),
                 sds((nb, seq, ML_WIDTH), BF16), sds((nb, seq, ML_WIDTH), BF16), sds((nb, seq, ML_WIDTH), BF16),
                 sds((nb, seq, ML_WIDTH), F32),
                 sds((nb, 2 * SUBLANES, seq), F32),
                 sds((nb, CONV_W - 1, 2 * ML_WIDTH), F32)]
    return pl.pallas_call(
        functools.partial(_inproj_kernel, bt=bt, lt=lt),
        grid=grid,
        in_specs=in_specs,
        out_specs=out_specs,
        out_shape=out_shape,
        scratch_shapes=[pltpu.VMEM((bt * lt, D_MODEL), BF16),
                        pltpu.VMEM((bt, SUBLANES, 2 * ML_WIDTH), F32)],
        compiler_params=_params(2),
        name="inproj",
    )(x, mod3, conv0, g1, w1, wif_row, bif_row, wconv, bconv)


def _cum_matrix(t):
    r = lax.broadcasted_iota(jnp.int32, (2 * t, LANES + t), 0)
    c = lax.broadcasted_iota(jnp.int32, (2 * t, LANES + t), 1)
    return jnp.where((c < LANES) | ((r & (t - 1)) >= (c - LANES)), 1.0, 0.0).astype(BF16)


def _attn_kernel(q_ref, kd_ref, vd_ref, kp_ref, vp_ref, o_ref, qm_sc, acc_sc, carry_sc, *, bt, tq, tk, n_past):
    cache_past = n_past is not None
    n_blocks = n_past if cache_past else pl.program_id(1)
    scale = 1.0 / math.sqrt(SB_HEAD_DIM)
    n_pairs = SB_WIDTH // LANES
    lane = lax.broadcasted_iota(jnp.int32, (1, LANES), 1)
    first_head = lane < SB_HEAD_DIM

    n_units = bt * n_pairs
    for u in range(n_units):
        s, p = divmod(u, n_pairs)
        qp = q_ref[s, :, p * LANES:(p + 1) * LANES] * scale
        zero = jnp.zeros_like(qp)
        qm_sc[(2 * u) * tq:(2 * u + 1) * tq, :] = jnp.where(first_head, qp, zero)
        qm_sc[(2 * u + 1) * tq:(2 * u + 2) * tq, :] = jnp.where(first_head, zero, qp)

    per_group = n_units // ATTN_GROUPS
    groups = [range(g, g + per_group) for g in range(0, n_units, per_group)]
    group_rows = [slice(2 * grp[0] * tq, 2 * (grp[-1] + 1) * tq) for grp in groups]

    def scores(k_of, transposed):
        dot_k = _dot if transposed else _dot_nt
        return [jnp.concatenate([dot_k(qm_sc[2 * u * tq:(2 * u + 2) * tq, :], k_of(u)) for u in grp], axis=0)
                for grp in groups]

    def weights(zs, t, visible):
        cum = _cum_matrix(t)
        if visible is not None:
            zs = [jnp.where(visible[rows], z, MASKED_SCORE) for z, rows in zip(zs, group_rows)]
        sums = []
        for gi, rows in enumerate(group_rows):
            sp = _softplus_pos(zs[gi])
            hi = sp.astype(BF16)
            lo = (sp - hi.astype(F32)).astype(BF16)
            sums.append(_dot(jnp.concatenate([hi, lo], axis=1), cum))
        out = []
        for gi, rows in enumerate(group_rows):
            logit = zs[gi] - sums[gi][:, LANES:]
            if visible is None:
                carry = carry_sc[rows, :]
                a = jnp.exp(logit - carry)
                carry_sc[rows, :] = carry + sums[gi][:, :LANES]
            else:
                a = jnp.exp(logit)
                carry_sc[rows, :] = sums[gi][:, :LANES]
            out.append(a.astype(BF16))
        return out

    def accumulate(a_groups, v_of, transposed, assign=False):
        dot_v = _dot_nt if transposed else _dot
        for gi, grp in enumerate(groups):
            for j, u in enumerate(grp):
                s, p = divmod(u, n_pairs)
                cols = slice(p * LANES, (p + 1) * LANES)
                pv = dot_v(a_groups[gi][2 * j * tq:(2 * j + 2) * tq, :], v_of(u))
                out = jnp.where(first_head, pv[:tq], pv[tq:])
                if assign:
                    acc_sc[s, :, cols] = out
                else:
                    acc_sc[s, :, cols] += out

    def unit_cols(ref):
        def of(u):
            s, p = divmod(u, n_pairs)
            return ref[s, :, p * LANES:(p + 1) * LANES]
        return of

    def past_of(ref, block_index):
        start = pl.multiple_of(block_index * tk, tk)

        def of(u):
            s, p = divmod(u, n_pairs)
            if cache_past:
                return ref[s, p * LANES:(p + 1) * LANES, pl.ds(start, tk)].astype(BF16)
            return ref[s, pl.ds(start, tk), p * LANES:(p + 1) * LANES].astype(BF16)
        return of

    rd = lax.broadcasted_iota(jnp.int32, (2 * n_units * tq, tq), 0) & (tq - 1)
    cd = lax.broadcasted_iota(jnp.int32, (2 * n_units * tq, tq), 1)
    a_diag = weights(scores(unit_cols(kd_ref), False), tq, cd < rd)
    accumulate(a_diag, unit_cols(vd_ref), False, assign=True)

    def not_done():
        return jnp.min(carry_sc[...]) <= EXP_UNDERFLOW

    def past_block(i):
        block_index = n_blocks - 1 - i
        a = weights(scores(past_of(kp_ref, block_index), cache_past), tk, None)
        accumulate(a, past_of(vp_ref, block_index), cache_past)

    def pair_cond(state):
        i, go = state
        return jnp.logical_and(i + 1 < n_blocks, go)

    def pair_body(state):
        i, _ = state
        past_block(i)
        past_block(i + 1)
        return i + 2, not_done()

    def cond(state):
        i, go = state
        return jnp.logical_and(i < n_blocks, go)

    def body(state):
        i, _ = state
        past_block(i)
        return i + 1, not_done()

    state = lax.while_loop(pair_cond, pair_body, (jnp.int32(0), not_done()))
    lax.while_loop(cond, body, state)
    o_ref[...] = acc_sc[...].astype(BF16)


def _attn(q, k_new, v_new, k_past, v_past, *, bt, tq, n_past):
    nb, seq, _ = q.shape
    assert nb % bt == 0
    tok = pl.BlockSpec((bt, tq, SB_WIDTH), lambda b, i: (b, i, 0))
    past = pl.BlockSpec((bt,) + k_past.shape[1:], lambda b, i: (b, 0, 0))
    rows = bt * SB_HEADS * tq
    return pl.pallas_call(
        functools.partial(_attn_kernel, bt=bt, tq=tq, tk=ATTN_BLOCK, n_past=n_past),
        grid=(nb // bt, seq // tq),
        in_specs=[tok, tok, tok, past, past],
        out_specs=tok,
        out_shape=jax.ShapeDtypeStruct((nb, seq, SB_WIDTH), BF16),
        scratch_shapes=[pltpu.VMEM((rows, LANES), BF16),
                        pltpu.VMEM((bt, tq, SB_WIDTH), F32),
                        pltpu.VMEM((rows, LANES), F32)],
        compiler_params=_params(2),
        name="attn",
    )(q, k_new, v_new, k_past, v_past)


def _mlstm_kernel(mq_ref, mk_ref, mv_ref, mo_ref, grow_ref, c0_ref, n0_ref, m0_ref, gn_ref,
                  yb_ref, c_out_ref, n_out_ref, m_out_ref,
                  c_sc, n_sc, m_sc, alpha_sc, beta_sc, r_sc, wg_sc, dec_sc, mprev_sc, *, bt, lc, nc):
    step = pl.program_id(1)
    dk = ML_HEAD_DIM

    ri = lax.broadcasted_iota(jnp.int32, (lc, lc), 0)
    ci = lax.broadcasted_iota(jnp.int32, (lc, lc), 1)
    causal = ci <= ri
    eye = jnp.where(ri == ci, 1.0, 0.0).astype(BF16)
    ones_blk = jnp.ones((lc, dk), BF16)

    @pl.when(step == 0)
    def _():
        c_sc[...] = c0_ref[...]
        n_sc[...] = n0_ref[...]
        incl_t = jnp.where(ri <= ci, 1.0, 0.0).astype(BF16)
        blocks = [(s, c) for s in range(bt) for c in range(nc)]
        li = jnp.concatenate([grow_ref[s, 0:SUBLANES, c * lc:(c + 1) * lc] for s, c in blocks], axis=0)
        lf = jnp.concatenate([grow_ref[s, SUBLANES:2 * SUBLANES, c * lc:(c + 1) * lc] for s, c in blocks],
                             axis=0)
        b = _split_dot(lf, incl_t, 3)
        r = li - b
        lane = lax.broadcasted_iota(jnp.int32, r.shape, 1)
        run = r
        d = 1
        while d < lc:
            run = jnp.maximum(run, jnp.where(lane >= d, pltpu.roll(run, d, axis=1), -jnp.inf))
            d *= 2
        b_last = b[:, lc - 1:lc]
        g = b_last - b + li
        g_max = jnp.max(g, axis=1, keepdims=True)
        m_before = []
        m_after = []
        for s in range(bt):
            m = m0_ref[s]
            for c in range(nc):
                rows = slice((s * nc + c) * SUBLANES, (s * nc + c + 1) * SUBLANES)
                m_before.append(m)
                m = jnp.maximum(b_last[rows] + m, g_max[rows])
                m_after.append(m)
            m_sc[s] = m
        m_prev = jnp.concatenate(m_before, axis=0)
        m_new = jnp.concatenate(m_after, axis=0)
        top = jnp.maximum(run, m_prev[:, :lc])
        alpha = (-(top + jnp.abs(top) * STABILISER_MARGIN)).astype(BF16).astype(F32)
        alpha_sc[...] = alpha
        beta_sc[...] = alpha - b
        r_sc[...] = r
        wg_sc[...] = jnp.exp(g - m_new[:, :lc])
        dec_sc[...] = jnp.exp(b_last + m_prev - m_new)
        mprev_sc[...] = m_prev

    units = [(s, h) for s in range(bt) for h in range(ML_HEADS)]
    cols = [slice(h * dk, (h + 1) * dk) for h in range(ML_HEADS)]
    alpha, beta, r, wg, decay, m_prev = [], [], [], [], [], []
    for s in range(bt):
        chunk = pl.ds(pl.multiple_of((s * nc + step) * SUBLANES, SUBLANES), SUBLANES)
        alpha.append(alpha_sc[chunk, :].astype(BF16))
        beta.append(beta_sc[chunk, :].astype(BF16))
        r.append(r_sc[chunk, :])
        wg.append(wg_sc[chunk, :])
        decay.append(dec_sc[chunk, :])
        m_prev.append(mprev_sc[chunk, :])

    def rows_of(x, h):
        return jnp.broadcast_to(x[h:h + 1, :], (dk, lc))
    alpha_b = _dot_nt(eye, jnp.concatenate([rows_of(alpha[s], h) for s, h in units], axis=0))
    beta_b = _dot_nt(eye, jnp.concatenate([rows_of(beta[s], h) for s, h in units], axis=0))
    qk = [_dot_nt(mq_ref[s, :, cols[h]], mk_ref[s, :, cols[h]]) for s, h in units]
    c_prev = [c_sc[s, h] for s, h in units]
    n_prev = [n_sc[s, h:h + 1, :] for s, h in units]
    qc = [_dot_nt(mq_ref[s, :, cols[h]],
                  jnp.concatenate([c_prev[u].astype(BF16),
                                   jnp.broadcast_to(n_prev[u], (dk, dk)).astype(BF16)], axis=0))
          for u, (s, h) in enumerate(units)]
    upd = []
    for s, h in units:
        wg_h = wg[s][h:h + 1, :]
        vt = mv_ref[s, :, cols[h]].astype(F32).T
        lhs = jnp.concatenate([(vt * wg_h).astype(BF16),
                               jnp.broadcast_to(wg_h, (SUBLANES, lc)).astype(BF16)], axis=0)
        upd.append(_dot(lhs, mk_ref[s, :, cols[h]]))

    sw = []
    for u, (s, h) in enumerate(units):
        w = jnp.where(causal, jnp.exp(alpha_b[:, u * dk:u * dk + lc] + r[s][h:h + 1, :]), 0.0)
        sw.append((qk[u] * w).astype(BF16))
    nv = [_dot(sw[u], jnp.concatenate([mv_ref[s, :, cols[h]], ones_blk], axis=1))
          for u, (s, h) in enumerate(units)]
    hval = []
    for u, (s, h) in enumerate(units):
        ucols = slice(u * dk, (u + 1) * dk)
        w_inter = jnp.exp(m_prev[s][h:h + 1, :] + alpha_b[:, ucols])
        num = nv[u][:, :dk] + w_inter * qc[u][:, :dk]
        den = nv[u][:, dk:] + w_inter * qc[u][:, dk:]
        hval.append(num / jnp.maximum(jnp.abs(den), jnp.exp(beta_b[:, ucols])))
    ms = _dot(jnp.concatenate([(hv * hv).astype(BF16) for hv in hval], axis=0),
              jnp.ones((dk, dk), BF16)) * (1.0 / dk)
    for u, (s, h) in enumerate(units):
        y = hval[u] * lax.rsqrt(ms[u * lc:(u + 1) * lc] + EPS) * gn_ref[:, cols[h]]
        yb_ref[s, :, cols[h]] = (_sigmoid(mo_ref[s, :, cols[h]]) * y).astype(BF16)

    for u, (s, h) in enumerate(units):
        dec = decay[s][h:h + 1, :]
        c_sc[s, h] = dec * c_prev[u] + upd[u][:dk]
        n_sc[s, h:h + 1, :] = dec * n_prev[u] + upd[u][dk:dk + 1]

    c_out_ref[...] = c_sc[...]
    n_out_ref[...] = n_sc[...]
    m_out_ref[...] = m_sc[...]


def _mlstm(mq, mk, mv, mo, grow, c0, n0, m0, gn, *, bt, lc):
    nb, seq, _ = mq.shape
    assert nb % bt == 0
    tok = pl.BlockSpec((bt, lc, ML_WIDTH), lambda b, i: (b, i, 0))
    c_spec = pl.BlockSpec((bt, ML_HEADS, ML_HEAD_DIM, ML_HEAD_DIM), lambda b, i: (b, 0, 0, 0))
    n_spec = pl.BlockSpec((bt, ML_HEADS, ML_HEAD_DIM), lambda b, i: (b, 0, 0))
    m_spec = pl.BlockSpec((bt, SUBLANES, LANES), lambda b, i: (b, 0, 0))
    nc = seq // lc
    table = pltpu.VMEM((bt * nc * SUBLANES, lc), F32)
    table_m = pltpu.VMEM((bt * nc * SUBLANES, LANES), F32)
    return pl.pallas_call(
        functools.partial(_mlstm_kernel, bt=bt, lc=lc, nc=nc),
        grid=(nb // bt, nc),
        in_specs=[tok, tok, tok, tok,
                  pl.BlockSpec((bt, 2 * SUBLANES, seq), lambda b, i: (b, 0, 0)),
                  c_spec, n_spec, m_spec, _const_spec((1, ML_WIDTH))],
        out_specs=[tok, c_spec, n_spec, m_spec],
        out_shape=[jax.ShapeDtypeStruct((nb, seq, ML_WIDTH), BF16),
                   jax.ShapeDtypeStruct((nb, ML_HEADS, ML_HEAD_DIM, ML_HEAD_DIM), F32),
                   jax.ShapeDtypeStruct((nb, ML_HEADS, ML_HEAD_DIM), F32),
                   jax.ShapeDtypeStruct((nb, SUBLANES, LANES), F32)],
        scratch_shapes=[pltpu.VMEM((bt, ML_HEADS, ML_HEAD_DIM, ML_HEAD_DIM), F32),
                        pltpu.VMEM((bt, ML_HEADS, ML_HEAD_DIM), F32),
                        pltpu.VMEM((bt, SUBLANES, LANES), F32),
                        table, table, table, table, table_m, table_m],
        compiler_params=_params(2),
        name="mlstm",
    )(mq, mk, mv, mo, grow, c0, n0, m0, gn)


def _post_kernel(x_ref, ya_ref, yb_ref, mod_ref, g1_ref, g2_ref, gf_ref,
                 wgab_ref, wa_ref, wb_ref, wout_ref, wfg_ref, wfu_ref, wfd_ref,
                 y_ref, u_sc, *, bt, lt):
    g1 = g1_ref[...]
    for j in range(bt):
        u = _norm_mod(x_ref[j], g1, mod_ref[j, 1:2, :], mod_ref[j, 0:1, :])
        u_sc[j * lt:(j + 1) * lt, :] = u.astype(BF16)
    u = u_sc[...]
    ya = ya_ref[...].reshape(bt * lt, SB_WIDTH)
    yb = yb_ref[...].reshape(bt * lt, ML_WIDTH)
    ga = _dot_nt(u, wgab_ref[:D_MODEL, :])
    gb = _dot_nt(u, wgab_ref[D_MODEL:, :])
    merged = _sigmoid(ga) * _dot(ya, wa_ref[...]) + _sigmoid(gb) * _dot(yb, wb_ref[...])
    attn_out = _dot(merged.astype(BF16), wout_ref[...])

    g2 = g2_ref[...]
    for j in range(bt):
        rows = slice(j * lt, (j + 1) * lt)
        x1 = x_ref[j] + mod_ref[j, 2:3, :] * attn_out[rows]
        y_ref[j] = x1
        u2 = _norm_mod(x1, g2, mod_ref[j, 4:5, :], mod_ref[j, 3:4, :])
        u_sc[rows, :] = u2.astype(BF16)
    u2 = u_sc[...]
    hg = _dot(u2, wfg_ref[...])
    hu = _dot(u2, wfu_ref[...])
    act = (hg * _sigmoid(hg) * hu).astype(BF16)
    ff = _dot(act, wfd_ref[...])
    gf = gf_ref[...]
    for j in range(bt):
        rows = slice(j * lt, (j + 1) * lt)
        x2 = y_ref[j] + mod_ref[j, 5:6, :] * ff[rows]
        ms = jnp.mean(x2 * x2, axis=-1, keepdims=True)
        y_ref[j] = x2 * lax.rsqrt(ms + EPS) * gf


def _post(x, ya, yb, mod3, g1, g2, gf, wgab, wa, wb, wout, wfg, wfu, wfd, *, bt, lt):
    nb, seq, _ = x.shape
    assert nb % bt == 0 and seq % lt == 0

    def tok(width):
        return pl.BlockSpec((bt, lt, width), lambda b, l: (b, l, 0))

    in_specs = [tok(D_MODEL), tok(SB_WIDTH), tok(ML_WIDTH),
                pl.BlockSpec((bt, 6, D_MODEL), lambda b, l: (b, 0, 0)),
                _const_spec((1, D_MODEL)), _const_spec((1, D_MODEL)), _const_spec((1, D_MODEL)),
                _const_spec(wgab.shape), _const_spec(wa.shape), _const_spec(wb.shape), _const_spec(wout.shape),
                _const_spec(wfg.shape), _const_spec(wfu.shape), _const_spec(wfd.shape)]
    return pl.pallas_call(
        functools.partial(_post_kernel, bt=bt, lt=lt),
        grid=(nb // bt, seq // lt),
        in_specs=in_specs,
        out_specs=tok(D_MODEL),
        out_shape=jax.ShapeDtypeStruct((nb, seq, D_MODEL), F32),
        scratch_shapes=[pltpu.VMEM((bt * lt, D_MODEL), BF16)],
        compiler_params=_params(2),
        name="post",
    )(x, ya, yb, mod3, g1, g2, gf, wgab, wa, wb, wout, wfg, wfu, wfd)


def _group(x, mod3, k_past, v_past, c0, n0, m0, conv0, wts, *, bt, lt_in, lt, tq, n_past, lc):
    (g1, g2, gf, w1, wif_row, bif_row, wconv, bconv, gn,
     wgab, wa, wb, wout, wfg, wfu, wfd) = wts
    nb, seq, _ = x.shape
    (q, k32, v32, kb, vb, mq, mk, mv, mo, grow, conv_new) = _inproj(
        x, mod3, conv0, g1, w1, wif_row, bif_row, wconv, bconv, bt=bt, lt=lt_in)
    if k_past is None:
        k_past, v_past = kb, vb
    ya = _attn(q, kb, vb, k_past, v_past, bt=ATTN_ROWS // tq, tq=tq, n_past=n_past)
    m0b = jnp.broadcast_to(jnp.pad(m0, ((0, 0), (0, SUBLANES - ML_HEADS)))[:, :, None], (nb, SUBLANES, LANES))
    yb, c1, n1, m1 = _mlstm(mq, mk, mv, mo, grow, c0, n0, m0b, gn, bt=ML_SEQS, lc=lc)
    y = _post(x, ya, yb, mod3, g1, g2, gf, wgab, wa, wb, wout, wfg, wfu, wfd, bt=bt, lt=lt)
    def rows_out(t):
        if _tokens_on_lanes(lt_in):
            return jnp.transpose(t.reshape(nb, SB_HEADS, SB_HEAD_DIM, seq), (0, 3, 1, 2))[None]
        return t.reshape(1, nb, seq, SB_HEADS, SB_HEAD_DIM)

    states = (rows_out(k32), rows_out(v32),
              c1[None], n1[None], m1[None, :, :ML_HEADS, 0], conv_new[None])
    return y, states


def kernel(x_prompt, x_sample, c_prompt, c_sample, cache_sb_k, cache_sb_v, state_mlstm_C, state_mlstm_n, state_mlstm_m, state_conv, norm1_g, norm2_g, w_ada, b_ada, w_in, b_if, w_conv, b_conv, ml_norm_g, w_a, w_b, w_out, w_ff_gate, w_ff_up, w_ff_down, final_g):
    assert w_in.shape[0] == 1, "single layer"
    bp, seq_p, _ = x_prompt.shape
    bs, seq_s, _ = x_sample.shape
    past = cache_sb_k.shape[2]

    c_all = jnp.concatenate([c_prompt, c_sample], axis=0)
    mod3 = _ada(c_all, w_ada[0], b_ada[0]).reshape(bp + bs, 6, D_MODEL)

    w1, wgab, wif_row, wa, wb, wout, wfg, wfu, wfd = _prep(
        jnp.transpose(w_in[0]), w_a[0], w_b[0], w_out[0], w_ff_gate[0], w_ff_up[0], w_ff_down[0])
    pad = SUBLANES - ML_HEADS
    bif_row = jnp.pad(b_if[0].reshape(2, ML_HEADS), ((0, 0), (0, pad))).reshape(2 * SUBLANES, 1)
    wts = (norm1_g[0].reshape(1, D_MODEL), norm2_g[0].reshape(1, D_MODEL), final_g.reshape(1, D_MODEL),
           w1, wif_row, bif_row, w_conv[0], b_conv[0].reshape(1, 2 * ML_WIDTH),
           ml_norm_g[0].reshape(1, ML_WIDTH),
           wgab, wa, wb, wout, wfg, wfu, wfd)

    zeros = functools.partial(jnp.zeros, dtype=F32)
    y_p, st_p = _group(
        x_prompt, mod3[:bp], None, None,
        zeros((bp, ML_HEADS, ML_HEAD_DIM, ML_HEAD_DIM)), zeros((bp, ML_HEADS, ML_HEAD_DIM)),
        zeros((bp, ML_HEADS)), zeros((bp, CONV_W - 1, 2 * ML_WIDTH)), wts,
        bt=1, lt_in=512, lt=256, tq=ATTN_BLOCK, n_past=None, lc=ML_CHUNK)
    def cache_t(c):
        return jnp.transpose(c[0], (0, 2, 3, 1)).reshape(bs, SB_WIDTH, past)

    y_s, st_s = _group(
        x_sample, mod3[bp:], cache_t(cache_sb_k), cache_t(cache_sb_v),
        state_mlstm_C[0], state_mlstm_n[0], state_mlstm_m[0], state_conv[0], wts,
        bt=4, lt_in=seq_s, lt=seq_s, tq=seq_s, n_past=past // ATTN_BLOCK, lc=seq_s)
    return (y_p, y_s) + st_p + st_s
```

```python
import functools
import math

import jax
import jax.numpy as jnp
from jax import lax
from jax.experimental import pallas as pl
from jax.experimental.pallas import tpu as pltpu

D_MODEL = 1024
SB_HEADS = 8
SB_HEAD_DIM = 64
SB_WIDTH = SB_HEADS * SB_HEAD_DIM
ML_HEADS = 4
ML_HEAD_DIM = 128
ML_WIDTH = ML_HEADS * ML_HEAD_DIM
CONV_W = 4
D_FF = 2816
EPS = 1e-6
LANES = 128
SUBLANES = 8
ATTN_BLOCK = 128
ML_CHUNK = 128
ML_SEQS = 8
STABILISER_MARGIN = 2.0 ** -7
EXP_UNDERFLOW = 88.0
SOFTPLUS_CLAMP = 60.0
MASKED_SCORE = -1e30
ATTN_ROWS = 256
ATTN_GROUPS = 4
V7X_VMEM_BYTES = 64 * 1024 * 1024
VMEM_LIMIT = V7X_VMEM_BYTES - 8 * 1024 * 1024

BF16 = jnp.bfloat16
F32 = jnp.float32

_C_Q, _C_K, _C_V = 0, SB_WIDTH, 2 * SB_WIDTH
_C_MQK = 3 * SB_WIDTH
_C_MV = _C_MQK + 2 * ML_WIDTH
_C_MO = _C_MV + ML_WIDTH
_C_IF = _C_MO + ML_WIDTH
_C_GAB = _C_IF + 2 * ML_HEADS
_C_END = _C_GAB + 2 * D_MODEL


def _sigmoid(x):
    return 1.0 / (1.0 + jnp.exp(-x))


def _log_sigmoid(x):
    return jnp.minimum(x, 0.0) - jnp.log1p(jnp.exp(-jnp.abs(x)))


def _softplus_pos(x):
    return jnp.maximum(jnp.log(1.0 + jnp.exp(jnp.minimum(x, SOFTPLUS_CLAMP))), x)


def _dot(a, b):
    return jnp.dot(a, b, preferred_element_type=F32)


def _dot_nt(a, b):
    return lax.dot_general(a, b, (((1,), (1,)), ((), ())), preferred_element_type=F32)


def _split_dot(x, tri, parts):
    acc = None
    rem = x
    for i in range(parts):
        piece = rem.astype(BF16)
        term = _dot(piece, tri)
        acc = term if acc is None else acc + term
        if i + 1 < parts:
            rem = rem - piece.astype(F32)
    return acc


def _norm_mod(x, g, sc, sh):
    ms = jnp.mean(x * x, axis=-1, keepdims=True)
    return (x * lax.rsqrt(ms + EPS)) * (g * (1.0 + sc)) + sh


def _const_spec(shape):
    nd = len(shape)
    return pl.BlockSpec(shape, lambda *_: (0,) * nd, pipeline_mode=pl.Buffered(1))


def _params(n_axes):
    return pltpu.CompilerParams(dimension_semantics=("arbitrary",) * n_axes,
                                vmem_limit_bytes=VMEM_LIMIT)


def _ada_kernel(c_ref, w_ref, b_ref, o_ref):
    c = c_ref[...]
    s = c * _sigmoid(c)
    o_ref[...] = _dot(s.astype(BF16), w_ref[...].astype(BF16)) + b_ref[...]


def _ada(c_all, w_ada, b_ada):
    nb = c_all.shape[0]
    n_out = w_ada.shape[1]
    tn = D_MODEL
    return pl.pallas_call(
        _ada_kernel,
        grid=(n_out // tn,),
        in_specs=[pl.BlockSpec((nb, D_MODEL), lambda j: (0, 0)),
                  pl.BlockSpec((D_MODEL, tn), lambda j: (0, j)),
                  pl.BlockSpec((1, tn), lambda j: (0, j))],
        out_specs=pl.BlockSpec((nb, tn), lambda j: (0, j)),
        out_shape=jax.ShapeDtypeStruct((nb, n_out), F32),
        compiler_params=_params(1),
        name="ada",
    )(c_all, w_ada, b_ada.reshape(1, n_out))


PREP_STEPS = 8
W1_BLOCK = 512


def _prep_kernel(w1_ref, wgab_ref, wif_ref, wa_ref, wb_ref, wout_ref, wfg_ref, wfu_ref, wfd_ref,
                 w1_o, wgab_o, wif_o, wa_o, wb_o, wout_o, wfg_o, wfu_o, wfd_o):
    w1_o[...] = w1_ref[...].T.astype(BF16)
    for src, dst in ((wgab_ref, wgab_o), (wa_ref, wa_o), (wb_ref, wb_o), (wout_ref, wout_o),
                     (wfg_ref, wfg_o), (wfu_ref, wfu_o), (wfd_ref, wfd_o)):
        dst[...] = src[...].astype(BF16)
    gates = wif_ref[...]
    pad = jnp.zeros((SUBLANES - ML_HEADS, D_MODEL), F32)
    wif_o[...] = jnp.concatenate([gates[:ML_HEADS], pad, gates[ML_HEADS:], pad], axis=0).astype(BF16)


def _prep(w_in_t, w_a, w_b, w_out, w_fg, w_fu, w_fd):
    def rows(a, n_rows=None, first=0):
        n_rows = a.shape[0] if n_rows is None else n_rows
        step = n_rows // PREP_STEPS
        assert n_rows % (PREP_STEPS * 2 * SUBLANES) == 0 and first % SUBLANES == 0
        if first == 0:
            return pl.BlockSpec((step, a.shape[1]), lambda i: (i, 0))
        return pl.BlockSpec((pl.Element(step), pl.Element(a.shape[1])),
                            lambda i: (pl.multiple_of(first + i * step, SUBLANES), 0))

    def out(n_rows, n_cols):
        return (jax.ShapeDtypeStruct((n_rows, n_cols), BF16),
                pl.BlockSpec((n_rows // PREP_STEPS, n_cols), lambda i: (i, 0)))

    last = _C_IF // W1_BLOCK - 1
    assert _C_IF % W1_BLOCK == 0 and last < PREP_STEPS
    w1_in = pl.BlockSpec((W1_BLOCK, D_MODEL), lambda i: (jnp.minimum(i, last), 0))
    w1_out = (jax.ShapeDtypeStruct((D_MODEL, _C_IF), BF16),
              pl.BlockSpec((D_MODEL, W1_BLOCK), lambda i: (0, jnp.minimum(i, last))))

    others = (w_a, w_b, w_out, w_fg, w_fu, w_fd)
    outs = [w1_out, out(_C_END - _C_GAB, D_MODEL),
            (jax.ShapeDtypeStruct((2 * SUBLANES, D_MODEL), BF16),
             pl.BlockSpec((2 * SUBLANES, D_MODEL), lambda i: (0, 0)))] + [out(*a.shape) for a in others]
    in_specs = [w1_in, rows(w_in_t, _C_END - _C_GAB, _C_GAB),
                pl.BlockSpec((pl.Element(2 * ML_HEADS), pl.Element(D_MODEL)), lambda i: (_C_IF, 0))
                ] + [rows(a) for a in others]
    return pl.pallas_call(
        _prep_kernel,
        grid=(PREP_STEPS,),
        in_specs=in_specs,
        out_specs=[o[1] for o in outs],
        out_shape=[o[0] for o in outs],
        compiler_params=_params(1),
        name="prep",
    )(w_in_t, w_in_t, w_in_t, *others)


def _tokens_on_lanes(tile_len):
    return tile_len % LANES == 0


def _inproj_kernel(x_ref, mod_ref, conv0_ref, g1_ref, w1_ref, wifr_ref, bifr_ref,
                   wconv_ref, bconv_ref,
                   q_ref, k32_ref, v32_ref, kb_ref, vb_ref, mq_ref, mk_ref, mv_ref, mo_ref,
                   grow_ref, convn_ref,
                   u_sc, xp_sc, *, bt, lt):
    @pl.when(pl.program_id(1) == 0)
    def _():
        for j in range(bt):
            xp_sc[j] = jnp.concatenate(
                [jnp.zeros((SUBLANES - (CONV_W - 1), 2 * ML_WIDTH), F32), conv0_ref[j]], axis=0)

    g1 = g1_ref[...]
    for j in range(bt):
        u = _norm_mod(x_ref[j], g1, mod_ref[j, 1:2, :], mod_ref[j, 0:1, :])
        u_sc[j * lt:(j + 1) * lt, :] = u.astype(BF16)
    u = u_sc[...]

    def mm(lo, hi):
        return _dot(u, w1_ref[:, lo:hi])

    def new_rows(t):
        return t.T if _tokens_on_lanes(lt) else t

    wconv = wconv_ref[...]
    bconv = bconv_ref[...]
    k_scale = 1.0 / math.sqrt(ML_HEAD_DIM)
    mqk = mm(_C_MQK, _C_MV)
    row8 = lax.broadcasted_iota(jnp.int32, (SUBLANES, 2 * ML_WIDTH), 0)
    for j in range(bt):
        x = mqk[j * lt:(j + 1) * lt]
        before = xp_sc[j]
        conv = bconv + x * wconv[CONV_W - 1:CONV_W, :]
        for k in range(1, CONV_W):
            xr = pltpu.roll(x, k, axis=0)
            head = jnp.where(row8 < k, pltpu.roll(before, k, axis=0), xr[:SUBLANES])
            xk = jnp.concatenate([head, xr[SUBLANES:]], axis=0)
            conv = conv + xk * wconv[CONV_W - 1 - k:CONV_W - k, :]
        conv = conv * _sigmoid(conv)
        mq_ref[j] = conv[:, :ML_WIDTH].astype(BF16)
        mk_ref[j] = (conv[:, ML_WIDTH:] * k_scale).astype(BF16)
        convn_ref[j] = x[lt - (CONV_W - 1):]
        xp_sc[j] = x[lt - SUBLANES:]

    sq = mm(_C_Q, _C_K)
    sk = mm(_C_K, _C_V)
    sv = mm(_C_V, _C_MQK)
    mv = mm(_C_MV, _C_MO)
    mo = mm(_C_MO, _C_IF)
    for j in range(bt):
        rows = slice(j * lt, (j + 1) * lt)
        q_ref[j] = sq[rows].astype(BF16)
        k32_ref[j] = new_rows(sk[rows])
        kb_ref[j] = sk[rows].astype(BF16)
        v32_ref[j] = new_rows(sv[rows])
        vb_ref[j] = sv[rows].astype(BF16)
        mv_ref[j] = mv[rows].astype(BF16)
    for j in range(bt):
        rows = slice(j * lt, (j + 1) * lt)
        mo_ref[j] = mo[rows]
        gr = _dot_nt(wifr_ref[...], u_sc[rows, :]) + bifr_ref[...]
        row = lax.broadcasted_iota(jnp.int32, gr.shape, 0)
        grow_ref[j] = jnp.where(row >= SUBLANES, _log_sigmoid(gr), gr)


def _inproj(x, mod3, conv0, g1, w1, wif_row, bif_row, wconv, bconv, *, bt, lt):
    nb, seq, _ = x.shape
    assert nb % bt == 0 and seq % lt == 0 and (bt == 1 or lt == seq)
    grid = (nb // bt, seq // lt)

    def tok(width):
        return pl.BlockSpec((bt, lt, width), lambda b, l: (b, l, 0))

    def per_batch(rows, width):
        return pl.BlockSpec((bt, rows, width), lambda b, l: (b, 0, 0))

    in_specs = [tok(D_MODEL), per_batch(6, D_MODEL), per_batch(CONV_W - 1, 2 * ML_WIDTH),
                _const_spec((1, D_MODEL)), _const_spec(w1.shape),
                _const_spec((2 * SUBLANES, D_MODEL)),
                _const_spec((2 * SUBLANES, 1)),
                _const_spec((CONV_W, 2 * ML_WIDTH)), _const_spec((1, 2 * ML_WIDTH))]

    def tok_on_lanes(rows):
        return pl.BlockSpec((bt, rows, lt), lambda b, l: (b, 0, l))

    kv_spec = tok_on_lanes(SB_WIDTH) if _tokens_on_lanes(lt) else tok(SB_WIDTH)
    kv_shape = (nb, SB_WIDTH, seq) if _tokens_on_lanes(lt) else (nb, seq, SB_WIDTH)
    out_specs = [tok(SB_WIDTH), kv_spec, kv_spec, tok(SB_WIDTH), tok(SB_WIDTH)] + [tok(ML_WIDTH)] * 4 + [
        tok_on_lanes(2 * SUBLANES), per_batch(CONV_W - 1, 2 * ML_WIDTH)]

    def sds(shape, dt):
        return jax.ShapeDtypeStruct(shape, dt)

    out_shape = [sds((nb, seq, SB_WIDTH), BF16), sds(kv_shape, F32), sds(kv_shape, F32),
                 sds((nb, seq, SB_WIDTH), BF16), sds((nb, seq, SB_WIDTH), BF16),
                 sds((nb, seq, ML_WIDTH), BF16), sds((nb, seq, ML_WIDTH), BF16), sds((nb, seq, ML_WIDTH), BF16),
                 sds((nb, seq, ML_WIDTH), F32),
                 sds((nb, 2 * SUBLANES, seq), F32),
                 sds((nb, CONV_W - 1, 2 * ML_WIDTH), F32)]
    return pl.pallas_call(
        functools.partial(_inproj_kernel, bt=bt, lt=lt),
        grid=grid,
        in_specs=in_specs,
        out_specs=out_specs,
        out_shape=out_shape,
        scratch_shapes=[pltpu.VMEM((bt * lt, D_MODEL), BF16),
                        pltpu.VMEM((bt, SUBLANES, 2 * ML_WIDTH), F32)],
        compiler_params=_params(2),
        name="inproj",
    )(x, mod3, conv0, g1, w1, wif_row, bif_row, wconv, bconv)


def _cum_matrix(t):
    r = lax.broadcasted_iota(jnp.int32, (2 * t, LANES + t), 0)
    c = lax.broadcasted_iota(jnp.int32, (2 * t, LANES + t), 1)
    return jnp.where((c < LANES) | ((r & (t - 1)) >= (c - LANES)), 1.0, 0.0).astype(BF16)


def _attn_kernel(q_ref, kd_ref, vd_ref, kp_ref, vp_ref, o_ref, qm_sc, acc_sc, carry_sc, kbuf, vbuf, dma_sem,
                 *, bt, tq, tk, n_past):
    cache_past = n_past is not None
    n_blocks = n_past if cache_past else pl.program_id(1)
    scale = 1.0 / math.sqrt(SB_HEAD_DIM)
    n_pairs = SB_WIDTH // LANES
    lane = lax.broadcasted_iota(jnp.int32, (1, LANES), 1)
    first_head = lane < SB_HEAD_DIM

    n_units = bt * n_pairs
    for u in range(n_units):
        s, p = divmod(u, n_pairs)
        qp = q_ref[s, :, p * LANES:(p + 1) * LANES] * scale
        zero = jnp.zeros_like(qp)
        qm_sc[(2 * u) * tq:(2 * u + 1) * tq, :] = jnp.where(first_head, qp, zero)
        qm_sc[(2 * u + 1) * tq:(2 * u + 2) * tq, :] = jnp.where(first_head, zero, qp)

    per_group = n_units // ATTN_GROUPS
    groups = [range(g, g + per_group) for g in range(0, n_units, per_group)]
    group_rows = [slice(2 * grp[0] * tq, 2 * (grp[-1] + 1) * tq) for grp in groups]

    def scores(k_of, transposed):
        dot_k = _dot if transposed else _dot_nt
        return [jnp.concatenate([dot_k(qm_sc[2 * u * tq:(2 * u + 2) * tq, :], k_of(u)) for u in grp], axis=0)
                for grp in groups]

    def weights(zs, t, visible):
        cum = _cum_matrix(t)
        if visible is not None:
            zs = [jnp.where(visible[rows], z, MASKED_SCORE) for z, rows in zip(zs, group_rows)]
        sums = []
        for gi, rows in enumerate(group_rows):
            sp = _softplus_pos(zs[gi])
            hi = sp.astype(BF16)
            lo = (sp - hi.astype(F32)).astype(BF16)
            sums.append(_dot(jnp.concatenate([hi, lo], axis=1), cum))
        out = []
        for gi, rows in enumerate(group_rows):
            logit = zs[gi] - sums[gi][:, LANES:]
            if visible is None:
                carry = carry_sc[rows, :]
                a = jnp.exp(logit - carry)
                carry_sc[rows, :] = carry + sums[gi][:, :LANES]
            else:
                a = jnp.exp(logit)
                carry_sc[rows, :] = sums[gi][:, :LANES]
            out.append(a.astype(BF16))
        return out

    def accumulate(a_groups, v_of, transposed, assign=False):
        dot_v = _dot_nt if transposed else _dot
        for gi, grp in enumerate(groups):
            for j, u in enumerate(grp):
                s, p = divmod(u, n_pairs)
                cols = slice(p * LANES, (p + 1) * LANES)
                pv = dot_v(a_groups[gi][2 * j * tq:(2 * j + 2) * tq, :], v_of(u))
                out = jnp.where(first_head, pv[:tq], pv[tq:])
                if assign:
                    acc_sc[s, :, cols] = out
                else:
                    acc_sc[s, :, cols] += out

    def unit_cols(ref):
        def of(u):
            s, p = divmod(u, n_pairs)
            return ref[s, :, p * LANES:(p + 1) * LANES]
        return of

    step = pl.program_id(0)
    on_demand = 4

    def cache_copies(step_index, block_index, buf):
        start = pl.multiple_of(block_index * tk, tk)
        return [pltpu.make_async_copy(src.at[pl.ds(step_index * bt, bt), :, pl.ds(start, tk)],
                                      dst.at[buf], dma_sem.at[buf, n])
                for n, (src, dst) in enumerate(((kp_ref, kbuf), (vp_ref, vbuf)))]

    def newest_pair(step_index):
        half = 2 * (step_index % 2)
        return cache_copies(step_index, n_blocks - 1, half) + cache_copies(step_index, n_blocks - 2, half + 1)

    if cache_past:
        @pl.when(step == 0)
        def _():
            for copy in newest_pair(step):
                copy.start()

        @pl.when(step + 1 < pl.num_programs(0))
        def _():
            for copy in newest_pair(step + 1):
                copy.start()

    def past_of(ref, block_index):
        start = pl.multiple_of(block_index * tk, tk)

        def of(u):
            s, p = divmod(u, n_pairs)
            if cache_past:
                return ref[s, p * LANES:(p + 1) * LANES, :].astype(BF16)
            return ref[s, pl.ds(start, tk), p * LANES:(p + 1) * LANES].astype(BF16)
        return of

    rd = lax.broadcasted_iota(jnp.int32, (2 * n_units * tq, tq), 0) & (tq - 1)
    cd = lax.broadcasted_iota(jnp.int32, (2 * n_units * tq, tq), 1)
    a_diag = weights(scores(unit_cols(kd_ref), False), tq, cd < rd)
    accumulate(a_diag, unit_cols(vd_ref), False, assign=True)

    def not_done():
        return jnp.min(carry_sc[...]) <= EXP_UNDERFLOW

    def past_block(i, buf=None):
        block_index = n_blocks - 1 - i
        if cache_past:
            if buf is None:
                buf = on_demand
                copies = cache_copies(step, block_index, buf)
                for copy in copies:
                    copy.start()
                for copy in copies:
                    copy.wait()
            k_src, v_src = kbuf.at[buf], vbuf.at[buf]
        else:
            k_src, v_src = kp_ref, vp_ref
        a = weights(scores(past_of(k_src, block_index), cache_past), tk, None)
        accumulate(a, past_of(v_src, block_index), cache_past)

    def pair_cond(state):
        i, go = state
        return jnp.logical_and(i + 1 < n_blocks, go)

    def pair_body(state):
        i, _ = state
        past_block(i)
        past_block(i + 1)
        return i + 2, not_done()

    def cond(state):
        i, go = state
        return jnp.logical_and(i < n_blocks, go)

    def body(state):
        i, _ = state
        past_block(i)
        return i + 1, not_done()

    if cache_past:
        for copy in newest_pair(step):
            copy.wait()
        half = 2 * (step % 2)
        past_block(0, half)
        past_block(1, half + 1)
        state = (jnp.int32(2), not_done())
    else:
        state = (jnp.int32(0), not_done())
    state = lax.while_loop(pair_cond, pair_body, state)
    lax.while_loop(cond, body, state)
    o_ref[...] = acc_sc[...].astype(BF16)


def _attn(q, k_new, v_new, k_past, v_past, *, bt, tq, n_past):
    nb, seq, _ = q.shape
    assert nb % bt == 0
    tok = pl.BlockSpec((bt, tq, SB_WIDTH), lambda b, i: (b, i, 0))
    if n_past is None:
        past = pl.BlockSpec((bt,) + k_past.shape[1:], lambda b, i: (b, 0, 0))
    else:
        past = pl.BlockSpec(memory_space=pl.ANY)
        assert seq == tq and n_past >= 2
    rows = bt * SB_HEADS * tq
    n_bufs = 5
    buf_shape = (n_bufs, bt, SB_WIDTH, ATTN_BLOCK) if n_past is not None else (n_bufs, 1, SUBLANES, LANES)
    block_buf = pltpu.VMEM(buf_shape, F32)
    return pl.pallas_call(
        functools.partial(_attn_kernel, bt=bt, tq=tq, tk=ATTN_BLOCK, n_past=n_past),
        grid=(nb // bt, seq // tq),
        in_specs=[tok, tok, tok, past, past],
        out_specs=tok,
        out_shape=jax.ShapeDtypeStruct((nb, seq, SB_WIDTH), BF16),
        scratch_shapes=[pltpu.VMEM((rows, LANES), BF16),
                        pltpu.VMEM((bt, tq, SB_WIDTH), F32),
                        pltpu.VMEM((rows, LANES), F32),
                        block_buf, block_buf, pltpu.SemaphoreType.DMA((n_bufs, 2))],
        compiler_params=_params(2),
        name="attn",
    )(q, k_new, v_new, k_past, v_past)


def _mlstm_kernel(mq_ref, mk_ref, mv_ref, mo_ref, grow_ref, c0_ref, n0_ref, m0_ref, gn_ref,
                  yb_ref, c_out_ref, n_out_ref, m_out_ref,
                  c_sc, n_sc, m_sc, alpha_sc, beta_sc, r_sc, wg_sc, dec_sc, mprev_sc, *, bt, lc, nc):
    step = pl.program_id(1)
    dk = ML_HEAD_DIM

    ri = lax.broadcasted_iota(jnp.int32, (lc, lc), 0)
    ci = lax.broadcasted_iota(jnp.int32, (lc, lc), 1)
    causal = ci <= ri
    eye = jnp.where(ri == ci, 1.0, 0.0).astype(BF16)
    ones_blk = jnp.ones((lc, dk), BF16)

    @pl.when(step == 0)
    def _():
        c_sc[...] = c0_ref[...]
        n_sc[...] = n0_ref[...]
        incl_t = jnp.where(ri <= ci, 1.0, 0.0).astype(BF16)
        blocks = [(s, c) for s in range(bt) for c in range(nc)]
        li = jnp.concatenate([grow_ref[s, 0:SUBLANES, c * lc:(c + 1) * lc] for s, c in blocks], axis=0)
        lf = jnp.concatenate([grow_ref[s, SUBLANES:2 * SUBLANES, c * lc:(c + 1) * lc] for s, c in blocks],
                             axis=0)
        b = _split_dot(lf, incl_t, 3)
        r = li - b
        lane = lax.broadcasted_iota(jnp.int32, r.shape, 1)
        run = r
        d = 1
        while d < lc:
            run = jnp.maximum(run, jnp.where(lane >= d, pltpu.roll(run, d, axis=1), -jnp.inf))
            d *= 2
        b_last = b[:, lc - 1:lc]
        g = b_last - b + li
        g_max = jnp.max(g, axis=1, keepdims=True)
        m_before = []
        m_after = []
        for s in range(bt):
            m = m0_ref[s]
            for c in range(nc):
                rows = slice((s * nc + c) * SUBLANES, (s * nc + c + 1) * SUBLANES)
                m_before.append(m)
                m = jnp.maximum(b_last[rows] + m, g_max[rows])
                m_after.append(m)
            m_sc[s] = m
        m_prev = jnp.concatenate(m_before, axis=0)
        m_new = jnp.concatenate(m_after, axis=0)
        top = jnp.maximum(run, m_prev[:, :lc])
        alpha = (-(top + jnp.abs(top) * STABILISER_MARGIN)).astype(BF16).astype(F32)
        alpha_sc[...] = alpha
        beta_sc[...] = alpha - b
        r_sc[...] = r
        wg_sc[...] = jnp.exp(g - m_new[:, :lc])
        dec_sc[...] = jnp.exp(b_last + m_prev - m_new)
        mprev_sc[...] = m_prev

    units = [(s, h) for s in range(bt) for h in range(ML_HEADS)]
    cols = [slice(h * dk, (h + 1) * dk) for h in range(ML_HEADS)]
    alpha, beta, r, wg, decay, m_prev = [], [], [], [], [], []
    for s in range(bt):
        chunk = pl.ds(pl.multiple_of((s * nc + step) * SUBLANES, SUBLANES), SUBLANES)
        alpha.append(alpha_sc[chunk, :].astype(BF16))
        beta.append(beta_sc[chunk, :].astype(BF16))
        r.append(r_sc[chunk, :])
        wg.append(wg_sc[chunk, :])
        decay.append(dec_sc[chunk, :])
        m_prev.append(mprev_sc[chunk, :])

    def rows_of(x, h):
        return jnp.broadcast_to(x[h:h + 1, :], (dk, lc))
    alpha_b = _dot_nt(eye, jnp.concatenate([rows_of(alpha[s], h) for s, h in units], axis=0))
    beta_b = _dot_nt(eye, jnp.concatenate([rows_of(beta[s], h) for s, h in units], axis=0))
    qk = [_dot_nt(mq_ref[s, :, cols[h]], mk_ref[s, :, cols[h]]) for s, h in units]
    c_prev = [c_sc[s, h] for s, h in units]
    n_prev = [n_sc[s, h:h + 1, :] for s, h in units]
    qc = [_dot_nt(mq_ref[s, :, cols[h]],
                  jnp.concatenate([c_prev[u].astype(BF16),
                                   jnp.broadcast_to(n_prev[u], (dk, dk)).astype(BF16)], axis=0))
          for u, (s, h) in enumerate(units)]
    upd = []
    for s, h in units:
        wg_h = wg[s][h:h + 1, :]
        vt = mv_ref[s, :, cols[h]].astype(F32).T
        lhs = jnp.concatenate([(vt * wg_h).astype(BF16),
                               jnp.broadcast_to(wg_h, (SUBLANES, lc)).astype(BF16)], axis=0)
        upd.append(_dot(lhs, mk_ref[s, :, cols[h]]))

    sw = []
    for u, (s, h) in enumerate(units):
        w = jnp.where(causal, jnp.exp(alpha_b[:, u * dk:u * dk + lc] + r[s][h:h + 1, :]), 0.0)
        sw.append((qk[u] * w).astype(BF16))
    nv = [_dot(sw[u], jnp.concatenate([mv_ref[s, :, cols[h]], ones_blk], axis=1))
          for u, (s, h) in enumerate(units)]
    hval = []
    for u, (s, h) in enumerate(units):
        ucols = slice(u * dk, (u + 1) * dk)
        w_inter = jnp.exp(m_prev[s][h:h + 1, :] + alpha_b[:, ucols])
        num = nv[u][:, :dk] + w_inter * qc[u][:, :dk]
        den = nv[u][:, dk:] + w_inter * qc[u][:, dk:]
        hval.append(num / jnp.maximum(jnp.abs(den), jnp.exp(beta_b[:, ucols])))
    ms = _dot(jnp.concatenate([(hv * hv).astype(BF16) for hv in hval], axis=0),
              jnp.ones((dk, dk), BF16)) * (1.0 / dk)
    for u, (s, h) in enumerate(units):
        y = hval[u] * lax.rsqrt(ms[u * lc:(u + 1) * lc] + EPS) * gn_ref[:, cols[h]]
        yb_ref[s, :, cols[h]] = (_sigmoid(mo_ref[s, :, cols[h]]) * y).astype(BF16)

    for u, (s, h) in enumerate(units):
        dec = decay[s][h:h + 1, :]
        c_sc[s, h] = dec * c_prev[u] + upd[u][:dk]
        n_sc[s, h:h + 1, :] = dec * n_prev[u] + upd[u][dk:dk + 1]

    c_out_ref[...] = c_sc[...]
    n_out_ref[...] = n_sc[...]
    m_out_ref[...] = m_sc[...]


def _mlstm(mq, mk, mv, mo, grow, c0, n0, m0, gn, *, bt, lc):
    nb, seq, _ = mq.shape
    assert nb % bt == 0
    tok = pl.BlockSpec((bt, lc, ML_WIDTH), lambda b, i: (b, i, 0))
    c_spec = pl.BlockSpec((bt, ML_HEADS, ML_HEAD_DIM, ML_HEAD_DIM), lambda b, i: (b, 0, 0, 0))
    n_spec = pl.BlockSpec((bt, ML_HEADS, ML_HEAD_DIM), lambda b, i: (b, 0, 0))
    m_spec = pl.BlockSpec((bt, SUBLANES, LANES), lambda b, i: (b, 0, 0))
    nc = seq // lc
    table = pltpu.VMEM((bt * nc * SUBLANES, lc), F32)
    table_m = pltpu.VMEM((bt * nc * SUBLANES, LANES), F32)
    return pl.pallas_call(
        functools.partial(_mlstm_kernel, bt=bt, lc=lc, nc=nc),
        grid=(nb // bt, nc),
        in_specs=[tok, tok, tok, tok,
                  pl.BlockSpec((bt, 2 * SUBLANES, seq), lambda b, i: (b, 0, 0)),
                  c_spec, n_spec, m_spec, _const_spec((1, ML_WIDTH))],
        out_specs=[tok, c_spec, n_spec, m_spec],
        out_shape=[jax.ShapeDtypeStruct((nb, seq, ML_WIDTH), BF16),
                   jax.ShapeDtypeStruct((nb, ML_HEADS, ML_HEAD_DIM, ML_HEAD_DIM), F32),
                   jax.ShapeDtypeStruct((nb, ML_HEADS, ML_HEAD_DIM), F32),
                   jax.ShapeDtypeStruct((nb, SUBLANES, LANES), F32)],
        scratch_shapes=[pltpu.VMEM((bt, ML_HEADS, ML_HEAD_DIM, ML_HEAD_DIM), F32),
                        pltpu.VMEM((bt, ML_HEADS, ML_HEAD_DIM), F32),
                        pltpu.VMEM((bt, SUBLANES, LANES), F32),
                        table, table, table, table, table_m, table_m],
        compiler_params=_params(2),
        name="mlstm",
    )(mq, mk, mv, mo, grow, c0, n0, m0, gn)


def _post_kernel(x_ref, ya_ref, yb_ref, mod_ref, g1_ref, g2_ref, gf_ref,
                 wgab_ref, wa_ref, wb_ref, wout_ref, wfg_ref, wfu_ref, wfd_ref,
                 y_ref, u_sc, *, bt, lt):
    g1 = g1_ref[...]
    for j in range(bt):
        u = _norm_mod(x_ref[j], g1, mod_ref[j, 1:2, :], mod_ref[j, 0:1, :])
        u_sc[j * lt:(j + 1) * lt, :] = u.astype(BF16)
    u = u_sc[...]
    ya = ya_ref[...].reshape(bt * lt, SB_WIDTH)
    yb = yb_ref[...].reshape(bt * lt, ML_WIDTH)
    ga = _dot_nt(u, wgab_ref[:D_MODEL, :])
    gb = _dot_nt(u, wgab_ref[D_MODEL:, :])
    merged = _sigmoid(ga) * _dot(ya, wa_ref[...]) + _sigmoid(gb) * _dot(yb, wb_ref[...])
    attn_out = _dot(merged.astype(BF16), wout_ref[...])

    g2 = g2_ref[...]
    for j in range(bt):
        rows = slice(j * lt, (j + 1) * lt)
        x1 = x_ref[j] + mod_ref[j, 2:3, :] * attn_out[rows]
        y_ref[j] = x1
        u2 = _norm_mod(x1, g2, mod_ref[j, 4:5, :], mod_ref[j, 3:4, :])
        u_sc[rows, :] = u2.astype(BF16)
    u2 = u_sc[...]
    hg = _dot(u2, wfg_ref[...])
    hu = _dot(u2, wfu_ref[...])
    act = (hg * _sigmoid(hg) * hu).astype(BF16)
    ff = _dot(act, wfd_ref[...])
    gf = gf_ref[...]
    for j in range(bt):
        rows = slice(j * lt, (j + 1) * lt)
        x2 = y_ref[j] + mod_ref[j, 5:6, :] * ff[rows]
        ms = jnp.mean(x2 * x2, axis=-1, keepdims=True)
        y_ref[j] = x2 * lax.rsqrt(ms + EPS) * gf


def _post(x, ya, yb, mod3, g1, g2, gf, wgab, wa, wb, wout, wfg, wfu, wfd, *, bt, lt):
    nb, seq, _ = x.shape
    assert nb % bt == 0 and seq % lt == 0

    def tok(width):
        return pl.BlockSpec((bt, lt, width), lambda b, l: (b, l, 0))

    in_specs = [tok(D_MODEL), tok(SB_WIDTH), tok(ML_WIDTH),
                pl.BlockSpec((bt, 6, D_MODEL), lambda b, l: (b, 0, 0)),
                _const_spec((1, D_MODEL)), _const_spec((1, D_MODEL)), _const_spec((1, D_MODEL)),
                _const_spec(wgab.shape), _const_spec(wa.shape), _const_spec(wb.shape), _const_spec(wout.shape),
                _const_spec(wfg.shape), _const_spec(wfu.shape), _const_spec(wfd.shape)]
    return pl.pallas_call(
        functools.partial(_post_kernel, bt=bt, lt=lt),
        grid=(nb // bt, seq // lt),
        in_specs=in_specs,
        out_specs=tok(D_MODEL),
        out_shape=jax.ShapeDtypeStruct((nb, seq, D_MODEL), F32),
        scratch_shapes=[pltpu.VMEM((bt * lt, D_MODEL), BF16)],
        compiler_params=_params(2),
        name="post",
    )(x, ya, yb, mod3, g1, g2, gf, wgab, wa, wb, wout, wfg, wfu, wfd)


def _group(x, mod3, k_past, v_past, c0, n0, m0, conv0, wts, *, bt, lt_in, lt, tq, n_past, lc):
    (g1, g2, gf, w1, wif_row, bif_row, wconv, bconv, gn,
     wgab, wa, wb, wout, wfg, wfu, wfd) = wts
    nb, seq, _ = x.shape
    (q, k32, v32, kb, vb, mq, mk, mv, mo, grow, conv_new) = _inproj(
        x, mod3, conv0, g1, w1, wif_row, bif_row, wconv, bconv, bt=bt, lt=lt_in)
    if k_past is None:
        k_past, v_past = kb, vb
    ya = _attn(q, kb, vb, k_past, v_past, bt=ATTN_ROWS // tq, tq=tq, n_past=n_past)
    m0b = jnp.broadcast_to(jnp.pad(m0, ((0, 0), (0, SUBLANES - ML_HEADS)))[:, :, None], (nb, SUBLANES, LANES))
    yb, c1, n1, m1 = _mlstm(mq, mk, mv, mo, grow, c0, n0, m0b, gn, bt=ML_SEQS, lc=lc)
    y = _post(x, ya, yb, mod3, g1, g2, gf, wgab, wa, wb, wout, wfg, wfu, wfd, bt=bt, lt=lt)
    def rows_out(t):
        if _tokens_on_lanes(lt_in):
            return jnp.transpose(t.reshape(nb, SB_HEADS, SB_HEAD_DIM, seq), (0, 3, 1, 2))[None]
        return t.reshape(1, nb, seq, SB_HEADS, SB_HEAD_DIM)

    states = (rows_out(k32), rows_out(v32),
              c1[None], n1[None], m1[None, :, :ML_HEADS, 0], conv_new[None])
    return y, states


def kernel(x_prompt, x_sample, c_prompt, c_sample, cache_sb_k, cache_sb_v, state_mlstm_C, state_mlstm_n, state_mlstm_m, state_conv, norm1_g, norm2_g, w_ada, b_ada, w_in, b_if, w_conv, b_conv, ml_norm_g, w_a, w_b, w_out, w_ff_gate, w_ff_up, w_ff_down, final_g):
    assert w_in.shape[0] == 1, "single layer"
    bp, seq_p, _ = x_prompt.shape
    bs, seq_s, _ = x_sample.shape
    past = cache_sb_k.shape[2]

    c_all = jnp.concatenate([c_prompt, c_sample], axis=0)
    mod3 = _ada(c_all, w_ada[0], b_ada[0]).reshape(bp + bs, 6, D_MODEL)

    w1, wgab, wif_row, wa, wb, wout, wfg, wfu, wfd = _prep(
        jnp.transpose(w_in[0]), w_a[0], w_b[0], w_out[0], w_ff_gate[0], w_ff_up[0], w_ff_down[0])
    pad = SUBLANES - ML_HEADS
    bif_row = jnp.pad(b_if[0].reshape(2, ML_HEADS), ((0, 0), (0, pad))).reshape(2 * SUBLANES, 1)
    wts = (norm1_g[0].reshape(1, D_MODEL), norm2_g[0].reshape(1, D_MODEL), final_g.reshape(1, D_MODEL),
           w1, wif_row, bif_row, w_conv[0], b_conv[0].reshape(1, 2 * ML_WIDTH),
           ml_norm_g[0].reshape(1, ML_WIDTH),
           wgab, wa, wb, wout, wfg, wfu, wfd)

    zeros = functools.partial(jnp.zeros, dtype=F32)
    y_p, st_p = _group(
        x_prompt, mod3[:bp], None, None,
        zeros((bp, ML_HEADS, ML_HEAD_DIM, ML_HEAD_DIM)), zeros((bp, ML_HEADS, ML_HEAD_DIM)),
        zeros((bp, ML_HEADS)), zeros((bp, CONV_W - 1, 2 * ML_WIDTH)), wts,
        bt=1, lt_in=512, lt=256, tq=ATTN_BLOCK, n_past=None, lc=ML_CHUNK)
    def cache_t(c):
        return jnp.transpose(c[0], (0, 2, 3, 1)).reshape(bs, SB_WIDTH, past)

    y_s, st_s = _group(
        x_sample, mod3[bp:], cache_t(cache_sb_k), cache_t(cache_sb_v),
        state_mlstm_C[0], state_mlstm_n[0], state_mlstm_m[0], state_conv[0], wts,
        bt=4, lt_in=seq_s, lt=seq_s, tq=seq_s, n_past=past // ATTN_BLOCK, lc=seq_s)
    return (y_p, y_s) + st_p + st_s
```

```python
import functools
import math

import jax
import jax.numpy as jnp
from jax import lax
from jax.experimental import pallas as pl
from jax.experimental.pallas import tpu as pltpu

D_MODEL = 1024
SB_HEADS = 8
SB_HEAD_DIM = 64
SB_WIDTH = SB_HEADS * SB_HEAD_DIM
ML_HEADS = 4
ML_HEAD_DIM = 128
ML_WIDTH = ML_HEADS * ML_HEAD_DIM
CONV_W = 4
D_FF = 2816
EPS = 1e-6
LANES = 128
SUBLANES = 8
ATTN_BLOCK = 128
ML_CHUNK = 128
ML_SEQS = 8
STABILISER_MARGIN = 2.0 ** -7
EXP_UNDERFLOW = 88.0
SOFTPLUS_CLAMP = 60.0
MASKED_SCORE = -1e30
ATTN_ROWS = 256
ATTN_GROUPS = 4
V7X_VMEM_BYTES = 64 * 1024 * 1024
VMEM_LIMIT = V7X_VMEM_BYTES - 8 * 1024 * 1024

BF16 = jnp.bfloat16
F32 = jnp.float32

_C_Q, _C_K, _C_V = 0, SB_WIDTH, 2 * SB_WIDTH
_C_MQK = 3 * SB_WIDTH
_C_MV = _C_MQK + 2 * ML_WIDTH
_C_MO = _C_MV + ML_WIDTH
_C_IF = _C_MO + ML_WIDTH
_C_GAB = _C_IF + 2 * ML_HEADS
_C_END = _C_GAB + 2 * D_MODEL


def _sigmoid(x):
    return 1.0 / (1.0 + jnp.exp(-x))


def _log_sigmoid(x):
    return jnp.minimum(x, 0.0) - jnp.log1p(jnp.exp(-jnp.abs(x)))


def _softplus_pos(x):
    return jnp.maximum(jnp.log(1.0 + jnp.exp(jnp.minimum(x, SOFTPLUS_CLAMP))), x)


def _dot(a, b):
    return jnp.dot(a, b, preferred_element_type=F32)


def _dot_nt(a, b):
    return lax.dot_general(a, b, (((1,), (1,)), ((), ())), preferred_element_type=F32)


def _split_dot(x, tri, parts):
    acc = None
    rem = x
    for i in range(parts):
        piece = rem.astype(BF16)
        term = _dot(piece, tri)
        acc = term if acc is None else acc + term
        if i + 1 < parts:
            rem = rem - piece.astype(F32)
    return acc


def _norm_mod(x, g, sc, sh):
    ms = jnp.mean(x * x, axis=-1, keepdims=True)
    return (x * lax.rsqrt(ms + EPS)) * (g * (1.0 + sc)) + sh


def _const_spec(shape):
    nd = len(shape)
    return pl.BlockSpec(shape, lambda *_: (0,) * nd, pipeline_mode=pl.Buffered(1))


def _params(n_axes):
    return pltpu.CompilerParams(dimension_semantics=("arbitrary",) * n_axes,
                                vmem_limit_bytes=VMEM_LIMIT)


def _ada_kernel(c_ref, w_ref, b_ref, o_ref):
    c = c_ref[...]
    s = c * _sigmoid(c)
    o_ref[...] = _dot(s.astype(BF16), w_ref[...].astype(BF16)) + b_ref[...]


def _ada(c_all, w_ada, b_ada):
    nb = c_all.shape[0]
    n_out = w_ada.shape[1]
    tn = D_MODEL
    return pl.pallas_call(
        _ada_kernel,
        grid=(n_out // tn,),
        in_specs=[pl.BlockSpec((nb, D_MODEL), lambda j: (0, 0)),
                  pl.BlockSpec((D_MODEL, tn), lambda j: (0, j)),
                  pl.BlockSpec((1, tn), lambda j: (0, j))],
        out_specs=pl.BlockSpec((nb, tn), lambda j: (0, j)),
        out_shape=jax.ShapeDtypeStruct((nb, n_out), F32),
        compiler_params=_params(1),
        name="ada",
    )(c_all, w_ada, b_ada.reshape(1, n_out))


PREP_STEPS = 8
W1_BLOCK = 512


def _prep_kernel(w1_ref, wgab_ref, wif_ref, wa_ref, wb_ref, wout_ref, wfg_ref, wfu_ref, wfd_ref,
                 w1_o, wgab_o, wif_o, wa_o, wb_o, wout_o, wfg_o, wfu_o, wfd_o):
    w1_o[...] = w1_ref[...].T.astype(BF16)
    for src, dst in ((wgab_ref, wgab_o), (wa_ref, wa_o), (wb_ref, wb_o), (wout_ref, wout_o),
                     (wfg_ref, wfg_o), (wfu_ref, wfu_o), (wfd_ref, wfd_o)):
        dst[...] = src[...].astype(BF16)
    gates = wif_ref[...]
    pad = jnp.zeros((SUBLANES - ML_HEADS, D_MODEL), F32)
    wif_o[...] = jnp.concatenate([gates[:ML_HEADS], pad, gates[ML_HEADS:], pad], axis=0).astype(BF16)


def _prep(w_in_t, w_a, w_b, w_out, w_fg, w_fu, w_fd):
    def rows(a, n_rows=None, first=0):
        n_rows = a.shape[0] if n_rows is None else n_rows
        step = n_rows // PREP_STEPS
        assert n_rows % (PREP_STEPS * 2 * SUBLANES) == 0 and first % SUBLANES == 0
        if first == 0:
            return pl.BlockSpec((step, a.shape[1]), lambda i: (i, 0))
        return pl.BlockSpec((pl.Element(step), pl.Element(a.shape[1])),
                            lambda i: (pl.multiple_of(first + i * step, SUBLANES), 0))

    def out(n_rows, n_cols):
        return (jax.ShapeDtypeStruct((n_rows, n_cols), BF16),
                pl.BlockSpec((n_rows // PREP_STEPS, n_cols), lambda i: (i, 0)))

    last = _C_IF // W1_BLOCK - 1
    assert _C_IF % W1_BLOCK == 0 and last < PREP_STEPS
    w1_in = pl.BlockSpec((W1_BLOCK, D_MODEL), lambda i: (jnp.minimum(i, last), 0))
    w1_out = (jax.ShapeDtypeStruct((D_MODEL, _C_IF), BF16),
              pl.BlockSpec((D_MODEL, W1_BLOCK), lambda i: (0, jnp.minimum(i, last))))

    others = (w_a, w_b, w_out, w_fg, w_fu, w_fd)
    outs = [w1_out, out(_C_END - _C_GAB, D_MODEL),
            (jax.ShapeDtypeStruct((2 * SUBLANES, D_MODEL), BF16),
             pl.BlockSpec((2 * SUBLANES, D_MODEL), lambda i: (0, 0)))] + [out(*a.shape) for a in others]
    in_specs = [w1_in, rows(w_in_t, _C_END - _C_GAB, _C_GAB),
                pl.BlockSpec((pl.Element(2 * ML_HEADS), pl.Element(D_MODEL)), lambda i: (_C_IF, 0))
                ] + [rows(a) for a in others]
    return pl.pallas_call(
        _prep_kernel,
        grid=(PREP_STEPS,),
        in_specs=in_specs,
        out_specs=[o[1] for o in outs],
        out_shape=[o[0] for o in outs],
        compiler_params=_params(1),
        name="prep",
    )(w_in_t, w_in_t, w_in_t, *others)


def _tokens_on_lanes(tile_len):
    return tile_len % LANES == 0


def _inproj_kernel(x_ref, mod_ref, conv0_ref, g1_ref, w1_ref, wifr_ref, bifr_ref,
                   wconv_ref, bconv_ref,
                   q_ref, k32_ref, v32_ref, kb_ref, vb_ref, mq_ref, mk_ref, mv_ref, mo_ref,
                   grow_ref, convn_ref,
                   u_sc, xp_sc, *, bt, lt):
    @pl.when(pl.program_id(1) == 0)
    def _():
        for j in range(bt):
            xp_sc[j] = jnp.concatenate(
                [jnp.zeros((SUBLANES - (CONV_W - 1), 2 * ML_WIDTH), F32), conv0_ref[j]], axis=0)

    g1 = g1_ref[...]
    for j in range(bt):
        u = _norm_mod(x_ref[j], g1, mod_ref[j, 1:2, :], mod_ref[j, 0:1, :])
        u_sc[j * lt:(j + 1) * lt, :] = u.astype(BF16)
    u = u_sc[...]

    def mm(lo, hi):
        return _dot(u, w1_ref[:, lo:hi])

    def new_rows(t):
        return t.T if _tokens_on_lanes(lt) else t

    wconv = wconv_ref[...]
    bconv = bconv_ref[...]
    k_scale = 1.0 / math.sqrt(ML_HEAD_DIM)
    mqk = mm(_C_MQK, _C_MV)
    row8 = lax.broadcasted_iota(jnp.int32, (SUBLANES, 2 * ML_WIDTH), 0)
    for j in range(bt):
        x = mqk[j * lt:(j + 1) * lt]
        before = xp_sc[j]
        conv = bconv + x * wconv[CONV_W - 1:CONV_W, :]
        for k in range(1, CONV_W):
            xr = pltpu.roll(x, k, axis=0)
            head = jnp.where(row8 < k, pltpu.roll(before, k, axis=0), xr[:SUBLANES])
            xk = jnp.concatenate([head, xr[SUBLANES:]], axis=0)
            conv = conv + xk * wconv[CONV_W - 1 - k:CONV_W - k, :]
        conv = conv * _sigmoid(conv)
        mq_ref[j] = conv[:, :ML_WIDTH].astype(BF16)
        mk_ref[j] = (conv[:, ML_WIDTH:] * k_scale).astype(BF16)
        convn_ref[j] = x[lt - (CONV_W - 1):]
        xp_sc[j] = x[lt - SUBLANES:]

    sq = mm(_C_Q, _C_K)
    sk = mm(_C_K, _C_V)
    sv = mm(_C_V, _C_MQK)
    mv = mm(_C_MV, _C_MO)
    mo = mm(_C_MO, _C_IF)
    for j in range(bt):
        rows = slice(j * lt, (j + 1) * lt)
        q_ref[j] = sq[rows].astype(BF16)
        k32_ref[j] = new_rows(sk[rows])
        kb_ref[j] = sk[rows].astype(BF16)
        v32_ref[j] = new_rows(sv[rows])
        vb_ref[j] = sv[rows].astype(BF16)
        mv_ref[j] = mv[rows].astype(BF16)
    for j in range(bt):
        rows = slice(j * lt, (j + 1) * lt)
        mo_ref[j] = mo[rows]
        gr = _dot_nt(wifr_ref[...], u_sc[rows, :]) + bifr_ref[...]
        row = lax.broadcasted_iota(jnp.int32, gr.shape, 0)
        grow_ref[j] = jnp.where(row >= SUBLANES, _log_sigmoid(gr), gr)


def _inproj(x, mod3, conv0, g1, w1, wif_row, bif_row, wconv, bconv, *, bt, lt):
    nb, seq, _ = x.shape
    assert nb % bt == 0 and seq % lt == 0 and (bt == 1 or lt == seq)
    grid = (nb // bt, seq // lt)

    def tok(width):
        return pl.BlockSpec((bt, lt, width), lambda b, l: (b, l, 0))

    def per_batch(rows, width):
        return pl.BlockSpec((bt, rows, width), lambda b, l: (b, 0, 0))

    in_specs = [tok(D_MODEL), per_batch(6, D_MODEL), per_batch(CONV_W - 1, 2 * ML_WIDTH),
                _const_spec((1, D_MODEL)), _const_spec(w1.shape),
                _const_spec((2 * SUBLANES, D_MODEL)),
                _const_spec((2 * SUBLANES, 1)),
                _const_spec((CONV_W, 2 * ML_WIDTH)), _const_spec((1, 2 * ML_WIDTH))]

    def tok_on_lanes(rows):
        return pl.BlockSpec((bt, rows, lt), lambda b, l: (b, 0, l))

    kv_spec = tok_on_lanes(SB_WIDTH) if _tokens_on_lanes(lt) else tok(SB_WIDTH)
    kv_shape = (nb, SB_WIDTH, seq) if _tokens_on_lanes(lt) else (nb, seq, SB_WIDTH)
    out_specs = [tok(SB_WIDTH), kv_spec, kv_spec, tok(SB_WIDTH), tok(SB_WIDTH)] + [tok(ML_WIDTH)] * 4 + [
        tok_on_lanes(2 * SUBLANES), per_batch(CONV_W - 1, 2 * ML_WIDTH)]

    def sds(shape, dt):
        return jax.ShapeDtypeStruct(shape, dt)

    out_shape = [sds((nb, seq, SB_WIDTH), BF16), sds(kv_shape, F32), sds(kv_shape, F32),
                 sds((nb, seq, SB_WIDTH), BF16), sds((nb, seq, SB_WIDTH), BF16),
                 sds((nb, seq, ML_WIDTH), BF16), sds((nb, seq, ML_WIDTH), BF16), sds((nb, seq, ML_WIDTH), BF16),
                 sds((nb, seq, ML_WIDTH), F32),
                 sds((nb, 2 * SUBLANES, seq), F32),
                 sds((nb, CONV_W - 1, 2 * ML_WIDTH), F32)]
    return pl.pallas_call(
        functools.partial(_inproj_kernel, bt=bt, lt=lt),
        grid=grid,
        in_specs=in_specs,
        out_specs=out_specs,
        out_shape=out_shape,
        scratch_shapes=[pltpu.VMEM((bt * lt, D_MODEL), BF16),
                        pltpu.VMEM((bt, SUBLANES, 2 * ML_WIDTH), F32)],
        compiler_params=_params(2),
        name="inproj",
    )(x, mod3, conv0, g1, w1, wif_row, bif_row, wconv, bconv)


def _cum_matrix(t):
    r = lax.broadcasted_iota(jnp.int32, (2 * t, LANES + t), 0)
    c = lax.broadcasted_iota(jnp.int32, (2 * t, LANES + t), 1)
    return jnp.where((c < LANES) | ((r & (t - 1)) >= (c - LANES)), 1.0, 0.0).astype(BF16)


def _attn_kernel(q_ref, kd_ref, vd_ref, kp_ref, vp_ref, o_ref, qm_sc, acc_sc, carry_sc, kbuf, vbuf, dma_sem,
                 *, bt, tq, tk, n_past):
    cache_past = n_past is not None
    n_blocks = n_past if cache_past else pl.program_id(1)
    scale = 1.0 / math.sqrt(SB_HEAD_DIM)
    n_pairs = SB_WIDTH // LANES
    lane = lax.broadcasted_iota(jnp.int32, (1, LANES), 1)
    first_head = lane < SB_HEAD_DIM

    n_units = bt * n_pairs
    for u in range(n_units):
        s, p = divmod(u, n_pairs)
        qp = q_ref[s, :, p * LANES:(p + 1) * LANES] * scale
        zero = jnp.zeros_like(qp)
        qm_sc[(2 * u) * tq:(2 * u + 1) * tq, :] = jnp.where(first_head, qp, zero)
        qm_sc[(2 * u + 1) * tq:(2 * u + 2) * tq, :] = jnp.where(first_head, zero, qp)

    per_group = n_units // ATTN_GROUPS
    groups = [range(g, g + per_group) for g in range(0, n_units, per_group)]
    group_rows = [slice(2 * grp[0] * tq, 2 * (grp[-1] + 1) * tq) for grp in groups]

    def scores(k_of, transposed):
        dot_k = _dot if transposed else _dot_nt
        return [jnp.concatenate([dot_k(qm_sc[2 * u * tq:(2 * u + 2) * tq, :], k_of(u)) for u in grp], axis=0)
                for grp in groups]

    def weights(zs, t, visible):
        cum = _cum_matrix(t)
        if visible is not None:
            zs = [jnp.where(visible[rows], z, MASKED_SCORE) for z, rows in zip(zs, group_rows)]
        sums = []
        for gi, rows in enumerate(group_rows):
            sp = _softplus_pos(zs[gi])
            hi = sp.astype(BF16)
            lo = (sp - hi.astype(F32)).astype(BF16)
            sums.append(_dot(jnp.concatenate([hi, lo], axis=1), cum))
        out = []
        for gi, rows in enumerate(group_rows):
            logit = zs[gi] - sums[gi][:, LANES:]
            if visible is None:
                carry = carry_sc[rows, :]
                a = jnp.exp(logit - carry)
                carry_sc[rows, :] = carry + sums[gi][:, :LANES]
            else:
                a = jnp.exp(logit)
                carry_sc[rows, :] = sums[gi][:, :LANES]
            out.append(a.astype(BF16))
        return out

    def accumulate(a_groups, v_of, transposed, assign=False):
        dot_v = _dot_nt if transposed else _dot
        for gi, grp in enumerate(groups):
            for j, u in enumerate(grp):
                s, p = divmod(u, n_pairs)
                cols = slice(p * LANES, (p + 1) * LANES)
                pv = dot_v(a_groups[gi][2 * j * tq:(2 * j + 2) * tq, :], v_of(u))
                out = jnp.where(first_head, pv[:tq], pv[tq:])
                if assign:
                    acc_sc[s, :, cols] = out
                else:
                    acc_sc[s, :, cols] += out

    def unit_cols(ref):
        def of(u):
            s, p = divmod(u, n_pairs)
            return ref[s, :, p * LANES:(p + 1) * LANES]
        return of

    step = pl.program_id(0)
    on_demand = 4

    def cache_copies(step_index, block_index, buf):
        start = pl.multiple_of(block_index * tk, tk)
        return [pltpu.make_async_copy(src.at[pl.ds(step_index * bt, bt), :, pl.ds(start, tk)],
                                      dst.at[buf], dma_sem.at[buf, n])
                for n, (src, dst) in enumerate(((kp_ref, kbuf), (vp_ref, vbuf)))]

    def newest_pair(step_index):
        half = 2 * (step_index % 2)
        return cache_copies(step_index, n_blocks - 1, half) + cache_copies(step_index, n_blocks - 2, half + 1)

    if cache_past:
        @pl.when(step == 0)
        def _():
            for copy in newest_pair(step):
                copy.start()

        @pl.when(step + 1 < pl.num_programs(0))
        def _():
            for copy in newest_pair(step + 1):
                copy.start()

    def past_of(ref, block_index):
        start = pl.multiple_of(block_index * tk, tk)

        def of(u):
            s, p = divmod(u, n_pairs)
            if cache_past:
                return ref[s, p * LANES:(p + 1) * LANES, :].astype(BF16)
            return ref[s, pl.ds(start, tk), p * LANES:(p + 1) * LANES].astype(BF16)
        return of

    rd = lax.broadcasted_iota(jnp.int32, (2 * n_units * tq, tq), 0) & (tq - 1)
    cd = lax.broadcasted_iota(jnp.int32, (2 * n_units * tq, tq), 1)
    a_diag = weights(scores(unit_cols(kd_ref), False), tq, cd < rd)
    accumulate(a_diag, unit_cols(vd_ref), False, assign=True)

    def not_done():
        return jnp.min(carry_sc[...]) <= EXP_UNDERFLOW

    def past_block(i, buf=None):
        block_index = n_blocks - 1 - i
        if cache_past:
            if buf is None:
                buf = on_demand
                copies = cache_copies(step, block_index, buf)
                for copy in copies:
                    copy.start()
                for copy in copies:
                    copy.wait()
            k_src, v_src = kbuf.at[buf], vbuf.at[buf]
        else:
            k_src, v_src = kp_ref, vp_ref
        a = weights(scores(past_of(k_src, block_index), cache_past), tk, None)
        accumulate(a, past_of(v_src, block_index), cache_past)

    def pair_cond(state):
        i, go = state
        return jnp.logical_and(i + 1 < n_blocks, go)

    def pair_body(state):
        i, _ = state
        past_block(i)
        past_block(i + 1)
        return i + 2, not_done()

    def cond(state):
        i, go = state
        return jnp.logical_and(i < n_blocks, go)

    def body(state):
        i, _ = state
        past_block(i)
        return i + 1, not_done()

    if cache_past:
        for copy in newest_pair(step):
            copy.wait()
        half = 2 * (step % 2)
        past_block(0, half)
        past_block(1, half + 1)
        state = (jnp.int32(2), not_done())
    else:
        state = (jnp.int32(0), not_done())
    state = lax.while_loop(pair_cond, pair_body, state)
    lax.while_loop(cond, body, state)
    o_ref[...] = acc_sc[...].astype(BF16)


def _attn(q, k_new, v_new, k_past, v_past, *, bt, tq, n_past):
    nb, seq, _ = q.shape
    assert nb % bt == 0
    tok = pl.BlockSpec((bt, tq, SB_WIDTH), lambda b, i: (b, i, 0))
    if n_past is None:
        past = pl.BlockSpec((bt,) + k_past.shape[1:], lambda b, i: (b, 0, 0))
    else:
        past = pl.BlockSpec(memory_space=pl.ANY)
        assert seq == tq and n_past >= 2
    rows = bt * SB_HEADS * tq
    n_bufs = 5
    buf_shape = (n_bufs, bt, SB_WIDTH, ATTN_BLOCK) if n_past is not None else (n_bufs, 1, SUBLANES, LANES)
    block_buf = pltpu.VMEM(buf_shape, F32)
    return pl.pallas_call(
        functools.partial(_attn_kernel, bt=bt, tq=tq, tk=ATTN_BLOCK, n_past=n_past),
        grid=(nb // bt, seq // tq),
        in_specs=[tok, tok, tok, past, past],
        out_specs=tok,
        out_shape=jax.ShapeDtypeStruct((nb, seq, SB_WIDTH), BF16),
        scratch_shapes=[pltpu.VMEM((rows, LANES), BF16),
                        pltpu.VMEM((bt, tq, SB_WIDTH), F32),
                        pltpu.VMEM((rows, LANES), F32),
                        block_buf, block_buf, pltpu.SemaphoreType.DMA((n_bufs, 2))],
        compiler_params=_params(2),
        name="attn",
    )(q, k_new, v_new, k_past, v_past)


def _mlstm_kernel(mq_ref, mk_ref, mv_ref, mo_ref, grow_ref, c0_ref, n0_ref, m0_ref, gn_ref,
                  yb_ref, c_out_ref, n_out_ref, m_out_ref,
                  c_sc, n_sc, m_sc, alpha_sc, beta_sc, r_sc, wg_sc, dec_sc, mprev_sc, *, bt, lc, nc):
    step = pl.program_id(1)
    dk = ML_HEAD_DIM

    ri = lax.broadcasted_iota(jnp.int32, (lc, lc), 0)
    ci = lax.broadcasted_iota(jnp.int32, (lc, lc), 1)
    causal = ci <= ri
    eye = jnp.where(ri == ci, 1.0, 0.0).astype(BF16)
    ones_blk = jnp.ones((lc, dk), BF16)

    @pl.when(step == 0)
    def _():
        c_sc[...] = c0_ref[...]
        n_sc[...] = n0_ref[...]
        incl_t = jnp.where(ri <= ci, 1.0, 0.0).astype(BF16)
        blocks = [(s, c) for s in range(bt) for c in range(nc)]
        li = jnp.concatenate([grow_ref[s, 0:SUBLANES, c * lc:(c + 1) * lc] for s, c in blocks], axis=0)
        lf = jnp.concatenate([grow_ref[s, SUBLANES:2 * SUBLANES, c * lc:(c + 1) * lc] for s, c in blocks],
                             axis=0)
        b = _split_dot(lf, incl_t, 3)
        r = li - b
        lane = lax.broadcasted_iota(jnp.int32, r.shape, 1)
        run = r
        d = 1
        while d < lc:
            run = jnp.maximum(run, jnp.where(lane >= d, pltpu.roll(run, d, axis=1), -jnp.inf))
            d *= 2
        b_last = b[:, lc - 1:lc]
        g = b_last - b + li
        g_max = jnp.max(g, axis=1, keepdims=True)
        m_before = []
        m_after = []
        for s in range(bt):
            m = m0_ref[s]
            for c in range(nc):
                rows = slice((s * nc + c) * SUBLANES, (s * nc + c + 1) * SUBLANES)
                m_before.append(m)
                m = jnp.maximum(b_last[rows] + m, g_max[rows])
                m_after.append(m)
            m_sc[s] = m
        m_prev = jnp.concatenate(m_before, axis=0)
        m_new = jnp.concatenate(m_after, axis=0)
        top = jnp.maximum(run, m_prev[:, :lc])
        alpha = (-(top + jnp.abs(top) * STABILISER_MARGIN)).astype(BF16).astype(F32)
        alpha_sc[...] = alpha
        beta_sc[...] = alpha - b
        r_sc[...] = r
        wg_sc[...] = jnp.exp(g - m_new[:, :lc])
        dec_sc[...] = jnp.exp(b_last + m_prev - m_new)
        mprev_sc[...] = m_prev

    units = [(s, h) for s in range(bt) for h in range(ML_HEADS)]
    cols = [slice(h * dk, (h + 1) * dk) for h in range(ML_HEADS)]
    alpha, beta, r, wg, decay, m_prev = [], [], [], [], [], []
    for s in range(bt):
        chunk = pl.ds(pl.multiple_of((s * nc + step) * SUBLANES, SUBLANES), SUBLANES)
        alpha.append(alpha_sc[chunk, :].astype(BF16))
        beta.append(beta_sc[chunk, :].astype(BF16))
        r.append(r_sc[chunk, :])
        wg.append(wg_sc[chunk, :])
        decay.append(dec_sc[chunk, :])
        m_prev.append(mprev_sc[chunk, :])

    def rows_of(x, h):
        return jnp.broadcast_to(x[h:h + 1, :], (dk, lc))
    alpha_b = _dot_nt(eye, jnp.concatenate([rows_of(alpha[s], h) for s, h in units], axis=0))
    beta_b = _dot_nt(eye, jnp.concatenate([rows_of(beta[s], h) for s, h in units], axis=0))
    qk = [_dot_nt(mq_ref[s, :, cols[h]], mk_ref[s, :, cols[h]]) for s, h in units]
    c_prev = [c_sc[s, h] for s, h in units]
    n_prev = [n_sc[s, h:h + 1, :] for s, h in units]
    qc = [_dot_nt(mq_ref[s, :, cols[h]],
                  jnp.concatenate([c_prev[u].astype(BF16),
                                   jnp.broadcast_to(n_prev[u], (dk, dk)).astype(BF16)], axis=0))
          for u, (s, h) in enumerate(units)]
    upd = []
    for s, h in units:
        wg_h = wg[s][h:h + 1, :]
        vt = mv_ref[s, :, cols[h]].astype(F32).T
        lhs = jnp.concatenate([(vt * wg_h).astype(BF16),
                               jnp.broadcast_to(wg_h, (SUBLANES, lc)).astype(BF16)], axis=0)
        upd.append(_dot(lhs, mk_ref[s, :, cols[h]]))

    sw = []
    for u, (s, h) in enumerate(units):
        w = jnp.where(causal, jnp.exp(alpha_b[:, u * dk:u * dk + lc] + r[s][h:h + 1, :]), 0.0)
        sw.append((qk[u] * w).astype(BF16))
    nv = [_dot(sw[u], jnp.concatenate([mv_ref[s, :, cols[h]], ones_blk], axis=1))
          for u, (s, h) in enumerate(units)]
    hval = []
    for u, (s, h) in enumerate(units):
        ucols = slice(u * dk, (u + 1) * dk)
        w_inter = jnp.exp(m_prev[s][h:h + 1, :] + alpha_b[:, ucols])
        num = nv[u][:, :dk] + w_inter * qc[u][:, :dk]
        den = nv[u][:, dk:] + w_inter * qc[u][:, dk:]
        hval.append(num / jnp.maximum(jnp.abs(den), jnp.exp(beta_b[:, ucols])))
    ms = _dot(jnp.concatenate([(hv * hv).astype(BF16) for hv in hval], axis=0),
              jnp.full((dk, dk), 1.0 / dk, BF16))
    for u, (s, h) in enumerate(units):
        y = hval[u] * lax.rsqrt(ms[u * lc:(u + 1) * lc] + EPS) * gn_ref[:, cols[h]]
        yb_ref[s, :, cols[h]] = (_sigmoid(mo_ref[s, :, cols[h]]) * y).astype(BF16)

    for u, (s, h) in enumerate(units):
        dec = decay[s][h:h + 1, :]
        c_sc[s, h] = dec * c_prev[u] + upd[u][:dk]
        n_sc[s, h:h + 1, :] = dec * n_prev[u] + upd[u][dk:dk + 1]

    c_out_ref[...] = c_sc[...]
    n_out_ref[...] = n_sc[...]
    m_out_ref[...] = m_sc[...]


def _mlstm(mq, mk, mv, mo, grow, c0, n0, m0, gn, *, bt, lc):
    nb, seq, _ = mq.shape
    assert nb % bt == 0
    tok = pl.BlockSpec((bt, lc, ML_WIDTH), lambda b, i: (b, i, 0))
    c_spec = pl.BlockSpec((bt, ML_HEADS, ML_HEAD_DIM, ML_HEAD_DIM), lambda b, i: (b, 0, 0, 0))
    n_spec = pl.BlockSpec((bt, ML_HEADS, ML_HEAD_DIM), lambda b, i: (b, 0, 0))
    m_spec = pl.BlockSpec((bt, SUBLANES, LANES), lambda b, i: (b, 0, 0))
    nc = seq // lc
    table = pltpu.VMEM((bt * nc * SUBLANES, lc), F32)
    table_m = pltpu.VMEM((bt * nc * SUBLANES, LANES), F32)
    return pl.pallas_call(
        functools.partial(_mlstm_kernel, bt=bt, lc=lc, nc=nc),
        grid=(nb // bt, nc),
        in_specs=[tok, tok, tok, tok,
                  pl.BlockSpec((bt, 2 * SUBLANES, seq), lambda b, i: (b, 0, 0)),
                  c_spec, n_spec, m_spec, _const_spec((1, ML_WIDTH))],
        out_specs=[tok, c_spec, n_spec, m_spec],
        out_shape=[jax.ShapeDtypeStruct((nb, seq, ML_WIDTH), BF16),
                   jax.ShapeDtypeStruct((nb, ML_HEADS, ML_HEAD_DIM, ML_HEAD_DIM), F32),
                   jax.ShapeDtypeStruct((nb, ML_HEADS, ML_HEAD_DIM), F32),
                   jax.ShapeDtypeStruct((nb, SUBLANES, LANES), F32)],
        scratch_shapes=[pltpu.VMEM((bt, ML_HEADS, ML_HEAD_DIM, ML_HEAD_DIM), F32),
                        pltpu.VMEM((bt, ML_HEADS, ML_HEAD_DIM), F32),
                        pltpu.VMEM((bt, SUBLANES, LANES), F32),
                        table, table, table, table, table_m, table_m],
        compiler_params=_params(2),
        name="mlstm",
    )(mq, mk, mv, mo, grow, c0, n0, m0, gn)


def _post_kernel(x_ref, ya_ref, yb_ref, mod_ref, g1_ref, g2_ref, gf_ref,
                 wgab_ref, wa_ref, wb_ref, wout_ref, wfg_ref, wfu_ref, wfd_ref,
                 y_ref, u_sc, *, bt, lt):
    g1 = g1_ref[...]
    for j in range(bt):
        u = _norm_mod(x_ref[j], g1, mod_ref[j, 1:2, :], mod_ref[j, 0:1, :])
        u_sc[j * lt:(j + 1) * lt, :] = u.astype(BF16)
    u = u_sc[...]
    ya = ya_ref[...].reshape(bt * lt, SB_WIDTH)
    yb = yb_ref[...].reshape(bt * lt, ML_WIDTH)
    ga = _dot_nt(u, wgab_ref[:D_MODEL, :])
    gb = _dot_nt(u, wgab_ref[D_MODEL:, :])
    merged = _sigmoid(ga) * _dot(ya, wa_ref[...]) + _sigmoid(gb) * _dot(yb, wb_ref[...])
    attn_out = _dot(merged.astype(BF16), wout_ref[...])

    g2 = g2_ref[...]
    for j in range(bt):
        rows = slice(j * lt, (j + 1) * lt)
        x1 = x_ref[j] + mod_ref[j, 2:3, :] * attn_out[rows]
        y_ref[j] = x1
        u2 = _norm_mod(x1, g2, mod_ref[j, 4:5, :], mod_ref[j, 3:4, :])
        u_sc[rows, :] = u2.astype(BF16)
    u2 = u_sc[...]
    hg = _dot(u2, wfg_ref[...])
    hu = _dot(u2, wfu_ref[...])
    act = (hg * _sigmoid(hg) * hu).astype(BF16)
    ff = _dot(act, wfd_ref[...])
    gf = gf_ref[...]
    for j in range(bt):
        rows = slice(j * lt, (j + 1) * lt)
        x2 = y_ref[j] + mod_ref[j, 5:6, :] * ff[rows]
        ms = jnp.mean(x2 * x2, axis=-1, keepdims=True)
        y_ref[j] = x2 * lax.rsqrt(ms + EPS) * gf


def _post(x, ya, yb, mod3, g1, g2, gf, wgab, wa, wb, wout, wfg, wfu, wfd, *, bt, lt):
    nb, seq, _ = x.shape
    assert nb % bt == 0 and seq % lt == 0

    def tok(width):
        return pl.BlockSpec((bt, lt, width), lambda b, l: (b, l, 0))

    in_specs = [tok(D_MODEL), tok(SB_WIDTH), tok(ML_WIDTH),
                pl.BlockSpec((bt, 6, D_MODEL), lambda b, l: (b, 0, 0)),
                _const_spec((1, D_MODEL)), _const_spec((1, D_MODEL)), _const_spec((1, D_MODEL)),
                _const_spec(wgab.shape), _const_spec(wa.shape), _const_spec(wb.shape), _const_spec(wout.shape),
                _const_spec(wfg.shape), _const_spec(wfu.shape), _const_spec(wfd.shape)]
    return pl.pallas_call(
        functools.partial(_post_kernel, bt=bt, lt=lt),
        grid=(nb // bt, seq // lt),
        in_specs=in_specs,
        out_specs=tok(D_MODEL),
        out_shape=jax.ShapeDtypeStruct((nb, seq, D_MODEL), F32),
        scratch_shapes=[pltpu.VMEM((bt * lt, D_MODEL), BF16)],
        compiler_params=_params(2),
        name="post",
    )(x, ya, yb, mod3, g1, g2, gf, wgab, wa, wb, wout, wfg, wfu, wfd)


def _group(x, mod3, k_past, v_past, c0, n0, m0, conv0, wts, *, bt, lt_in, lt, tq, n_past, lc):
    (g1, g2, gf, w1, wif_row, bif_row, wconv, bconv, gn,
     wgab, wa, wb, wout, wfg, wfu, wfd) = wts
    nb, seq, _ = x.shape
    (q, k32, v32, kb, vb, mq, mk, mv, mo, grow, conv_new) = _inproj(
        x, mod3, conv0, g1, w1, wif_row, bif_row, wconv, bconv, bt=bt, lt=lt_in)
    if k_past is None:
        k_past, v_past = kb, vb
    ya = _attn(q, kb, vb, k_past, v_past, bt=ATTN_ROWS // tq, tq=tq, n_past=n_past)
    m0b = jnp.broadcast_to(jnp.pad(m0, ((0, 0), (0, SUBLANES - ML_HEADS)))[:, :, None], (nb, SUBLANES, LANES))
    yb, c1, n1, m1 = _mlstm(mq, mk, mv, mo, grow, c0, n0, m0b, gn, bt=ML_SEQS, lc=lc)
    y = _post(x, ya, yb, mod3, g1, g2, gf, wgab, wa, wb, wout, wfg, wfu, wfd, bt=bt, lt=lt)
    def rows_out(t):
        if _tokens_on_lanes(lt_in):
            return jnp.transpose(t.reshape(nb, SB_HEADS, SB_HEAD_DIM, seq), (0, 3, 1, 2))[None]
        return t.reshape(1, nb, seq, SB_HEADS, SB_HEAD_DIM)

    states = (rows_out(k32), rows_out(v32),
              c1[None], n1[None], m1[None, :, :ML_HEADS, 0], conv_new[None])
    return y, states


def kernel(x_prompt, x_sample, c_prompt, c_sample, cache_sb_k, cache_sb_v, state_mlstm_C, state_mlstm_n, state_mlstm_m, state_conv, norm1_g, norm2_g, w_ada, b_ada, w_in, b_if, w_conv, b_conv, ml_norm_g, w_a, w_b, w_out, w_ff_gate, w_ff_up, w_ff_down, final_g):
    assert w_in.shape[0] == 1, "single layer"
    bp, seq_p, _ = x_prompt.shape
    bs, seq_s, _ = x_sample.shape
    past = cache_sb_k.shape[2]

    c_all = jnp.concatenate([c_prompt, c_sample], axis=0)
    mod3 = _ada(c_all, w_ada[0], b_ada[0]).reshape(bp + bs, 6, D_MODEL)

    w1, wgab, wif_row, wa, wb, wout, wfg, wfu, wfd = _prep(
        jnp.transpose(w_in[0]), w_a[0], w_b[0], w_out[0], w_ff_gate[0], w_ff_up[0], w_ff_down[0])
    pad = SUBLANES - ML_HEADS
    bif_row = jnp.pad(b_if[0].reshape(2, ML_HEADS), ((0, 0), (0, pad))).reshape(2 * SUBLANES, 1)
    wts = (norm1_g[0].reshape(1, D_MODEL), norm2_g[0].reshape(1, D_MODEL), final_g.reshape(1, D_MODEL),
           w1, wif_row, bif_row, w_conv[0], b_conv[0].reshape(1, 2 * ML_WIDTH),
           ml_norm_g[0].reshape(1, ML_WIDTH),
           wgab, wa, wb, wout, wfg, wfu, wfd)

    zeros = functools.partial(jnp.zeros, dtype=F32)
    y_p, st_p = _group(
        x_prompt, mod3[:bp], None, None,
        zeros((bp, ML_HEADS, ML_HEAD_DIM, ML_HEAD_DIM)), zeros((bp, ML_HEADS, ML_HEAD_DIM)),
        zeros((bp, ML_HEADS)), zeros((bp, CONV_W - 1, 2 * ML_WIDTH)), wts,
        bt=1, lt_in=512, lt=256, tq=ATTN_BLOCK, n_past=None, lc=ML_CHUNK)
    def cache_t(c):
        return jnp.transpose(c[0], (0, 2, 3, 1)).reshape(bs, SB_WIDTH, past)

    y_s, st_s = _group(
        x_sample, mod3[bp:], cache_t(cache_sb_k), cache_t(cache_sb_v),
        state_mlstm_C[0], state_mlstm_n[0], state_mlstm_m[0], state_conv[0], wts,
        bt=4, lt_in=seq_s, lt=seq_s, tq=seq_s, n_past=past // ATTN_BLOCK, lc=seq_s)
    return (y_p, y_s) + st_p + st_s
```
